```python
import jax, jax.numpy as jnp
from jax import lax
import numpy as np

D_MODEL = 2048
BATCH = 2
SEQ = 4096
DEPTH = 4
DEC_BATCH = 32
DEC_SEQ = 64
PAST_LEN = 2048

CHUNK = 64
D_MIX = D_MODEL
HEAD_DIM = 128
D_LRU = D_MIX // 2
LRU_BLOCKS = 8
LRU_BLOCK = D_LRU // LRU_BLOCKS
CONV_W = 4
LRU_C = 8.0
D_BAND = D_MIX // 4
N_BAND_HEADS = D_BAND // HEAD_DIM
BAND_PAST_CHUNKS = 8
BAND_PAST = BAND_PAST_CHUNKS * CHUNK
REL_CLIP = 256
D_DSA = D_MIX - D_LRU - D_BAND
N_DSA_HEADS = D_DSA // HEAD_DIM
N_IDX_HEADS = 16
IDX_DIM = 64
TOPK_MAX = 256
Q_BLOCK = 128
ROPE_THETA = 500000.0
ROPE_FRAC = 4
EPS = 1e-6
NEG = -1e30
SPLITS = (D_LRU, D_LRU, D_BAND, D_BAND, D_BAND, D_BAND, D_DSA, D_DSA, D_DSA, D_DSA,
          N_IDX_HEADS * IDX_DIM, IDX_DIM, N_IDX_HEADS)
D_IN = sum(SPLITS)

kernel_name = 'hybrid_streaming_encoder_step'


def rmsnorm(x, g):
    xf = x.astype(jnp.float32)
    y = xf * lax.rsqrt(jnp.mean(xf * xf, axis=-1, keepdims=True) + EPS)
    return (y * g.astype(jnp.float32)).astype(x.dtype)


def modulate(x, c, g, w_ada, b_ada):
    mod = jnp.dot(jax.nn.silu(c), w_ada) + b_ada
    shift, scale, gate = jnp.split(mod, 3, axis=-1)
    h = rmsnorm(x, g) * (1 + scale[:, None]) + shift[:, None]
    return h, gate[:, None]


def project(h, w_in):
    u = jnp.dot(h, w_in)
    return jnp.split(u, np.cumsum(SPLITS)[:-1].tolist(), axis=-1)


def split_heads(t, n):
    return t.reshape(t.shape[:-1] + (n, t.shape[-1] // n))


def rope_partial(x, pos):
    d = x.shape[-1]
    r = d // ROPE_FRAC
    half = r // 2
    inv = ROPE_THETA ** (-jnp.arange(half, dtype=jnp.float32) / half)
    ang = pos.astype(jnp.float32)[:, None] * inv[None]
    cos = jnp.cos(ang)[:, None, :]
    sin = jnp.sin(ang)[:, None, :]
    xr = x[..., :r].astype(jnp.float32)
    x1, x2 = xr[..., :half], xr[..., half:]
    rot = jnp.concatenate([x1 * cos - x2 * sin, x2 * cos + x1 * sin], axis=-1).astype(x.dtype)
    return jnp.concatenate([rot, x[..., r:]], axis=-1)


def causal_conv(xp, w, b):
    T = xp.shape[1] - (CONV_W - 1)
    acc = xp[:, 0:T] * w[0]
    for j in range(1, CONV_W):
        acc = acc + xp[:, j:j + T] * w[j]
    return acc + b


def rglru(x, h0, w_rg, b_rg, w_ig, b_ig, lam):
    B, T, _ = x.shape
    xb = x.reshape(B, T, LRU_BLOCKS, LRU_BLOCK)
    r = jax.nn.sigmoid(jnp.einsum('btgi,gij->btgj', xb, w_rg).reshape(B, T, D_LRU) + b_rg)
    i = jax.nn.sigmoid(jnp.einsum('btgi,gij->btgj', xb, w_ig).reshape(B, T, D_LRU) + b_ig)
    log_a = -LRU_C * r.astype(jnp.float32) * jax.nn.softplus(-lam.astype(jnp.float32))
    a = jnp.exp(log_a)
    u = jnp.sqrt(-jnp.expm1(2.0 * log_a)) * (i * x).astype(jnp.float32)
    u = u.at[:, 0].add(a[:, 0] * h0)

    def comb(l, rr):
        return (l[0] * rr[0], rr[0] * l[1] + rr[1])

    _, h = lax.associative_scan(comb, (a, u), axis=1)
    return h


def band_attn_prompt(q, k, v, rel_bias):
    B, S, H, D = q.shape
    NC = S // CHUNK
    P = BAND_PAST_CHUNKS
    KB = (P + 1) * CHUNK
    qc = q.reshape(B, NC, CHUNK, H, D)
    pad = ((0, 0), (P * CHUNK, 0), (0, 0), (0, 0))
    kp = jnp.pad(k, pad).reshape(B, NC + P, CHUNK, H, D)
    vp = jnp.pad(v, pad).reshape(B, NC + P, CHUNK, H, D)
    kb = jnp.concatenate([kp[:, j:j + NC] for j in range(P + 1)], axis=2)
    vb = jnp.concatenate([vp[:, j:j + NC] for j in range(P + 1)], axis=2)
    qi = jnp.arange(CHUNK)
    kj = jnp.arange(KB)
    rel = qi[:, None] + P * CHUNK - kj[None]
    bias = rel_bias[:, jnp.clip(rel, -REL_CLIP, REL_CLIP) + REL_CLIP].astype(jnp.float32)
    valid = (jnp.arange(NC)[:, None] - P) * CHUNK + kj[None] >= 0
    s = jnp.einsum('bnqhd,bnkhd->bnhqk', qc, kb).astype(jnp.float32) * HEAD_DIM ** -0.5 + bias
    s = jnp.where(valid[None, :, None, None, :], s, NEG)
    p = jax.nn.softmax(s, axis=-1).astype(v.dtype)
    return jnp.einsum('bnhqk,bnkhd->bnqhd', p, vb).reshape(B, S, H * D)


def band_attn_sample(q, k, v, rel_bias, q_pos, k_pos):
    B, T = q.shape[:2]
    rel = q_pos[:, None] - k_pos[None]
    bias = rel_bias[:, jnp.clip(rel, -REL_CLIP, REL_CLIP) + REL_CLIP].astype(jnp.float32)
    s = jnp.einsum('bqhd,bkhd->bhqk', q, k).astype(jnp.float32) * HEAD_DIM ** -0.5 + bias
    p = jax.nn.softmax(s, axis=-1).astype(v.dtype)
    return jnp.einsum('bhqk,bkhd->bqhd', p, v).reshape(B, T, -1)


def dsa_attend(q, qi, w, k, v, ki, q_pos, k_pos, topk):
    sc = jnp.einsum('thd,sd->ths', qi, ki).astype(jnp.float32) * IDX_DIM ** -0.5
    score = jnp.einsum('ths,th->ts', jax.nn.relu(sc), w.astype(jnp.float32))
    adm = (k_pos[None, :] // CHUNK) <= (q_pos[:, None] // CHUNK)
    score = jnp.where(adm, score, NEG)
    vals, idx = lax.top_k(score, topk)
    sel = vals > NEG * 0.5
    ks = k[idx]
    vs = v[idx]
    s = jnp.einsum('thd,tkhd->htk', q, ks).astype(jnp.float32) * HEAD_DIM ** -0.5
    s = jnp.where(sel[None], s, NEG)
    p = jax.nn.softmax(s, axis=-1).astype(v.dtype)
    return jnp.einsum('htk,tkhd->thd', p, vs)


def dsa_prompt(q, qi, w, k, v, ki, topk):
    B, S = q.shape[:2]
    nb = S // Q_BLOCK
    pos = jnp.arange(S)

    def to_blocks(a):
        return jnp.moveaxis(a.reshape((B, nb, Q_BLOCK) + a.shape[2:]), 1, 0)

    def one_block(args):
        qb, qib, wb, pb = args
        f = lambda qq, qqi, ww, kk, vv, kki: dsa_attend(qq, qqi, ww, kk, vv, kki, pb, pos, topk)
        return jax.vmap(f)(qb, qib, wb, k, v, ki)

    o = lax.map(one_block, (to_blocks(q), to_blocks(qi), to_blocks(w), pos.reshape(nb, Q_BLOCK)))
    return jnp.moveaxis(o, 0, 1).reshape(B, S, -1)


def dsa_sample(q, qi, w, k, v, ki, q_pos, k_pos, topk):
    o = lax.map(lambda a: dsa_attend(a[0], a[1], a[2], a[3], a[4], a[5], q_pos, k_pos, topk),
                (q, qi, w, k, v, ki))
    return o.reshape(q.shape[0], q.shape[1], -1)


def merge(oa, ob, oc, za, zb, zc, g_branch, w_out):
    ga = g_branch[:D_LRU]
    gb = g_branch[D_LRU:D_LRU + D_BAND]
    gc = g_branch[D_LRU + D_BAND:]
    y = jnp.concatenate([rmsnorm(oa, ga) * jax.nn.silu(za),
                         rmsnorm(ob, gb) * jax.nn.silu(zb),
                         rmsnorm(oc, gc) * jax.nn.silu(zc)], axis=-1)
    return jnp.dot(y, w_out)


def prompt_layer(x, c, lp):
    (g_n, w_a, b_a, w_i, cw, cb, wrg, brg, wig, big, lam, relb, gbr, w_o) = lp
    B, S, _ = x.shape
    pos = jnp.arange(S)
    h, gate = modulate(x, c, g_n, w_a, b_a)
    xa, za, qb, kb, vb, zb, qc, kc, vc, zc, qi, ki, wi = project(h, w_i)
    xpad = jnp.concatenate([jnp.zeros((B, CONV_W - 1, D_LRU), xa.dtype), xa], axis=1)
    hs = rglru(causal_conv(xpad, cw, cb), jnp.zeros((B, D_LRU), jnp.float32), wrg, brg, wig, big, lam)
    qb = split_heads(qb, N_BAND_HEADS)
    kb = split_heads(kb, N_BAND_HEADS)
    vb = split_heads(vb, N_BAND_HEADS)
    ob = band_attn_prompt(qb, kb, vb, relb)
    qc = rope_partial(split_heads(qc, N_DSA_HEADS), pos)
    kc = rope_partial(split_heads(kc, N_DSA_HEADS), pos)
    vc = split_heads(vc, N_DSA_HEADS)
    qi = rope_partial(split_heads(qi, N_IDX_HEADS), pos)
    ki = rope_partial(ki[:, :, None], pos)[:, :, 0]
    oc = dsa_prompt(qc, qi, wi * N_IDX_HEADS ** -0.5, kc, vc, ki, min(TOPK_MAX, S // 4))
    out = merge(hs.astype(x.dtype), ob, oc, za, zb, zc, gbr, w_o)
    nbr = min(BAND_PAST, S)
    new = (xpad[:, -(CONV_W - 1):], hs[:, -1], kb[:, -nbr:], vb[:, -nbr:], kc, vc, ki)
    return x + gate * out, new


def sample_layer(x, c, st, lp):
    (g_n, w_a, b_a, w_i, cw, cb, wrg, brg, wig, big, lam, relb, gbr, w_o) = lp
    conv_s, lru_s, bk, bv, dk, dv, dik = st
    B, T, _ = x.shape
    P = dk.shape[1]
    W = bk.shape[1]
    pos = P + jnp.arange(T)
    h, gate = modulate(x, c, g_n, w_a, b_a)
    xa, za, qb, kb, vb, zb, qc, kc, vc, zc, qi, ki, wi = project(h, w_i)
    xpad = jnp.concatenate([conv_s.astype(xa.dtype), xa], axis=1)
    hs = rglru(causal_conv(xpad, cw, cb), lru_s.astype(jnp.float32), wrg, brg, wig, big, lam)
    qb = split_heads(qb, N_BAND_HEADS)
    kb = split_heads(kb, N_BAND_HEADS)
    vb = split_heads(vb, N_BAND_HEADS)
    kb_all = jnp.concatenate([bk.astype(kb.dtype), kb], axis=1)
    vb_all = jnp.concatenate([bv.astype(vb.dtype), vb], axis=1)
    ob = band_attn_sample(qb, kb_all, vb_all, relb, pos, P - W + jnp.arange(W + T))
    qc = rope_partial(split_heads(qc, N_DSA_HEADS), pos)
    kc = rope_partial(split_heads(kc, N_DSA_HEADS), pos)
    vc = split_heads(vc, N_DSA_HEADS)
    qi = rope_partial(split_heads(qi, N_IDX_HEADS), pos)
    ki = rope_partial(ki[:, :, None], pos)[:, :, 0]
    L = P + T
    kc_all = jnp.concatenate([dk.astype(kc.dtype), kc], axis=1)
    vc_all = jnp.concatenate([dv.astype(vc.dtype), vc], axis=1)
    ki_all = jnp.concatenate([dik.astype(ki.dtype), ki], axis=1)
    oc = dsa_sample(qc, qi, wi * N_IDX_HEADS ** -0.5, kc_all, vc_all, ki_all, pos, jnp.arange(L),
                    min(TOPK_MAX, L // 4))
    out = merge(hs.astype(x.dtype), ob, oc, za, zb, zc, gbr, w_o)
    new = (xpad[:, -(CONV_W - 1):], hs[:, -1], kb, vb, kc, vc, ki)
    return x + gate * out, new


def setup_inputs(seed: int = 0) -> dict:
    key = jax.random.key(seed)
    ks = jax.random.split(key, 32)
    f32 = jnp.float32

    def nrm(k, shape, s):
        return jax.random.normal(k, shape, f32) * s

    band_rows = min(BAND_PAST, PAST_LEN)
    return {
        'x_prompt': nrm(ks[0], (BATCH, SEQ, D_MODEL), 1.0),
        'x_sample': nrm(ks[1], (DEC_BATCH, DEC_SEQ, D_MODEL), 1.0),
        'c_prompt': nrm(ks[2], (BATCH, D_MODEL), 1.0),
        'c_sample': nrm(ks[3], (DEC_BATCH, D_MODEL), 1.0),
        'state_conv': nrm(ks[4], (DEPTH, DEC_BATCH, CONV_W - 1, D_LRU), 1.0),
        'state_lru': nrm(ks[5], (DEPTH, DEC_BATCH, D_LRU), 0.5),
        'cache_band_k': nrm(ks[6], (DEPTH, DEC_BATCH, band_rows, N_BAND_HEADS, HEAD_DIM), 1.0),
        'cache_band_v': nrm(ks[7], (DEPTH, DEC_BATCH, band_rows, N_BAND_HEADS, HEAD_DIM), 1.0),
        'cache_dsa_k': nrm(ks[8], (DEPTH, DEC_BATCH, PAST_LEN, N_DSA_HEADS, HEAD_DIM), 1.0),
        'cache_dsa_v': nrm(ks[9], (DEPTH, DEC_BATCH, PAST_LEN, N_DSA_HEADS, HEAD_DIM), 1.0),
        'cache_dsa_idx_k': nrm(ks[10], (DEPTH, DEC_BATCH, PAST_LEN, IDX_DIM), 1.0),
        'g_norm': 1.0 + nrm(ks[11], (DEPTH, D_MODEL), 0.02),
        'w_ada': nrm(ks[12], (DEPTH, D_MODEL, 3 * D_MODEL), 0.5 * D_MODEL ** -0.5),
        'b_ada': nrm(ks[13], (DEPTH, 3 * D_MODEL), 0.02),
        'w_in': nrm(ks[14], (DEPTH, D_MODEL, D_IN), D_MODEL ** -0.5),
        'conv_w': nrm(ks[15], (DEPTH, CONV_W, D_LRU), CONV_W ** -0.5),
        'conv_b': nrm(ks[16], (DEPTH, D_LRU), 0.02),
        'w_rg': nrm(ks[17], (DEPTH, LRU_BLOCKS, LRU_BLOCK, LRU_BLOCK), LRU_BLOCK ** -0.5),
        'b_rg': nrm(ks[18], (DEPTH, D_LRU), 0.02),
        'w_ig': nrm(ks[19], (DEPTH, LRU_BLOCKS, LRU_BLOCK, LRU_BLOCK), LRU_BLOCK ** -0.5),
        'b_ig': nrm(ks[20], (DEPTH, D_LRU), 0.02),
        'lru_lambda': jax.random.uniform(ks[21], (DEPTH, D_LRU), f32, 4.3, 9.0),
        'rel_bias': nrm(ks[22], (DEPTH, N_BAND_HEADS, 2 * REL_CLIP + 1), 0.5),
        'g_branch': 1.0 + nrm(ks[23], (DEPTH, D_MIX), 0.02),
        'w_out': nrm(ks[24], (DEPTH, D_MIX, D_MODEL), D_MIX ** -0.5),
        'g_final': 1.0 + nrm(ks[25], (D_MODEL,), 0.02),
    }


def reference(x_prompt, x_sample, c_prompt, c_sample, state_conv, state_lru, cache_band_k, cache_band_v,
              cache_dsa_k, cache_dsa_v, cache_dsa_idx_k, g_norm, w_ada, b_ada, w_in, conv_w, conv_b,
              w_rg, b_rg, w_ig, b_ig, lru_lambda, rel_bias, g_branch, w_out, g_final):
    xp = x_prompt
    xs = x_sample
    p_new = []
    s_new = []
    for l in range(DEPTH):
        lp = (g_norm[l], w_ada[l], b_ada[l], w_in[l], conv_w[l], conv_b[l], w_rg[l], b_rg[l],
              w_ig[l], b_ig[l], lru_lambda[l], rel_bias[l], g_branch[l], w_out[l])
        xp, pn = prompt_layer(xp, c_prompt, lp)
        st = (state_conv[l], state_lru[l], cache_band_k[l], cache_band_v[l],
              cache_dsa_k[l], cache_dsa_v[l], cache_dsa_idx_k[l])
        xs, sn = sample_layer(xs, c_sample, st, lp)
        p_new.append(pn)
        s_new.append(sn)
    y_prompt = rmsnorm(xp, g_final)
    y_sample = rmsnorm(xs, g_final)
    p_conv, p_lru, p_band_k, p_band_v, p_dsa_k, p_dsa_v, p_dsa_idx_k = [
        jnp.stack([t[j] for t in p_new], axis=0) for j in range(7)]
    s_conv, s_lru, s_band_k, s_band_v, s_dsa_k, s_dsa_v, s_dsa_idx_k = [
        jnp.stack([t[j] for t in s_new], axis=0) for j in range(7)]
    return (y_prompt, y_sample, p_conv, p_lru, p_band_k, p_band_v, p_dsa_k, p_dsa_v, p_dsa_idx_k,
            s_conv, s_lru, s_band_k, s_band_v, s_dsa_k, s_dsa_v, s_dsa_idx_k)
```

```python
import functools
import struct

import numpy as np
import jax
import jax.numpy as jnp
from jax import lax
from jax.experimental import pallas as pl
from jax.experimental.pallas import tpu as pltpu

F32 = jnp.float32
BF16 = jnp.bfloat16
I32 = jnp.int32

CHUNK = 64
HEAD_DIM = 128
LRU_BLOCKS = 8
CONV_W = 4
LRU_C = 8.0
BAND_PAST_CHUNKS = 8
BAND_PAST = BAND_PAST_CHUNKS * CHUNK
BAND_KEYS = BAND_PAST + CHUNK
REL_CLIP = 256
N_IDX_HEADS = 16
IDX_DIM = 64
TOPK_MAX = 256
ROPE_THETA = 500000.0
ROPE_FRAC = 4
EPS = 1e-6
NEG = -1e30
LANES = 128

INT_MIN = -(2 ** 31)


def _sortable_key_of(x):
    b = struct.unpack("<i", struct.pack("<f", x))[0]
    return b ^ 0x7FFFFFFF if b < 0 else b


KEY_HALF_NEG = _sortable_key_of(NEG * 0.5)

VMEM_LIMIT = 48 * 1024 * 1024
NT_DIMS = (((1,), (1,)), ((), ()))


def _params(n_grid):
    return pltpu.CompilerParams(dimension_semantics=("arbitrary",) * n_grid,
                                vmem_limit_bytes=VMEM_LIMIT)


def _dot_nt(a, b):
    return lax.dot_general(a, b, NT_DIMS, preferred_element_type=F32)


def _ada_kernel(c_ref, w_ref, b_ref, o_ref):
    c = c_ref[...]
    s = (c * jax.nn.sigmoid(c)).astype(BF16)
    o_ref[...] = jnp.dot(s, w_ref[...].astype(BF16), preferred_element_type=F32) + b_ref[...]


def ada_all(c_all, w_ada, b_ada):
    depth, d, n = w_ada.shape
    nb = c_all.shape[0]
    tn = 512
    return pl.pallas_call(
        _ada_kernel,
        grid=(depth, n // tn),
        in_specs=[pl.BlockSpec((nb, d), lambda l, j: (0, 0)),
                  pl.BlockSpec((None, d, tn), lambda l, j: (l, 0, j)),
                  pl.BlockSpec((None, 1, tn), lambda l, j: (l, 0, j))],
        out_specs=pl.BlockSpec((None, nb, tn), lambda l, j: (l, 0, j)),
        out_shape=jax.ShapeDtypeStruct((depth, nb, n), F32),
        compiler_params=_params(2),
        name="ada",
    )(c_all, w_ada, b_ada.reshape(depth, 1, n))


def _normmod_kernel(x_ref, g_ref, mod_ref, h_ref):
    x = x_ref[...]
    ms = jnp.mean(x * x, axis=-1, keepdims=True)
    y = x * lax.rsqrt(ms + EPS) * g_ref[...]
    shift = mod_ref[0:1, :]
    scale = mod_ref[1:2, :]
    h_ref[...] = (y * (1.0 + scale) + shift).astype(BF16)


def normmod(x, g, mod3, nseq, t, tb):
    rows, d = x.shape
    nblk = t // tb
    return pl.pallas_call(
        _normmod_kernel,
        grid=(nseq, nblk),
        in_specs=[pl.BlockSpec((tb, d), lambda b, i: (b * nblk + i, 0)),
                  pl.BlockSpec((1, d), lambda b, i: (0, 0)),
                  pl.BlockSpec((None, 3, d), lambda b, i: (b, 0, 0))],
        out_specs=pl.BlockSpec((tb, d), lambda b, i: (b * nblk + i, 0)),
        out_shape=jax.ShapeDtypeStruct((rows, d), BF16),
        compiler_params=_params(2),
        name="normmod",
    )(x, g, mod3)


def _mm_kernel(a_ref, b_ref, o_ref):
    o_ref[...] = jnp.dot(a_ref[...], b_ref[...], preferred_element_type=F32)


def matmul(a, w, layer, ncols, tm, tn):
    m, k = a.shape
    return pl.pallas_call(
        _mm_kernel,
        grid=(ncols // tn, m // tm),
        in_specs=[pl.BlockSpec((tm, k), lambda j, i: (i, 0)),
                  pl.BlockSpec((None, k, tn), lambda j, i: (layer, 0, j))],
        out_specs=pl.BlockSpec((tm, tn), lambda j, i: (i, j)),
        out_shape=jax.ShapeDtypeStruct((m, ncols), F32),
        compiler_params=_params(2),
        name="inproj",
    )(a, w)


def _rope_kernel(qc_ref, kc_ref, vc_ref, qi_ref, t_ref, c128_ref, s128_ref, c64_ref, s64_ref,
                 q_out, krot_out, kbf_out, vbf_out, qi_out, trot_out, kibf_out, *, n_dsa, n_idx):
    tm = qc_ref.shape[0]
    lane = lax.broadcasted_iota(I32, (tm, LANES), 1)
    c128 = c128_ref[...]
    s128 = s128_ref[...]
    c64 = c64_ref[...]
    s64 = s64_ref[...]
    half128 = HEAD_DIM // ROPE_FRAC // 2
    half64 = IDX_DIM // ROPE_FRAC // 2

    def rope128(x):
        partner = jnp.where(lane < half128, pltpu.roll(x, LANES - half128, 1), pltpu.roll(x, half128, 1))
        return x * c128 + partner * s128

    def rope64(x):
        partner = jnp.where((lane & (IDX_DIM - 1)) < half64,
                            pltpu.roll(x, LANES - half64, 1), pltpu.roll(x, half64, 1))
        return x * c64 + partner * s64

    for h in range(n_dsa):
        sl = slice(h * LANES, (h + 1) * LANES)
        q_out[:, sl] = rope128(qc_ref[:, sl]).astype(BF16)
        kr = rope128(kc_ref[:, sl])
        krot_out[:, sl] = kr
        kbf_out[:, sl] = kr.astype(BF16)
        vbf_out[:, sl] = vc_ref[:, sl].astype(BF16)
    low = lane < IDX_DIM
    for j in range(n_idx // 2):
        r = rope64(qi_ref[:, j * LANES:(j + 1) * LANES]) * (IDX_DIM ** -0.5)
        qi_out[:, (2 * j) * LANES:(2 * j + 1) * LANES] = jnp.where(low, r, 0.0).astype(BF16)
        qi_out[:, (2 * j + 1) * LANES:(2 * j + 2) * LANES] = jnp.where(
            low, pltpu.roll(r, IDX_DIM, 1), 0.0).astype(BF16)
    t = t_ref[...]
    r = rope64(t)
    trot_out[...] = jnp.where(low, r, jnp.where(lane < IDX_DIM + N_IDX_HEADS, t * (N_IDX_HEADS ** -0.5), 0.0))
    kibf_out[...] = jnp.where(low, r, 0.0).astype(BF16)


def rope_call(u, tail, tables, cols, tm):
    rows = u.shape[0]
    d_dsa = cols["d_dsa"]
    n_dsa = d_dsa // HEAD_DIM
    d_qi = N_IDX_HEADS * IDX_DIM
    c128, s128, c64, s64 = tables
    tab_spec = pl.BlockSpec((tm, LANES), lambda i: (i, 0))
    blk = lambda off: pl.BlockSpec((tm, d_dsa), lambda i, o=off // d_dsa: (i, o))
    return pl.pallas_call(
        functools.partial(_rope_kernel, n_dsa=n_dsa, n_idx=N_IDX_HEADS),
        grid=(rows // tm,),
        in_specs=[blk(cols["qc"]), blk(cols["kc"]), blk(cols["vc"]),
                  pl.BlockSpec((tm, d_qi), lambda i, o=cols["qi"] // d_qi: (i, o)),
                  tab_spec, tab_spec, tab_spec, tab_spec, tab_spec],
        out_specs=[pl.BlockSpec((tm, d_dsa), lambda i: (i, 0)),
                   pl.BlockSpec((tm, d_dsa), lambda i: (i, 0)),
                   pl.BlockSpec((tm, d_dsa), lambda i: (i, 0)),
                   pl.BlockSpec((tm, d_dsa), lambda i: (i, 0)),
                   pl.BlockSpec((tm, N_IDX_HEADS * LANES), lambda i: (i, 0)),
                   tab_spec, tab_spec],
        out_shape=[jax.ShapeDtypeStruct((rows, d_dsa), BF16),
                   jax.ShapeDtypeStruct((rows, d_dsa), F32),
                   jax.ShapeDtypeStruct((rows, d_dsa), BF16),
                   jax.ShapeDtypeStruct((rows, d_dsa), BF16),
                   jax.ShapeDtypeStruct((rows, N_IDX_HEADS * LANES), BF16),
                   jax.ShapeDtypeStruct((rows, LANES), F32),
                   jax.ShapeDtypeStruct((rows, LANES), BF16)],
        compiler_params=_params(1),
        name="rope",
    )(u, u, u, u, tail, c128, s128, c64, s64)


def rope_tables(pos):
    pos = pos.astype(F32)[:, None]
    n = pos.shape[0]

    def tab(dim):
        half = dim // ROPE_FRAC // 2
        inv = ROPE_THETA ** (-jnp.arange(half, dtype=F32) / half)
        ang = pos * inv[None]
        cos, sin = jnp.cos(ang), jnp.sin(ang)
        c = jnp.concatenate([cos, cos, jnp.ones((n, dim - 2 * half), F32)], axis=1)
        s = jnp.concatenate([-sin, sin, jnp.zeros((n, dim - 2 * half), F32)], axis=1)
        return jnp.tile(c, (1, LANES // dim)), jnp.tile(s, (1, LANES // dim))

    c128, s128 = tab(HEAD_DIM)
    c64, s64 = tab(IDX_DIM)
    return c128, s128, c64, s64


def _lru_kernel(xa_ref, cs_ref, h0_ref, cw_ref, cb_ref, wrg_ref, brg_ref, wig_ref, big_ref, lam_ref,
                hs_ref, ext_ref, hc_ref):
    tb, d = xa_ref.shape
    blk = d // LRU_BLOCKS

    @pl.when(pl.program_id(1) == 0)
    def _():
        ext_ref[0:8, :] = cs_ref[...]
        hc_ref[...] = h0_ref[...]

    ext_ref[8:8 + tb, :] = xa_ref[...]
    conv = ext_ref[5:5 + tb, :] * cw_ref[0:1, :]
    for j in range(1, CONV_W):
        conv = conv + ext_ref[5 + j:5 + j + tb, :] * cw_ref[j:j + 1, :]
    conv = conv + cb_ref[...]
    tail = ext_ref[tb:tb + 8, :]
    ext_ref[0:8, :] = tail

    xb = conv.astype(BF16)
    rs, gs = [], []
    for g in range(LRU_BLOCKS):
        xg = xb[:, g * blk:(g + 1) * blk]
        rs.append(jnp.dot(xg, wrg_ref[g], preferred_element_type=F32))
        gs.append(jnp.dot(xg, wig_ref[g], preferred_element_type=F32))
    r = jax.nn.sigmoid(jnp.concatenate(rs, axis=1) + brg_ref[...])
    ig = jax.nn.sigmoid(jnp.concatenate(gs, axis=1) + big_ref[...])
    lam = lam_ref[...]
    softplus_neg_lam = jnp.maximum(-lam, 0.0) + jnp.log1p(jnp.exp(-jnp.abs(lam)))
    log_a = (-LRU_C) * r * softplus_neg_lam
    a = jnp.exp(log_a)
    u = jnp.sqrt(-jnp.tanh(log_a) * (a * a + 1.0)) * (ig * conv)

    row = lax.broadcasted_iota(I32, (tb, d), 0)
    s = 1
    while s < tb:
        keep = row >= s
        a_prev = jnp.where(keep, pltpu.roll(a, s, 0), 1.0)
        u_prev = jnp.where(keep, pltpu.roll(u, s, 0), 0.0)
        u = a * u_prev + u
        a = a * a_prev
        s *= 2
    h = a * hc_ref[...] + u
    hs_ref[...] = h
    hc_ref[...] = h[tb - 1:tb, :]


def lru_call(u, cs8, h0, cw, cb, wrg, brg, wig, big, lam, nseq, t, tb):
    rows = u.shape[0]
    d = cw.shape[1]
    blk = d // LRU_BLOCKS
    nblk = t // tb
    vec = pl.BlockSpec((1, d), lambda b, i: (0, 0))
    wspec = pl.BlockSpec((LRU_BLOCKS, blk, blk), lambda b, i: (0, 0, 0))
    return pl.pallas_call(
        _lru_kernel,
        grid=(nseq, nblk),
        in_specs=[pl.BlockSpec((tb, d), lambda b, i: (b * nblk + i, 0)),
                  pl.BlockSpec((None, 8, d), lambda b, i: (b, 0, 0)),
                  pl.BlockSpec((None, 1, d), lambda b, i: (b, 0, 0)),
                  pl.BlockSpec((CONV_W, d), lambda b, i: (0, 0)),
                  vec, wspec, vec, wspec, vec, vec],
        out_specs=pl.BlockSpec((tb, d), lambda b, i: (b * nblk + i, 0)),
        out_shape=jax.ShapeDtypeStruct((rows, d), F32),
        scratch_shapes=[pltpu.VMEM((tb + 8, d), F32), pltpu.VMEM((1, d), F32)],
        compiler_params=_params(2),
        name="lru",
    )(u, cs8, h0, cw, cb, wrg, brg, wig, big, lam)


def _bias_kernel(relb_ref, o_ref):
    h = pl.program_id(0)
    qi = lax.broadcasted_iota(I32, (CHUNK, BAND_KEYS), 0)
    kj = lax.broadcasted_iota(I32, (CHUNK, BAND_KEYS), 1)
    idx = jnp.clip(qi + BAND_PAST - kj, -REL_CLIP, REL_CLIP) + REL_CLIP

    def body(r, acc):
        return jnp.where(idx == r, relb_ref[h, r], acc)

    o_ref[...] = lax.fori_loop(0, 2 * REL_CLIP + 1, body, jnp.zeros((CHUNK, BAND_KEYS), F32))


def bias_strip(relb):
    nh = relb.shape[0]
    return pl.pallas_call(
        _bias_kernel,
        grid=(nh,),
        in_specs=[pl.BlockSpec(memory_space=pltpu.SMEM)],
        out_specs=pl.BlockSpec((None, CHUNK, BAND_KEYS), lambda h: (h, 0, 0)),
        out_shape=jax.ShapeDtypeStruct((nh, CHUNK, BAND_KEYS), F32),
        compiler_params=_params(1),
        name="bias_strip",
    )(relb)


def _band_chunk(q, kwin, vwin, strip, min_col):
    s = _dot_nt(q, kwin) * (HEAD_DIM ** -0.5) + strip
    col = lax.broadcasted_iota(I32, s.shape, 1)
    s = jnp.where(col >= min_col, s, NEG)
    m = jnp.max(s, axis=-1, keepdims=True)
    p = jnp.exp(s - m)
    l = jnp.sum(p, axis=-1, keepdims=True)
    return jnp.dot(p.astype(BF16), vwin, preferred_element_type=F32) / l


def _band_prompt_kernel(q_ref, kp_ref, kc_ref, vp_ref, vc_ref, strip_ref, o_ref, kw_ref, vw_ref):
    i = pl.program_id(2)
    tb = q_ref.shape[0]
    kw_ref[0:tb, :] = kp_ref[...].astype(BF16)
    kw_ref[tb:2 * tb, :] = kc_ref[...].astype(BF16)
    vw_ref[0:tb, :] = vp_ref[...].astype(BF16)
    vw_ref[tb:2 * tb, :] = vc_ref[...].astype(BF16)
    strip = strip_ref[...]
    for a in range(tb // CHUNK):
        q = q_ref[a * CHUNK:(a + 1) * CHUNK, :].astype(BF16)
        lo = a * CHUNK
        min_col = jnp.where(i > 0, 0, (BAND_PAST_CHUNKS - a) * CHUNK)
        o_ref[a * CHUNK:(a + 1) * CHUNK, :] = _band_chunk(
            q, kw_ref[lo:lo + BAND_KEYS, :], vw_ref[lo:lo + BAND_KEYS, :], strip, min_col)


def band_prompt(u, strip, cols, nseq, t):
    rows = u.shape[0]
    tb = BAND_PAST
    nblk = t // tb
    nh = strip.shape[0]
    qo, ko, vo = cols["qb"] // HEAD_DIM, cols["kb"] // HEAD_DIM, cols["vb"] // HEAD_DIM
    cur = lambda off: pl.BlockSpec((tb, HEAD_DIM), lambda b, h, i: (b * nblk + i, off + h))
    prev = lambda off: pl.BlockSpec((tb, HEAD_DIM), lambda b, h, i: (b * nblk + jnp.maximum(i - 1, 0), off + h))
    return pl.pallas_call(
        _band_prompt_kernel,
        grid=(nseq, nh, nblk),
        in_specs=[cur(qo), prev(ko), cur(ko), prev(vo), cur(vo),
                  pl.BlockSpec((None, CHUNK, BAND_KEYS), lambda b, h, i: (h, 0, 0))],
        out_specs=pl.BlockSpec((tb, HEAD_DIM), lambda b, h, i: (b * nblk + i, h)),
        out_shape=jax.ShapeDtypeStruct((rows, nh * HEAD_DIM), F32),
        scratch_shapes=[pltpu.VMEM((2 * tb, HEAD_DIM), BF16), pltpu.VMEM((2 * tb, HEAD_DIM), BF16)],
        compiler_params=_params(3),
        name="band_prompt",
    )(u, u, u, u, u, strip)


def _band_sample_kernel(q_ref, kn_ref, vn_ref, ck_ref, cv_ref, strip_ref, o_ref, kw_ref, vw_ref):
    w = ck_ref.shape[0]
    t = q_ref.shape[0]
    kw_ref[0:w, :] = ck_ref[...].astype(BF16)
    kw_ref[w:w + t, :] = kn_ref[...].astype(BF16)
    vw_ref[0:w, :] = cv_ref[...].astype(BF16)
    vw_ref[w:w + t, :] = vn_ref[...].astype(BF16)
    for h in range(strip_ref.shape[0]):
        sl = slice(h * HEAD_DIM, (h + 1) * HEAD_DIM)
        o_ref[:, sl] = _band_chunk(q_ref[:, sl].astype(BF16), kw_ref[:, sl], vw_ref[:, sl], strip_ref[h], 0)


def band_sample(u, ck, cv, strip, cols, nseq, t):
    rows = u.shape[0]
    nh = strip.shape[0]
    d = nh * HEAD_DIM
    w = ck.shape[1]
    ublk = lambda off: pl.BlockSpec((t, d), lambda b, o=off // d: (b, o))
    cblk = pl.BlockSpec((None, w, d), lambda b: (b, 0, 0))
    return pl.pallas_call(
        _band_sample_kernel,
        grid=(nseq,),
        in_specs=[ublk(cols["qb"]), ublk(cols["kb"]), ublk(cols["vb"]), cblk, cblk,
                  pl.BlockSpec((nh, CHUNK, BAND_KEYS), lambda b: (0, 0, 0))],
        out_specs=pl.BlockSpec((t, d), lambda b: (b, 0)),
        out_shape=jax.ShapeDtypeStruct((rows, d), F32),
        scratch_shapes=[pltpu.VMEM((w + t, d), BF16), pltpu.VMEM((w + t, d), BF16)],
        compiler_params=_params(1),
        name="band_sample",
    )(u, u, u, ck, cv, strip)


def _loop(n, body, init):
    if isinstance(n, int):
        val = init
        for k in range(n):
            val = body(k, val)
        return val
    return lax.fori_loop(0, n, body, init)


def _dsa_core(qi_ref, t_ref, q_ref, segs, o_ref, sk_refs, jstar_ref, m_ref, l_ref, acc_ref, *,
              topk, q_chunk0):
    qb = q_ref.shape[0]
    n_dsa = q_ref.shape[1] // HEAD_DIM
    wq = t_ref[:, IDX_DIM:IDX_DIM + N_IDX_HEADS]
    wcols = [wq[:, h:h + 1] for h in range(N_IDX_HEADS)]

    def lanes_of(kblk):
        return min(kblk, LANES)

    for seg, sk_ref in zip(segs, sk_refs):
        kblk = seg["kblk"]
        qchunk = q_chunk0 + lax.broadcasted_iota(I32, (qb, kblk), 0) // CHUNK
        col = lax.broadcasted_iota(I32, (qb, kblk), 1)

        def score_blk(kb, carry, seg=seg, sk_ref=sk_ref, kblk=kblk, qchunk=qchunk, col=col):
            start = kb * kblk if isinstance(kb, int) else pl.multiple_of(kb * kblk, kblk)
            kib = seg["ki"][pl.ds(start, kblk), :].astype(BF16)
            acc = jnp.zeros((qb, kblk), F32)
            for h in range(N_IDX_HEADS):
                sc = _dot_nt(qi_ref[:, h * LANES:(h + 1) * LANES], kib)
                acc = acc + jnp.maximum(sc, 0.0) * wcols[h]
            kchunk = (seg["pos0"] + kb * kblk + col) // CHUNK
            acc = jnp.where(kchunk <= qchunk, acc, NEG)
            bits = lax.bitcast_convert_type(acc, I32)
            sk_ref[kb] = jnp.where(bits < 0, bits ^ 0x7FFFFFFF, bits)
            return carry

        _loop(seg["nblk"], score_blk, 0)

    def count(pred):
        total = jnp.zeros((qb, 1), I32)
        for seg, sk_ref in zip(segs, sk_refs):
            kblk = seg["kblk"]
            lw = lanes_of(kblk)
            lane = lax.broadcasted_iota(I32, (qb, lw), 1)

            def body(kb, part, seg=seg, sk_ref=sk_ref, kblk=kblk, lw=lw, lane=lane):
                blk = sk_ref[kb]
                for j in range(kblk // lw):
                    idx = seg["pos0"] + kb * kblk + j * lw + lane
                    part = part + jnp.where(pred(blk[:, j * lw:(j + 1) * lw], idx, lw), 1, 0)
                return part

            part = _loop(seg["nblk"], body, jnp.zeros((qb, lw), I32))
            total = total + jnp.sum(part, axis=1, keepdims=True)
        return jnp.broadcast_to(total, (qb, LANES))

    def fit(x, lw):
        return x if lw == LANES else x[:, :lw]

    def count_ge(t):
        return count(lambda key, idx, lw: key >= fit(t, lw))

    c0 = count_ge(jnp.zeros((qb, LANES), I32))
    thr = jnp.where(c0 >= topk, 0, INT_MIN).astype(I32)

    def bit_body(it, t):
        cand = t + jnp.left_shift(jnp.int32(1), 30 - it)
        return jnp.where(count_ge(cand) >= topk, cand, t)

    thr = lax.fori_loop(0, 31, bit_body, thr)
    n_gt = count_ge(thr + 1)
    n_ge = count_ge(thr)
    need = topk - n_gt

    max_idx_bits = 16
    jstar_ref[...] = jnp.full((qb, LANES), 2 ** max_idx_bits, I32)
    surplus = jnp.where((n_ge > topk) & (thr > KEY_HALF_NEG), 1, 0)

    @pl.when(jnp.max(surplus) > 0)
    def _():
        def idx_body(it, j):
            cand = j + jnp.left_shift(jnp.int32(1), max_idx_bits - 1 - it)
            f = count(lambda key, idx, lw: (key == fit(thr, lw)) & (idx < fit(cand, lw)))
            return jnp.where(f <= need, cand, j)

        jstar_ref[...] = lax.fori_loop(0, max_idx_bits, idx_body, jnp.zeros((qb, LANES), I32))

    jstar = jstar_ref[...]

    m_ref[...] = jnp.full(m_ref.shape, NEG, F32)
    l_ref[...] = jnp.zeros(l_ref.shape, F32)
    acc_ref[...] = jnp.zeros(acc_ref.shape, F32)
    thr1 = thr[:, 0:1]
    jstar1 = jstar[:, 0:1]
    for seg, sk_ref in zip(segs, sk_refs):
        kblk = seg["kblk"]
        col = lax.broadcasted_iota(I32, (qb, kblk), 1)

        def att_blk(kb, carry, seg=seg, sk_ref=sk_ref, kblk=kblk, col=col):
            start = kb * kblk if isinstance(kb, int) else pl.multiple_of(kb * kblk, kblk)
            key = sk_ref[kb]
            idx = seg["pos0"] + kb * kblk + col
            mask = (key > KEY_HALF_NEG) & ((key > thr1) | ((key == thr1) & (idx < jstar1)))
            for h in range(n_dsa):
                sl = slice(h * HEAD_DIM, (h + 1) * HEAD_DIM)
                kh = seg["k"][pl.ds(start, kblk), sl].astype(BF16)
                vh = seg["v"][pl.ds(start, kblk), sl].astype(BF16)
                s = _dot_nt(q_ref[:, sl], kh) * (HEAD_DIM ** -0.5)
                s = jnp.where(mask, s, NEG)
                m_old = m_ref[h]
                m_new = jnp.maximum(m_old, jnp.max(s, axis=-1, keepdims=True))
                alpha = jnp.exp(m_old - m_new)
                p = jnp.exp(s - m_new)
                l_ref[h] = alpha * l_ref[h] + jnp.sum(p, axis=-1, keepdims=True)
                acc_ref[h] = alpha * acc_ref[h] + jnp.dot(p.astype(BF16), vh, preferred_element_type=F32)
                m_ref[h] = m_new
            return carry

        _loop(seg["nblk"], att_blk, 0)

    for h in range(n_dsa):
        o_ref[:, h * HEAD_DIM:(h + 1) * HEAD_DIM] = acc_ref[h] / l_ref[h]


def _dsa_scratch(qb, seg_shapes, n_dsa):
    return ([pltpu.VMEM((nblk, qb, kblk), I32) for nblk, kblk in seg_shapes]
            + [pltpu.VMEM((qb, LANES), I32),
               pltpu.VMEM((n_dsa, qb, 1), F32), pltpu.VMEM((n_dsa, qb, 1), F32),
               pltpu.VMEM((n_dsa, qb, HEAD_DIM), F32)])


def _dsa_prompt_kernel(qi_ref, t_ref, q_ref, ki_ref, k_ref, v_ref, o_ref, sk_ref, jstar_ref, m_ref, l_ref,
                       acc_ref, *, kblk, topk):
    i = pl.program_id(1)
    qb = q_ref.shape[0]
    nblk = (i * qb + qb + kblk - 1) // kblk
    seg = dict(ki=ki_ref, k=k_ref, v=v_ref, nblk=nblk, kblk=kblk, pos0=0)
    _dsa_core(qi_ref, t_ref, q_ref, [seg], o_ref, [sk_ref], jstar_ref, m_ref, l_ref, acc_ref,
              topk=topk, q_chunk0=(i * qb) // CHUNK)


def dsa_prompt(qi_pad, trot, q_bf, ki_bf, k_bf, v_bf, nseq, t, qb, kblk, topk):
    rows, d = q_bf.shape
    n_dsa = d // HEAD_DIM
    nq = t // qb
    row_blk = lambda w: pl.BlockSpec((qb, w), lambda b, i: (b * nq + i, 0))
    seq_blk = lambda w: pl.BlockSpec((t, w), lambda b, i: (b, 0))
    return pl.pallas_call(
        functools.partial(_dsa_prompt_kernel, kblk=kblk, topk=topk),
        grid=(nseq, nq),
        in_specs=[row_blk(qi_pad.shape[1]), row_blk(LANES), row_blk(d),
                  seq_blk(LANES), seq_blk(d), seq_blk(d)],
        out_specs=row_blk(d),
        out_shape=jax.ShapeDtypeStruct((rows, d), F32),
        scratch_shapes=_dsa_scratch(qb, [(t // kblk, kblk)], n_dsa),
        compiler_params=_params(2),
        name="dsa_prompt",
    )(qi_pad, trot, q_bf, ki_bf, k_bf, v_bf)


def _dsa_sample_kernel(qi_ref, t_ref, q_ref, kin_ref, kn_ref, vn_ref, cki_ref, ck_ref, cv_ref, o_ref,
                       sk0_ref, sk1_ref, jstar_ref, m_ref, l_ref, acc_ref, *, kblk, topk):
    past = ck_ref.shape[0]
    tq = q_ref.shape[0]
    segs = [dict(ki=cki_ref, k=ck_ref, v=cv_ref, nblk=past // kblk, kblk=kblk, pos0=0),
            dict(ki=kin_ref, k=kn_ref, v=vn_ref, nblk=1, kblk=tq, pos0=past)]
    _dsa_core(qi_ref, t_ref, q_ref, segs, o_ref, [sk0_ref, sk1_ref], jstar_ref, m_ref, l_ref, acc_ref,
              topk=topk, q_chunk0=past // CHUNK)


def dsa_sample(qi_pad, trot, q_bf, ki_bf, k_bf, v_bf, cki, ck, cv, nseq, t, kblk, topk):
    rows, d = q_bf.shape
    n_dsa = d // HEAD_DIM
    past = ck.shape[1]
    row_blk = lambda w: pl.BlockSpec((t, w), lambda b: (b, 0))
    cache_blk = lambda w: pl.BlockSpec((None, past, w), lambda b: (b, 0, 0))
    return pl.pallas_call(
        functools.partial(_dsa_sample_kernel, kblk=kblk, topk=topk),
        grid=(nseq,),
        in_specs=[row_blk(qi_pad.shape[1]), row_blk(LANES), row_blk(d),
                  row_blk(LANES), row_blk(d), row_blk(d),
                  cache_blk(LANES), cache_blk(d), cache_blk(d)],
        out_specs=row_blk(d),
        out_shape=jax.ShapeDtypeStruct((rows, d), F32),
        scratch_shapes=_dsa_scratch(t, [(past // kblk, kblk), (1, t)], n_dsa),
        compiler_params=_params(1),
        name="dsa_sample",
    )(qi_pad, trot, q_bf, ki_bf, k_bf, v_bf, cki, ck, cv)


def _outproj_kernel(oa_ref, ob_ref, oc_ref, za_ref, zb_ref, zc_ref, g_ref, w_ref, x_ref, mod_ref, xo_ref,
                    y_ref, *, d_lru, d_band):
    def branch(o_ref, z_ref, lo, hi):
        o = o_ref[...]
        z = z_ref[...]
        ms = jnp.mean(o * o, axis=-1, keepdims=True)
        y = o * lax.rsqrt(ms + EPS) * g_ref[:, lo:hi]
        y_ref[:, lo:hi] = (y * (z * jax.nn.sigmoid(z))).astype(BF16)

    d_mix = y_ref.shape[1]
    branch(oa_ref, za_ref, 0, d_lru)
    branch(ob_ref, zb_ref, d_lru, d_lru + d_band)
    branch(oc_ref, zc_ref, d_lru + d_band, d_mix)
    out = jnp.dot(y_ref[...], w_ref[...], preferred_element_type=F32)
    xo_ref[...] = x_ref[...] + mod_ref[2:3, :] * out


def outproj(oa, ob, oc, u, g_branch, w_out, layer, x, mod3, cols, nseq, t, tm):
    rows, d = x.shape
    d_lru, d_band, d_dsa = oa.shape[1], ob.shape[1], oc.shape[1]
    d_mix = d_lru + d_band + d_dsa
    nblk = t // tm
    rb = lambda w, o=0: pl.BlockSpec((tm, w), lambda b, i, o=o: (b * nblk + i, o))
    return pl.pallas_call(
        functools.partial(_outproj_kernel, d_lru=d_lru, d_band=d_band),
        grid=(nseq, nblk),
        in_specs=[rb(d_lru), rb(d_band), rb(d_dsa),
                  rb(d_lru, cols["za"] // d_lru), rb(d_band, cols["zb"] // d_band), rb(d_dsa, cols["zc"] // d_dsa),
                  pl.BlockSpec((1, d_mix), lambda b, i: (0, 0)),
                  pl.BlockSpec((None, d_mix, d), lambda b, i: (layer, 0, 0)),
                  rb(d),
                  pl.BlockSpec((None, 3, d), lambda b, i: (b, 0, 0))],
        out_specs=rb(d),
        out_shape=jax.ShapeDtypeStruct((rows, d), F32),
        scratch_shapes=[pltpu.VMEM((tm, d_mix), BF16)],
        compiler_params=_params(2),
        name="outproj",
    )(oa, ob, oc, u, u, u, g_branch, w_out, x, mod3)


def _rmsnorm_kernel(x_ref, g_ref, o_ref):
    x = x_ref[...]
    ms = jnp.mean(x * x, axis=-1, keepdims=True)
    o_ref[...] = x * lax.rsqrt(ms + EPS) * g_ref[...]


def final_norm(x, g, tm):
    rows, d = x.shape
    return pl.pallas_call(
        _rmsnorm_kernel,
        grid=(rows // tm,),
        in_specs=[pl.BlockSpec((tm, d), lambda i: (i, 0)), pl.BlockSpec((1, d), lambda i: (0, 0))],
        out_specs=pl.BlockSpec((tm, d), lambda i: (i, 0)),
        out_shape=jax.ShapeDtypeStruct((rows, d), F32),
        compiler_params=_params(1),
        name="final_norm",
    )(x, g)


def _column_offsets(d_lru, d_band, d_dsa):
    names = ["xa", "za", "qb", "kb", "vb", "zb", "qc", "kc", "vc", "zc", "qi"]
    widths = [d_lru, d_lru, d_band, d_band, d_band, d_band, d_dsa, d_dsa, d_dsa, d_dsa, N_IDX_HEADS * IDX_DIM]
    cols, off = {}, 0
    for n, w in zip(names, widths):
        cols[n] = off
        off += w
    cols["main"] = off
    cols["d_lru"], cols["d_band"], cols["d_dsa"] = d_lru, d_band, d_dsa
    return cols


def _layer_stream(x, mod3, lw, layer, cols, nseq, t, tables, state, strip, prompt):
    d_lru, d_band, d_dsa = cols["d_lru"], cols["d_band"], cols["d_dsa"]
    n_band, n_dsa = d_band // HEAD_DIM, d_dsa // HEAD_DIM
    rows = nseq * t
    tb = 256 if t % 256 == 0 else t
    h = normmod(x, lw["g_norm"], mod3, nseq, t, tb)
    tm = 512 if rows % 512 == 0 else rows
    u = matmul(h, lw["w_main"], layer, cols["main"], tm, 1024)
    tail = matmul(h, lw["w_tail"], layer, LANES, tm, LANES)

    q_bf, k_rot, k_bf, v_bf, qi_pad, t_rot, ki_bf = rope_call(u, tail, tables, cols, tb)

    conv_s, lru_s = state[0], state[1]
    cs8 = jnp.concatenate([jnp.zeros((nseq, 8 - (CONV_W - 1), d_lru), F32), conv_s], axis=1)
    hs = lru_call(u, cs8, lru_s.reshape(nseq, 1, d_lru), lw["conv_w"], lw["conv_b"], lw["w_rg"], lw["b_rg"],
                  lw["w_ig"], lw["b_ig"], lw["lam"], nseq, t, tb)

    if prompt:
        ob = band_prompt(u, strip, cols, nseq, t)
        oc = dsa_prompt(qi_pad, t_rot, q_bf, ki_bf, k_bf, v_bf, nseq, t, 128, 512, min(TOPK_MAX, t // 4))
    else:
        bk, bv, dk, dv, dik = state[2:]
        past = dk.shape[1]
        assert (past + t - 1) // CHUNK <= past // CHUNK and t == CHUNK and bk.shape[1] == BAND_PAST
        ob = band_sample(u, bk.reshape(nseq, BAND_PAST, d_band), bv.reshape(nseq, BAND_PAST, d_band),
                         strip, cols, nseq, t)
        dik_pad = jnp.pad(dik, ((0, 0), (0, 0), (0, LANES - IDX_DIM)))
        oc = dsa_sample(qi_pad, t_rot, q_bf, ki_bf, k_bf, v_bf, dik_pad,
                        dk.reshape(nseq, past, d_dsa), dv.reshape(nseq, past, d_dsa),
                        nseq, t, 512, min(TOPK_MAX, (past + t) // 4))

    x_new = outproj(hs, ob, oc, u, lw["g_branch"], lw["w_out"], layer, x, mod3, cols, nseq, t,
                    256 if t % 256 == 0 else t)

    u3 = u.reshape(nseq, t, -1)
    xa = u3[:, :, cols["xa"]:cols["xa"] + d_lru]
    if prompt:
        new_conv = xa[:, t - (CONV_W - 1):]
    else:
        new_conv = jnp.concatenate([conv_s, xa], axis=1)[:, -(CONV_W - 1):]
    nbr = min(BAND_PAST, t)
    new = (new_conv,
           hs.reshape(nseq, t, d_lru)[:, -1],
           u3[:, t - nbr:, cols["kb"]:cols["kb"] + d_band].reshape(nseq, nbr, n_band, HEAD_DIM),
           u3[:, t - nbr:, cols["vb"]:cols["vb"] + d_band].reshape(nseq, nbr, n_band, HEAD_DIM),
           k_rot.reshape(nseq, t, n_dsa, HEAD_DIM),
           u3[:, :, cols["vc"]:cols["vc"] + d_dsa].reshape(nseq, t, n_dsa, HEAD_DIM),
           t_rot.reshape(nseq, t, LANES)[:, :, :IDX_DIM])
    return x_new, new


def kernel(x_prompt, x_sample, c_prompt, c_sample, state_conv, state_lru, cache_band_k, cache_band_v,
           cache_dsa_k, cache_dsa_v, cache_dsa_idx_k, g_norm, w_ada, b_ada, w_in, conv_w, conv_b,
           w_rg, b_rg, w_ig, b_ig, lru_lambda, rel_bias, g_branch, w_out, g_final):
    depth = w_in.shape[0]
    nb_p, t_p, d = x_prompt.shape
    nb_s, t_s, _ = x_sample.shape
    past = cache_dsa_k.shape[2]
    d_lru = conv_w.shape[2]
    d_band = cache_band_k.shape[3] * HEAD_DIM
    d_dsa = cache_dsa_k.shape[3] * HEAD_DIM
    cols = _column_offsets(d_lru, d_band, d_dsa)
    n_main = cols["main"]

    mod = ada_all(jnp.concatenate([c_prompt, c_sample], axis=0), w_ada, b_ada)
    mod = mod.reshape(depth, nb_p + nb_s, 3, d)

    w_in_bf = w_in.astype(BF16)
    w_tail = jnp.pad(w_in_bf[:, :, n_main:], ((0, 0), (0, 0), (0, LANES - (w_in.shape[2] - n_main))))
    w_out_bf = w_out.astype(BF16)
    w_rg_bf = w_rg.astype(BF16)
    w_ig_bf = w_ig.astype(BF16)

    tab_p = rope_tables(jnp.tile(jnp.arange(t_p), nb_p))
    tab_s = rope_tables(jnp.tile(past + jnp.arange(t_s), nb_s))

    xp = x_prompt.reshape(nb_p * t_p, d)
    xs = x_sample.reshape(nb_s * t_s, d)
    zero_state = (jnp.zeros((nb_p, CONV_W - 1, d_lru), F32), jnp.zeros((nb_p, d_lru), F32))
    p_new, s_new = [], []
    for l in range(depth):
        lw = dict(g_norm=g_norm[l][None], w_main=w_in_bf, w_tail=w_tail, conv_w=conv_w[l], conv_b=conv_b[l][None],
                  w_rg=w_rg_bf[l], b_rg=b_rg[l][None], w_ig=w_ig_bf[l], b_ig=b_ig[l][None],
                  lam=lru_lambda[l][None], g_branch=g_branch[l][None], w_out=w_out_bf)
        strip = bias_strip(rel_bias[l])
        xp, pn = _layer_stream(xp, mod[l, :nb_p], lw, l, cols, nb_p, t_p, tab_p, zero_state, strip, True)
        st = (state_conv[l], state_lru[l], cache_band_k[l], cache_band_v[l],
              cache_dsa_k[l], cache_dsa_v[l], cache_dsa_idx_k[l])
        xs, sn = _layer_stream(xs, mod[l, nb_p:], lw, l, cols, nb_s, t_s, tab_s, st, strip, False)
        p_new.append(pn)
        s_new.append(sn)

    y_prompt = final_norm(xp, g_final[None], 256).reshape(nb_p, t_p, d)
    y_sample = final_norm(xs, g_final[None], 256).reshape(nb_s, t_s, d)
    p_out = [jnp.stack([t[j] for t in p_new], axis=0) for j in range(7)]
    s_out = [jnp.stack([t[j] for t in s_new], axis=0) for j in range(7)]
    return (y_prompt, y_sample, *p_out, *s_out)
```

```python
import functools
import struct

import jax
import jax.numpy as jnp
from jax import lax
from jax.experimental import pallas as pl
from jax.experimental.pallas import tpu as pltpu

F32 = jnp.float32
BF16 = jnp.bfloat16
I32 = jnp.int32

CHUNK = 64
HEAD_DIM = 128
LRU_BLOCKS = 8
CONV_W = 4
LRU_C = 8.0
BAND_PAST_CHUNKS = 8
BAND_PAST = BAND_PAST_CHUNKS * CHUNK
BAND_KEYS = BAND_PAST + CHUNK
REL_CLIP = 256
N_IDX_HEADS = 16
IDX_DIM = 64
TOPK_MAX = 256
ROPE_THETA = 500000.0
ROPE_FRAC = 4
EPS = 1e-6
NEG = -1e30
LANES = 128

INT_MIN = -(2 ** 31)
INT_MAX = 2 ** 31 - 1


def _sortable_key_of(x):
    b = struct.unpack("<i", struct.pack("<f", x))[0]
    return b ^ 0x7FFFFFFF if b < 0 else b


KEY_HALF_NEG = _sortable_key_of(NEG * 0.5)

VMEM_LIMIT = 48 * 1024 * 1024
NT_DIMS = (((1,), (1,)), ((), ()))


def _params(n_grid):
    return pltpu.CompilerParams(dimension_semantics=("arbitrary",) * n_grid,
                                vmem_limit_bytes=VMEM_LIMIT)


def _dot_nt(a, b):
    return lax.dot_general(a, b, NT_DIMS, preferred_element_type=F32)


def _ada_kernel(c_ref, w_ref, b_ref, o_ref):
    c = c_ref[...]
    s = (c * jax.nn.sigmoid(c)).astype(BF16)
    o_ref[...] = jnp.dot(s, w_ref[...].astype(BF16), preferred_element_type=F32) + b_ref[...]


def ada_all(c_all, w_ada, b_ada):
    depth, d, n = w_ada.shape
    nb = c_all.shape[0]
    tn = 512
    return pl.pallas_call(
        _ada_kernel,
        grid=(depth, n // tn),
        in_specs=[pl.BlockSpec((nb, d), lambda l, j: (0, 0)),
                  pl.BlockSpec((None, d, tn), lambda l, j: (l, 0, j)),
                  pl.BlockSpec((None, 1, tn), lambda l, j: (l, 0, j))],
        out_specs=pl.BlockSpec((None, nb, tn), lambda l, j: (l, 0, j)),
        out_shape=jax.ShapeDtypeStruct((depth, nb, n), F32),
        compiler_params=_params(2),
        name="ada",
    )(c_all, w_ada, b_ada.reshape(depth, 1, n))


def _normmod_kernel(x_ref, g_ref, mod_ref, h_ref):
    x = x_ref[...]
    ms = jnp.mean(x * x, axis=-1, keepdims=True)
    y = x * lax.rsqrt(ms + EPS) * g_ref[...]
    shift = mod_ref[0:1, :]
    scale = mod_ref[1:2, :]
    h_ref[...] = (y * (1.0 + scale) + shift).astype(BF16)


def normmod(x, g, mod3, nseq, t, tb):
    rows, d = x.shape
    nblk = t // tb
    return pl.pallas_call(
        _normmod_kernel,
        grid=(nseq, nblk),
        in_specs=[pl.BlockSpec((tb, d), lambda b, i: (b * nblk + i, 0)),
                  pl.BlockSpec((1, d), lambda b, i: (0, 0)),
                  pl.BlockSpec((None, 3, d), lambda b, i: (b, 0, 0))],
        out_specs=pl.BlockSpec((tb, d), lambda b, i: (b * nblk + i, 0)),
        out_shape=jax.ShapeDtypeStruct((rows, d), BF16),
        compiler_params=_params(2),
        name="normmod",
    )(x, g, mod3)


def _mm_kernel(a_ref, b_ref, o_ref, *scratch):
    if scratch:
        wb_ref, = scratch

        @pl.when(pl.program_id(1) == 0)
        def _():
            wb_ref[...] = b_ref[...].astype(BF16)

        w = wb_ref[...]
    else:
        w = b_ref[...]
    o_ref[...] = jnp.dot(a_ref[...], w, preferred_element_type=F32)


def matmul(a, w, layer, ncols, tm, tn):
    m, k = a.shape
    scratch = [] if w.dtype == BF16 else [pltpu.VMEM((k, tn), BF16)]
    return pl.pallas_call(
        _mm_kernel,
        grid=(ncols // tn, m // tm),
        in_specs=[pl.BlockSpec((tm, k), lambda j, i: (i, 0)),
                  pl.BlockSpec((None, k, tn), lambda j, i: (layer, 0, j))],
        out_specs=pl.BlockSpec((tm, tn), lambda j, i: (i, j)),
        out_shape=jax.ShapeDtypeStruct((m, ncols), F32),
        scratch_shapes=scratch,
        compiler_params=_params(2),
        name="inproj",
    )(a, w)


def _rope_kernel(qc_ref, kc_ref, vc_ref, qi_ref, t_ref, c128_ref, s128_ref, c64_ref, s64_ref,
                 q_out, krot_out, kbf_out, vbf_out, qi_out, trot_out, kibf_out, *, n_dsa, n_idx):
    tm = qc_ref.shape[0]
    lane = lax.broadcasted_iota(I32, (tm, LANES), 1)
    c128 = c128_ref[...]
    s128 = s128_ref[...]
    c64 = c64_ref[...]
    s64 = s64_ref[...]
    half128 = HEAD_DIM // ROPE_FRAC // 2
    half64 = IDX_DIM // ROPE_FRAC // 2

    def rope128(x):
        partner = jnp.where(lane < half128, pltpu.roll(x, LANES - half128, 1), pltpu.roll(x, half128, 1))
        return x * c128 + partner * s128

    def rope64(x):
        partner = jnp.where((lane & (IDX_DIM - 1)) < half64,
                            pltpu.roll(x, LANES - half64, 1), pltpu.roll(x, half64, 1))
        return x * c64 + partner * s64

    for h in range(n_dsa):
        sl = slice(h * LANES, (h + 1) * LANES)
        q_out[:, sl] = rope128(qc_ref[:, sl]).astype(BF16)
        kr = rope128(kc_ref[:, sl])
        krot_out[:, sl] = kr
        kbf_out[:, sl] = kr.astype(BF16)
        vbf_out[:, sl] = vc_ref[:, sl].astype(BF16)
    low = lane < IDX_DIM
    for j in range(n_idx // 2):
        r = rope64(qi_ref[:, j * LANES:(j + 1) * LANES]) * (IDX_DIM ** -0.5)
        qi_out[:, (2 * j) * LANES:(2 * j + 1) * LANES] = jnp.where(low, r, 0.0).astype(BF16)
        qi_out[:, (2 * j + 1) * LANES:(2 * j + 2) * LANES] = jnp.where(
            low, pltpu.roll(r, IDX_DIM, 1), 0.0).astype(BF16)
    t = t_ref[...]
    r = rope64(t)
    trot_out[...] = jnp.where(low, r, jnp.where(lane < IDX_DIM + N_IDX_HEADS, t * (N_IDX_HEADS ** -0.5), 0.0))
    kibf_out[...] = jnp.where(low, r, 0.0).astype(BF16)


def rope_call(u, tail, tables, cols, tm):
    rows = u.shape[0]
    d_dsa = cols["d_dsa"]
    n_dsa = d_dsa // HEAD_DIM
    d_qi = N_IDX_HEADS * IDX_DIM
    c128, s128, c64, s64 = tables
    tab_spec = pl.BlockSpec((tm, LANES), lambda i: (i, 0))
    blk = lambda off: pl.BlockSpec((tm, d_dsa), lambda i, o=off // d_dsa: (i, o))
    return pl.pallas_call(
        functools.partial(_rope_kernel, n_dsa=n_dsa, n_idx=N_IDX_HEADS),
        grid=(rows // tm,),
        in_specs=[blk(cols["qc"]), blk(cols["kc"]), blk(cols["vc"]),
                  pl.BlockSpec((tm, d_qi), lambda i, o=cols["qi"] // d_qi: (i, o)),
                  tab_spec, tab_spec, tab_spec, tab_spec, tab_spec],
        out_specs=[pl.BlockSpec((tm, d_dsa), lambda i: (i, 0)),
                   pl.BlockSpec((tm, d_dsa), lambda i: (i, 0)),
                   pl.BlockSpec((tm, d_dsa), lambda i: (i, 0)),
                   pl.BlockSpec((tm, d_dsa), lambda i: (i, 0)),
                   pl.BlockSpec((tm, N_IDX_HEADS * LANES), lambda i: (i, 0)),
                   tab_spec, tab_spec],
        out_shape=[jax.ShapeDtypeStruct((rows, d_dsa), BF16),
                   jax.ShapeDtypeStruct((rows, d_dsa), F32),
                   jax.ShapeDtypeStruct((rows, d_dsa), BF16),
                   jax.ShapeDtypeStruct((rows, d_dsa), BF16),
                   jax.ShapeDtypeStruct((rows, N_IDX_HEADS * LANES), BF16),
                   jax.ShapeDtypeStruct((rows, LANES), F32),
                   jax.ShapeDtypeStruct((rows, LANES), BF16)],
        compiler_params=_params(1),
        name="rope",
    )(u, u, u, u, tail, c128, s128, c64, s64)


def rope_tables(pos):
    pos = pos.astype(F32)[:, None]
    n = pos.shape[0]

    def tab(dim):
        half = dim // ROPE_FRAC // 2
        inv = ROPE_THETA ** (-jnp.arange(half, dtype=F32) / half)
        ang = pos * inv[None]
        cos, sin = jnp.cos(ang), jnp.sin(ang)
        c = jnp.concatenate([cos, cos, jnp.ones((n, dim - 2 * half), F32)], axis=1)
        s = jnp.concatenate([-sin, sin, jnp.zeros((n, dim - 2 * half), F32)], axis=1)
        return jnp.tile(c, (1, LANES // dim)), jnp.tile(s, (1, LANES // dim))

    c128, s128 = tab(HEAD_DIM)
    c64, s64 = tab(IDX_DIM)
    return c128, s128, c64, s64


def _lru_kernel(xa_ref, cs_ref, h0_ref, cw_ref, cb_ref, wrg_ref, brg_ref, wig_ref, big_ref, lam_ref,
                hs_ref, ext_ref, hc_ref):
    tb, d = xa_ref.shape
    blk = d // LRU_BLOCKS

    @pl.when(pl.program_id(1) == 0)
    def _():
        ext_ref[0:8, :] = cs_ref[...]
        hc_ref[...] = h0_ref[...]

    ext_ref[8:8 + tb, :] = xa_ref[...]
    conv = ext_ref[5:5 + tb, :] * cw_ref[0:1, :]
    for j in range(1, CONV_W):
        conv = conv + ext_ref[5 + j:5 + j + tb, :] * cw_ref[j:j + 1, :]
    conv = conv + cb_ref[...]
    tail = ext_ref[tb:tb + 8, :]
    ext_ref[0:8, :] = tail

    xb = conv.astype(BF16)
    rs, gs = [], []
    for g in range(LRU_BLOCKS):
        xg = xb[:, g * blk:(g + 1) * blk]
        rs.append(jnp.dot(xg, wrg_ref[g], preferred_element_type=F32))
        gs.append(jnp.dot(xg, wig_ref[g], preferred_element_type=F32))
    r = jax.nn.sigmoid(jnp.concatenate(rs, axis=1) + brg_ref[...])
    ig = jax.nn.sigmoid(jnp.concatenate(gs, axis=1) + big_ref[...])
    lam = lam_ref[...]
    softplus_neg_lam = jnp.maximum(-lam, 0.0) + jnp.log1p(jnp.exp(-jnp.abs(lam)))
    log_a = (-LRU_C) * r * softplus_neg_lam
    a = jnp.exp(log_a)
    u = jnp.sqrt(-jnp.tanh(log_a) * (a * a + 1.0)) * (ig * conv)

    row = lax.broadcasted_iota(I32, (tb, d), 0)
    s = 1
    while s < tb:
        keep = row >= s
        a_prev = jnp.where(keep, pltpu.roll(a, s, 0), 1.0)
        u_prev = jnp.where(keep, pltpu.roll(u, s, 0), 0.0)
        u = a * u_prev + u
        a = a * a_prev
        s *= 2
    h = a * hc_ref[...] + u
    hs_ref[...] = h
    hc_ref[...] = h[tb - 1:tb, :]


def lru_call(u, cs8, h0, cw, cb, wrg, brg, wig, big, lam, nseq, t, tb):
    rows = u.shape[0]
    d = cw.shape[1]
    blk = d // LRU_BLOCKS
    nblk = t // tb
    vec = pl.BlockSpec((1, d), lambda b, i: (0, 0))
    wspec = pl.BlockSpec((LRU_BLOCKS, blk, blk), lambda b, i: (0, 0, 0))
    return pl.pallas_call(
        _lru_kernel,
        grid=(nseq, nblk),
        in_specs=[pl.BlockSpec((tb, d), lambda b, i: (b * nblk + i, 0)),
                  pl.BlockSpec((None, 8, d), lambda b, i: (b, 0, 0)),
                  pl.BlockSpec((None, 1, d), lambda b, i: (b, 0, 0)),
                  pl.BlockSpec((CONV_W, d), lambda b, i: (0, 0)),
                  vec, wspec, vec, wspec, vec, vec],
        out_specs=pl.BlockSpec((tb, d), lambda b, i: (b * nblk + i, 0)),
        out_shape=jax.ShapeDtypeStruct((rows, d), F32),
        scratch_shapes=[pltpu.VMEM((tb + 8, d), F32), pltpu.VMEM((1, d), F32)],
        compiler_params=_params(2),
        name="lru",
    )(u, cs8, h0, cw, cb, wrg, brg, wig, big, lam)


def _bias_kernel(relb_ref, o_ref):
    h = pl.program_id(0)
    qi = lax.broadcasted_iota(I32, (CHUNK, BAND_KEYS), 0)
    kj = lax.broadcasted_iota(I32, (CHUNK, BAND_KEYS), 1)
    idx = jnp.clip(qi + BAND_PAST - kj, -REL_CLIP, REL_CLIP) + REL_CLIP

    def body(r, acc):
        return jnp.where(idx == r, relb_ref[h, r], acc)

    o_ref[...] = lax.fori_loop(0, 2 * REL_CLIP + 1, body, jnp.zeros((CHUNK, BAND_KEYS), F32))


def bias_strip(relb):
    nh = relb.shape[0]
    return pl.pallas_call(
        _bias_kernel,
        grid=(nh,),
        in_specs=[pl.BlockSpec(memory_space=pltpu.SMEM)],
        out_specs=pl.BlockSpec((None, CHUNK, BAND_KEYS), lambda h: (h, 0, 0)),
        out_shape=jax.ShapeDtypeStruct((nh, CHUNK, BAND_KEYS), F32),
        compiler_params=_params(1),
        name="bias_strip",
    )(relb)


def _band_chunk(q, kwin, vwin, strip, min_col):
    s = _dot_nt(q, kwin) * (HEAD_DIM ** -0.5) + strip
    col = lax.broadcasted_iota(I32, s.shape, 1)
    s = jnp.where(col >= min_col, s, NEG)
    m = jnp.max(s, axis=-1, keepdims=True)
    p = jnp.exp(s - m)
    l = jnp.sum(p, axis=-1, keepdims=True)
    return jnp.dot(p.astype(BF16), vwin, preferred_element_type=F32) / l


def _band_prompt_kernel(q_ref, kp_ref, kc_ref, vp_ref, vc_ref, strip_ref, o_ref, kw_ref, vw_ref):
    i = pl.program_id(2)
    tb = q_ref.shape[0]
    kw_ref[0:tb, :] = kp_ref[...].astype(BF16)
    kw_ref[tb:2 * tb, :] = kc_ref[...].astype(BF16)
    vw_ref[0:tb, :] = vp_ref[...].astype(BF16)
    vw_ref[tb:2 * tb, :] = vc_ref[...].astype(BF16)
    strip = strip_ref[...]
    for a in range(tb // CHUNK):
        q = q_ref[a * CHUNK:(a + 1) * CHUNK, :].astype(BF16)
        lo = a * CHUNK
        min_col = jnp.where(i > 0, 0, (BAND_PAST_CHUNKS - a) * CHUNK)
        o_ref[a * CHUNK:(a + 1) * CHUNK, :] = _band_chunk(
            q, kw_ref[lo:lo + BAND_KEYS, :], vw_ref[lo:lo + BAND_KEYS, :], strip, min_col)


def band_prompt(u, strip, cols, nseq, t):
    rows = u.shape[0]
    tb = BAND_PAST
    nblk = t // tb
    nh = strip.shape[0]
    qo, ko, vo = cols["qb"] // HEAD_DIM, cols["kb"] // HEAD_DIM, cols["vb"] // HEAD_DIM
    cur = lambda off: pl.BlockSpec((tb, HEAD_DIM), lambda b, h, i: (b * nblk + i, off + h))
    prev = lambda off: pl.BlockSpec((tb, HEAD_DIM), lambda b, h, i: (b * nblk + jnp.maximum(i - 1, 0), off + h))
    return pl.pallas_call(
        _band_prompt_kernel,
        grid=(nseq, nh, nblk),
        in_specs=[cur(qo), prev(ko), cur(ko), prev(vo), cur(vo),
                  pl.BlockSpec((None, CHUNK, BAND_KEYS), lambda b, h, i: (h, 0, 0))],
        out_specs=pl.BlockSpec((tb, HEAD_DIM), lambda b, h, i: (b * nblk + i, h)),
        out_shape=jax.ShapeDtypeStruct((rows, nh * HEAD_DIM), F32),
        scratch_shapes=[pltpu.VMEM((2 * tb, HEAD_DIM), BF16), pltpu.VMEM((2 * tb, HEAD_DIM), BF16)],
        compiler_params=_params(3),
        name="band_prompt",
    )(u, u, u, u, u, strip)


def _band_sample_kernel(q_ref, kn_ref, vn_ref, ck_ref, cv_ref, strip_ref, o_ref, kw_ref, vw_ref):
    nh = strip_ref.shape[0]
    w = ck_ref.shape[0] // nh
    t = q_ref.shape[0]
    kw_ref[w:w + t, :] = kn_ref[...].astype(BF16)
    vw_ref[w:w + t, :] = vn_ref[...].astype(BF16)
    for h in range(nh):
        sl = slice(h * HEAD_DIM, (h + 1) * HEAD_DIM)
        kw_ref[0:w, sl] = ck_ref[pl.ds(h, w, stride=nh), :].astype(BF16)
        vw_ref[0:w, sl] = cv_ref[pl.ds(h, w, stride=nh), :].astype(BF16)
        o_ref[:, sl] = _band_chunk(q_ref[:, sl].astype(BF16), kw_ref[:, sl], vw_ref[:, sl], strip_ref[h], 0)


def band_sample(u, ck, cv, layer, strip, cols, nseq, t):
    rows = u.shape[0]
    nh = strip.shape[0]
    d = nh * HEAD_DIM
    w = ck.shape[2] // nh
    ublk = lambda off: pl.BlockSpec((t, d), lambda b, o=off // d: (b, o))
    cblk = pl.BlockSpec((None, None, w * nh, HEAD_DIM), lambda b: (layer, b, 0, 0))
    return pl.pallas_call(
        _band_sample_kernel,
        grid=(nseq,),
        in_specs=[ublk(cols["qb"]), ublk(cols["kb"]), ublk(cols["vb"]), cblk, cblk,
                  pl.BlockSpec((nh, CHUNK, BAND_KEYS), lambda b: (0, 0, 0))],
        out_specs=pl.BlockSpec((t, d), lambda b: (b, 0)),
        out_shape=jax.ShapeDtypeStruct((rows, d), F32),
        scratch_shapes=[pltpu.VMEM((w + t, d), BF16), pltpu.VMEM((w + t, d), BF16)],
        compiler_params=_params(1),
        name="band_sample",
    )(u, u, u, ck, cv, strip)


def _loop(n, body, init):
    if isinstance(n, int):
        val = init
        for k in range(n):
            val = body(k, val)
        return val
    return lax.fori_loop(0, n, body, init)


INTERP_STEPS = 10
MAX_SEARCH_STEPS = INTERP_STEPS + 34
IDX_BITS = 16


def _key_to_f32(key):
    return lax.bitcast_convert_type(jnp.where(key < 0, key ^ 0x7FFFFFFF, key), F32)


def _f32_to_key(x):
    bits = lax.bitcast_convert_type(x, I32)
    return jnp.where(bits < 0, bits ^ 0x7FFFFFFF, bits)


def _dsa_core(qi_ref, t_ref, q_ref, segs, o_ref, sk_refs, s_refs, jstar_ref, mpart_ref, lpart_ref, acc_ref, *,
              topk, q_chunk0):
    qb = q_ref.shape[0]
    n_dsa = q_ref.shape[1] // HEAD_DIM
    wq = t_ref[:, IDX_DIM:IDX_DIM + N_IDX_HEADS]
    wcols = [wq[:, h:h + 1] for h in range(N_IDX_HEADS)]
    lane = lax.broadcasted_iota(I32, (qb, LANES), 1)

    def blk_start(kb, kblk):
        return kb * kblk if isinstance(kb, int) else pl.multiple_of(kb * kblk, kblk)

    for seg, sk_ref in zip(segs, sk_refs):
        kblk = seg["kblk"]
        kdim = seg["ki"].shape[0 if seg["ki_t"] else 1]
        qchunk = q_chunk0 + lax.broadcasted_iota(I32, (qb, kblk), 0) // CHUNK
        col = lax.broadcasted_iota(I32, (qb, kblk), 1)

        def score_blk(kb, carry, seg=seg, sk_ref=sk_ref, kblk=kblk, kdim=kdim, qchunk=qchunk, col=col):
            if seg["ki_t"]:
                kib = seg["ki"][:, pl.ds(blk_start(kb, kblk), kblk)].astype(BF16)
            else:
                kib = seg["ki"][pl.ds(blk_start(kb, kblk), kblk), :].astype(BF16)
            acc = jnp.zeros((qb, kblk), F32)
            for h in range(N_IDX_HEADS):
                qh = qi_ref[:, h * LANES:h * LANES + kdim]
                sc = jnp.dot(qh, kib, preferred_element_type=F32) if seg["ki_t"] else _dot_nt(qh, kib)
                acc = acc + jnp.maximum(sc, 0.0) * wcols[h]
            kchunk = (seg["pos0"] + kb * kblk + col) // CHUNK
            key = _f32_to_key(jnp.where(kchunk <= qchunk, acc, NEG))
            if kblk == sk_ref.shape[2]:
                sk_ref[kb] = key
            else:
                sk_ref[kb] = jnp.full(sk_ref.shape[1:], INT_MIN, I32)
                sk_ref[kb, :, 0:kblk] = key
            return carry

        _loop(seg["nblk"], score_blk, 0)

    def fold(tile_fn, init):
        parts = init
        for seg, sk_ref in zip(segs, sk_refs):
            def body(kb, parts, seg=seg, sk_ref=sk_ref):
                blk = sk_ref[kb]
                for j in range(sk_ref.shape[2] // LANES):
                    idx = seg["pos0"] + kb * seg["kblk"] + j * LANES + lane
                    parts = tile_fn(parts, blk[:, j * LANES:(j + 1) * LANES], idx)
                return parts

            parts = _loop(seg["nblk"], body, parts)
        return parts

    def rep(x):
        return jnp.broadcast_to(x, (qb, LANES))

    def count(pred):
        part = fold(lambda p, key, idx: p + jnp.where(pred(key, idx), 1, 0), jnp.zeros((qb, LANES), I32))
        return rep(jnp.sum(part, axis=1, keepdims=True))

    def stats_tile(parts, key, idx):
        cnt, mx, mn = parts
        real = key > KEY_HALF_NEG
        return (cnt + jnp.where(real, 1, 0), jnp.maximum(mx, key),
                jnp.minimum(mn, jnp.where(real, key, INT_MAX)))

    cnt, mx, mn = fold(stats_tile, (jnp.zeros((qb, LANES), I32), jnp.full((qb, LANES), INT_MIN, I32),
                                    jnp.full((qb, LANES), INT_MAX, I32)))
    n_real = rep(jnp.sum(cnt, axis=1, keepdims=True))
    kmax = rep(jnp.max(mx, axis=1, keepdims=True))
    kmin = rep(jnp.min(mn, axis=1, keepdims=True))

    few = n_real <= topk
    lo0 = jnp.where(few, KEY_HALF_NEG + 1, kmin)
    hi0 = jnp.where(few, lo0 + 1, kmax + 1)
    zero = jnp.zeros((qb, LANES), I32)

    def pending(lo, hi, c_lo):
        return (c_lo > topk) & (lo + 1 != hi)

    def search_cond(carry):
        it, flag = carry[0], carry[1]
        return (flag > 0) & (it < MAX_SEARCH_STEPS)

    def search_body(carry):
        it, _, lo, hi, c_lo, c_hi = carry
        todo = pending(lo, hi, c_lo)
        lo_f, hi_f = _key_to_f32(lo), _key_to_f32(hi)
        frac = ((c_lo - topk).astype(F32) + 0.5) / jnp.maximum(c_lo - c_hi, 1).astype(F32)
        interp = _f32_to_key(lo_f + (hi_f - lo_f) * frac)
        mid = (lo >> 1) + (hi >> 1) + (lo & hi & 1)
        cand = jnp.where((zero + it) < INTERP_STEPS, interp, mid)
        cand = jnp.where(todo, jnp.minimum(jnp.maximum(cand, lo + 1), hi - 1), lo)
        c = count(lambda key, idx: key >= cand)
        up = todo & (c >= topk)
        down = todo & (c < topk)
        lo, c_lo = jnp.where(up, cand, lo), jnp.where(up, c, c_lo)
        hi, c_hi = jnp.where(down, cand, hi), jnp.where(down, c, c_hi)
        flag = jnp.max(jnp.where(pending(lo, hi, c_lo), 1, 0))
        return it + 1, flag, lo, hi, c_lo, c_hi

    flag0 = jnp.max(jnp.where(pending(lo0, hi0, n_real), 1, 0))
    _, _, lo, hi, c_lo, c_hi = lax.while_loop(
        search_cond, search_body, (jnp.int32(0), flag0, lo0, hi0, n_real, zero))

    big = 2 ** IDX_BITS
    jstar_ref[...] = jnp.full((qb, LANES), big, I32)
    surplus = c_lo > topk
    need = topk - c_hi

    @pl.when(jnp.max(jnp.where(surplus, 1, 0)) > 0)
    def _():
        def idx_body(it, j):
            cand = j + jnp.left_shift(jnp.int32(1), IDX_BITS - 1 - it)
            f = count(lambda key, idx: (key == lo) & (idx < cand))
            return jnp.where(f <= need, cand, j)

        j = lax.fori_loop(0, IDX_BITS, idx_body, zero)
        jstar_ref[...] = jnp.where(surplus, j, big)

    lo1 = lo[:, 0:1]
    jstar1 = jstar_ref[:, 0:1]

    mpart_ref[...] = jnp.full(mpart_ref.shape, NEG, F32)
    lpart_ref[...] = jnp.zeros(lpart_ref.shape, F32)
    acc_ref[...] = jnp.zeros(acc_ref.shape, F32)

    def head_rows(ref, seg, kb, h):
        kblk = seg["kblk"]
        if seg["hstride"] is None:
            return ref[pl.ds(blk_start(kb, kblk), kblk), h * HEAD_DIM:(h + 1) * HEAD_DIM].astype(BF16)
        hs = seg["hstride"]
        return ref[pl.ds(kb * kblk * hs + h, kblk, stride=hs), :].astype(BF16)

    for seg, sk_ref, s_ref in zip(segs, sk_refs, s_refs):
        kblk = seg["kblk"]
        col = lax.broadcasted_iota(I32, (qb, kblk), 1)

        def pass_a(kb, carry, seg=seg, sk_ref=sk_ref, s_ref=s_ref, kblk=kblk, col=col):
            key = sk_ref[kb]
            if kblk != sk_ref.shape[2]:
                key = key[:, 0:kblk]
            idx = seg["pos0"] + kb * kblk + col
            mask = (key >= lo1) & ((key > lo1) | (idx < jstar1))
            for h in range(n_dsa):
                s = _dot_nt(q_ref[:, h * HEAD_DIM:(h + 1) * HEAD_DIM], head_rows(seg["k"], seg, kb, h))
                s = jnp.where(mask, s * (HEAD_DIM ** -0.5), NEG)
                s_ref[h, kb] = s
                mp = mpart_ref[h]
                if kblk % LANES == 0:
                    for j in range(kblk // LANES):
                        mp = jnp.maximum(mp, s[:, j * LANES:(j + 1) * LANES])
                else:
                    mp = jnp.maximum(mp, jnp.max(s, axis=1, keepdims=True))
                mpart_ref[h] = mp
            return carry

        _loop(seg["nblk"], pass_a, 0)

    m_rows = [jnp.max(mpart_ref[h], axis=1, keepdims=True) for h in range(n_dsa)]

    for seg, s_ref in zip(segs, s_refs):
        kblk = seg["kblk"]

        def pass_b(kb, carry, seg=seg, s_ref=s_ref, kblk=kblk):
            for h in range(n_dsa):
                p = jnp.exp(s_ref[h, kb] - m_rows[h])
                lp = lpart_ref[h]
                if kblk % LANES == 0:
                    for j in range(kblk // LANES):
                        lp = lp + p[:, j * LANES:(j + 1) * LANES]
                else:
                    lp = lp + jnp.where(lane == 0, jnp.sum(p, axis=1, keepdims=True), 0.0)
                lpart_ref[h] = lp
                acc_ref[h] += jnp.dot(p.astype(BF16), head_rows(seg["v"], seg, kb, h),
                                      preferred_element_type=F32)
            return carry

        _loop(seg["nblk"], pass_b, 0)

    for h in range(n_dsa):
        l = jnp.sum(lpart_ref[h], axis=1, keepdims=True)
        o_ref[:, h * HEAD_DIM:(h + 1) * HEAD_DIM] = acc_ref[h] / l


def _dsa_scratch(qb, seg_shapes, n_dsa):
    return ([pltpu.VMEM((nblk, qb, max(kblk, LANES)), I32) for nblk, kblk in seg_shapes]
            + [pltpu.VMEM((n_dsa, nblk, qb, kblk), F32) for nblk, kblk in seg_shapes]
            + [pltpu.VMEM((qb, LANES), I32),
               pltpu.VMEM((n_dsa, qb, LANES), F32), pltpu.VMEM((n_dsa, qb, LANES), F32),
               pltpu.VMEM((n_dsa, qb, HEAD_DIM), F32)])


def _dsa_prompt_kernel(qi_ref, t_ref, q_ref, ki_ref, k_ref, v_ref, o_ref, sk_ref, s_ref, jstar_ref,
                       mpart_ref, lpart_ref, acc_ref, *, kblk, topk):
    i = pl.program_id(1)
    qb = q_ref.shape[0]
    nblk = (i * qb + qb + kblk - 1) // kblk
    seg = dict(ki=ki_ref, ki_t=False, k=k_ref, v=v_ref, nblk=nblk, kblk=kblk, pos0=0, hstride=None)
    _dsa_core(qi_ref, t_ref, q_ref, [seg], o_ref, [sk_ref], [s_ref], jstar_ref, mpart_ref, lpart_ref, acc_ref,
              topk=topk, q_chunk0=(i * qb) // CHUNK)


def dsa_prompt(qi_pad, trot, q_bf, ki_bf, k_bf, v_bf, nseq, t, qb, kblk, topk):
    rows, d = q_bf.shape
    n_dsa = d // HEAD_DIM
    nq = t // qb
    row_blk = lambda w: pl.BlockSpec((qb, w), lambda b, i: (b * nq + i, 0))
    seq_blk = lambda w: pl.BlockSpec((t, w), lambda b, i: (b, 0))
    return pl.pallas_call(
        functools.partial(_dsa_prompt_kernel, kblk=kblk, topk=topk),
        grid=(nseq, nq),
        in_specs=[row_blk(qi_pad.shape[1]), row_blk(LANES), row_blk(d),
                  seq_blk(LANES), seq_blk(d), seq_blk(d)],
        out_specs=row_blk(d),
        out_shape=jax.ShapeDtypeStruct((rows, d), F32),
        scratch_shapes=_dsa_scratch(qb, [(t // kblk, kblk)], n_dsa),
        compiler_params=_params(2),
        name="dsa_prompt",
    )(qi_pad, trot, q_bf, ki_bf, k_bf, v_bf)


def _dsa_sample_kernel(qi_ref, t_ref, q_ref, kin_ref, kn_ref, vn_ref, cki_ref, ck_ref, cv_ref, o_ref,
                       sk0_ref, sk1_ref, s0_ref, s1_ref, jstar_ref, mpart_ref, lpart_ref, acc_ref, *,
                       kblk, topk):
    past = cki_ref.shape[1]
    tq = q_ref.shape[0]
    n_dsa = q_ref.shape[1] // HEAD_DIM
    segs = [dict(ki=cki_ref, ki_t=True, k=ck_ref, v=cv_ref, nblk=past // kblk, kblk=kblk, pos0=0, hstride=n_dsa),
            dict(ki=kin_ref, ki_t=False, k=kn_ref, v=vn_ref, nblk=1, kblk=tq, pos0=past, hstride=None)]
    _dsa_core(qi_ref, t_ref, q_ref, segs, o_ref, [sk0_ref, sk1_ref], [s0_ref, s1_ref], jstar_ref,
              mpart_ref, lpart_ref, acc_ref, topk=topk, q_chunk0=past // CHUNK)


def dsa_sample(qi_pad, trot, q_bf, ki_bf, k_bf, v_bf, cki, ck, cv, layer, nseq, t, kblk, topk):
    rows, d = q_bf.shape
    n_dsa = d // HEAD_DIM
    past = cki.shape[3]
    row_blk = lambda w: pl.BlockSpec((t, w), lambda b: (b, 0))
    cache_blk = lambda r, w: pl.BlockSpec((None, None, r, w), lambda b: (layer, b, 0, 0))
    return pl.pallas_call(
        functools.partial(_dsa_sample_kernel, kblk=kblk, topk=topk),
        grid=(nseq,),
        in_specs=[row_blk(qi_pad.shape[1]), row_blk(LANES), row_blk(d),
                  row_blk(LANES), row_blk(d), row_blk(d),
                  cache_blk(cki.shape[2], past), cache_blk(past * n_dsa, HEAD_DIM),
                  cache_blk(past * n_dsa, HEAD_DIM)],
        out_specs=row_blk(d),
        out_shape=jax.ShapeDtypeStruct((rows, d), F32),
        scratch_shapes=_dsa_scratch(t, [(past // kblk, kblk), (1, t)], n_dsa),
        compiler_params=_params(1),
        name="dsa_sample",
    )(qi_pad, trot, q_bf, ki_bf, k_bf, v_bf, cki, ck, cv)


def _outproj_kernel(oa_ref, ob_ref, oc_ref, za_ref, zb_ref, zc_ref, g_ref, w_ref, x_ref, mod_ref, xo_ref,
                    y_ref, *, d_lru, d_band):
    def branch(o_ref, z_ref, lo, hi):
        o = o_ref[...]
        z = z_ref[...]
        ms = jnp.mean(o * o, axis=-1, keepdims=True)
        y = o * lax.rsqrt(ms + EPS) * g_ref[:, lo:hi]
        y_ref[:, lo:hi] = (y * (z * jax.nn.sigmoid(z))).astype(BF16)

    d_mix = y_ref.shape[1]
    branch(oa_ref, za_ref, 0, d_lru)
    branch(ob_ref, zb_ref, d_lru, d_lru + d_band)
    branch(oc_ref, zc_ref, d_lru + d_band, d_mix)
    out = jnp.dot(y_ref[...], w_ref[...], preferred_element_type=F32)
    xo_ref[...] = x_ref[...] + mod_ref[2:3, :] * out


def outproj(oa, ob, oc, u, g_branch, w_out, layer, x, mod3, cols, nseq, t, tm):
    rows, d = x.shape
    d_lru, d_band, d_dsa = oa.shape[1], ob.shape[1], oc.shape[1]
    d_mix = d_lru + d_band + d_dsa
    nblk = t // tm
    rb = lambda w, o=0: pl.BlockSpec((tm, w), lambda b, i, o=o: (b * nblk + i, o))
    return pl.pallas_call(
        functools.partial(_outproj_kernel, d_lru=d_lru, d_band=d_band),
        grid=(nseq, nblk),
        in_specs=[rb(d_lru), rb(d_band), rb(d_dsa),
                  rb(d_lru, cols["za"] // d_lru), rb(d_band, cols["zb"] // d_band), rb(d_dsa, cols["zc"] // d_dsa),
                  pl.BlockSpec((1, d_mix), lambda b, i: (0, 0)),
                  pl.BlockSpec((None, d_mix, d), lambda b, i: (layer, 0, 0)),
                  rb(d),
                  pl.BlockSpec((None, 3, d), lambda b, i: (b, 0, 0))],
        out_specs=rb(d),
        out_shape=jax.ShapeDtypeStruct((rows, d), F32),
        scratch_shapes=[pltpu.VMEM((tm, d_mix), BF16)],
        compiler_params=_params(2),
        name="outproj",
    )(oa, ob, oc, u, u, u, g_branch, w_out, x, mod3)


def _rmsnorm_kernel(x_ref, g_ref, o_ref):
    x = x_ref[...]
    ms = jnp.mean(x * x, axis=-1, keepdims=True)
    o_ref[...] = x * lax.rsqrt(ms + EPS) * g_ref[...]


def final_norm(x, g, tm):
    rows, d = x.shape
    return pl.pallas_call(
        _rmsnorm_kernel,
        grid=(rows // tm,),
        in_specs=[pl.BlockSpec((tm, d), lambda i: (i, 0)), pl.BlockSpec((1, d), lambda i: (0, 0))],
        out_specs=pl.BlockSpec((tm, d), lambda i: (i, 0)),
        out_shape=jax.ShapeDtypeStruct((rows, d), F32),
        compiler_params=_params(1),
        name="final_norm",
    )(x, g)


def _column_offsets(d_lru, d_band, d_dsa):
    names = ["xa", "za", "qb", "kb", "vb", "zb", "qc", "kc", "vc", "zc", "qi"]
    widths = [d_lru, d_lru, d_band, d_band, d_band, d_band, d_dsa, d_dsa, d_dsa, d_dsa, N_IDX_HEADS * IDX_DIM]
    cols, off = {}, 0
    for n, w in zip(names, widths):
        cols[n] = off
        off += w
    cols["main"] = off
    cols["d_lru"], cols["d_band"], cols["d_dsa"] = d_lru, d_band, d_dsa
    return cols


def _layer_stream(x, mod3, lw, layer, cols, nseq, t, tables, state, strip, prompt):
    d_lru, d_band, d_dsa = cols["d_lru"], cols["d_band"], cols["d_dsa"]
    n_band, n_dsa = d_band // HEAD_DIM, d_dsa // HEAD_DIM
    rows = nseq * t
    tb = 256 if t % 256 == 0 else t
    h = normmod(x, lw["g_norm"], mod3, nseq, t, tb)
    tm = 512 if rows % 512 == 0 else rows
    u = matmul(h, lw["w_in"], layer, cols["main"], tm, 1024)
    tail = matmul(h, lw["w_tail"], layer, LANES, tm, LANES)

    q_bf, k_rot, k_bf, v_bf, qi_pad, t_rot, ki_bf = rope_call(u, tail, tables, cols, tb)

    conv_s, lru_s = state[0], state[1]
    cs8 = jnp.concatenate([jnp.zeros((nseq, 8 - (CONV_W - 1), d_lru), F32), conv_s], axis=1)
    hs = lru_call(u, cs8, lru_s.reshape(nseq, 1, d_lru), lw["conv_w"], lw["conv_b"], lw["w_rg"], lw["b_rg"],
                  lw["w_ig"], lw["b_ig"], lw["lam"], nseq, t, tb)

    if prompt:
        ob = band_prompt(u, strip, cols, nseq, t)
        oc = dsa_prompt(qi_pad, t_rot, q_bf, ki_bf, k_bf, v_bf, nseq, t, 128, 512, min(TOPK_MAX, t // 4))
    else:
        bk, bv, dk, dv, dik = state[2:]
        past = dik.shape[3]
        assert (past + t - 1) // CHUNK <= past // CHUNK and t == CHUNK and bk.shape[2] == BAND_PAST * n_band
        ob = band_sample(u, bk, bv, layer, strip, cols, nseq, t)
        oc = dsa_sample(qi_pad, t_rot, q_bf, ki_bf, k_bf, v_bf, dik, dk, dv, layer,
                        nseq, t, 512, min(TOPK_MAX, (past + t) // 4))

    x_new = outproj(hs, ob, oc, u, lw["g_branch"], lw["w_out"], layer, x, mod3, cols, nseq, t,
                    256 if t % 256 == 0 else t)

    u3 = u.reshape(nseq, t, -1)
    xa = u3[:, :, cols["xa"]:cols["xa"] + d_lru]
    if prompt:
        new_conv = xa[:, t - (CONV_W - 1):]
    else:
        new_conv = jnp.concatenate([conv_s, xa], axis=1)[:, -(CONV_W - 1):]
    nbr = min(BAND_PAST, t)
    new = (new_conv,
           hs.reshape(nseq, t, d_lru)[:, -1],
           u3[:, t - nbr:, cols["kb"]:cols["kb"] + d_band].reshape(nseq, nbr, n_band, HEAD_DIM),
           u3[:, t - nbr:, cols["vb"]:cols["vb"] + d_band].reshape(nseq, nbr, n_band, HEAD_DIM),
           k_rot.reshape(nseq, t, n_dsa, HEAD_DIM),
           u3[:, :, cols["vc"]:cols["vc"] + d_dsa].reshape(nseq, t, n_dsa, HEAD_DIM),
           t_rot.reshape(nseq, t, LANES)[:, :, :IDX_DIM])
    return x_new, new


def kernel(x_prompt, x_sample, c_prompt, c_sample, state_conv, state_lru, cache_band_k, cache_band_v,
           cache_dsa_k, cache_dsa_v, cache_dsa_idx_k, g_norm, w_ada, b_ada, w_in, conv_w, conv_b,
           w_rg, b_rg, w_ig, b_ig, lru_lambda, rel_bias, g_branch, w_out, g_final):
    depth = w_in.shape[0]
    nb_p, t_p, d = x_prompt.shape
    nb_s, t_s, _ = x_sample.shape
    past = cache_dsa_k.shape[2]
    d_lru = conv_w.shape[2]
    d_band = cache_band_k.shape[3] * HEAD_DIM
    d_dsa = cache_dsa_k.shape[3] * HEAD_DIM
    cols = _column_offsets(d_lru, d_band, d_dsa)
    n_main = cols["main"]

    mod = ada_all(jnp.concatenate([c_prompt, c_sample], axis=0), w_ada, b_ada)
    mod = mod.reshape(depth, nb_p + nb_s, 3, d)

    w_tail = jnp.pad(w_in[:, :, n_main:], ((0, 0), (0, 0), (0, LANES - (w_in.shape[2] - n_main)))).astype(BF16)
    w_out_bf = w_out.astype(BF16)
    w_rg_bf = w_rg.astype(BF16)
    w_ig_bf = w_ig.astype(BF16)

    band_k = cache_band_k.reshape(depth, nb_s, -1, HEAD_DIM)
    band_v = cache_band_v.reshape(depth, nb_s, -1, HEAD_DIM)
    dsa_k = cache_dsa_k.reshape(depth, nb_s, -1, HEAD_DIM)
    dsa_v = cache_dsa_v.reshape(depth, nb_s, -1, HEAD_DIM)
    idx_k_t = jnp.swapaxes(cache_dsa_idx_k, 2, 3)

    tab_p = rope_tables(jnp.tile(jnp.arange(t_p), nb_p))
    tab_s = rope_tables(jnp.tile(past + jnp.arange(t_s), nb_s))

    xp = x_prompt.reshape(nb_p * t_p, d)
    xs = x_sample.reshape(nb_s * t_s, d)
    zero_state = (jnp.zeros((nb_p, CONV_W - 1, d_lru), F32), jnp.zeros((nb_p, d_lru), F32))
    p_new, s_new = [], []
    for l in range(depth):
        lw = dict(g_norm=g_norm[l][None], w_in=w_in, w_tail=w_tail, conv_w=conv_w[l], conv_b=conv_b[l][None],
                  w_rg=w_rg_bf[l], b_rg=b_rg[l][None], w_ig=w_ig_bf[l], b_ig=b_ig[l][None],
                  lam=lru_lambda[l][None], g_branch=g_branch[l][None], w_out=w_out_bf)
        strip = bias_strip(rel_bias[l])
        xp, pn = _layer_stream(xp, mod[l, :nb_p], lw, l, cols, nb_p, t_p, tab_p, zero_state, strip, True)
        st = (state_conv[l], state_lru[l], band_k, band_v, dsa_k, dsa_v, idx_k_t)
        xs, sn = _layer_stream(xs, mod[l, nb_p:], lw, l, cols, nb_s, t_s, tab_s, st, strip, False)
        p_new.append(pn)
        s_new.append(sn)

    y_prompt = final_norm(xp, g_final[None], 256).reshape(nb_p, t_p, d)
    y_sample = final_norm(xs, g_final[None], 256).reshape(nb_s, t_s, d)
    p_out = [jnp.stack([t[j] for t in p_new], axis=0) for j in range(7)]
    s_out = [jnp.stack([t[j] for t in s_new], axis=0) for j in range(7)]
    return (y_prompt, y_sample, *p_out, *s_out)
```

```python
import functools
import struct

import jax
import jax.numpy as jnp
from jax import lax
from jax.experimental import pallas as pl
from jax.experimental.pallas import tpu as pltpu

F32 = jnp.float32
BF16 = jnp.bfloat16
I32 = jnp.int32

CHUNK = 64
HEAD_DIM = 128
LRU_BLOCKS = 8
CONV_W = 4
LRU_C = 8.0
BAND_PAST_CHUNKS = 8
BAND_PAST = BAND_PAST_CHUNKS * CHUNK
BAND_KEYS = BAND_PAST + CHUNK
REL_CLIP = 256
N_IDX_HEADS = 16
IDX_DIM = 64
TOPK_MAX = 256
ROPE_THETA = 500000.0
ROPE_FRAC = 4
EPS = 1e-6
NEG = -1e30
LANES = 128

INT_MIN = -(2 ** 31)


def _sortable_key_of(x):
    b = struct.unpack("<i", struct.pack("<f", x))[0]
    return b ^ 0x7FFFFFFF if b < 0 else b


KEY_HALF_NEG = _sortable_key_of(NEG * 0.5)

VMEM_LIMIT = 48 * 1024 * 1024
NT_DIMS = (((1,), (1,)), ((), ()))


def _params(n_grid):
    return pltpu.CompilerParams(dimension_semantics=("arbitrary",) * n_grid,
                                vmem_limit_bytes=VMEM_LIMIT)


def _dot_nt(a, b):
    return lax.dot_general(a, b, NT_DIMS, preferred_element_type=F32)


def _ada_kernel(c_ref, w_ref, b_ref, o_ref):
    c = c_ref[...]
    s = (c * jax.nn.sigmoid(c)).astype(BF16)
    o_ref[...] = jnp.dot(s, w_ref[...].astype(BF16), preferred_element_type=F32) + b_ref[...]


def ada_all(c_all, w_ada, b_ada):
    depth, d, n = w_ada.shape
    nb = c_all.shape[0]
    tn = 512
    return pl.pallas_call(
        _ada_kernel,
        grid=(depth, n // tn),
        in_specs=[pl.BlockSpec((nb, d), lambda l, j: (0, 0)),
                  pl.BlockSpec((None, d, tn), lambda l, j: (l, 0, j)),
                  pl.BlockSpec((None, 1, tn), lambda l, j: (l, 0, j))],
        out_specs=pl.BlockSpec((None, nb, tn), lambda l, j: (l, 0, j)),
        out_shape=jax.ShapeDtypeStruct((depth, nb, n), F32),
        compiler_params=_params(2),
        name="ada",
    )(c_all, w_ada, b_ada.reshape(depth, 1, n))


def _normmod_kernel(x_ref, g_ref, mod_ref, h_ref):
    x = x_ref[...]
    ms = jnp.mean(x * x, axis=-1, keepdims=True)
    y = x * lax.rsqrt(ms + EPS) * g_ref[...]
    shift = mod_ref[0:1, :]
    scale = mod_ref[1:2, :]
    h_ref[...] = (y * (1.0 + scale) + shift).astype(BF16)


def normmod(x, g, mod3, nseq, t, tb):
    rows, d = x.shape
    nblk = t // tb
    return pl.pallas_call(
        _normmod_kernel,
        grid=(nseq, nblk),
        in_specs=[pl.BlockSpec((tb, d), lambda b, i: (b * nblk + i, 0)),
                  pl.BlockSpec((1, d), lambda b, i: (0, 0)),
                  pl.BlockSpec((None, 3, d), lambda b, i: (b, 0, 0))],
        out_specs=pl.BlockSpec((tb, d), lambda b, i: (b * nblk + i, 0)),
        out_shape=jax.ShapeDtypeStruct((rows, d), BF16),
        compiler_params=_params(2),
        name="normmod",
    )(x, g, mod3)


def _mm_kernel(a_ref, b_ref, o_ref, *scratch):
    if scratch:
        wb_ref, = scratch

        @pl.when(pl.program_id(1) == 0)
        def _():
            wb_ref[...] = b_ref[...].T.astype(BF16)

        w = wb_ref[...]
    else:
        w = b_ref[...]
    o_ref[...] = jnp.dot(a_ref[...], w, preferred_element_type=F32)


def matmul(a, w, layer, ncols, tm, tn):
    m, k = a.shape
    if w.dtype == BF16:
        scratch = []
        w_spec = pl.BlockSpec((None, k, tn), lambda j, i: (layer, 0, j))
    else:
        scratch = [pltpu.VMEM((k, tn), BF16)]
        w_spec = pl.BlockSpec((None, tn, k), lambda j, i: (layer, j, 0))
    return pl.pallas_call(
        _mm_kernel,
        grid=(ncols // tn, m // tm),
        in_specs=[pl.BlockSpec((tm, k), lambda j, i: (i, 0)), w_spec],
        out_specs=pl.BlockSpec((tm, tn), lambda j, i: (i, j)),
        out_shape=jax.ShapeDtypeStruct((m, ncols), F32),
        scratch_shapes=scratch,
        compiler_params=_params(2),
        name="inproj",
    )(a, w)


def _rope_kernel(qc_ref, kc_ref, vc_ref, qi_ref, t_ref, c128_ref, s128_ref, c64_ref, s64_ref,
                 q_out, krot_out, kbf_out, vbf_out, qi_out, trot_out, kibf_out, *, n_dsa, n_idx):
    tm = qc_ref.shape[0]
    lane = lax.broadcasted_iota(I32, (tm, LANES), 1)
    c128 = c128_ref[...]
    s128 = s128_ref[...]
    c64 = c64_ref[...]
    s64 = s64_ref[...]
    half128 = HEAD_DIM // ROPE_FRAC // 2
    half64 = IDX_DIM // ROPE_FRAC // 2

    def rope128(x):
        partner = jnp.where(lane < half128, pltpu.roll(x, LANES - half128, 1), pltpu.roll(x, half128, 1))
        return x * c128 + partner * s128

    def rope64(x):
        partner = jnp.where((lane & (IDX_DIM - 1)) < half64,
                            pltpu.roll(x, LANES - half64, 1), pltpu.roll(x, half64, 1))
        return x * c64 + partner * s64

    for h in range(n_dsa):
        sl = slice(h * LANES, (h + 1) * LANES)
        q_out[:, sl] = rope128(qc_ref[:, sl]).astype(BF16)
        kr = rope128(kc_ref[:, sl])
        krot_out[:, sl] = kr
        kbf_out[:, sl] = kr.astype(BF16)
        vbf_out[:, sl] = vc_ref[:, sl].astype(BF16)
    low = lane < IDX_DIM
    for j in range(n_idx // 2):
        r = rope64(qi_ref[:, j * LANES:(j + 1) * LANES]) * (IDX_DIM ** -0.5)
        qi_out[:, (2 * j) * LANES:(2 * j + 1) * LANES] = jnp.where(low, r, 0.0).astype(BF16)
        qi_out[:, (2 * j + 1) * LANES:(2 * j + 2) * LANES] = jnp.where(
            low, pltpu.roll(r, IDX_DIM, 1), 0.0).astype(BF16)
    t = t_ref[...]
    r = rope64(t)
    trot_out[...] = jnp.where(low, r, jnp.where(lane < IDX_DIM + N_IDX_HEADS, t * (N_IDX_HEADS ** -0.5), 0.0))
    kibf_out[...] = jnp.where(low, r, 0.0).astype(BF16)


def rope_call(u, tail, tables, cols, tm):
    rows = u.shape[0]
    d_dsa = cols["d_dsa"]
    n_dsa = d_dsa // HEAD_DIM
    d_qi = N_IDX_HEADS * IDX_DIM
    c128, s128, c64, s64 = tables
    tab_spec = pl.BlockSpec((tm, LANES), lambda i: (i, 0))
    blk = lambda off: pl.BlockSpec((tm, d_dsa), lambda i, o=off // d_dsa: (i, o))
    return pl.pallas_call(
        functools.partial(_rope_kernel, n_dsa=n_dsa, n_idx=N_IDX_HEADS),
        grid=(rows // tm,),
        in_specs=[blk(cols["qc"]), blk(cols["kc"]), blk(cols["vc"]),
                  pl.BlockSpec((tm, d_qi), lambda i, o=cols["qi"] // d_qi: (i, o)),
                  tab_spec, tab_spec, tab_spec, tab_spec, tab_spec],
        out_specs=[pl.BlockSpec((tm, d_dsa), lambda i: (i, 0)),
                   pl.BlockSpec((tm, d_dsa), lambda i: (i, 0)),
                   pl.BlockSpec((tm, d_dsa), lambda i: (i, 0)),
                   pl.BlockSpec((tm, d_dsa), lambda i: (i, 0)),
                   pl.BlockSpec((tm, N_IDX_HEADS * LANES), lambda i: (i, 0)),
                   tab_spec, tab_spec],
        out_shape=[jax.ShapeDtypeStruct((rows, d_dsa), BF16),
                   jax.ShapeDtypeStruct((rows, d_dsa), F32),
                   jax.ShapeDtypeStruct((rows, d_dsa), BF16),
                   jax.ShapeDtypeStruct((rows, d_dsa), BF16),
                   jax.ShapeDtypeStruct((rows, N_IDX_HEADS * LANES), BF16),
                   jax.ShapeDtypeStruct((rows, LANES), F32),
                   jax.ShapeDtypeStruct((rows, LANES), BF16)],
        compiler_params=_params(1),
        name="rope",
    )(u, u, u, u, tail, c128, s128, c64, s64)


def rope_tables(pos):
    pos = pos.astype(F32)[:, None]
    n = pos.shape[0]

    def tab(dim):
        half = dim // ROPE_FRAC // 2
        inv = ROPE_THETA ** (-jnp.arange(half, dtype=F32) / half)
        ang = pos * inv[None]
        cos, sin = jnp.cos(ang), jnp.sin(ang)
        c = jnp.concatenate([cos, cos, jnp.ones((n, dim - 2 * half), F32)], axis=1)
        s = jnp.concatenate([-sin, sin, jnp.zeros((n, dim - 2 * half), F32)], axis=1)
        return jnp.tile(c, (1, LANES // dim)), jnp.tile(s, (1, LANES // dim))

    c128, s128 = tab(HEAD_DIM)
    c64, s64 = tab(IDX_DIM)
    return c128, s128, c64, s64


def _lru_kernel(xa_ref, cs_ref, h0_ref, cw_ref, cb_ref, wrg_ref, brg_ref, wig_ref, big_ref, lam_ref,
                hs_ref, ext_ref, hc_ref):
    tb, d = xa_ref.shape
    blk = d // LRU_BLOCKS

    @pl.when(pl.program_id(1) == 0)
    def _():
        ext_ref[0:8, :] = cs_ref[...]
        hc_ref[...] = h0_ref[...]

    ext_ref[8:8 + tb, :] = xa_ref[...]
    conv = ext_ref[5:5 + tb, :] * cw_ref[0:1, :]
    for j in range(1, CONV_W):
        conv = conv + ext_ref[5 + j:5 + j + tb, :] * cw_ref[j:j + 1, :]
    conv = conv + cb_ref[...]
    tail = ext_ref[tb:tb + 8, :]
    ext_ref[0:8, :] = tail

    xb = conv.astype(BF16)
    rs, gs = [], []
    for g in range(LRU_BLOCKS):
        xg = xb[:, g * blk:(g + 1) * blk]
        rs.append(jnp.dot(xg, wrg_ref[g], preferred_element_type=F32))
        gs.append(jnp.dot(xg, wig_ref[g], preferred_element_type=F32))
    r = jax.nn.sigmoid(jnp.concatenate(rs, axis=1) + brg_ref[...])
    ig = jax.nn.sigmoid(jnp.concatenate(gs, axis=1) + big_ref[...])
    lam = lam_ref[...]
    softplus_neg_lam = jnp.maximum(-lam, 0.0) + jnp.log1p(jnp.exp(-jnp.abs(lam)))
    log_a = (-LRU_C) * r * softplus_neg_lam
    a = jnp.exp(log_a)
    u = jnp.sqrt(-jnp.tanh(log_a) * (a * a + 1.0)) * (ig * conv)

    row = lax.broadcasted_iota(I32, (tb, d), 0)
    s = 1
    while s < tb:
        keep = row >= s
        a_prev = jnp.where(keep, pltpu.roll(a, s, 0), 1.0)
        u_prev = jnp.where(keep, pltpu.roll(u, s, 0), 0.0)
        u = a * u_prev + u
        a = a * a_prev
        s *= 2
    h = a * hc_ref[...] + u
    hs_ref[...] = h
    hc_ref[...] = h[tb - 1:tb, :]


def lru_call(u, cs8, h0, cw, cb, wrg, brg, wig, big, lam, nseq, t, tb):
    rows = u.shape[0]
    d = cw.shape[1]
    blk = d // LRU_BLOCKS
    nblk = t // tb
    vec = pl.BlockSpec((1, d), lambda b, i: (0, 0))
    wspec = pl.BlockSpec((LRU_BLOCKS, blk, blk), lambda b, i: (0, 0, 0))
    return pl.pallas_call(
        _lru_kernel,
        grid=(nseq, nblk),
        in_specs=[pl.BlockSpec((tb, d), lambda b, i: (b * nblk + i, 0)),
                  pl.BlockSpec((None, 8, d), lambda b, i: (b, 0, 0)),
                  pl.BlockSpec((None, 1, d), lambda b, i: (b, 0, 0)),
                  pl.BlockSpec((CONV_W, d), lambda b, i: (0, 0)),
                  vec, wspec, vec, wspec, vec, vec],
        out_specs=pl.BlockSpec((tb, d), lambda b, i: (b * nblk + i, 0)),
        out_shape=jax.ShapeDtypeStruct((rows, d), F32),
        scratch_shapes=[pltpu.VMEM((tb + 8, d), F32), pltpu.VMEM((1, d), F32)],
        compiler_params=_params(2),
        name="lru",
    )(u, cs8, h0, cw, cb, wrg, brg, wig, big, lam)


def _bias_kernel(relb_ref, o_ref):
    h = pl.program_id(0)
    qi = lax.broadcasted_iota(I32, (CHUNK, BAND_KEYS), 0)
    kj = lax.broadcasted_iota(I32, (CHUNK, BAND_KEYS), 1)
    idx = jnp.clip(qi + BAND_PAST - kj, -REL_CLIP, REL_CLIP) + REL_CLIP

    def body(r, acc):
        return jnp.where(idx == r, relb_ref[h, r], acc)

    o_ref[...] = lax.fori_loop(0, 2 * REL_CLIP + 1, body, jnp.zeros((CHUNK, BAND_KEYS), F32))


def bias_strip(relb):
    nh = relb.shape[0]
    return pl.pallas_call(
        _bias_kernel,
        grid=(nh,),
        in_specs=[pl.BlockSpec(memory_space=pltpu.SMEM)],
        out_specs=pl.BlockSpec((None, CHUNK, BAND_KEYS), lambda h: (h, 0, 0)),
        out_shape=jax.ShapeDtypeStruct((nh, CHUNK, BAND_KEYS), F32),
        compiler_params=_params(1),
        name="bias_strip",
    )(relb)


def _band_chunk(q, kwin, vwin, strip, min_col):
    s = _dot_nt(q, kwin) * (HEAD_DIM ** -0.5) + strip
    col = lax.broadcasted_iota(I32, s.shape, 1)
    s = jnp.where(col >= min_col, s, NEG)
    m = jnp.max(s, axis=-1, keepdims=True)
    p = jnp.exp(s - m)
    l = jnp.sum(p, axis=-1, keepdims=True)
    return jnp.dot(p.astype(BF16), vwin, preferred_element_type=F32) / l


def _band_prompt_kernel(q_ref, kp_ref, kc_ref, vp_ref, vc_ref, strip_ref, o_ref, kw_ref, vw_ref):
    i = pl.program_id(2)
    tb = q_ref.shape[0]
    kw_ref[0:tb, :] = kp_ref[...].astype(BF16)
    kw_ref[tb:2 * tb, :] = kc_ref[...].astype(BF16)
    vw_ref[0:tb, :] = vp_ref[...].astype(BF16)
    vw_ref[tb:2 * tb, :] = vc_ref[...].astype(BF16)
    strip = strip_ref[...]
    for a in range(tb // CHUNK):
        q = q_ref[a * CHUNK:(a + 1) * CHUNK, :].astype(BF16)
        lo = a * CHUNK
        min_col = jnp.where(i > 0, 0, (BAND_PAST_CHUNKS - a) * CHUNK)
        o_ref[a * CHUNK:(a + 1) * CHUNK, :] = _band_chunk(
            q, kw_ref[lo:lo + BAND_KEYS, :], vw_ref[lo:lo + BAND_KEYS, :], strip, min_col)


def band_prompt(u, strip, cols, nseq, t):
    rows = u.shape[0]
    tb = BAND_PAST
    nblk = t // tb
    nh = strip.shape[0]
    qo, ko, vo = cols["qb"] // HEAD_DIM, cols["kb"] // HEAD_DIM, cols["vb"] // HEAD_DIM
    cur = lambda off: pl.BlockSpec((tb, HEAD_DIM), lambda b, h, i: (b * nblk + i, off + h))
    prev = lambda off: pl.BlockSpec((tb, HEAD_DIM), lambda b, h, i: (b * nblk + jnp.maximum(i - 1, 0), off + h))
    return pl.pallas_call(
        _band_prompt_kernel,
        grid=(nseq, nh, nblk),
        in_specs=[cur(qo), prev(ko), cur(ko), prev(vo), cur(vo),
                  pl.BlockSpec((None, CHUNK, BAND_KEYS), lambda b, h, i: (h, 0, 0))],
        out_specs=pl.BlockSpec((tb, HEAD_DIM), lambda b, h, i: (b * nblk + i, h)),
        out_shape=jax.ShapeDtypeStruct((rows, nh * HEAD_DIM), F32),
        scratch_shapes=[pltpu.VMEM((2 * tb, HEAD_DIM), BF16), pltpu.VMEM((2 * tb, HEAD_DIM), BF16)],
        compiler_params=_params(3),
        name="band_prompt",
    )(u, u, u, u, u, strip)


def _band_sample_kernel(q_ref, kn_ref, vn_ref, ck_ref, cv_ref, strip_ref, o_ref, kw_ref, vw_ref):
    nh = strip_ref.shape[0]
    w = ck_ref.shape[0] // nh
    t = q_ref.shape[0]
    kw_ref[w:w + t, :] = kn_ref[...].astype(BF16)
    vw_ref[w:w + t, :] = vn_ref[...].astype(BF16)
    for h in range(nh):
        sl = slice(h * HEAD_DIM, (h + 1) * HEAD_DIM)
        kw_ref[0:w, sl] = ck_ref[pl.ds(h, w, stride=nh), :].astype(BF16)
        vw_ref[0:w, sl] = cv_ref[pl.ds(h, w, stride=nh), :].astype(BF16)
        o_ref[:, sl] = _band_chunk(q_ref[:, sl].astype(BF16), kw_ref[:, sl], vw_ref[:, sl], strip_ref[h], 0)


def band_sample(u, ck, cv, layer, strip, cols, nseq, t):
    rows = u.shape[0]
    nh = strip.shape[0]
    d = nh * HEAD_DIM
    w = ck.shape[2] // nh
    ublk = lambda off: pl.BlockSpec((t, d), lambda b, o=off // d: (b, o))
    cblk = pl.BlockSpec((None, None, w * nh, HEAD_DIM), lambda b: (layer, b, 0, 0))
    return pl.pallas_call(
        _band_sample_kernel,
        grid=(nseq,),
        in_specs=[ublk(cols["qb"]), ublk(cols["kb"]), ublk(cols["vb"]), cblk, cblk,
                  pl.BlockSpec((nh, CHUNK, BAND_KEYS), lambda b: (0, 0, 0))],
        out_specs=pl.BlockSpec((t, d), lambda b: (b, 0)),
        out_shape=jax.ShapeDtypeStruct((rows, d), F32),
        scratch_shapes=[pltpu.VMEM((w + t, d), BF16), pltpu.VMEM((w + t, d), BF16)],
        compiler_params=_params(1),
        name="band_sample",
    )(u, u, u, ck, cv, strip)


def _loop(n, body, init):
    if isinstance(n, int):
        val = init
        for k in range(n):
            val = body(k, val)
        return val
    return lax.fori_loop(0, n, body, init)


IDX_BITS = 16


def _f32_to_key(x):
    bits = lax.bitcast_convert_type(x, I32)
    return jnp.where(bits < 0, bits ^ 0x7FFFFFFF, bits)


def _dsa_core(qi_ref, t_ref, q_ref, segs, o_ref, sk_refs, s_refs, jstar_ref, mpart_ref, lpart_ref, acc_ref, *,
              topk, q_chunk0):
    qb = q_ref.shape[0]
    n_dsa = q_ref.shape[1] // HEAD_DIM
    wq = t_ref[:, IDX_DIM:IDX_DIM + N_IDX_HEADS]
    wcols = [wq[:, h:h + 1] for h in range(N_IDX_HEADS)]
    lane = lax.broadcasted_iota(I32, (qb, LANES), 1)

    def blk_start(kb, kblk):
        return kb * kblk if isinstance(kb, int) else pl.multiple_of(kb * kblk, kblk)

    for seg, sk_ref in zip(segs, sk_refs):
        kblk = seg["kblk"]
        kdim = seg["ki"].shape[0 if seg["ki_t"] else 1]
        qchunk = q_chunk0 + lax.broadcasted_iota(I32, (qb, kblk), 0) // CHUNK
        col = lax.broadcasted_iota(I32, (qb, kblk), 1)

        def score_blk(kb, carry, seg=seg, sk_ref=sk_ref, kblk=kblk, kdim=kdim, qchunk=qchunk, col=col):
            if seg["ki_t"]:
                kib = seg["ki"][:, pl.ds(blk_start(kb, kblk), kblk)].astype(BF16)
            else:
                kib = seg["ki"][pl.ds(blk_start(kb, kblk), kblk), :].astype(BF16)
            acc = jnp.zeros((qb, kblk), F32)
            for h in range(N_IDX_HEADS):
                qh = qi_ref[:, h * LANES:h * LANES + kdim]
                sc = jnp.dot(qh, kib, preferred_element_type=F32) if seg["ki_t"] else _dot_nt(qh, kib)
                acc = acc + jnp.maximum(sc, 0.0) * wcols[h]
            kchunk = (seg["pos0"] + kb * kblk + col) // CHUNK
            key = _f32_to_key(jnp.where(kchunk <= qchunk, acc, NEG))
            if kblk == sk_ref.shape[2]:
                sk_ref[kb] = key
            else:
                sk_ref[kb] = jnp.full(sk_ref.shape[1:], INT_MIN, I32)
                sk_ref[kb, :, 0:kblk] = key
            return carry

        _loop(seg["nblk"], score_blk, 0)

    def fold(tile_fn, init):
        parts = init
        for seg, sk_ref in zip(segs, sk_refs):
            def body(kb, parts, seg=seg, sk_ref=sk_ref):
                blk = sk_ref[kb]
                for j in range(sk_ref.shape[2] // LANES):
                    idx = seg["pos0"] + kb * seg["kblk"] + j * LANES + lane
                    parts = tile_fn(parts, blk[:, j * LANES:(j + 1) * LANES], idx)
                return parts

            parts = _loop(seg["nblk"], body, parts)
        return parts

    def rep(x):
        return jnp.broadcast_to(x, (qb, LANES))

    def count(pred):
        part = fold(lambda p, key, idx: p + jnp.where(pred(key, idx), 1, 0), jnp.zeros((qb, LANES), I32))
        return rep(jnp.sum(part, axis=1, keepdims=True))

    zero = jnp.zeros((qb, LANES), I32)
    thr = jnp.where(count(lambda key, idx: key >= zero) >= topk, 0, INT_MIN).astype(I32)

    def bit_body(it, t):
        cand = t + jnp.left_shift(jnp.int32(1), 30 - it)
        return jnp.where(count(lambda key, idx: key >= cand) >= topk, cand, t)

    thr = lax.fori_loop(0, 31, bit_body, thr)
    c_lo = count(lambda key, idx: key >= thr)
    c_hi = count(lambda key, idx: key > thr)
    lo = jnp.maximum(thr, KEY_HALF_NEG + 1)

    big = 2 ** IDX_BITS
    jstar_ref[...] = jnp.full((qb, LANES), big, I32)
    surplus = (c_lo > topk) & (thr > KEY_HALF_NEG)
    need = topk - c_hi

    @pl.when(jnp.max(jnp.where(surplus, 1, 0)) > 0)
    def _():
        def idx_body(it, j):
            cand = j + jnp.left_shift(jnp.int32(1), IDX_BITS - 1 - it)
            f = count(lambda key, idx: (key == lo) & (idx < cand))
            return jnp.where(f <= need, cand, j)

        j = lax.fori_loop(0, IDX_BITS, idx_body, zero)
        jstar_ref[...] = jnp.where(surplus, j, big)

    lo1 = lo[:, 0:1]
    jstar1 = jstar_ref[:, 0:1]

    mpart_ref[...] = jnp.full(mpart_ref.shape, NEG, F32)
    lpart_ref[...] = jnp.zeros(lpart_ref.shape, F32)
    acc_ref[...] = jnp.zeros(acc_ref.shape, F32)

    def head_rows(ref, seg, kb, h):
        kblk = seg["kblk"]
        if seg["hstride"] is None:
            return ref[pl.ds(blk_start(kb, kblk), kblk), h * HEAD_DIM:(h + 1) * HEAD_DIM].astype(BF16)
        hs = seg["hstride"]
        return ref[pl.ds(kb * kblk * hs + h, kblk, stride=hs), :].astype(BF16)

    for seg, sk_ref, s_ref in zip(segs, sk_refs, s_refs):
        kblk = seg["kblk"]
        col = lax.broadcasted_iota(I32, (qb, kblk), 1)

        def pass_a(kb, carry, seg=seg, sk_ref=sk_ref, s_ref=s_ref, kblk=kblk, col=col):
            key = sk_ref[kb]
            if kblk != sk_ref.shape[2]:
                key = key[:, 0:kblk]
            idx = seg["pos0"] + kb * kblk + col
            mask = (key >= lo1) & ((key > lo1) | (idx < jstar1))
            for h in range(n_dsa):
                s = _dot_nt(q_ref[:, h * HEAD_DIM:(h + 1) * HEAD_DIM], head_rows(seg["k"], seg, kb, h))
                s = jnp.where(mask, s * (HEAD_DIM ** -0.5), NEG)
                s_ref[h, kb] = s
                mp = mpart_ref[h]
                if kblk % LANES == 0:
                    for j in range(kblk // LANES):
                        mp = jnp.maximum(mp, s[:, j * LANES:(j + 1) * LANES])
                else:
                    mp = jnp.maximum(mp, jnp.max(s, axis=1, keepdims=True))
                mpart_ref[h] = mp
            return carry

        _loop(seg["nblk"], pass_a, 0)

    m_rows = [jnp.max(mpart_ref[h], axis=1, keepdims=True) for h in range(n_dsa)]

    for seg, s_ref in zip(segs, s_refs):
        kblk = seg["kblk"]

        def pass_b(kb, carry, seg=seg, s_ref=s_ref, kblk=kblk):
            for h in range(n_dsa):
                p = jnp.exp(s_ref[h, kb] - m_rows[h])
                lp = lpart_ref[h]
                if kblk % LANES == 0:
                    for j in range(kblk // LANES):
                        lp = lp + p[:, j * LANES:(j + 1) * LANES]
                else:
                    lp = lp + jnp.where(lane == 0, jnp.sum(p, axis=1, keepdims=True), 0.0)
                lpart_ref[h] = lp
                acc_ref[h] += jnp.dot(p.astype(BF16), head_rows(seg["v"], seg, kb, h),
                                      preferred_element_type=F32)
            return carry

        _loop(seg["nblk"], pass_b, 0)

    for h in range(n_dsa):
        l = jnp.sum(lpart_ref[h], axis=1, keepdims=True)
        o_ref[:, h * HEAD_DIM:(h + 1) * HEAD_DIM] = acc_ref[h] / l


def _dsa_scratch(qb, seg_shapes, n_dsa):
    return ([pltpu.VMEM((nblk, qb, max(kblk, LANES)), I32) for nblk, kblk in seg_shapes]
            + [pltpu.VMEM((n_dsa, nblk, qb, kblk), F32) for nblk, kblk in seg_shapes]
            + [pltpu.VMEM((qb, LANES), I32),
               pltpu.VMEM((n_dsa, qb, LANES), F32), pltpu.VMEM((n_dsa, qb, LANES), F32),
               pltpu.VMEM((n_dsa, qb, HEAD_DIM), F32)])


def _dsa_prompt_kernel(qi_ref, t_ref, q_ref, ki_ref, k_ref, v_ref, o_ref, sk_ref, s_ref, jstar_ref,
                       mpart_ref, lpart_ref, acc_ref, *, kblk, topk):
    i = pl.program_id(1)
    qb = q_ref.shape[0]
    nblk = (i * qb + qb + kblk - 1) // kblk
    seg = dict(ki=ki_ref, ki_t=False, k=k_ref, v=v_ref, nblk=nblk, kblk=kblk, pos0=0, hstride=None)
    _dsa_core(qi_ref, t_ref, q_ref, [seg], o_ref, [sk_ref], [s_ref], jstar_ref, mpart_ref, lpart_ref, acc_ref,
              topk=topk, q_chunk0=(i * qb) // CHUNK)


def dsa_prompt(qi_pad, trot, q_bf, ki_bf, k_bf, v_bf, nseq, t, qb, kblk, topk):
    rows, d = q_bf.shape
    n_dsa = d // HEAD_DIM
    nq = t // qb
    row_blk = lambda w: pl.BlockSpec((qb, w), lambda b, i: (b * nq + i, 0))
    seq_blk = lambda w: pl.BlockSpec((t, w), lambda b, i: (b, 0))
    return pl.pallas_call(
        functools.partial(_dsa_prompt_kernel, kblk=kblk, topk=topk),
        grid=(nseq, nq),
        in_specs=[row_blk(qi_pad.shape[1]), row_blk(LANES), row_blk(d),
                  seq_blk(LANES), seq_blk(d), seq_blk(d)],
        out_specs=row_blk(d),
        out_shape=jax.ShapeDtypeStruct((rows, d), F32),
        scratch_shapes=_dsa_scratch(qb, [(t // kblk, kblk)], n_dsa),
        compiler_params=_params(2),
        name="dsa_prompt",
    )(qi_pad, trot, q_bf, ki_bf, k_bf, v_bf)


def _dsa_sample_kernel(qi_ref, t_ref, q_ref, kin_ref, kn_ref, vn_ref, cki_ref, ck_ref, cv_ref, o_ref,
                       sk0_ref, sk1_ref, s0_ref, s1_ref, jstar_ref, mpart_ref, lpart_ref, acc_ref, *,
                       kblk, topk):
    past = cki_ref.shape[1]
    tq = q_ref.shape[0]
    n_dsa = q_ref.shape[1] // HEAD_DIM
    segs = [dict(ki=cki_ref, ki_t=True, k=ck_ref, v=cv_ref, nblk=past // kblk, kblk=kblk, pos0=0, hstride=n_dsa),
            dict(ki=kin_ref, ki_t=False, k=kn_ref, v=vn_ref, nblk=1, kblk=tq, pos0=past, hstride=None)]
    _dsa_core(qi_ref, t_ref, q_ref, segs, o_ref, [sk0_ref, sk1_ref], [s0_ref, s1_ref], jstar_ref,
              mpart_ref, lpart_ref, acc_ref, topk=topk, q_chunk0=past // CHUNK)


def dsa_sample(qi_pad, trot, q_bf, ki_bf, k_bf, v_bf, cki, ck, cv, layer, nseq, t, kblk, topk):
    rows, d = q_bf.shape
    n_dsa = d // HEAD_DIM
    past = cki.shape[3]
    row_blk = lambda w: pl.BlockSpec((t, w), lambda b: (b, 0))
    cache_blk = lambda r, w: pl.BlockSpec((None, None, r, w), lambda b: (layer, b, 0, 0))
    return pl.pallas_call(
        functools.partial(_dsa_sample_kernel, kblk=kblk, topk=topk),
        grid=(nseq,),
        in_specs=[row_blk(qi_pad.shape[1]), row_blk(LANES), row_blk(d),
                  row_blk(LANES), row_blk(d), row_blk(d),
                  cache_blk(cki.shape[2], past), cache_blk(past * n_dsa, HEAD_DIM),
                  cache_blk(past * n_dsa, HEAD_DIM)],
        out_specs=row_blk(d),
        out_shape=jax.ShapeDtypeStruct((rows, d), F32),
        scratch_shapes=_dsa_scratch(t, [(past // kblk, kblk), (1, t)], n_dsa),
        compiler_params=_params(1),
        name="dsa_sample",
    )(qi_pad, trot, q_bf, ki_bf, k_bf, v_bf, cki, ck, cv)


def _outproj_kernel(oa_ref, ob_ref, oc_ref, za_ref, zb_ref, zc_ref, g_ref, w_ref, x_ref, mod_ref, g2_ref,
                    *rest, d_lru, d_band, last):
    if last:
        yo_ref, y_ref = rest
    else:
        mod2_ref, xo_ref, h_ref, y_ref = rest

    def branch(o_ref, z_ref, lo, hi):
        o = o_ref[...]
        z = z_ref[...]
        ms = jnp.mean(o * o, axis=-1, keepdims=True)
        y = o * lax.rsqrt(ms + EPS) * g_ref[:, lo:hi]
        y_ref[:, lo:hi] = (y * (z * jax.nn.sigmoid(z))).astype(BF16)

    d_mix = y_ref.shape[1]
    branch(oa_ref, za_ref, 0, d_lru)
    branch(ob_ref, zb_ref, d_lru, d_lru + d_band)
    branch(oc_ref, zc_ref, d_lru + d_band, d_mix)
    out = jnp.dot(y_ref[...], w_ref[...], preferred_element_type=F32)
    x_new = x_ref[...] + mod_ref[2:3, :] * out
    ms = jnp.mean(x_new * x_new, axis=-1, keepdims=True)
    normed = x_new * lax.rsqrt(ms + EPS) * g2_ref[...]
    if last:
        yo_ref[...] = normed
    else:
        xo_ref[...] = x_new
        h_ref[...] = (normed * (1.0 + mod2_ref[1:2, :]) + mod2_ref[0:1, :]).astype(BF16)


def outproj(oa, ob, oc, u, g_branch, w_out, layer, x, mod3, g2, mod3_next, cols, nseq, t, tm):
    rows, d = x.shape
    d_lru, d_band, d_dsa = oa.shape[1], ob.shape[1], oc.shape[1]
    d_mix = d_lru + d_band + d_dsa
    nblk = t // tm
    last = mod3_next is None
    rb = lambda w, o=0: pl.BlockSpec((tm, w), lambda b, i, o=o: (b * nblk + i, o))
    vec = pl.BlockSpec((1, d), lambda b, i: (0, 0))
    mod_spec = pl.BlockSpec((None, 3, d), lambda b, i: (b, 0, 0))
    in_specs = [rb(d_lru), rb(d_band), rb(d_dsa),
                rb(d_lru, cols["za"] // d_lru), rb(d_band, cols["zb"] // d_band), rb(d_dsa, cols["zc"] // d_dsa),
                pl.BlockSpec((1, d_mix), lambda b, i: (0, 0)),
                pl.BlockSpec((None, d_mix, d), lambda b, i: (layer, 0, 0)),
                rb(d), mod_spec, vec]
    args = [oa, ob, oc, u, u, u, g_branch, w_out, x, mod3, g2]
    if last:
        out_specs = rb(d)
        out_shape = jax.ShapeDtypeStruct((rows, d), F32)
    else:
        in_specs.append(mod_spec)
        args.append(mod3_next)
        out_specs = [rb(d), rb(d)]
        out_shape = [jax.ShapeDtypeStruct((rows, d), F32), jax.ShapeDtypeStruct((rows, d), BF16)]
    return pl.pallas_call(
        functools.partial(_outproj_kernel, d_lru=d_lru, d_band=d_band, last=last),
        grid=(nseq, nblk),
        in_specs=in_specs,
        out_specs=out_specs,
        out_shape=out_shape,
        scratch_shapes=[pltpu.VMEM((tm, d_mix), BF16)],
        compiler_params=_params(2),
        name="outproj",
    )(*args)


def _column_offsets(d_lru, d_band, d_dsa):
    names = ["xa", "za", "qb", "kb", "vb", "zb", "qc", "kc", "vc", "zc", "qi"]
    widths = [d_lru, d_lru, d_band, d_band, d_band, d_band, d_dsa, d_dsa, d_dsa, d_dsa, N_IDX_HEADS * IDX_DIM]
    cols, off = {}, 0
    for n, w in zip(names, widths):
        cols[n] = off
        off += w
    cols["main"] = off
    cols["d_lru"], cols["d_band"], cols["d_dsa"] = d_lru, d_band, d_dsa
    return cols


def _layer_stream(x, h, mod3, mod3_next, g_next, lw, layer, cols, nseq, t, tables, state, strip, prompt):
    d_lru, d_band, d_dsa = cols["d_lru"], cols["d_band"], cols["d_dsa"]
    n_band, n_dsa = d_band // HEAD_DIM, d_dsa // HEAD_DIM
    rows = nseq * t
    tb = 256 if t % 256 == 0 else t
    tm = 512 if rows % 512 == 0 else rows
    u = matmul(h, lw["w_in"], layer, cols["main"], tm, 1024)
    tail = matmul(h, lw["w_tail"], layer, LANES, tm, LANES)

    q_bf, k_rot, k_bf, v_bf, qi_pad, t_rot, ki_bf = rope_call(u, tail, tables, cols, tb)

    conv_s, lru_s = state[0], state[1]
    cs8 = jnp.concatenate([jnp.zeros((nseq, 8 - (CONV_W - 1), d_lru), F32), conv_s], axis=1)
    hs = lru_call(u, cs8, lru_s.reshape(nseq, 1, d_lru), lw["conv_w"], lw["conv_b"], lw["w_rg"], lw["b_rg"],
                  lw["w_ig"], lw["b_ig"], lw["lam"], nseq, t, tb)

    if prompt:
        ob = band_prompt(u, strip, cols, nseq, t)
        oc = dsa_prompt(qi_pad, t_rot, q_bf, ki_bf, k_bf, v_bf, nseq, t, 128, 512, min(TOPK_MAX, t // 4))
    else:
        bk, bv, dk, dv, dik = state[2:]
        past = dik.shape[3]
        assert (past + t - 1) // CHUNK <= past // CHUNK and t == CHUNK and bk.shape[2] == BAND_PAST * n_band
        ob = band_sample(u, bk, bv, layer, strip, cols, nseq, t)
        oc = dsa_sample(qi_pad, t_rot, q_bf, ki_bf, k_bf, v_bf, dik, dk, dv, layer,
                        nseq, t, 512, min(TOPK_MAX, (past + t) // 4))

    res = outproj(hs, ob, oc, u, lw["g_branch"], lw["w_out"], layer, x, mod3, g_next, mod3_next, cols, nseq, t, tb)
    x_new, h_next = (res, None) if mod3_next is None else res

    u3 = u.reshape(nseq, t, -1)
    xa = u3[:, :, cols["xa"]:cols["xa"] + d_lru]
    if prompt:
        new_conv = xa[:, t - (CONV_W - 1):]
    else:
        new_conv = jnp.concatenate([conv_s, xa], axis=1)[:, -(CONV_W - 1):]
    nbr = min(BAND_PAST, t)
    new = (new_conv,
           hs.reshape(nseq, t, d_lru)[:, -1],
           u3[:, t - nbr:, cols["kb"]:cols["kb"] + d_band].reshape(nseq, nbr, n_band, HEAD_DIM),
           u3[:, t - nbr:, cols["vb"]:cols["vb"] + d_band].reshape(nseq, nbr, n_band, HEAD_DIM),
           k_rot.reshape(nseq, t, n_dsa, HEAD_DIM),
           u3[:, :, cols["vc"]:cols["vc"] + d_dsa].reshape(nseq, t, n_dsa, HEAD_DIM),
           t_rot.reshape(nseq, t, LANES)[:, :, :IDX_DIM])
    return x_new, h_next, new


def kernel(x_prompt, x_sample, c_prompt, c_sample, state_conv, state_lru, cache_band_k, cache_band_v,
           cache_dsa_k, cache_dsa_v, cache_dsa_idx_k, g_norm, w_ada, b_ada, w_in, conv_w, conv_b,
           w_rg, b_rg, w_ig, b_ig, lru_lambda, rel_bias, g_branch, w_out, g_final):
    depth = w_in.shape[0]
    nb_p, t_p, d = x_prompt.shape
    nb_s, t_s, _ = x_sample.shape
    past = cache_dsa_k.shape[2]
    d_lru = conv_w.shape[2]
    d_band = cache_band_k.shape[3] * HEAD_DIM
    d_dsa = cache_dsa_k.shape[3] * HEAD_DIM
    cols = _column_offsets(d_lru, d_band, d_dsa)
    n_main = cols["main"]

    mod = ada_all(jnp.concatenate([c_prompt, c_sample], axis=0), w_ada, b_ada)
    mod = mod.reshape(depth, nb_p + nb_s, 3, d)

    w_tail = jnp.pad(w_in[:, :, n_main:], ((0, 0), (0, 0), (0, LANES - (w_in.shape[2] - n_main)))).astype(BF16)
    w_out_bf = w_out.astype(BF16)
    w_rg_bf = w_rg.astype(BF16)
    w_ig_bf = w_ig.astype(BF16)

    band_k = cache_band_k.reshape(depth, nb_s, -1, HEAD_DIM)
    band_v = cache_band_v.reshape(depth, nb_s, -1, HEAD_DIM)
    dsa_k = cache_dsa_k.reshape(depth, nb_s, -1, HEAD_DIM)
    dsa_v = cache_dsa_v.reshape(depth, nb_s, -1, HEAD_DIM)
    idx_k_t = jnp.swapaxes(cache_dsa_idx_k, 2, 3)
    w_in_t = jnp.swapaxes(w_in, 1, 2)

    tab_p = rope_tables(jnp.tile(jnp.arange(t_p), nb_p))
    tab_s = rope_tables(jnp.tile(past + jnp.arange(t_s), nb_s))

    xp = x_prompt.reshape(nb_p * t_p, d)
    xs = x_sample.reshape(nb_s * t_s, d)
    zero_state = (jnp.zeros((nb_p, CONV_W - 1, d_lru), F32), jnp.zeros((nb_p, d_lru), F32))
    p_new, s_new = [], []
    hp = normmod(xp, g_norm[0][None], mod[0, :nb_p], nb_p, t_p, 256 if t_p % 256 == 0 else t_p)
    hsm = normmod(xs, g_norm[0][None], mod[0, nb_p:], nb_s, t_s, 256 if t_s % 256 == 0 else t_s)
    for l in range(depth):
        lw = dict(w_in=w_in_t, w_tail=w_tail, conv_w=conv_w[l], conv_b=conv_b[l][None],
                  w_rg=w_rg_bf[l], b_rg=b_rg[l][None], w_ig=w_ig_bf[l], b_ig=b_ig[l][None],
                  lam=lru_lambda[l][None], g_branch=g_branch[l][None], w_out=w_out_bf)
        last = l == depth - 1
        g_next = g_final[None] if last else g_norm[l + 1][None]
        modp_next = None if last else mod[l + 1, :nb_p]
        mods_next = None if last else mod[l + 1, nb_p:]
        strip = bias_strip(rel_bias[l])
        xp, hp, pn = _layer_stream(xp, hp, mod[l, :nb_p], modp_next, g_next, lw, l, cols, nb_p, t_p, tab_p,
                                   zero_state, strip, True)
        st = (state_conv[l], state_lru[l], band_k, band_v, dsa_k, dsa_v, idx_k_t)
        xs, hsm, sn = _layer_stream(xs, hsm, mod[l, nb_p:], mods_next, g_next, lw, l, cols, nb_s, t_s, tab_s,
                                    st, strip, False)
        p_new.append(pn)
        s_new.append(sn)

    y_prompt = xp.reshape(nb_p, t_p, d)
    y_sample = xs.reshape(nb_s, t_s, d)
    p_out = [jnp.stack([t[j] for t in p_new], axis=0) for j in range(7)]
    s_out = [jnp.stack([t[j] for t in s_new], axis=0) for j in range(7)]
    return (y_prompt, y_sample, *p_out, *s_out)
```

```python
import functools
import struct

import jax
import jax.numpy as jnp
from jax import lax
from jax.experimental import pallas as pl
from jax.experimental.pallas import tpu as pltpu

F32 = jnp.float32
BF16 = jnp.bfloat16
I32 = jnp.int32

CHUNK = 64
HEAD_DIM = 128
LRU_BLOCKS = 8
CONV_W = 4
LRU_C = 8.0
BAND_PAST_CHUNKS = 8
BAND_PAST = BAND_PAST_CHUNKS * CHUNK
BAND_KEYS = BAND_PAST + CHUNK
REL_CLIP = 256
N_IDX_HEADS = 16
IDX_DIM = 64
TOPK_MAX = 256
ROPE_THETA = 500000.0
ROPE_FRAC = 4
EPS = 1e-6
NEG = -1e30
LANES = 128
SUBLANES = 8

INT_MIN = -(2 ** 31)


def _sortable_key_of(x):
    b = struct.unpack("<i", struct.pack("<f", x))[0]
    return b ^ 0x7FFFFFFF if b < 0 else b


KEY_HALF_NEG = _sortable_key_of(NEG * 0.5)

VMEM_LIMIT = 48 * 1024 * 1024
NT_DIMS = (((1,), (1,)), ((), ()))


def _params(n_grid):
    return pltpu.CompilerParams(dimension_semantics=("arbitrary",) * n_grid,
                                vmem_limit_bytes=VMEM_LIMIT)


def _dot_nt(a, b):
    return lax.dot_general(a, b, NT_DIMS, preferred_element_type=F32)


def _ada_kernel(c_ref, w_ref, b_ref, o_ref):
    c = c_ref[...]
    s = (c * jax.nn.sigmoid(c)).astype(BF16)
    o_ref[...] = jnp.dot(s, w_ref[...].astype(BF16), preferred_element_type=F32) + b_ref[...]


def ada_all(c_all, w_ada, b_ada):
    depth, d, n = w_ada.shape
    nb = c_all.shape[0]
    tn = 512
    return pl.pallas_call(
        _ada_kernel,
        grid=(depth, n // tn),
        in_specs=[pl.BlockSpec((nb, d), lambda l, j: (0, 0)),
                  pl.BlockSpec((None, d, tn), lambda l, j: (l, 0, j)),
                  pl.BlockSpec((None, 1, tn), lambda l, j: (l, 0, j))],
        out_specs=pl.BlockSpec((None, nb, tn), lambda l, j: (l, 0, j)),
        out_shape=jax.ShapeDtypeStruct((depth, nb, n), F32),
        compiler_params=_params(2),
        name="ada",
    )(c_all, w_ada, b_ada.reshape(depth, 1, n))


def _normmod_kernel(x_ref, g_ref, mod_ref, h_ref):
    x = x_ref[...]
    ms = jnp.mean(x * x, axis=-1, keepdims=True)
    y = x * lax.rsqrt(ms + EPS) * g_ref[...]
    shift = mod_ref[0:1, :]
    scale = mod_ref[1:2, :]
    h_ref[...] = (y * (1.0 + scale) + shift).astype(BF16)


def normmod(x, g, mod3, nseq, t, tb):
    rows, d = x.shape
    nblk = t // tb
    return pl.pallas_call(
        _normmod_kernel,
        grid=(nseq, nblk),
        in_specs=[pl.BlockSpec((tb, d), lambda b, i: (b * nblk + i, 0)),
                  pl.BlockSpec((1, d), lambda b, i: (0, 0)),
                  pl.BlockSpec((None, 3, d), lambda b, i: (b, 0, 0))],
        out_specs=pl.BlockSpec((tb, d), lambda b, i: (b * nblk + i, 0)),
        out_shape=jax.ShapeDtypeStruct((rows, d), BF16),
        compiler_params=_params(2),
        name="normmod",
    )(x, g, mod3)


def _mm_kernel(a_ref, b_ref, o_ref, *scratch):
    if scratch:
        wb_ref, = scratch

        @pl.when(pl.program_id(1) == 0)
        def _():
            wb_ref[...] = b_ref[...].T.astype(BF16)

        w = wb_ref[...]
    else:
        w = b_ref[...]
    o_ref[...] = jnp.dot(a_ref[...], w, preferred_element_type=F32)


def matmul(a, w, layer, ncols, tm, tn):
    m, k = a.shape
    if w.dtype == BF16:
        scratch = []
        w_spec = pl.BlockSpec((None, k, tn), lambda j, i: (layer, 0, j))
    else:
        scratch = [pltpu.VMEM((k, tn), BF16)]
        w_spec = pl.BlockSpec((None, tn, k), lambda j, i: (layer, j, 0))
    return pl.pallas_call(
        _mm_kernel,
        grid=(ncols // tn, m // tm),
        in_specs=[pl.BlockSpec((tm, k), lambda j, i: (i, 0)), w_spec],
        out_specs=pl.BlockSpec((tm, tn), lambda j, i: (i, j)),
        out_shape=jax.ShapeDtypeStruct((m, ncols), F32),
        scratch_shapes=scratch,
        compiler_params=_params(2),
        name="inproj",
    )(a, w)


def _rope_kernel(qc_ref, kc_ref, vc_ref, qi_ref, t_ref, c128_ref, s128_ref, c64_ref, s64_ref,
                 q_out, krot_out, kbf_out, qi_out, trot_out, kibf_out, *maybe_vt_out, n_dsa, n_idx):
    tm = qc_ref.shape[0]
    lane = lax.broadcasted_iota(I32, (tm, LANES), 1)
    c128 = c128_ref[...]
    s128 = s128_ref[...]
    c64 = c64_ref[...]
    s64 = s64_ref[...]
    half128 = HEAD_DIM // ROPE_FRAC // 2
    half64 = IDX_DIM // ROPE_FRAC // 2

    def rope128(x):
        partner = jnp.where(lane < half128, pltpu.roll(x, LANES - half128, 1), pltpu.roll(x, half128, 1))
        return x * c128 + partner * s128

    def rope64(x):
        partner = jnp.where((lane & (IDX_DIM - 1)) < half64,
                            pltpu.roll(x, LANES - half64, 1), pltpu.roll(x, half64, 1))
        return x * c64 + partner * s64

    for h in range(n_dsa):
        sl = slice(h * LANES, (h + 1) * LANES)
        q_out[:, sl] = rope128(qc_ref[:, sl]).astype(BF16)
        kr = rope128(kc_ref[:, sl])
        krot_out[:, sl] = kr
        kbf_out[:, sl] = kr.astype(BF16)
    if maybe_vt_out:
        maybe_vt_out[0][...] = vc_ref[...].T.astype(BF16)
    low = lane < IDX_DIM
    for j in range(n_idx // 2):
        r = rope64(qi_ref[:, j * LANES:(j + 1) * LANES]) * (IDX_DIM ** -0.5)
        qi_out[:, (2 * j) * LANES:(2 * j + 1) * LANES] = jnp.where(low, r, 0.0).astype(BF16)
        qi_out[:, (2 * j + 1) * LANES:(2 * j + 2) * LANES] = jnp.where(
            low, pltpu.roll(r, IDX_DIM, 1), 0.0).astype(BF16)
    t = t_ref[...]
    r = rope64(t)
    trot_out[...] = jnp.where(low, r, jnp.where(lane < IDX_DIM + N_IDX_HEADS, t * (N_IDX_HEADS ** -0.5), 0.0))
    kibf_out[...] = jnp.where(low, r, 0.0).astype(BF16)


def rope_call(u, tail, tables, cols, tm, v_t):
    rows = u.shape[0]
    d_dsa = cols["d_dsa"]
    n_dsa = d_dsa // HEAD_DIM
    d_qi = N_IDX_HEADS * IDX_DIM
    c128, s128, c64, s64 = tables
    tab_spec = pl.BlockSpec((tm, LANES), lambda i: (i, 0))
    blk = lambda off: pl.BlockSpec((tm, d_dsa), lambda i, o=off // d_dsa: (i, o))
    row_spec = pl.BlockSpec((tm, d_dsa), lambda i: (i, 0))
    out_specs = [row_spec, row_spec, row_spec, pl.BlockSpec((tm, N_IDX_HEADS * LANES), lambda i: (i, 0)),
                 tab_spec, tab_spec]
    out_shape = [jax.ShapeDtypeStruct((rows, d_dsa), BF16),
                 jax.ShapeDtypeStruct((rows, d_dsa), F32),
                 jax.ShapeDtypeStruct((rows, d_dsa), BF16),
                 jax.ShapeDtypeStruct((rows, N_IDX_HEADS * LANES), BF16),
                 jax.ShapeDtypeStruct((rows, LANES), F32),
                 jax.ShapeDtypeStruct((rows, LANES), BF16)]
    if v_t:
        out_specs.append(pl.BlockSpec((d_dsa, tm), lambda i: (0, i)))
        out_shape.append(jax.ShapeDtypeStruct((d_dsa, rows), BF16))
    return pl.pallas_call(
        functools.partial(_rope_kernel, n_dsa=n_dsa, n_idx=N_IDX_HEADS),
        grid=(rows // tm,),
        in_specs=[blk(cols["qc"]), blk(cols["kc"]), blk(cols["vc"]),
                  pl.BlockSpec((tm, d_qi), lambda i, o=cols["qi"] // d_qi: (i, o)),
                  tab_spec, tab_spec, tab_spec, tab_spec, tab_spec],
        out_specs=out_specs,
        out_shape=out_shape,
        compiler_params=_params(1),
        name="rope",
    )(u, u, u, u, tail, c128, s128, c64, s64)


def rope_tables(pos):
    pos = pos.astype(F32)[:, None]
    n = pos.shape[0]

    def tab(dim):
        half = dim // ROPE_FRAC // 2
        inv = ROPE_THETA ** (-jnp.arange(half, dtype=F32) / half)
        ang = pos * inv[None]
        cos, sin = jnp.cos(ang), jnp.sin(ang)
        c = jnp.concatenate([cos, cos, jnp.ones((n, dim - 2 * half), F32)], axis=1)
        s = jnp.concatenate([-sin, sin, jnp.zeros((n, dim - 2 * half), F32)], axis=1)
        return jnp.tile(c, (1, LANES // dim)), jnp.tile(s, (1, LANES // dim))

    c128, s128 = tab(HEAD_DIM)
    c64, s64 = tab(IDX_DIM)
    return c128, s128, c64, s64


def _lru_kernel(xa_ref, cs_ref, h0_ref, cw_ref, cb_ref, wrg_ref, brg_ref, wig_ref, big_ref, lam_ref,
                hs_ref, ext_ref, hc_ref):
    tb, d = xa_ref.shape
    blk = d // LRU_BLOCKS

    @pl.when(pl.program_id(1) == 0)
    def _():
        ext_ref[0:8, :] = cs_ref[...]
        hc_ref[...] = h0_ref[...]

    ext_ref[8:8 + tb, :] = xa_ref[...]
    conv = ext_ref[5:5 + tb, :] * cw_ref[0:1, :]
    for j in range(1, CONV_W):
        conv = conv + ext_ref[5 + j:5 + j + tb, :] * cw_ref[j:j + 1, :]
    conv = conv + cb_ref[...]
    tail = ext_ref[tb:tb + 8, :]
    ext_ref[0:8, :] = tail

    xb = conv.astype(BF16)
    rs, gs = [], []
    for g in range(LRU_BLOCKS):
        xg = xb[:, g * blk:(g + 1) * blk]
        rs.append(jnp.dot(xg, wrg_ref[g], preferred_element_type=F32))
        gs.append(jnp.dot(xg, wig_ref[g], preferred_element_type=F32))
    r = jax.nn.sigmoid(jnp.concatenate(rs, axis=1) + brg_ref[...])
    ig = jax.nn.sigmoid(jnp.concatenate(gs, axis=1) + big_ref[...])
    lam = lam_ref[...]
    softplus_neg_lam = jnp.maximum(-lam, 0.0) + jnp.log1p(jnp.exp(-jnp.abs(lam)))
    log_a = (-LRU_C) * r * softplus_neg_lam
    a = jnp.exp(log_a)
    u = jnp.sqrt(-jnp.tanh(log_a) * (a * a + 1.0)) * (ig * conv)

    row = lax.broadcasted_iota(I32, (tb, d), 0)
    s = 1
    while s < tb:
        keep = row >= s
        a_prev = jnp.where(keep, pltpu.roll(a, s, 0), 1.0)
        u_prev = jnp.where(keep, pltpu.roll(u, s, 0), 0.0)
        u = a * u_prev + u
        a = a * a_prev
        s *= 2
    h = a * hc_ref[...] + u
    hs_ref[...] = h
    hc_ref[...] = h[tb - 1:tb, :]


def lru_call(u, cs8, h0, cw, cb, wrg, brg, wig, big, lam, nseq, t, tb):
    rows = u.shape[0]
    d = cw.shape[1]
    blk = d // LRU_BLOCKS
    nblk = t // tb
    vec = pl.BlockSpec((1, d), lambda b, i: (0, 0))
    wspec = pl.BlockSpec((LRU_BLOCKS, blk, blk), lambda b, i: (0, 0, 0))
    return pl.pallas_call(
        _lru_kernel,
        grid=(nseq, nblk),
        in_specs=[pl.BlockSpec((tb, d), lambda b, i: (b * nblk + i, 0)),
                  pl.BlockSpec((None, 8, d), lambda b, i: (b, 0, 0)),
                  pl.BlockSpec((None, 1, d), lambda b, i: (b, 0, 0)),
                  pl.BlockSpec((CONV_W, d), lambda b, i: (0, 0)),
                  vec, wspec, vec, wspec, vec, vec],
        out_specs=pl.BlockSpec((tb, d), lambda b, i: (b * nblk + i, 0)),
        out_shape=jax.ShapeDtypeStruct((rows, d), F32),
        scratch_shapes=[pltpu.VMEM((tb + 8, d), F32), pltpu.VMEM((1, d), F32)],
        compiler_params=_params(2),
        name="lru",
    )(u, cs8, h0, cw, cb, wrg, brg, wig, big, lam)


def _bias_kernel(relb_ref, o_ref):
    h = pl.program_id(0)
    qi = lax.broadcasted_iota(I32, (CHUNK, BAND_KEYS), 0)
    kj = lax.broadcasted_iota(I32, (CHUNK, BAND_KEYS), 1)
    idx = jnp.clip(qi + BAND_PAST - kj, -REL_CLIP, REL_CLIP) + REL_CLIP

    def body(r, acc):
        return jnp.where(idx == r, relb_ref[h, r], acc)

    o_ref[...] = lax.fori_loop(0, 2 * REL_CLIP + 1, body, jnp.zeros((CHUNK, BAND_KEYS), F32))


def bias_strip(relb):
    nh = relb.shape[0]
    return pl.pallas_call(
        _bias_kernel,
        grid=(nh,),
        in_specs=[pl.BlockSpec(memory_space=pltpu.SMEM)],
        out_specs=pl.BlockSpec((None, CHUNK, BAND_KEYS), lambda h: (h, 0, 0)),
        out_shape=jax.ShapeDtypeStruct((nh, CHUNK, BAND_KEYS), F32),
        compiler_params=_params(1),
        name="bias_strip",
    )(relb)


def _band_chunk(q, kwin, vwin, strip, min_col):
    s = _dot_nt(q, kwin) * (HEAD_DIM ** -0.5) + strip
    col = lax.broadcasted_iota(I32, s.shape, 1)
    s = jnp.where(col >= min_col, s, NEG)
    m = jnp.max(s, axis=-1, keepdims=True)
    p = jnp.exp(s - m)
    l = jnp.sum(p, axis=-1, keepdims=True)
    return jnp.dot(p.astype(BF16), vwin, preferred_element_type=F32) / l


def _band_prompt_kernel(q_ref, kp_ref, kc_ref, vp_ref, vc_ref, strip_ref, o_ref, kw_ref, vw_ref):
    i = pl.program_id(2)
    tb = q_ref.shape[0]
    kw_ref[0:tb, :] = kp_ref[...].astype(BF16)
    kw_ref[tb:2 * tb, :] = kc_ref[...].astype(BF16)
    vw_ref[0:tb, :] = vp_ref[...].astype(BF16)
    vw_ref[tb:2 * tb, :] = vc_ref[...].astype(BF16)
    strip = strip_ref[...]
    col = lax.broadcasted_iota(I32, (CHUNK, BAND_KEYS), 1)
    chunks = range(tb // CHUNK)
    scores = []
    for a in chunks:
        q = q_ref[a * CHUNK:(a + 1) * CHUNK, :].astype(BF16)
        s = _dot_nt(q, kw_ref[a * CHUNK:a * CHUNK + BAND_KEYS, :]) * (HEAD_DIM ** -0.5) + strip
        min_col = jnp.where(i > 0, 0, (BAND_PAST_CHUNKS - a) * CHUNK)
        scores.append(jnp.where(col >= min_col, s, NEG))
    maxes = [jnp.max(s, axis=-1, keepdims=True) for s in scores]
    probs = [jnp.exp(s - m) for s, m in zip(scores, maxes)]
    sums = [jnp.sum(p, axis=-1, keepdims=True) for p in probs]
    for a in chunks:
        pv = jnp.dot(probs[a].astype(BF16), vw_ref[a * CHUNK:a * CHUNK + BAND_KEYS, :],
                     preferred_element_type=F32)
        o_ref[a * CHUNK:(a + 1) * CHUNK, :] = pv / sums[a]


def band_prompt(u, strip, cols, nseq, t):
    rows = u.shape[0]
    tb = BAND_PAST
    nblk = t // tb
    nh = strip.shape[0]
    qo, ko, vo = cols["qb"] // HEAD_DIM, cols["kb"] // HEAD_DIM, cols["vb"] // HEAD_DIM
    cur = lambda off: pl.BlockSpec((tb, HEAD_DIM), lambda b, h, i: (b * nblk + i, off + h))
    prev = lambda off: pl.BlockSpec((tb, HEAD_DIM), lambda b, h, i: (b * nblk + jnp.maximum(i - 1, 0), off + h))
    return pl.pallas_call(
        _band_prompt_kernel,
        grid=(nseq, nh, nblk),
        in_specs=[cur(qo), prev(ko), cur(ko), prev(vo), cur(vo),
                  pl.BlockSpec((None, CHUNK, BAND_KEYS), lambda b, h, i: (h, 0, 0))],
        out_specs=pl.BlockSpec((tb, HEAD_DIM), lambda b, h, i: (b * nblk + i, h)),
        out_shape=jax.ShapeDtypeStruct((rows, nh * HEAD_DIM), F32),
        scratch_shapes=[pltpu.VMEM((2 * tb, HEAD_DIM), BF16), pltpu.VMEM((2 * tb, HEAD_DIM), BF16)],
        compiler_params=_params(3),
        name="band_prompt",
    )(u, u, u, u, u, strip)


def _band_sample_kernel(q_ref, kn_ref, vn_ref, ck_ref, cv_ref, strip_ref, o_ref, kw_ref, vw_ref):
    nh = strip_ref.shape[0]
    w = ck_ref.shape[0] // nh
    t = q_ref.shape[0]
    kw_ref[w:w + t, :] = kn_ref[...].astype(BF16)
    vw_ref[w:w + t, :] = vn_ref[...].astype(BF16)
    for h in range(nh):
        sl = slice(h * HEAD_DIM, (h + 1) * HEAD_DIM)
        kw_ref[0:w, sl] = ck_ref[pl.ds(h, w, stride=nh), :].astype(BF16)
        vw_ref[0:w, sl] = cv_ref[pl.ds(h, w, stride=nh), :].astype(BF16)
        o_ref[:, sl] = _band_chunk(q_ref[:, sl].astype(BF16), kw_ref[:, sl], vw_ref[:, sl], strip_ref[h], 0)


def band_sample(u, ck, cv, layer, strip, cols, nseq, t):
    rows = u.shape[0]
    nh = strip.shape[0]
    d = nh * HEAD_DIM
    w = ck.shape[2] // nh
    ublk = lambda off: pl.BlockSpec((t, d), lambda b, o=off // d: (b, o))
    cblk = pl.BlockSpec((None, None, w * nh, HEAD_DIM), lambda b: (layer, b, 0, 0))
    return pl.pallas_call(
        _band_sample_kernel,
        grid=(nseq,),
        in_specs=[ublk(cols["qb"]), ublk(cols["kb"]), ublk(cols["vb"]), cblk, cblk,
                  pl.BlockSpec((nh, CHUNK, BAND_KEYS), lambda b: (0, 0, 0))],
        out_specs=pl.BlockSpec((t, d), lambda b: (b, 0)),
        out_shape=jax.ShapeDtypeStruct((rows, d), F32),
        scratch_shapes=[pltpu.VMEM((w + t, d), BF16), pltpu.VMEM((w + t, d), BF16)],
        compiler_params=_params(1),
        name="band_sample",
    )(u, u, u, ck, cv, strip)


def _loop(n, body, init):
    if isinstance(n, int):
        val = init
        for k in range(n):
            val = body(k, val)
        return val
    return lax.fori_loop(0, n, body, init)


IDX_BITS = 16


def _f32_to_key(x):
    bits = lax.bitcast_convert_type(x, I32)
    return jnp.where(bits < 0, bits ^ 0x7FFFFFFF, bits)


def _dsa_core(qi_ref, t_ref, q_ref, segs, store_out, sk_refs, s_refs, jstar_ref, mpart_ref, lpart_ref, acc_ref, *,
              topk, q_chunk0, n_q):
    qb = q_ref.shape[0]
    n_dsa = q_ref.shape[1] // HEAD_DIM
    t_t = t_ref[...].T
    wrows = [t_t[IDX_DIM + h:IDX_DIM + h + 1, :] for h in range(N_IDX_HEADS)]

    def tiles(x):
        return x.reshape(x.shape[0] // SUBLANES, SUBLANES, qb)

    for seg, sk_ref in zip(segs, sk_refs):
        kblk = seg["kblk"]
        krow = lax.broadcasted_iota(I32, (kblk, qb), 0)
        qcol = lax.broadcasted_iota(I32, (kblk, qb), 1)
        qchunk = q_chunk0 + qcol // CHUNK
        live = (krow < seg["valid"]) & (qcol < n_q)

        def score_blk(kb, carry, seg=seg, sk_ref=sk_ref, kblk=kblk, krow=krow, qchunk=qchunk, live=live):
            kib = seg["ki"](kb)
            acc = jnp.zeros((kblk, qb), F32)
            for h in range(N_IDX_HEADS):
                sc = _dot_nt(kib, qi_ref[:, h * LANES:(h + 1) * LANES])
                acc = acc + jnp.maximum(sc, 0.0) * wrows[h]
            kchunk = (seg["pos0"] + kb * kblk + krow) // CHUNK
            key = _f32_to_key(jnp.where(kchunk <= qchunk, acc, NEG))
            sk_ref[kb] = tiles(jnp.where(live, key, INT_MIN))
            return carry

        _loop(seg["nblk"], score_blk, 0)

    def count(pred):
        part = jnp.zeros((SUBLANES, qb), I32)
        for seg, sk_ref in zip(segs, sk_refs):
            n_t = seg["kblk"] // SUBLANES
            off = (lax.broadcasted_iota(I32, (n_t, SUBLANES, qb), 0) * SUBLANES
                   + lax.broadcasted_iota(I32, (n_t, SUBLANES, qb), 1))

            def body(kb, part, seg=seg, sk_ref=sk_ref, off=off):
                idx = seg["pos0"] + kb * seg["kblk"] + off
                return part + jnp.sum(jnp.where(pred(sk_ref[kb], idx), 1, 0), axis=0)

            part = _loop(seg["nblk"], body, part)
        return jnp.broadcast_to(jnp.sum(part, axis=0, keepdims=True), (SUBLANES, qb))

    zero = jnp.zeros((SUBLANES, qb), I32)
    thr = jnp.where(count(lambda key, idx: key >= zero) >= topk, 0, INT_MIN).astype(I32)

    def bit_body(it, t):
        cand = t + jnp.left_shift(jnp.int32(1), 30 - it)
        return jnp.where(count(lambda key, idx: key >= cand) >= topk, cand, t)

    thr = lax.fori_loop(0, 31, bit_body, thr)
    c_lo = count(lambda key, idx: key >= thr)
    c_hi = count(lambda key, idx: key > thr)
    lo = jnp.maximum(thr, KEY_HALF_NEG + 1)

    big = 2 ** IDX_BITS
    jstar_ref[...] = jnp.full((SUBLANES, qb), big, I32)
    surplus = (c_lo > topk) & (thr > KEY_HALF_NEG)
    need = topk - c_hi

    @pl.when(jnp.max(jnp.where(surplus, 1, 0)) > 0)
    def _():
        def idx_body(it, j):
            cand = j + jnp.left_shift(jnp.int32(1), IDX_BITS - 1 - it)
            f = count(lambda key, idx: (key == lo) & (idx < cand))
            return jnp.where(f <= need, cand, j)

        j = lax.fori_loop(0, IDX_BITS, idx_body, zero)
        jstar_ref[...] = jnp.where(surplus, j, big)

    lo_row = lo[0:1, :]
    jstar_row = jstar_ref[0:1, :]

    mpart_ref[...] = jnp.full(mpart_ref.shape, NEG, F32)
    lpart_ref[...] = jnp.zeros(lpart_ref.shape, F32)
    acc_ref[...] = jnp.zeros(acc_ref.shape, F32)

    for seg, sk_ref, s_ref in zip(segs, sk_refs, s_refs):
        kblk = seg["kblk"]
        krow = lax.broadcasted_iota(I32, (kblk, qb), 0)

        def pass_a(kb, carry, seg=seg, sk_ref=sk_ref, s_ref=s_ref, kblk=kblk, krow=krow):
            key = sk_ref[kb].reshape(kblk, qb)
            idx = seg["pos0"] + kb * kblk + krow
            mask = (key >= lo_row) & ((key > lo_row) | (idx < jstar_row))
            for h in range(n_dsa):
                s = _dot_nt(seg["k"](kb, h), q_ref[:, h * HEAD_DIM:(h + 1) * HEAD_DIM])
                s = jnp.where(mask, s * (HEAD_DIM ** -0.5), NEG)
                s_ref[h, kb] = s
                mpart_ref[h] = jnp.maximum(mpart_ref[h], jnp.max(tiles(s), axis=0))
            return carry

        _loop(seg["nblk"], pass_a, 0)

    m_rows = [jnp.max(mpart_ref[h], axis=0, keepdims=True) for h in range(n_dsa)]

    for seg, s_ref in zip(segs, s_refs):
        def pass_b(kb, carry, seg=seg, s_ref=s_ref):
            for h in range(n_dsa):
                p = jnp.exp(s_ref[h, kb] - m_rows[h])
                lpart_ref[h] += jnp.sum(tiles(p), axis=0)
                acc_ref[h] += jnp.dot(seg["vt"](kb, h), p.astype(BF16), preferred_element_type=F32)
            return carry

        _loop(seg["nblk"], pass_b, 0)

    for h in range(n_dsa):
        l = jnp.sum(lpart_ref[h], axis=0, keepdims=True)
        store_out(h, (acc_ref[h] / l).T)


def _blk_start(kb, kblk):
    return kb * kblk if isinstance(kb, int) else pl.multiple_of(kb * kblk, kblk)


def _dsa_scratch(qb, seg_shapes, n_dsa):
    return ([pltpu.VMEM((nblk, kblk // SUBLANES, SUBLANES, qb), I32) for nblk, kblk in seg_shapes]
            + [pltpu.VMEM((n_dsa, nblk, kblk, qb), F32) for nblk, kblk in seg_shapes]
            + [pltpu.VMEM((SUBLANES, qb), I32),
               pltpu.VMEM((n_dsa, SUBLANES, qb), F32), pltpu.VMEM((n_dsa, SUBLANES, qb), F32),
               pltpu.VMEM((n_dsa, HEAD_DIM, qb), F32)])


def _dsa_prompt_kernel(qi_ref, t_ref, q_ref, ki_ref, k_ref, vt_ref, o_ref, sk_ref, s_ref, jstar_ref,
                       mpart_ref, lpart_ref, acc_ref, *, kblk, topk):
    i = pl.program_id(1)
    qb = q_ref.shape[0]
    nblk = (i * qb + qb + kblk - 1) // kblk

    def head(h):
        return slice(h * HEAD_DIM, (h + 1) * HEAD_DIM)

    seg = dict(ki=lambda kb: ki_ref[pl.ds(_blk_start(kb, kblk), kblk), :],
               k=lambda kb, h: k_ref[pl.ds(_blk_start(kb, kblk), kblk), head(h)],
               vt=lambda kb, h: vt_ref[head(h), pl.ds(_blk_start(kb, kblk), kblk)],
               nblk=nblk, kblk=kblk, pos0=0, valid=kblk)

    def store_out(h, x):
        o_ref[:, head(h)] = x

    _dsa_core(qi_ref, t_ref, q_ref, [seg], store_out, [sk_ref], [s_ref], jstar_ref, mpart_ref, lpart_ref,
              acc_ref, topk=topk, q_chunk0=(i * qb) // CHUNK, n_q=qb)


def dsa_prompt(qi_pad, trot, q_bf, ki_bf, k_bf, v_t, nseq, t, qb, kblk, topk):
    rows, d = q_bf.shape
    n_dsa = d // HEAD_DIM
    nq = t // qb
    row_blk = lambda w: pl.BlockSpec((qb, w), lambda b, i: (b * nq + i, 0))
    seq_blk = lambda w: pl.BlockSpec((t, w), lambda b, i: (b, 0))
    return pl.pallas_call(
        functools.partial(_dsa_prompt_kernel, kblk=kblk, topk=topk),
        grid=(nseq, nq),
        in_specs=[row_blk(qi_pad.shape[1]), row_blk(LANES), row_blk(d),
                  seq_blk(LANES), seq_blk(d), pl.BlockSpec((d, t), lambda b, i: (0, b))],
        out_specs=row_blk(d),
        out_shape=jax.ShapeDtypeStruct((rows, d), F32),
        scratch_shapes=_dsa_scratch(qb, [(t // kblk, kblk)], n_dsa),
        compiler_params=_params(2),
        name="dsa_prompt",
    )(qi_pad, trot, q_bf, ki_bf, k_bf, v_t)


def _dsa_sample_kernel(qi_ref, t_ref, q_ref, kin_ref, kn_ref, vn_ref, cki_ref, ck_ref, cv_ref, o_ref,
                       qi_s, t_s, q_s, kin_s, kn_s, vn_s, kit_s, sk0_ref, sk1_ref, s0_ref, s1_ref, jstar_ref,
                       mpart_ref, lpart_ref, acc_ref, *, kblk, topk):
    past = cki_ref.shape[1]
    tq = q_ref.shape[0]
    qb = q_s.shape[0]
    n_dsa = q_ref.shape[1] // HEAD_DIM

    for dst, src in ((qi_s, qi_ref), (t_s, t_ref), (q_s, q_ref), (kin_s, kin_ref), (kn_s, kn_ref), (vn_s, vn_ref)):
        dst[0:tq, :] = src[...]
        dst[tq:, :] = jnp.zeros((qb - tq, dst.shape[1]), dst.dtype)
    kit_s[IDX_DIM:, :] = jnp.zeros((kit_s.shape[0] - IDX_DIM, kblk), F32)

    def head(h):
        return slice(h * HEAD_DIM, (h + 1) * HEAD_DIM)

    def cache_rows(ref, kb, h):
        return ref[pl.ds(kb * kblk * n_dsa + h, kblk, stride=n_dsa), :]

    def cache_ki(kb):
        kit_s[0:IDX_DIM, :] = cki_ref[:, pl.ds(kb * kblk, kblk)]
        return kit_s[...].T.astype(BF16)

    segs = [dict(ki=cache_ki,
                 k=lambda kb, h: cache_rows(ck_ref, kb, h).astype(BF16),
                 vt=lambda kb, h: cache_rows(cv_ref, kb, h).T.astype(BF16),
                 nblk=past // kblk, kblk=kblk, pos0=0, valid=kblk),
            dict(ki=lambda kb: kin_s[...],
                 k=lambda kb, h: kn_s[:, head(h)],
                 vt=lambda kb, h: vn_s[:, head(h)].T.astype(BF16),
                 nblk=1, kblk=qb, pos0=past, valid=tq)]

    def store_out(h, x):
        o_ref[:, head(h)] = x[0:tq, :]

    _dsa_core(qi_s, t_s, q_s, segs, store_out, [sk0_ref, sk1_ref], [s0_ref, s1_ref], jstar_ref,
              mpart_ref, lpart_ref, acc_ref, topk=topk, q_chunk0=past // CHUNK, n_q=tq)


def dsa_sample(qi_pad, trot, q_bf, ki_bf, k_bf, u, vc_col, cki, ck, cv, layer, nseq, t, kblk, topk):
    rows, d = q_bf.shape
    n_dsa = d // HEAD_DIM
    past = cki.shape[3]
    qb = LANES
    row_blk = lambda w, o=0: pl.BlockSpec((t, w), lambda b, o=o: (b, o))
    cache_blk = lambda r, w: pl.BlockSpec((None, None, r, w), lambda b: (layer, b, 0, 0))
    pad_scratch = [pltpu.VMEM((qb, qi_pad.shape[1]), BF16), pltpu.VMEM((qb, LANES), F32), pltpu.VMEM((qb, d), BF16),
                   pltpu.VMEM((qb, LANES), BF16), pltpu.VMEM((qb, d), BF16), pltpu.VMEM((qb, d), F32),
                   pltpu.VMEM((LANES, kblk), F32)]
    return pl.pallas_call(
        functools.partial(_dsa_sample_kernel, kblk=kblk, topk=topk),
        grid=(nseq,),
        in_specs=[row_blk(qi_pad.shape[1]), row_blk(LANES), row_blk(d),
                  row_blk(LANES), row_blk(d), row_blk(d, vc_col),
                  cache_blk(cki.shape[2], past), cache_blk(past * n_dsa, HEAD_DIM),
                  cache_blk(past * n_dsa, HEAD_DIM)],
        out_specs=row_blk(d),
        out_shape=jax.ShapeDtypeStruct((rows, d), F32),
        scratch_shapes=pad_scratch + _dsa_scratch(qb, [(past // kblk, kblk), (1, qb)], n_dsa),
        compiler_params=_params(1),
        name="dsa_sample",
    )(qi_pad, trot, q_bf, ki_bf, k_bf, u, cki, ck, cv)


def _outproj_kernel(oa_ref, ob_ref, oc_ref, za_ref, zb_ref, zc_ref, g_ref, w_ref, x_ref, mod_ref, g2_ref,
                    *rest, d_lru, d_band, last):
    if last:
        yo_ref, y_ref = rest
    else:
        mod2_ref, xo_ref, h_ref, y_ref = rest

    def branch(o_ref, z_ref, lo, hi):
        o = o_ref[...]
        z = z_ref[...]
        ms = jnp.mean(o * o, axis=-1, keepdims=True)
        y = o * lax.rsqrt(ms + EPS) * g_ref[:, lo:hi]
        y_ref[:, lo:hi] = (y * (z * jax.nn.sigmoid(z))).astype(BF16)

    d_mix = y_ref.shape[1]
    branch(oa_ref, za_ref, 0, d_lru)
    branch(ob_ref, zb_ref, d_lru, d_lru + d_band)
    branch(oc_ref, zc_ref, d_lru + d_band, d_mix)
    out = jnp.dot(y_ref[...], w_ref[...], preferred_element_type=F32)
    x_new = x_ref[...] + mod_ref[2:3, :] * out
    ms = jnp.mean(x_new * x_new, axis=-1, keepdims=True)
    normed = x_new * lax.rsqrt(ms + EPS) * g2_ref[...]
    if last:
        yo_ref[...] = normed
    else:
        xo_ref[...] = x_new
        h_ref[...] = (normed * (1.0 + mod2_ref[1:2, :]) + mod2_ref[0:1, :]).astype(BF16)


def outproj(oa, ob, oc, u, g_branch, w_out, layer, x, mod3, g2, mod3_next, cols, nseq, t, tm):
    rows, d = x.shape
    d_lru, d_band, d_dsa = oa.shape[1], ob.shape[1], oc.shape[1]
    d_mix = d_lru + d_band + d_dsa
    nblk = t // tm
    last = mod3_next is None
    rb = lambda w, o=0: pl.BlockSpec((tm, w), lambda b, i, o=o: (b * nblk + i, o))
    vec = pl.BlockSpec((1, d), lambda b, i: (0, 0))
    mod_spec = pl.BlockSpec((None, 3, d), lambda b, i: (b, 0, 0))
    in_specs = [rb(d_lru), rb(d_band), rb(d_dsa),
                rb(d_lru, cols["za"] // d_lru), rb(d_band, cols["zb"] // d_band), rb(d_dsa, cols["zc"] // d_dsa),
                pl.BlockSpec((1, d_mix), lambda b, i: (0, 0)),
                pl.BlockSpec((None, d_mix, d), lambda b, i: (layer, 0, 0)),
                rb(d), mod_spec, vec]
    args = [oa, ob, oc, u, u, u, g_branch, w_out, x, mod3, g2]
    if last:
        out_specs = rb(d)
        out_shape = jax.ShapeDtypeStruct((rows, d), F32)
    else:
        in_specs.append(mod_spec)
        args.append(mod3_next)
        out_specs = [rb(d), rb(d)]
        out_shape = [jax.ShapeDtypeStruct((rows, d), F32), jax.ShapeDtypeStruct((rows, d), BF16)]
    return pl.pallas_call(
        functools.partial(_outproj_kernel, d_lru=d_lru, d_band=d_band, last=last),
        grid=(nseq, nblk),
        in_specs=in_specs,
        out_specs=out_specs,
        out_shape=out_shape,
        scratch_shapes=[pltpu.VMEM((tm, d_mix), BF16)],
        compiler_params=_params(2),
        name="outproj",
    )(*args)


def _column_offsets(d_lru, d_band, d_dsa):
    names = ["xa", "za", "qb", "kb", "vb", "zb", "qc", "kc", "vc", "zc", "qi"]
    widths = [d_lru, d_lru, d_band, d_band, d_band, d_band, d_dsa, d_dsa, d_dsa, d_dsa, N_IDX_HEADS * IDX_DIM]
    cols, off = {}, 0
    for n, w in zip(names, widths):
        cols[n] = off
        off += w
    cols["main"] = off
    cols["d_lru"], cols["d_band"], cols["d_dsa"] = d_lru, d_band, d_dsa
    return cols


def _layer_stream(x, h, mod3, mod3_next, g_next, lw, layer, cols, nseq, t, tables, state, strip, prompt):
    d_lru, d_band, d_dsa = cols["d_lru"], cols["d_band"], cols["d_dsa"]
    n_band, n_dsa = d_band // HEAD_DIM, d_dsa // HEAD_DIM
    rows = nseq * t
    tb = 256 if t % 256 == 0 else t
    tm = 512 if rows % 512 == 0 else rows
    u = matmul(h, lw["w_in"], layer, cols["main"], tm, 1024)
    tail = matmul(h, lw["w_tail"], layer, LANES, tm, LANES)

    q_bf, k_rot, k_bf, qi_pad, t_rot, ki_bf, *maybe_vt = rope_call(u, tail, tables, cols, tb, prompt)

    conv_s, lru_s = state[0], state[1]
    cs8 = jnp.concatenate([jnp.zeros((nseq, 8 - (CONV_W - 1), d_lru), F32), conv_s], axis=1)
    hs = lru_call(u, cs8, lru_s.reshape(nseq, 1, d_lru), lw["conv_w"], lw["conv_b"], lw["w_rg"], lw["b_rg"],
                  lw["w_ig"], lw["b_ig"], lw["lam"], nseq, t, tb)

    if prompt:
        ob = band_prompt(u, strip, cols, nseq, t)
        oc = dsa_prompt(qi_pad, t_rot, q_bf, ki_bf, k_bf, maybe_vt[0], nseq, t, 128, 512, min(TOPK_MAX, t // 4))
    else:
        bk, bv, dk, dv, dik = state[2:]
        past = dik.shape[3]
        assert (past + t - 1) // CHUNK <= past // CHUNK and t == CHUNK and bk.shape[2] == BAND_PAST * n_band
        ob = band_sample(u, bk, bv, layer, strip, cols, nseq, t)
        oc = dsa_sample(qi_pad, t_rot, q_bf, ki_bf, k_bf, u, cols["vc"] // d_dsa, dik, dk, dv, layer,
                        nseq, t, 512, min(TOPK_MAX, (past + t) // 4))

    res = outproj(hs, ob, oc, u, lw["g_branch"], lw["w_out"], layer, x, mod3, g_next, mod3_next, cols, nseq, t, tb)
    x_new, h_next = (res, None) if mod3_next is None else res

    u3 = u.reshape(nseq, t, -1)
    xa = u3[:, :, cols["xa"]:cols["xa"] + d_lru]
    if prompt:
        new_conv = xa[:, t - (CONV_W - 1):]
    else:
        new_conv = jnp.concatenate([conv_s, xa], axis=1)[:, -(CONV_W - 1):]
    nbr = min(BAND_PAST, t)
    new = (new_conv,
           hs.reshape(nseq, t, d_lru)[:, -1],
           u3[:, t - nbr:, cols["kb"]:cols["kb"] + d_band].reshape(nseq, nbr, n_band, HEAD_DIM),
           u3[:, t - nbr:, cols["vb"]:cols["vb"] + d_band].reshape(nseq, nbr, n_band, HEAD_DIM),
           k_rot.reshape(nseq, t, n_dsa, HEAD_DIM),
           u3[:, :, cols["vc"]:cols["vc"] + d_dsa].reshape(nseq, t, n_dsa, HEAD_DIM),
           t_rot.reshape(nseq, t, LANES)[:, :, :IDX_DIM])
    return x_new, h_next, new


def kernel(x_prompt, x_sample, c_prompt, c_sample, state_conv, state_lru, cache_band_k, cache_band_v,
           cache_dsa_k, cache_dsa_v, cache_dsa_idx_k, g_norm, w_ada, b_ada, w_in, conv_w, conv_b,
           w_rg, b_rg, w_ig, b_ig, lru_lambda, rel_bias, g_branch, w_out, g_final):
    depth = w_in.shape[0]
    nb_p, t_p, d = x_prompt.shape
    nb_s, t_s, _ = x_sample.shape
    past = cache_dsa_k.shape[2]
    d_lru = conv_w.shape[2]
    d_band = cache_band_k.shape[3] * HEAD_DIM
    d_dsa = cache_dsa_k.shape[3] * HEAD_DIM
    cols = _column_offsets(d_lru, d_band, d_dsa)
    n_main = cols["main"]

    mod = ada_all(jnp.concatenate([c_prompt, c_sample], axis=0), w_ada, b_ada)
    mod = mod.reshape(depth, nb_p + nb_s, 3, d)

    w_tail = jnp.pad(w_in[:, :, n_main:], ((0, 0), (0, 0), (0, LANES - (w_in.shape[2] - n_main)))).astype(BF16)
    w_out_bf = w_out.astype(BF16)
    w_rg_bf = w_rg.astype(BF16)
    w_ig_bf = w_ig.astype(BF16)

    band_k = cache_band_k.reshape(depth, nb_s, -1, HEAD_DIM)
    band_v = cache_band_v.reshape(depth, nb_s, -1, HEAD_DIM)
    dsa_k = cache_dsa_k.reshape(depth, nb_s, -1, HEAD_DIM)
    dsa_v = cache_dsa_v.reshape(depth, nb_s, -1, HEAD_DIM)
    idx_k_t = jnp.swapaxes(cache_dsa_idx_k, 2, 3)
    w_in_t = jnp.swapaxes(w_in, 1, 2)

    tab_p = rope_tables(jnp.tile(jnp.arange(t_p), nb_p))
    tab_s = rope_tables(jnp.tile(past + jnp.arange(t_s), nb_s))

    xp = x_prompt.reshape(nb_p * t_p, d)
    xs = x_sample.reshape(nb_s * t_s, d)
    zero_state = (jnp.zeros((nb_p, CONV_W - 1, d_lru), F32), jnp.zeros((nb_p, d_lru), F32))
    p_new, s_new = [], []
    hp = normmod(xp, g_norm[0][None], mod[0, :nb_p], nb_p, t_p, 256 if t_p % 256 == 0 else t_p)
    hsm = normmod(xs, g_norm[0][None], mod[0, nb_p:], nb_s, t_s, 256 if t_s % 256 == 0 else t_s)
    for l in range(depth):
        lw = dict(w_in=w_in_t, w_tail=w_tail, conv_w=conv_w[l], conv_b=conv_b[l][None],
                  w_rg=w_rg_bf[l], b_rg=b_rg[l][None], w_ig=w_ig_bf[l], b_ig=b_ig[l][None],
                  lam=lru_lambda[l][None], g_branch=g_branch[l][None], w_out=w_out_bf)
        last = l == depth - 1
        g_next = g_final[None] if last else g_norm[l + 1][None]
        modp_next = None if last else mod[l + 1, :nb_p]
        mods_next = None if last else mod[l + 1, nb_p:]
        strip = bias_strip(rel_bias[l])
        xp, hp, pn = _layer_stream(xp, hp, mod[l, :nb_p], modp_next, g_next, lw, l, cols, nb_p, t_p, tab_p,
                                   zero_state, strip, True)
        st = (state_conv[l], state_lru[l], band_k, band_v, dsa_k, dsa_v, idx_k_t)
        xs, hsm, sn = _layer_stream(xs, hsm, mod[l, nb_p:], mods_next, g_next, lw, l, cols, nb_s, t_s, tab_s,
                                    st, strip, False)
        p_new.append(pn)
        s_new.append(sn)

    y_prompt = xp.reshape(nb_p, t_p, d)
    y_sample = xs.reshape(nb_s, t_s, d)
    p_out = [jnp.stack([t[j] for t in p_new], axis=0) for j in range(7)]
    s_out = [jnp.stack([t[j] for t in s_new], axis=0) for j in range(7)]
    return (y_prompt, y_sample, *p_out, *s_out)
```

```python
import functools
import struct

import jax
import jax.numpy as jnp
from jax import lax
from jax.experimental import pallas as pl
from jax.experimental.pallas import tpu as pltpu

F32 = jnp.float32
BF16 = jnp.bfloat16
I32 = jnp.int32

CHUNK = 64
HEAD_DIM = 128
LRU_BLOCKS = 8
CONV_W = 4
LRU_C = 8.0
BAND_PAST_CHUNKS = 8
BAND_PAST = BAND_PAST_CHUNKS * CHUNK
BAND_KEYS = BAND_PAST + CHUNK
REL_CLIP = 256
N_IDX_HEADS = 16
IDX_DIM = 64
TOPK_MAX = 256
ROPE_THETA = 500000.0
ROPE_FRAC = 4
EPS = 1e-6
NEG = -1e30
LANES = 128
SUBLANES = 8

INT_MIN = -(2 ** 31)


def _sortable_key_of(x):
    b = struct.unpack("<i", struct.pack("<f", x))[0]
    return b ^ 0x7FFFFFFF if b < 0 else b


KEY_HALF_NEG = _sortable_key_of(NEG * 0.5)

VMEM_LIMIT = 48 * 1024 * 1024
DSA_SAMPLE_VMEM_LIMIT = 56 * 1024 * 1024
NT_DIMS = (((1,), (1,)), ((), ()))


def _params(n_grid):
    return pltpu.CompilerParams(dimension_semantics=("arbitrary",) * n_grid,
                                vmem_limit_bytes=VMEM_LIMIT)


def _dot_nt(a, b):
    return lax.dot_general(a, b, NT_DIMS, preferred_element_type=F32)


def _ada_kernel(c_ref, w_ref, b_ref, o_ref):
    c = c_ref[...]
    s = (c * jax.nn.sigmoid(c)).astype(BF16)
    o_ref[...] = jnp.dot(s, w_ref[...].astype(BF16), preferred_element_type=F32) + b_ref[...]


def ada_all(c_all, w_ada, b_ada):
    depth, d, n = w_ada.shape
    nb = c_all.shape[0]
    tn = 512
    return pl.pallas_call(
        _ada_kernel,
        grid=(depth, n // tn),
        in_specs=[pl.BlockSpec((nb, d), lambda l, j: (0, 0)),
                  pl.BlockSpec((None, d, tn), lambda l, j: (l, 0, j)),
                  pl.BlockSpec((None, 1, tn), lambda l, j: (l, 0, j))],
        out_specs=pl.BlockSpec((None, nb, tn), lambda l, j: (l, 0, j)),
        out_shape=jax.ShapeDtypeStruct((depth, nb, n), F32),
        compiler_params=_params(2),
        name="ada",
    )(c_all, w_ada, b_ada.reshape(depth, 1, n))


def _normmod_kernel(x_ref, g_ref, mod_ref, h_ref):
    x = x_ref[...]
    ms = jnp.mean(x * x, axis=-1, keepdims=True)
    y = x * lax.rsqrt(ms + EPS) * g_ref[...]
    shift = mod_ref[0:1, :]
    scale = mod_ref[1:2, :]
    h_ref[...] = (y * (1.0 + scale) + shift).astype(BF16)


def normmod(x, g, mod3, nseq, t, tb):
    rows, d = x.shape
    nblk = t // tb
    return pl.pallas_call(
        _normmod_kernel,
        grid=(nseq, nblk),
        in_specs=[pl.BlockSpec((tb, d), lambda b, i: (b * nblk + i, 0)),
                  pl.BlockSpec((1, d), lambda b, i: (0, 0)),
                  pl.BlockSpec((None, 3, d), lambda b, i: (b, 0, 0))],
        out_specs=pl.BlockSpec((tb, d), lambda b, i: (b * nblk + i, 0)),
        out_shape=jax.ShapeDtypeStruct((rows, d), BF16),
        compiler_params=_params(2),
        name="normmod",
    )(x, g, mod3)


def _mm_kernel(a_ref, b_ref, o_ref, *scratch):
    if scratch:
        wb_ref, = scratch

        @pl.when(pl.program_id(1) == 0)
        def _():
            wb_ref[...] = b_ref[...].T.astype(BF16)

        w = wb_ref[...]
    else:
        w = b_ref[...]
    o_ref[...] = jnp.dot(a_ref[...], w, preferred_element_type=F32)


def matmul(a, w, layer, ncols, tm, tn):
    m, k = a.shape
    if w.dtype == BF16:
        scratch = []
        w_spec = pl.BlockSpec((None, k, tn), lambda j, i: (layer, 0, j))
    else:
        scratch = [pltpu.VMEM((k, tn), BF16)]
        w_spec = pl.BlockSpec((None, tn, k), lambda j, i: (layer, j, 0))
    return pl.pallas_call(
        _mm_kernel,
        grid=(ncols // tn, m // tm),
        in_specs=[pl.BlockSpec((tm, k), lambda j, i: (i, 0)), w_spec],
        out_specs=pl.BlockSpec((tm, tn), lambda j, i: (i, j)),
        out_shape=jax.ShapeDtypeStruct((m, ncols), F32),
        scratch_shapes=scratch,
        compiler_params=_params(2),
        name="inproj",
    )(a, w)


def _rope_kernel(qc_ref, kc_ref, vc_ref, qi_ref, t_ref, c128_ref, s128_ref, c64_ref, s64_ref,
                 q_out, krot_out, kbf_out, qi_out, trot_out, kibf_out, *maybe_vt_out, n_dsa, n_idx):
    tm = qc_ref.shape[0]
    lane = lax.broadcasted_iota(I32, (tm, LANES), 1)
    c128 = c128_ref[...]
    s128 = s128_ref[...]
    c64 = c64_ref[...]
    s64 = s64_ref[...]
    half128 = HEAD_DIM // ROPE_FRAC // 2
    half64 = IDX_DIM // ROPE_FRAC // 2

    def rope128(x):
        partner = jnp.where(lane < half128, pltpu.roll(x, LANES - half128, 1), pltpu.roll(x, half128, 1))
        return x * c128 + partner * s128

    def rope64(x):
        partner = jnp.where((lane & (IDX_DIM - 1)) < half64,
                            pltpu.roll(x, LANES - half64, 1), pltpu.roll(x, half64, 1))
        return x * c64 + partner * s64

    for h in range(n_dsa):
        sl = slice(h * LANES, (h + 1) * LANES)
        q_out[:, sl] = rope128(qc_ref[:, sl]).astype(BF16)
        kr = rope128(kc_ref[:, sl])
        krot_out[:, sl] = kr
        kbf_out[:, sl] = kr.astype(BF16)
    if maybe_vt_out:
        maybe_vt_out[0][...] = vc_ref[...].T.astype(BF16)
    low = lane < IDX_DIM
    for j in range(n_idx // 2):
        r = rope64(qi_ref[:, j * LANES:(j + 1) * LANES]) * (IDX_DIM ** -0.5)
        qi_out[:, (2 * j) * LANES:(2 * j + 1) * LANES] = jnp.where(low, r, 0.0).astype(BF16)
        qi_out[:, (2 * j + 1) * LANES:(2 * j + 2) * LANES] = jnp.where(
            low, pltpu.roll(r, IDX_DIM, 1), 0.0).astype(BF16)
    t = t_ref[...]
    r = rope64(t)
    trot_out[...] = jnp.where(low, r, jnp.where(lane < IDX_DIM + N_IDX_HEADS, t * (N_IDX_HEADS ** -0.5), 0.0))
    kibf_out[...] = jnp.where(low, r, 0.0).astype(BF16)


def rope_call(u, tail, tables, cols, tm, v_t):
    rows = u.shape[0]
    d_dsa = cols["d_dsa"]
    n_dsa = d_dsa // HEAD_DIM
    d_qi = N_IDX_HEADS * IDX_DIM
    c128, s128, c64, s64 = tables
    tab_spec = pl.BlockSpec((tm, LANES), lambda i: (i, 0))
    blk = lambda off: pl.BlockSpec((tm, d_dsa), lambda i, o=off // d_dsa: (i, o))
    row_spec = pl.BlockSpec((tm, d_dsa), lambda i: (i, 0))
    out_specs = [row_spec, row_spec, row_spec, pl.BlockSpec((tm, N_IDX_HEADS * LANES), lambda i: (i, 0)),
                 tab_spec, tab_spec]
    out_shape = [jax.ShapeDtypeStruct((rows, d_dsa), BF16),
                 jax.ShapeDtypeStruct((rows, d_dsa), F32),
                 jax.ShapeDtypeStruct((rows, d_dsa), BF16),
                 jax.ShapeDtypeStruct((rows, N_IDX_HEADS * LANES), BF16),
                 jax.ShapeDtypeStruct((rows, LANES), F32),
                 jax.ShapeDtypeStruct((rows, LANES), BF16)]
    if v_t:
        out_specs.append(pl.BlockSpec((d_dsa, tm), lambda i: (0, i)))
        out_shape.append(jax.ShapeDtypeStruct((d_dsa, rows), BF16))
    return pl.pallas_call(
        functools.partial(_rope_kernel, n_dsa=n_dsa, n_idx=N_IDX_HEADS),
        grid=(rows // tm,),
        in_specs=[blk(cols["qc"]), blk(cols["kc"]), blk(cols["vc"]),
                  pl.BlockSpec((tm, d_qi), lambda i, o=cols["qi"] // d_qi: (i, o)),
                  tab_spec, tab_spec, tab_spec, tab_spec, tab_spec],
        out_specs=out_specs,
        out_shape=out_shape,
        compiler_params=_params(1),
        name="rope",
    )(u, u, u, u, tail, c128, s128, c64, s64)


def rope_tables(pos):
    pos = pos.astype(F32)[:, None]
    n = pos.shape[0]

    def tab(dim):
        half = dim // ROPE_FRAC // 2
        inv = ROPE_THETA ** (-jnp.arange(half, dtype=F32) / half)
        ang = pos * inv[None]
        cos, sin = jnp.cos(ang), jnp.sin(ang)
        c = jnp.concatenate([cos, cos, jnp.ones((n, dim - 2 * half), F32)], axis=1)
        s = jnp.concatenate([-sin, sin, jnp.zeros((n, dim - 2 * half), F32)], axis=1)
        return jnp.tile(c, (1, LANES // dim)), jnp.tile(s, (1, LANES // dim))

    c128, s128 = tab(HEAD_DIM)
    c64, s64 = tab(IDX_DIM)
    return c128, s128, c64, s64


def _lru_kernel(xa_ref, cs_ref, h0_ref, cw_ref, cb_ref, wrg_ref, brg_ref, wig_ref, big_ref, lam_ref,
                hs_ref, ext_ref, hc_ref):
    tb, d = xa_ref.shape
    blk = d // LRU_BLOCKS

    @pl.when(pl.program_id(1) == 0)
    def _():
        ext_ref[0:8, :] = cs_ref[...]
        hc_ref[...] = h0_ref[...]

    ext_ref[8:8 + tb, :] = xa_ref[...]
    conv = ext_ref[5:5 + tb, :] * cw_ref[0:1, :]
    for j in range(1, CONV_W):
        conv = conv + ext_ref[5 + j:5 + j + tb, :] * cw_ref[j:j + 1, :]
    conv = conv + cb_ref[...]
    tail = ext_ref[tb:tb + 8, :]
    ext_ref[0:8, :] = tail

    xb = conv.astype(BF16)
    rs, gs = [], []
    for g in range(LRU_BLOCKS):
        xg = xb[:, g * blk:(g + 1) * blk]
        rs.append(jnp.dot(xg, wrg_ref[g], preferred_element_type=F32))
        gs.append(jnp.dot(xg, wig_ref[g], preferred_element_type=F32))
    r = jax.nn.sigmoid(jnp.concatenate(rs, axis=1) + brg_ref[...])
    ig = jax.nn.sigmoid(jnp.concatenate(gs, axis=1) + big_ref[...])
    lam = lam_ref[...]
    softplus_neg_lam = jnp.maximum(-lam, 0.0) + jnp.log1p(jnp.exp(-jnp.abs(lam)))
    log_a = (-LRU_C) * r * softplus_neg_lam
    a = jnp.exp(log_a)
    u = jnp.sqrt(-jnp.tanh(log_a) * (a * a + 1.0)) * (ig * conv)

    row = lax.broadcasted_iota(I32, (tb, d), 0)
    s = 1
    while s < tb:
        keep = row >= s
        a_prev = jnp.where(keep, pltpu.roll(a, s, 0), 1.0)
        u_prev = jnp.where(keep, pltpu.roll(u, s, 0), 0.0)
        u = a * u_prev + u
        a = a * a_prev
        s *= 2
    h = a * hc_ref[...] + u
    hs_ref[...] = h
    hc_ref[...] = h[tb - 1:tb, :]


def lru_call(u, cs8, h0, cw, cb, wrg, brg, wig, big, lam, nseq, t, tb):
    rows = u.shape[0]
    d = cw.shape[1]
    blk = d // LRU_BLOCKS
    nblk = t // tb
    vec = pl.BlockSpec((1, d), lambda b, i: (0, 0))
    wspec = pl.BlockSpec((LRU_BLOCKS, blk, blk), lambda b, i: (0, 0, 0))
    return pl.pallas_call(
        _lru_kernel,
        grid=(nseq, nblk),
        in_specs=[pl.BlockSpec((tb, d), lambda b, i: (b * nblk + i, 0)),
                  pl.BlockSpec((None, 8, d), lambda b, i: (b, 0, 0)),
                  pl.BlockSpec((None, 1, d), lambda b, i: (b, 0, 0)),
                  pl.BlockSpec((CONV_W, d), lambda b, i: (0, 0)),
                  vec, wspec, vec, wspec, vec, vec],
        out_specs=pl.BlockSpec((tb, d), lambda b, i: (b * nblk + i, 0)),
        out_shape=jax.ShapeDtypeStruct((rows, d), F32),
        scratch_shapes=[pltpu.VMEM((tb + 8, d), F32), pltpu.VMEM((1, d), F32)],
        compiler_params=_params(2),
        name="lru",
    )(u, cs8, h0, cw, cb, wrg, brg, wig, big, lam)


def _bias_kernel(relb_ref, o_ref):
    h = pl.program_id(0)
    qi = lax.broadcasted_iota(I32, (CHUNK, BAND_KEYS), 0)
    kj = lax.broadcasted_iota(I32, (CHUNK, BAND_KEYS), 1)
    idx = jnp.clip(qi + BAND_PAST - kj, -REL_CLIP, REL_CLIP) + REL_CLIP

    def body(r, acc):
        return jnp.where(idx == r, relb_ref[h, r], acc)

    o_ref[...] = lax.fori_loop(0, 2 * REL_CLIP + 1, body, jnp.zeros((CHUNK, BAND_KEYS), F32))


def bias_strip(relb):
    nh = relb.shape[0]
    return pl.pallas_call(
        _bias_kernel,
        grid=(nh,),
        in_specs=[pl.BlockSpec(memory_space=pltpu.SMEM)],
        out_specs=pl.BlockSpec((None, CHUNK, BAND_KEYS), lambda h: (h, 0, 0)),
        out_shape=jax.ShapeDtypeStruct((nh, CHUNK, BAND_KEYS), F32),
        compiler_params=_params(1),
        name="bias_strip",
    )(relb)


def _band_chunk(q, kwin, vwin, strip, min_col):
    s = _dot_nt(q, kwin) * (HEAD_DIM ** -0.5) + strip
    col = lax.broadcasted_iota(I32, s.shape, 1)
    s = jnp.where(col >= min_col, s, NEG)
    m = jnp.max(s, axis=-1, keepdims=True)
    p = jnp.exp(s - m)
    l = jnp.sum(p, axis=-1, keepdims=True)
    return jnp.dot(p.astype(BF16), vwin, preferred_element_type=F32) / l


def _band_prompt_kernel(q_ref, kp_ref, kc_ref, vp_ref, vc_ref, strip_ref, o_ref, kw_ref, vw_ref):
    i = pl.program_id(2)
    tb = q_ref.shape[0]
    kw_ref[0:tb, :] = kp_ref[...].astype(BF16)
    kw_ref[tb:2 * tb, :] = kc_ref[...].astype(BF16)
    vw_ref[0:tb, :] = vp_ref[...].astype(BF16)
    vw_ref[tb:2 * tb, :] = vc_ref[...].astype(BF16)
    strip = strip_ref[...]
    col = lax.broadcasted_iota(I32, (CHUNK, BAND_KEYS), 1)
    chunks = range(tb // CHUNK)
    scores = []
    for a in chunks:
        q = q_ref[a * CHUNK:(a + 1) * CHUNK, :].astype(BF16)
        s = _dot_nt(q, kw_ref[a * CHUNK:a * CHUNK + BAND_KEYS, :]) * (HEAD_DIM ** -0.5) + strip
        min_col = jnp.where(i > 0, 0, (BAND_PAST_CHUNKS - a) * CHUNK)
        scores.append(jnp.where(col >= min_col, s, NEG))
    maxes = [jnp.max(s, axis=-1, keepdims=True) for s in scores]
    probs = [jnp.exp(s - m) for s, m in zip(scores, maxes)]
    sums = [jnp.sum(p, axis=-1, keepdims=True) for p in probs]
    for a in chunks:
        pv = jnp.dot(probs[a].astype(BF16), vw_ref[a * CHUNK:a * CHUNK + BAND_KEYS, :],
                     preferred_element_type=F32)
        o_ref[a * CHUNK:(a + 1) * CHUNK, :] = pv / sums[a]


def band_prompt(u, strip, cols, nseq, t):
    rows = u.shape[0]
    tb = BAND_PAST
    nblk = t // tb
    nh = strip.shape[0]
    qo, ko, vo = cols["qb"] // HEAD_DIM, cols["kb"] // HEAD_DIM, cols["vb"] // HEAD_DIM
    cur = lambda off: pl.BlockSpec((tb, HEAD_DIM), lambda b, h, i: (b * nblk + i, off + h))
    prev = lambda off: pl.BlockSpec((tb, HEAD_DIM), lambda b, h, i: (b * nblk + jnp.maximum(i - 1, 0), off + h))
    return pl.pallas_call(
        _band_prompt_kernel,
        grid=(nseq, nh, nblk),
        in_specs=[cur(qo), prev(ko), cur(ko), prev(vo), cur(vo),
                  pl.BlockSpec((None, CHUNK, BAND_KEYS), lambda b, h, i: (h, 0, 0))],
        out_specs=pl.BlockSpec((tb, HEAD_DIM), lambda b, h, i: (b * nblk + i, h)),
        out_shape=jax.ShapeDtypeStruct((rows, nh * HEAD_DIM), F32),
        scratch_shapes=[pltpu.VMEM((2 * tb, HEAD_DIM), BF16), pltpu.VMEM((2 * tb, HEAD_DIM), BF16)],
        compiler_params=_params(3),
        name="band_prompt",
    )(u, u, u, u, u, strip)


def _band_sample_kernel(q_ref, kn_ref, vn_ref, ck_ref, cv_ref, strip_ref, o_ref, kw_ref, vw_ref):
    nh = strip_ref.shape[0]
    w = ck_ref.shape[0] // nh
    t = q_ref.shape[0]
    kw_ref[w:w + t, :] = kn_ref[...].astype(BF16)
    vw_ref[w:w + t, :] = vn_ref[...].astype(BF16)
    for h in range(nh):
        sl = slice(h * HEAD_DIM, (h + 1) * HEAD_DIM)
        kw_ref[0:w, sl] = ck_ref[pl.ds(h, w, stride=nh), :].astype(BF16)
        vw_ref[0:w, sl] = cv_ref[pl.ds(h, w, stride=nh), :].astype(BF16)
        o_ref[:, sl] = _band_chunk(q_ref[:, sl].astype(BF16), kw_ref[:, sl], vw_ref[:, sl], strip_ref[h], 0)


def band_sample(u, ck, cv, layer, strip, cols, nseq, t):
    rows = u.shape[0]
    nh = strip.shape[0]
    d = nh * HEAD_DIM
    w = ck.shape[2] // nh
    ublk = lambda off: pl.BlockSpec((t, d), lambda b, o=off // d: (b, o))
    cblk = pl.BlockSpec((None, None, w * nh, HEAD_DIM), lambda b: (layer, b, 0, 0))
    return pl.pallas_call(
        _band_sample_kernel,
        grid=(nseq,),
        in_specs=[ublk(cols["qb"]), ublk(cols["kb"]), ublk(cols["vb"]), cblk, cblk,
                  pl.BlockSpec((nh, CHUNK, BAND_KEYS), lambda b: (0, 0, 0))],
        out_specs=pl.BlockSpec((t, d), lambda b: (b, 0)),
        out_shape=jax.ShapeDtypeStruct((rows, d), F32),
        scratch_shapes=[pltpu.VMEM((w + t, d), BF16), pltpu.VMEM((w + t, d), BF16)],
        compiler_params=_params(1),
        name="band_sample",
    )(u, u, u, ck, cv, strip)


def _loop(n, body, init):
    if isinstance(n, int):
        val = init
        for k in range(n):
            val = body(k, val)
        return val
    return lax.fori_loop(0, n, body, init)


IDX_BITS = 16


def _f32_to_key(x):
    bits = lax.bitcast_convert_type(x, I32)
    return jnp.where(bits < 0, bits ^ 0x7FFFFFFF, bits)


def _dsa_core(qside, segs, store_out, sk_refs, s_refs, jstar_ref, mpart_ref, lpart_ref, acc_ref, *,
              topk, q_chunk0, n_q, n_dsa):
    qb = qside["t"].shape[0]
    t_t = qside["t"].T
    wrows = [t_t[IDX_DIM + h:IDX_DIM + h + 1, :] for h in range(N_IDX_HEADS)]

    def tiles(x):
        return x.reshape(x.shape[0] // SUBLANES, SUBLANES, qb)

    w8 = [jnp.broadcast_to(w, (SUBLANES, qb)) for w in wrows]
    qcol8 = lax.broadcasted_iota(I32, (SUBLANES, qb), 1)
    q_end8 = (q_chunk0 + qcol8 // CHUNK + 1) * CHUNK
    for seg, sk_ref in zip(segs, sk_refs):
        kblk = seg["kblk"]
        n_t = kblk // SUBLANES
        krow3 = (lax.broadcasted_iota(I32, (n_t, SUBLANES, qb), 0) * SUBLANES
                 + lax.broadcasted_iota(I32, (n_t, SUBLANES, qb), 1))
        padded = seg["valid"] < kblk or n_q < qb
        live3 = (krow3 < seg["valid"]) & (lax.broadcasted_iota(I32, (n_t, SUBLANES, qb), 2) < n_q)

        def score_blk(kb, carry, seg=seg, sk_ref=sk_ref, kblk=kblk, krow3=krow3, padded=padded, live3=live3):
            kib = seg["ki"](kb)
            acc = jnp.zeros(krow3.shape, F32)
            for h in range(N_IDX_HEADS):
                sc = _dot_nt(kib, qside["qi"](h))
                acc = acc + jnp.maximum(tiles(sc), 0.0) * w8[h]
            admissible = krow3 < q_end8 - (seg["pos0"] + kb * kblk)
            key = _f32_to_key(jnp.where(admissible, acc, NEG))
            sk_ref[kb] = jnp.where(live3, key, INT_MIN) if padded else key
            return carry

        _loop(seg["nblk"], score_blk, 0)

    def count(pred):
        part = jnp.zeros((SUBLANES, qb), I32)
        for seg, sk_ref in zip(segs, sk_refs):
            n_t = seg["kblk"] // SUBLANES
            off = (lax.broadcasted_iota(I32, (n_t, SUBLANES, qb), 0) * SUBLANES
                   + lax.broadcasted_iota(I32, (n_t, SUBLANES, qb), 1))

            def body(kb, part, seg=seg, sk_ref=sk_ref, off=off):
                idx = seg["pos0"] + kb * seg["kblk"] + off
                return part + jnp.sum(jnp.where(pred(sk_ref[kb], idx), 1, 0), axis=0)

            part = _loop(seg["nblk"], body, part)
        return jnp.broadcast_to(jnp.sum(part, axis=0, keepdims=True), (SUBLANES, qb))

    zero = jnp.zeros((SUBLANES, qb), I32)
    n_stored = sum(seg["nblk"] * seg["kblk"] for seg in segs)
    c_zero = count(lambda key, idx: key >= zero)
    thr0 = jnp.where(c_zero >= topk, 0, INT_MIN).astype(I32)
    c_lo0 = jnp.where(c_zero >= topk, c_zero, zero + n_stored)

    def bit_step(bit, state):
        t, c = state
        cand = t + jnp.left_shift(jnp.int32(1), bit)
        c_cand = count(lambda key, idx: key >= cand)
        ok = c_cand >= topk
        return jnp.where(ok, cand, t), jnp.where(ok, c_cand, c)

    thr, c_lo = lax.fori_loop(0, 31, lambda i, st: bit_step(30 - i, st), (thr0, c_lo0))
    lo = jnp.maximum(thr, KEY_HALF_NEG + 1)

    big = 2 ** IDX_BITS
    jstar_ref[...] = jnp.full((SUBLANES, qb), big, I32)
    surplus = (c_lo > topk) & (thr > KEY_HALF_NEG)

    @pl.when(jnp.max(jnp.where(surplus, 1, 0)) > 0)
    def _():
        need = topk - count(lambda key, idx: key > thr)

        def idx_body(it, j):
            cand = j + jnp.left_shift(jnp.int32(1), IDX_BITS - 1 - it)
            f = count(lambda key, idx: (key == lo) & (idx < cand))
            return jnp.where(f <= need, cand, j)

        j = lax.fori_loop(0, IDX_BITS, idx_body, zero)
        jstar_ref[...] = jnp.where(surplus, j, big)

    jstar = jstar_ref[...]

    mpart_ref[...] = jnp.full(mpart_ref.shape, NEG, F32)
    lpart_ref[...] = jnp.zeros(lpart_ref.shape, F32)
    acc_ref[...] = jnp.zeros(acc_ref.shape, F32)

    for seg, sk_ref, s_ref in zip(segs, sk_refs, s_refs):
        kblk = seg["kblk"]
        n_t = kblk // SUBLANES
        krow3 = (lax.broadcasted_iota(I32, (n_t, SUBLANES, qb), 0) * SUBLANES
                 + lax.broadcasted_iota(I32, (n_t, SUBLANES, qb), 1))

        def pass_a(kb, carry, seg=seg, sk_ref=sk_ref, s_ref=s_ref, kblk=kblk, krow3=krow3):
            key = sk_ref[kb]
            mask = (key >= lo) & ((key > lo) | (krow3 < jstar - (seg["pos0"] + kb * kblk)))
            for h in range(n_dsa):
                s = tiles(_dot_nt(seg["k"](kb, h), qside["q"](h)))
                s = jnp.where(mask, s * (HEAD_DIM ** -0.5), NEG)
                s_ref[h, kb] = s
                mpart_ref[h] = jnp.maximum(mpart_ref[h], jnp.max(s, axis=0))
            return carry

        _loop(seg["nblk"], pass_a, 0)

    m8 = [jnp.broadcast_to(jnp.max(mpart_ref[h], axis=0, keepdims=True), (SUBLANES, qb)) for h in range(n_dsa)]

    for seg, s_ref in zip(segs, s_refs):
        def pass_b(kb, carry, seg=seg, s_ref=s_ref):
            for h in range(n_dsa):
                p = jnp.exp(s_ref[h, kb] - m8[h])
                lpart_ref[h] += jnp.sum(p, axis=0)
                acc_ref[h] += seg["pv"](kb, h, p.reshape(seg["kblk"], qb).astype(BF16))
            return carry

        _loop(seg["nblk"], pass_b, 0)

    for h in range(n_dsa):
        l = jnp.sum(lpart_ref[h], axis=0, keepdims=True)
        store_out(h, (acc_ref[h] / l).T)


def _blk_start(kb, kblk):
    return kb * kblk if isinstance(kb, int) else pl.multiple_of(kb * kblk, kblk)


def _dsa_scratch(qb, seg_shapes, n_dsa):
    return ([pltpu.VMEM((nblk, kblk // SUBLANES, SUBLANES, qb), I32) for nblk, kblk in seg_shapes]
            + [pltpu.VMEM((n_dsa, nblk, kblk // SUBLANES, SUBLANES, qb), F32) for nblk, kblk in seg_shapes]
            + [pltpu.VMEM((SUBLANES, qb), I32),
               pltpu.VMEM((n_dsa, SUBLANES, qb), F32), pltpu.VMEM((n_dsa, SUBLANES, qb), F32),
               pltpu.VMEM((n_dsa, HEAD_DIM, qb), F32)])


def _dsa_prompt_kernel(qi_ref, t_ref, q_ref, ki_ref, k_ref, vt_ref, o_ref, sk_ref, s_ref, jstar_ref,
                       mpart_ref, lpart_ref, acc_ref, *, kblk, topk):
    i = pl.program_id(1)
    qb = q_ref.shape[0]
    nblk = (i * qb + qb + kblk - 1) // kblk

    def head(h):
        return slice(h * HEAD_DIM, (h + 1) * HEAD_DIM)

    seg = dict(ki=lambda kb: ki_ref[pl.ds(_blk_start(kb, kblk), kblk), :],
               k=lambda kb, h: k_ref[pl.ds(_blk_start(kb, kblk), kblk), head(h)],
               pv=lambda kb, h, p: jnp.dot(vt_ref[head(h), pl.ds(_blk_start(kb, kblk), kblk)], p,
                                           preferred_element_type=F32),
               nblk=nblk, kblk=kblk, pos0=0, valid=kblk)
    qside = dict(t=t_ref[...], qi=lambda h: qi_ref[:, h * LANES:(h + 1) * LANES], q=lambda h: q_ref[:, head(h)])

    def store_out(h, x):
        o_ref[:, head(h)] = x

    _dsa_core(qside, [seg], store_out, [sk_ref], [s_ref], jstar_ref, mpart_ref, lpart_ref, acc_ref,
              topk=topk, q_chunk0=(i * qb) // CHUNK, n_q=qb, n_dsa=q_ref.shape[1] // HEAD_DIM)


def dsa_prompt(qi_pad, trot, q_bf, ki_bf, k_bf, v_t, nseq, t, qb, kblk, topk):
    rows, d = q_bf.shape
    n_dsa = d // HEAD_DIM
    nq = t // qb
    row_blk = lambda w: pl.BlockSpec((qb, w), lambda b, i: (b * nq + i, 0))
    seq_blk = lambda w: pl.BlockSpec((t, w), lambda b, i: (b, 0))
    return pl.pallas_call(
        functools.partial(_dsa_prompt_kernel, kblk=kblk, topk=topk),
        grid=(nseq, nq),
        in_specs=[row_blk(qi_pad.shape[1]), row_blk(LANES), row_blk(d),
                  seq_blk(LANES), seq_blk(d), pl.BlockSpec((d, t), lambda b, i: (0, b))],
        out_specs=row_blk(d),
        out_shape=jax.ShapeDtypeStruct((rows, d), F32),
        scratch_shapes=_dsa_scratch(qb, [(t // kblk, kblk)], n_dsa),
        compiler_params=_params(2),
        name="dsa_prompt",
    )(qi_pad, trot, q_bf, ki_bf, k_bf, v_t)


def _dsa_sample_kernel(qi_ref, t_ref, q_ref, kin_ref, kn_ref, vn_ref, cki_ref, ck_ref, cv_ref, o_ref,
                       qi_s, q_s, kin_s, kn_s, vn_s, kit_s, sk0_ref, sk1_ref, s0_ref, s1_ref, jstar_ref,
                       mpart_ref, lpart_ref, acc_ref, *, kblk, topk):
    past = cki_ref.shape[2]
    tq = q_ref.shape[0] // 2
    qb = 2 * tq
    n_dsa = q_ref.shape[1] // HEAD_DIM
    pair_w = 2 * LANES

    def head(h):
        return slice(h * HEAD_DIM, (h + 1) * HEAD_DIM)

    def rows(a):
        return slice(a * tq, (a + 1) * tq)

    def half(h, a):
        return slice(h * pair_w + a * LANES, h * pair_w + (a + 1) * LANES)

    for ref in (qi_s, q_s, kin_s, kn_s):
        ref[...] = jnp.zeros(ref.shape, ref.dtype)
    for a in range(2):
        for h in range(N_IDX_HEADS):
            qi_s[rows(a), half(h, a)] = qi_ref[rows(a), h * LANES:(h + 1) * LANES]
        for h in range(n_dsa):
            q_s[rows(a), half(h, a)] = q_ref[rows(a), head(h)]
            kn_s[0:tq, half(h, a)] = kn_ref[rows(a), head(h)]
        kin_s[0:tq, half(0, a)] = kin_ref[rows(a), :]
        vn_s[a, 0:tq, :] = vn_ref[rows(a), :]
        vn_s[a, tq:, :] = jnp.zeros((qb - tq, vn_s.shape[2]), F32)
        kit_s[a, IDX_DIM:, :] = jnp.zeros((kit_s.shape[1] - IDX_DIM, kblk), F32)

    first_seq = lax.broadcasted_iota(I32, (HEAD_DIM, qb), 1) < tq

    def pair_pv(v_a, v_b, p):
        out_a = jnp.dot(v_a.T.astype(BF16), p, preferred_element_type=F32)
        out_b = jnp.dot(v_b.T.astype(BF16), p, preferred_element_type=F32)
        return jnp.where(first_seq, out_a, out_b)

    def cache_rows(ref, a, kb, h):
        return ref[a, pl.ds(kb * kblk * n_dsa + h, kblk, stride=n_dsa), :]

    def cache_ki(kb):
        parts = []
        for a in range(2):
            kit_s[a, 0:IDX_DIM, :] = cki_ref[a, :, pl.ds(kb * kblk, kblk)]
            parts.append(kit_s[a].T.astype(BF16))
        return jnp.concatenate(parts, axis=1)

    segs = [dict(ki=cache_ki,
                 k=lambda kb, h: jnp.concatenate([cache_rows(ck_ref, a, kb, h).astype(BF16) for a in range(2)],
                                                 axis=1),
                 pv=lambda kb, h, p: pair_pv(cache_rows(cv_ref, 0, kb, h), cache_rows(cv_ref, 1, kb, h), p),
                 nblk=past // kblk, kblk=kblk, pos0=0, valid=kblk),
            dict(ki=lambda kb: kin_s[...],
                 k=lambda kb, h: kn_s[:, h * pair_w:(h + 1) * pair_w],
                 pv=lambda kb, h, p: pair_pv(vn_s[0, :, head(h)], vn_s[1, :, head(h)], p),
                 nblk=1, kblk=qb, pos0=past, valid=tq)]
    qside = dict(t=t_ref[...],
                 qi=lambda h: qi_s[:, h * pair_w:(h + 1) * pair_w],
                 q=lambda h: q_s[:, h * pair_w:(h + 1) * pair_w])

    def store_out(h, x):
        o_ref[:, head(h)] = x

    _dsa_core(qside, segs, store_out, [sk0_ref, sk1_ref], [s0_ref, s1_ref], jstar_ref,
              mpart_ref, lpart_ref, acc_ref, topk=topk, q_chunk0=past // CHUNK, n_q=qb, n_dsa=n_dsa)


def dsa_sample(qi_pad, trot, q_bf, ki_bf, k_bf, u, vc_col, cki, ck, cv, layer, nseq, t, kblk, topk):
    rows, d = q_bf.shape
    n_dsa = d // HEAD_DIM
    past = cki.shape[3]
    qb = 2 * t
    assert qb == LANES and nseq % 2 == 0
    row_blk = lambda w, o=0: pl.BlockSpec((qb, w), lambda b, o=o: (b, o))
    cache_blk = lambda r, w: pl.BlockSpec((None, 2, r, w), lambda b: (layer, b, 0, 0))
    pair_scratch = [pltpu.VMEM((qb, 2 * qi_pad.shape[1]), BF16), pltpu.VMEM((qb, 2 * d), BF16),
                    pltpu.VMEM((qb, 2 * LANES), BF16), pltpu.VMEM((qb, 2 * d), BF16),
                    pltpu.VMEM((2, qb, d), F32), pltpu.VMEM((2, LANES, kblk), F32)]
    return pl.pallas_call(
        functools.partial(_dsa_sample_kernel, kblk=kblk, topk=topk),
        grid=(nseq // 2,),
        in_specs=[row_blk(qi_pad.shape[1]), row_blk(LANES), row_blk(d),
                  row_blk(LANES), row_blk(d), row_blk(d, vc_col),
                  cache_blk(cki.shape[2], past), cache_blk(past * n_dsa, HEAD_DIM),
                  cache_blk(past * n_dsa, HEAD_DIM)],
        out_specs=row_blk(d),
        out_shape=jax.ShapeDtypeStruct((rows, d), F32),
        scratch_shapes=pair_scratch + _dsa_scratch(qb, [(past // kblk, kblk), (1, qb)], n_dsa),
        compiler_params=pltpu.CompilerParams(dimension_semantics=("arbitrary",),
                                             vmem_limit_bytes=DSA_SAMPLE_VMEM_LIMIT),
        name="dsa_sample",
    )(qi_pad, trot, q_bf, ki_bf, k_bf, u, cki, ck, cv)


def _outproj_kernel(oa_ref, ob_ref, oc_ref, za_ref, zb_ref, zc_ref, g_ref, w_ref, x_ref, mod_ref, g2_ref,
                    *rest, d_lru, d_band, last):
    if last:
        yo_ref, y_ref = rest
    else:
        mod2_ref, xo_ref, h_ref, y_ref = rest

    def branch(o_ref, z_ref, lo, hi):
        o = o_ref[...]
        z = z_ref[...]
        ms = jnp.mean(o * o, axis=-1, keepdims=True)
        y = o * lax.rsqrt(ms + EPS) * g_ref[:, lo:hi]
        y_ref[:, lo:hi] = (y * (z * jax.nn.sigmoid(z))).astype(BF16)

    d_mix = y_ref.shape[1]
    branch(oa_ref, za_ref, 0, d_lru)
    branch(ob_ref, zb_ref, d_lru, d_lru + d_band)
    branch(oc_ref, zc_ref, d_lru + d_band, d_mix)
    out = jnp.dot(y_ref[...], w_ref[...], preferred_element_type=F32)
    x_new = x_ref[...] + mod_ref[2:3, :] * out
    ms = jnp.mean(x_new * x_new, axis=-1, keepdims=True)
    normed = x_new * lax.rsqrt(ms + EPS) * g2_ref[...]
    if last:
        yo_ref[...] = normed
    else:
        xo_ref[...] = x_new
        h_ref[...] = (normed * (1.0 + mod2_ref[1:2, :]) + mod2_ref[0:1, :]).astype(BF16)


def outproj(oa, ob, oc, u, g_branch, w_out, layer, x, mod3, g2, mod3_next, cols, nseq, t, tm):
    rows, d = x.shape
    d_lru, d_band, d_dsa = oa.shape[1], ob.shape[1], oc.shape[1]
    d_mix = d_lru + d_band + d_dsa
    nblk = t // tm
    last = mod3_next is None
    rb = lambda w, o=0: pl.BlockSpec((tm, w), lambda b, i, o=o: (b * nblk + i, o))
    vec = pl.BlockSpec((1, d), lambda b, i: (0, 0))
    mod_spec = pl.BlockSpec((None, 3, d), lambda b, i: (b, 0, 0))
    in_specs = [rb(d_lru), rb(d_band), rb(d_dsa),
                rb(d_lru, cols["za"] // d_lru), rb(d_band, cols["zb"] // d_band), rb(d_dsa, cols["zc"] // d_dsa),
                pl.BlockSpec((1, d_mix), lambda b, i: (0, 0)),
                pl.BlockSpec((None, d_mix, d), lambda b, i: (layer, 0, 0)),
                rb(d), mod_spec, vec]
    args = [oa, ob, oc, u, u, u, g_branch, w_out, x, mod3, g2]
    if last:
        out_specs = rb(d)
        out_shape = jax.ShapeDtypeStruct((rows, d), F32)
    else:
        in_specs.append(mod_spec)
        args.append(mod3_next)
        out_specs = [rb(d), rb(d)]
        out_shape = [jax.ShapeDtypeStruct((rows, d), F32), jax.ShapeDtypeStruct((rows, d), BF16)]
    return pl.pallas_call(
        functools.partial(_outproj_kernel, d_lru=d_lru, d_band=d_band, last=last),
        grid=(nseq, nblk),
        in_specs=in_specs,
        out_specs=out_specs,
        out_shape=out_shape,
        scratch_shapes=[pltpu.VMEM((tm, d_mix), BF16)],
        compiler_params=_params(2),
        name="outproj",
    )(*args)


def _column_offsets(d_lru, d_band, d_dsa):
    names = ["xa", "za", "qb", "kb", "vb", "zb", "qc", "kc", "vc", "zc", "qi"]
    widths = [d_lru, d_lru, d_band, d_band, d_band, d_band, d_dsa, d_dsa, d_dsa, d_dsa, N_IDX_HEADS * IDX_DIM]
    cols, off = {}, 0
    for n, w in zip(names, widths):
        cols[n] = off
        off += w
    cols["main"] = off
    cols["d_lru"], cols["d_band"], cols["d_dsa"] = d_lru, d_band, d_dsa
    return cols


def _layer_stream(x, h, mod3, mod3_next, g_next, lw, layer, cols, nseq, t, tables, state, strip, prompt):
    d_lru, d_band, d_dsa = cols["d_lru"], cols["d_band"], cols["d_dsa"]
    n_band, n_dsa = d_band // HEAD_DIM, d_dsa // HEAD_DIM
    rows = nseq * t
    tb = 256 if t % 256 == 0 else t
    tm = 512 if rows % 512 == 0 else rows
    u = matmul(h, lw["w_in"], layer, cols["main"], tm, 1024)
    tail = matmul(h, lw["w_tail"], layer, LANES, tm, LANES)

    q_bf, k_rot, k_bf, qi_pad, t_rot, ki_bf, *maybe_vt = rope_call(u, tail, tables, cols, tb, prompt)

    conv_s, lru_s = state[0], state[1]
    cs8 = jnp.concatenate([jnp.zeros((nseq, 8 - (CONV_W - 1), d_lru), F32), conv_s], axis=1)
    hs = lru_call(u, cs8, lru_s.reshape(nseq, 1, d_lru), lw["conv_w"], lw["conv_b"], lw["w_rg"], lw["b_rg"],
                  lw["w_ig"], lw["b_ig"], lw["lam"], nseq, t, tb)

    if prompt:
        ob = band_prompt(u, strip, cols, nseq, t)
        oc = dsa_prompt(qi_pad, t_rot, q_bf, ki_bf, k_bf, maybe_vt[0], nseq, t, 128, 512, min(TOPK_MAX, t // 4))
    else:
        bk, bv, dk, dv, dik = state[2:]
        past = dik.shape[3]
        assert (past + t - 1) // CHUNK <= past // CHUNK and t == CHUNK and bk.shape[2] == BAND_PAST * n_band
        ob = band_sample(u, bk, bv, layer, strip, cols, nseq, t)
        oc = dsa_sample(qi_pad, t_rot, q_bf, ki_bf, k_bf, u, cols["vc"] // d_dsa, dik, dk, dv, layer,
                        nseq, t, 512, min(TOPK_MAX, (past + t) // 4))

    res = outproj(hs, ob, oc, u, lw["g_branch"], lw["w_out"], layer, x, mod3, g_next, mod3_next, cols, nseq, t, tb)
    x_new, h_next = (res, None) if mod3_next is None else res

    u3 = u.reshape(nseq, t, -1)
    xa = u3[:, :, cols["xa"]:cols["xa"] + d_lru]
    if prompt:
        new_conv = xa[:, t - (CONV_W - 1):]
    else:
        new_conv = jnp.concatenate([conv_s, xa], axis=1)[:, -(CONV_W - 1):]
    nbr = min(BAND_PAST, t)
    new = (new_conv,
           hs.reshape(nseq, t, d_lru)[:, -1],
           u3[:, t - nbr:, cols["kb"]:cols["kb"] + d_band].reshape(nseq, nbr, n_band, HEAD_DIM),
           u3[:, t - nbr:, cols["vb"]:cols["vb"] + d_band].reshape(nseq, nbr, n_band, HEAD_DIM),
           k_rot.reshape(nseq, t, n_dsa, HEAD_DIM),
           u3[:, :, cols["vc"]:cols["vc"] + d_dsa].reshape(nseq, t, n_dsa, HEAD_DIM),
           t_rot.reshape(nseq, t, LANES)[:, :, :IDX_DIM])
    return x_new, h_next, new


def kernel(x_prompt, x_sample, c_prompt, c_sample, state_conv, state_lru, cache_band_k, cache_band_v,
           cache_dsa_k, cache_dsa_v, cache_dsa_idx_k, g_norm, w_ada, b_ada, w_in, conv_w, conv_b,
           w_rg, b_rg, w_ig, b_ig, lru_lambda, rel_bias, g_branch, w_out, g_final):
    depth = w_in.shape[0]
    nb_p, t_p, d = x_prompt.shape
    nb_s, t_s, _ = x_sample.shape
    past = cache_dsa_k.shape[2]
    d_lru = conv_w.shape[2]
    d_band = cache_band_k.shape[3] * HEAD_DIM
    d_dsa = cache_dsa_k.shape[3] * HEAD_DIM
    cols = _column_offsets(d_lru, d_band, d_dsa)
    n_main = cols["main"]

    mod = ada_all(jnp.concatenate([c_prompt, c_sample], axis=0), w_ada, b_ada)
    mod = mod.reshape(depth, nb_p + nb_s, 3, d)

    w_tail = jnp.pad(w_in[:, :, n_main:], ((0, 0), (0, 0), (0, LANES - (w_in.shape[2] - n_main)))).astype(BF16)
    w_out_bf = w_out.astype(BF16)
    w_rg_bf = w_rg.astype(BF16)
    w_ig_bf = w_ig.astype(BF16)

    band_k = cache_band_k.reshape(depth, nb_s, -1, HEAD_DIM)
    band_v = cache_band_v.reshape(depth, nb_s, -1, HEAD_DIM)
    dsa_k = cache_dsa_k.reshape(depth, nb_s, -1, HEAD_DIM)
    dsa_v = cache_dsa_v.reshape(depth, nb_s, -1, HEAD_DIM)
    idx_k_t = jnp.swapaxes(cache_dsa_idx_k, 2, 3)
    w_in_t = jnp.swapaxes(w_in, 1, 2)

    tab_p = rope_tables(jnp.tile(jnp.arange(t_p), nb_p))
    tab_s = rope_tables(jnp.tile(past + jnp.arange(t_s), nb_s))

    xp = x_prompt.reshape(nb_p * t_p, d)
    xs = x_sample.reshape(nb_s * t_s, d)
    zero_state = (jnp.zeros((nb_p, CONV_W - 1, d_lru), F32), jnp.zeros((nb_p, d_lru), F32))
    p_new, s_new = [], []
    hp = normmod(xp, g_norm[0][None], mod[0, :nb_p], nb_p, t_p, 256 if t_p % 256 == 0 else t_p)
    hsm = normmod(xs, g_norm[0][None], mod[0, nb_p:], nb_s, t_s, 256 if t_s % 256 == 0 else t_s)
    for l in range(depth):
        lw = dict(w_in=w_in_t, w_tail=w_tail, conv_w=conv_w[l], conv_b=conv_b[l][None],
                  w_rg=w_rg_bf[l], b_rg=b_rg[l][None], w_ig=w_ig_bf[l], b_ig=b_ig[l][None],
                  lam=lru_lambda[l][None], g_branch=g_branch[l][None], w_out=w_out_bf)
        last = l == depth - 1
        g_next = g_final[None] if last else g_norm[l + 1][None]
        modp_next = None if last else mod[l + 1, :nb_p]
        mods_next = None if last else mod[l + 1, nb_p:]
        strip = bias_strip(rel_bias[l])
        xp, hp, pn = _layer_stream(xp, hp, mod[l, :nb_p], modp_next, g_next, lw, l, cols, nb_p, t_p, tab_p,
                                   zero_state, strip, True)
        st = (state_conv[l], state_lru[l], band_k, band_v, dsa_k, dsa_v, idx_k_t)
        xs, hsm, sn = _layer_stream(xs, hsm, mod[l, nb_p:], mods_next, g_next, lw, l, cols, nb_s, t_s, tab_s,
                                    st, strip, False)
        p_new.append(pn)
        s_new.append(sn)

    y_prompt = xp.reshape(nb_p, t_p, d)
    y_sample = xs.reshape(nb_s, t_s, d)
    p_out = [jnp.stack([t[j] for t in p_new], axis=0) for j in range(7)]
    s_out = [jnp.stack([t[j] for t in s_new], axis=0) for j in range(7)]
    return (y_prompt, y_sample, *p_out, *s_out)
```

```python
import functools
import struct

import jax
import jax.numpy as jnp
from jax import lax
from jax.experimental import pallas as pl
from jax.experimental.pallas import tpu as pltpu

F32 = jnp.float32
BF16 = jnp.bfloat16
I32 = jnp.int32
I16 = jnp.int16

CHUNK = 64
HEAD_DIM = 128
LRU_BLOCKS = 8
CONV_W = 4
LRU_C = 8.0
BAND_PAST_CHUNKS = 8
BAND_PAST = BAND_PAST_CHUNKS * CHUNK
BAND_KEYS = BAND_PAST + CHUNK
REL_CLIP = 256
N_IDX_HEADS = 16
IDX_DIM = 64
TOPK_MAX = 256
ROPE_THETA = 500000.0
ROPE_FRAC = 4
EPS = 1e-6
NEG = -1e30
LANES = 128
SUBLANES = 8

INT_MIN = -(2 ** 31)


def _sortable_key_of(x):
    b = struct.unpack("<i", struct.pack("<f", x))[0]
    return b ^ 0x7FFFFFFF if b < 0 else b


KEY_HALF_NEG = _sortable_key_of(NEG * 0.5)

VMEM_LIMIT = 48 * 1024 * 1024
DSA_SAMPLE_VMEM_LIMIT = 56 * 1024 * 1024
NT_DIMS = (((1,), (1,)), ((), ()))


def _params(n_grid):
    return pltpu.CompilerParams(dimension_semantics=("arbitrary",) * n_grid,
                                vmem_limit_bytes=VMEM_LIMIT)


def _dot_nt(a, b):
    return lax.dot_general(a, b, NT_DIMS, preferred_element_type=F32)


def _ada_kernel(c_ref, w_ref, b_ref, o_ref):
    c = c_ref[...]
    s = (c * jax.nn.sigmoid(c)).astype(BF16)
    o_ref[...] = jnp.dot(s, w_ref[...].astype(BF16), preferred_element_type=F32) + b_ref[...]


def ada_all(c_all, w_ada, b_ada):
    depth, d, n = w_ada.shape
    nb = c_all.shape[0]
    tn = 512
    return pl.pallas_call(
        _ada_kernel,
        grid=(depth, n // tn),
        in_specs=[pl.BlockSpec((nb, d), lambda l, j: (0, 0)),
                  pl.BlockSpec((None, d, tn), lambda l, j: (l, 0, j)),
                  pl.BlockSpec((None, 1, tn), lambda l, j: (l, 0, j))],
        out_specs=pl.BlockSpec((None, nb, tn), lambda l, j: (l, 0, j)),
        out_shape=jax.ShapeDtypeStruct((depth, nb, n), F32),
        compiler_params=_params(2),
        name="ada",
    )(c_all, w_ada, b_ada.reshape(depth, 1, n))


def _normmod_kernel(x_ref, g_ref, mod_ref, h_ref):
    x = x_ref[...]
    ms = jnp.mean(x * x, axis=-1, keepdims=True)
    y = x * lax.rsqrt(ms + EPS) * g_ref[...]
    shift = mod_ref[0:1, :]
    scale = mod_ref[1:2, :]
    h_ref[...] = (y * (1.0 + scale) + shift).astype(BF16)


def normmod(x, g, mod3, nseq, t, tb):
    rows, d = x.shape
    nblk = t // tb
    return pl.pallas_call(
        _normmod_kernel,
        grid=(nseq, nblk),
        in_specs=[pl.BlockSpec((tb, d), lambda b, i: (b * nblk + i, 0)),
                  pl.BlockSpec((1, d), lambda b, i: (0, 0)),
                  pl.BlockSpec((None, 3, d), lambda b, i: (b, 0, 0))],
        out_specs=pl.BlockSpec((tb, d), lambda b, i: (b * nblk + i, 0)),
        out_shape=jax.ShapeDtypeStruct((rows, d), BF16),
        compiler_params=_params(2),
        name="normmod",
    )(x, g, mod3)


def _mm_kernel(a_ref, b_ref, o_ref, *scratch):
    if scratch:
        wb_ref, = scratch

        @pl.when(pl.program_id(1) == 0)
        def _():
            wb_ref[...] = b_ref[...].T.astype(BF16)

        w = wb_ref[...]
    else:
        w = b_ref[...]
    o_ref[...] = jnp.dot(a_ref[...], w, preferred_element_type=F32)


def matmul(a, w, layer, ncols, tm, tn):
    m, k = a.shape
    if w.dtype == BF16:
        scratch = []
        w_spec = pl.BlockSpec((None, k, tn), lambda j, i: (layer, 0, j))
    else:
        scratch = [pltpu.VMEM((k, tn), BF16)]
        w_spec = pl.BlockSpec((None, tn, k), lambda j, i: (layer, j, 0))
    return pl.pallas_call(
        _mm_kernel,
        grid=(ncols // tn, m // tm),
        in_specs=[pl.BlockSpec((tm, k), lambda j, i: (i, 0)), w_spec],
        out_specs=pl.BlockSpec((tm, tn), lambda j, i: (i, j)),
        out_shape=jax.ShapeDtypeStruct((m, ncols), F32),
        scratch_shapes=scratch,
        compiler_params=_params(2),
        name="inproj",
    )(a, w)


def _rope_kernel(qc_ref, kc_ref, vc_ref, qi_ref, t_ref, c128_ref, s128_ref, c64_ref, s64_ref,
                 q_out, krot_out, kbf_out, qi_out, trot_out, kibf_out, *maybe_vt_out, n_dsa, n_idx):
    tm = qc_ref.shape[0]
    lane = lax.broadcasted_iota(I32, (tm, LANES), 1)
    c128 = c128_ref[...]
    s128 = s128_ref[...]
    c64 = c64_ref[...]
    s64 = s64_ref[...]
    half128 = HEAD_DIM // ROPE_FRAC // 2
    half64 = IDX_DIM // ROPE_FRAC // 2

    def rope128(x):
        partner = jnp.where(lane < half128, pltpu.roll(x, LANES - half128, 1), pltpu.roll(x, half128, 1))
        return x * c128 + partner * s128

    def rope64(x):
        partner = jnp.where((lane & (IDX_DIM - 1)) < half64,
                            pltpu.roll(x, LANES - half64, 1), pltpu.roll(x, half64, 1))
        return x * c64 + partner * s64

    for h in range(n_dsa):
        sl = slice(h * LANES, (h + 1) * LANES)
        q_out[:, sl] = rope128(qc_ref[:, sl]).astype(BF16)
        kr = rope128(kc_ref[:, sl])
        krot_out[:, sl] = kr
        kbf_out[:, sl] = kr.astype(BF16)
    if maybe_vt_out:
        maybe_vt_out[0][...] = vc_ref[...].T.astype(BF16)
    low = lane < IDX_DIM
    for j in range(n_idx // 2):
        r = rope64(qi_ref[:, j * LANES:(j + 1) * LANES]) * (IDX_DIM ** -0.5)
        qi_out[:, (2 * j) * LANES:(2 * j + 1) * LANES] = jnp.where(low, r, 0.0).astype(BF16)
        qi_out[:, (2 * j + 1) * LANES:(2 * j + 2) * LANES] = jnp.where(
            low, pltpu.roll(r, IDX_DIM, 1), 0.0).astype(BF16)
    t = t_ref[...]
    r = rope64(t)
    trot_out[...] = jnp.where(low, r, jnp.where(lane < IDX_DIM + N_IDX_HEADS, t * (N_IDX_HEADS ** -0.5), 0.0))
    kibf_out[...] = jnp.where(low, r, 0.0).astype(BF16)


def rope_call(u, tail, tables, cols, tm, v_t):
    rows = u.shape[0]
    d_dsa = cols["d_dsa"]
    n_dsa = d_dsa // HEAD_DIM
    d_qi = N_IDX_HEADS * IDX_DIM
    c128, s128, c64, s64 = tables
    tab_spec = pl.BlockSpec((tm, LANES), lambda i: (i, 0))
    blk = lambda off: pl.BlockSpec((tm, d_dsa), lambda i, o=off // d_dsa: (i, o))
    row_spec = pl.BlockSpec((tm, d_dsa), lambda i: (i, 0))
    out_specs = [row_spec, row_spec, row_spec, pl.BlockSpec((tm, N_IDX_HEADS * LANES), lambda i: (i, 0)),
                 tab_spec, tab_spec]
    out_shape = [jax.ShapeDtypeStruct((rows, d_dsa), BF16),
                 jax.ShapeDtypeStruct((rows, d_dsa), F32),
                 jax.ShapeDtypeStruct((rows, d_dsa), BF16),
                 jax.ShapeDtypeStruct((rows, N_IDX_HEADS * LANES), BF16),
                 jax.ShapeDtypeStruct((rows, LANES), F32),
                 jax.ShapeDtypeStruct((rows, LANES), BF16)]
    if v_t:
        out_specs.append(pl.BlockSpec((d_dsa, tm), lambda i: (0, i)))
        out_shape.append(jax.ShapeDtypeStruct((d_dsa, rows), BF16))
    return pl.pallas_call(
        functools.partial(_rope_kernel, n_dsa=n_dsa, n_idx=N_IDX_HEADS),
        grid=(rows // tm,),
        in_specs=[blk(cols["qc"]), blk(cols["kc"]), blk(cols["vc"]),
                  pl.BlockSpec((tm, d_qi), lambda i, o=cols["qi"] // d_qi: (i, o)),
                  tab_spec, tab_spec, tab_spec, tab_spec, tab_spec],
        out_specs=out_specs,
        out_shape=out_shape,
        compiler_params=_params(1),
        name="rope",
    )(u, u, u, u, tail, c128, s128, c64, s64)


def rope_tables(pos):
    pos = pos.astype(F32)[:, None]
    n = pos.shape[0]

    def tab(dim):
        half = dim // ROPE_FRAC // 2
        inv = ROPE_THETA ** (-jnp.arange(half, dtype=F32) / half)
        ang = pos * inv[None]
        cos, sin = jnp.cos(ang), jnp.sin(ang)
        c = jnp.concatenate([cos, cos, jnp.ones((n, dim - 2 * half), F32)], axis=1)
        s = jnp.concatenate([-sin, sin, jnp.zeros((n, dim - 2 * half), F32)], axis=1)
        return jnp.tile(c, (1, LANES // dim)), jnp.tile(s, (1, LANES // dim))

    c128, s128 = tab(HEAD_DIM)
    c64, s64 = tab(IDX_DIM)
    return c128, s128, c64, s64


def _lru_kernel(xa_ref, cs_ref, h0_ref, cw_ref, cb_ref, wrg_ref, brg_ref, wig_ref, big_ref, lam_ref,
                hs_ref, ext_ref, hc_ref):
    tb, d = xa_ref.shape
    blk = d // LRU_BLOCKS

    @pl.when(pl.program_id(1) == 0)
    def _():
        ext_ref[0:8, :] = cs_ref[...]
        hc_ref[...] = h0_ref[...]

    ext_ref[8:8 + tb, :] = xa_ref[...]
    conv = ext_ref[5:5 + tb, :] * cw_ref[0:1, :]
    for j in range(1, CONV_W):
        conv = conv + ext_ref[5 + j:5 + j + tb, :] * cw_ref[j:j + 1, :]
    conv = conv + cb_ref[...]
    tail = ext_ref[tb:tb + 8, :]
    ext_ref[0:8, :] = tail

    xb = conv.astype(BF16)
    rs, gs = [], []
    for g in range(LRU_BLOCKS):
        xg = xb[:, g * blk:(g + 1) * blk]
        rs.append(jnp.dot(xg, wrg_ref[g], preferred_element_type=F32))
        gs.append(jnp.dot(xg, wig_ref[g], preferred_element_type=F32))
    r = jax.nn.sigmoid(jnp.concatenate(rs, axis=1) + brg_ref[...])
    ig = jax.nn.sigmoid(jnp.concatenate(gs, axis=1) + big_ref[...])
    lam = lam_ref[...]
    softplus_neg_lam = jnp.maximum(-lam, 0.0) + jnp.log1p(jnp.exp(-jnp.abs(lam)))
    log_a = (-LRU_C) * r * softplus_neg_lam
    a = jnp.exp(log_a)
    u = jnp.sqrt(-jnp.tanh(log_a) * (a * a + 1.0)) * (ig * conv)

    row = lax.broadcasted_iota(I32, (tb, d), 0)
    s = 1
    while s < tb:
        keep = row >= s
        a_prev = jnp.where(keep, pltpu.roll(a, s, 0), 1.0)
        u_prev = jnp.where(keep, pltpu.roll(u, s, 0), 0.0)
        u = a * u_prev + u
        a = a * a_prev
        s *= 2
    h = a * hc_ref[...] + u
    hs_ref[...] = h
    hc_ref[...] = h[tb - 1:tb, :]


def lru_call(u, cs8, h0, cw, cb, wrg, brg, wig, big, lam, nseq, t, tb):
    rows = u.shape[0]
    d = cw.shape[1]
    blk = d // LRU_BLOCKS
    nblk = t // tb
    vec = pl.BlockSpec((1, d), lambda b, i: (0, 0))
    wspec = pl.BlockSpec((LRU_BLOCKS, blk, blk), lambda b, i: (0, 0, 0))
    return pl.pallas_call(
        _lru_kernel,
        grid=(nseq, nblk),
        in_specs=[pl.BlockSpec((tb, d), lambda b, i: (b * nblk + i, 0)),
                  pl.BlockSpec((None, 8, d), lambda b, i: (b, 0, 0)),
                  pl.BlockSpec((None, 1, d), lambda b, i: (b, 0, 0)),
                  pl.BlockSpec((CONV_W, d), lambda b, i: (0, 0)),
                  vec, wspec, vec, wspec, vec, vec],
        out_specs=pl.BlockSpec((tb, d), lambda b, i: (b * nblk + i, 0)),
        out_shape=jax.ShapeDtypeStruct((rows, d), F32),
        scratch_shapes=[pltpu.VMEM((tb + 8, d), F32), pltpu.VMEM((1, d), F32)],
        compiler_params=_params(2),
        name="lru",
    )(u, cs8, h0, cw, cb, wrg, brg, wig, big, lam)


def _bias_kernel(relb_ref, o_ref):
    h = pl.program_id(0)
    qi = lax.broadcasted_iota(I32, (CHUNK, BAND_KEYS), 0)
    kj = lax.broadcasted_iota(I32, (CHUNK, BAND_KEYS), 1)
    idx = jnp.clip(qi + BAND_PAST - kj, -REL_CLIP, REL_CLIP) + REL_CLIP

    def body(r, acc):
        return jnp.where(idx == r, relb_ref[h, r], acc)

    o_ref[...] = lax.fori_loop(0, 2 * REL_CLIP + 1, body, jnp.zeros((CHUNK, BAND_KEYS), F32))


def bias_strip(relb):
    nh = relb.shape[0]
    return pl.pallas_call(
        _bias_kernel,
        grid=(nh,),
        in_specs=[pl.BlockSpec(memory_space=pltpu.SMEM)],
        out_specs=pl.BlockSpec((None, CHUNK, BAND_KEYS), lambda h: (h, 0, 0)),
        out_shape=jax.ShapeDtypeStruct((nh, CHUNK, BAND_KEYS), F32),
        compiler_params=_params(1),
        name="bias_strip",
    )(relb)


def _band_chunk(q, kwin, vwin, strip, min_col):
    s = _dot_nt(q, kwin) * (HEAD_DIM ** -0.5) + strip
    col = lax.broadcasted_iota(I32, s.shape, 1)
    s = jnp.where(col >= min_col, s, NEG)
    m = jnp.max(s, axis=-1, keepdims=True)
    p = jnp.exp(s - m)
    l = jnp.sum(p, axis=-1, keepdims=True)
    return jnp.dot(p.astype(BF16), vwin, preferred_element_type=F32) / l


def _band_prompt_kernel(q_ref, kp_ref, kc_ref, vp_ref, vc_ref, strip_ref, o_ref, kw_ref, vw_ref):
    i = pl.program_id(2)
    tb = q_ref.shape[0]
    kw_ref[0:tb, :] = kp_ref[...].astype(BF16)
    kw_ref[tb:2 * tb, :] = kc_ref[...].astype(BF16)
    vw_ref[0:tb, :] = vp_ref[...].astype(BF16)
    vw_ref[tb:2 * tb, :] = vc_ref[...].astype(BF16)
    strip = strip_ref[...]
    col = lax.broadcasted_iota(I32, (CHUNK, BAND_KEYS), 1)
    chunks = range(tb // CHUNK)
    scores = []
    for a in chunks:
        q = q_ref[a * CHUNK:(a + 1) * CHUNK, :].astype(BF16)
        s = _dot_nt(q, kw_ref[a * CHUNK:a * CHUNK + BAND_KEYS, :]) * (HEAD_DIM ** -0.5) + strip
        min_col = jnp.where(i > 0, 0, (BAND_PAST_CHUNKS - a) * CHUNK)
        scores.append(jnp.where(col >= min_col, s, NEG))
    maxes = [jnp.max(s, axis=-1, keepdims=True) for s in scores]
    probs = [jnp.exp(s - m) for s, m in zip(scores, maxes)]
    sums = [jnp.sum(p, axis=-1, keepdims=True) for p in probs]
    for a in chunks:
        pv = jnp.dot(probs[a].astype(BF16), vw_ref[a * CHUNK:a * CHUNK + BAND_KEYS, :],
                     preferred_element_type=F32)
        o_ref[a * CHUNK:(a + 1) * CHUNK, :] = pv / sums[a]


def band_prompt(u, strip, cols, nseq, t):
    rows = u.shape[0]
    tb = BAND_PAST
    nblk = t // tb
    nh = strip.shape[0]
    qo, ko, vo = cols["qb"] // HEAD_DIM, cols["kb"] // HEAD_DIM, cols["vb"] // HEAD_DIM
    cur = lambda off: pl.BlockSpec((tb, HEAD_DIM), lambda b, h, i: (b * nblk + i, off + h))
    prev = lambda off: pl.BlockSpec((tb, HEAD_DIM), lambda b, h, i: (b * nblk + jnp.maximum(i - 1, 0), off + h))
    return pl.pallas_call(
        _band_prompt_kernel,
        grid=(nseq, nh, nblk),
        in_specs=[cur(qo), prev(ko), cur(ko), prev(vo), cur(vo),
                  pl.BlockSpec((None, CHUNK, BAND_KEYS), lambda b, h, i: (h, 0, 0))],
        out_specs=pl.BlockSpec((tb, HEAD_DIM), lambda b, h, i: (b * nblk + i, h)),
        out_shape=jax.ShapeDtypeStruct((rows, nh * HEAD_DIM), F32),
        scratch_shapes=[pltpu.VMEM((2 * tb, HEAD_DIM), BF16), pltpu.VMEM((2 * tb, HEAD_DIM), BF16)],
        compiler_params=_params(3),
        name="band_prompt",
    )(u, u, u, u, u, strip)


def _band_sample_kernel(q_ref, kn_ref, vn_ref, ck_ref, cv_ref, strip_ref, o_ref, kw_ref, vw_ref):
    nh = strip_ref.shape[0]
    w = ck_ref.shape[0] // nh
    t = q_ref.shape[0]
    kw_ref[w:w + t, :] = kn_ref[...].astype(BF16)
    vw_ref[w:w + t, :] = vn_ref[...].astype(BF16)
    for h in range(nh):
        sl = slice(h * HEAD_DIM, (h + 1) * HEAD_DIM)
        kw_ref[0:w, sl] = ck_ref[pl.ds(h, w, stride=nh), :].astype(BF16)
        vw_ref[0:w, sl] = cv_ref[pl.ds(h, w, stride=nh), :].astype(BF16)
        o_ref[:, sl] = _band_chunk(q_ref[:, sl].astype(BF16), kw_ref[:, sl], vw_ref[:, sl], strip_ref[h], 0)


def band_sample(u, ck, cv, layer, strip, cols, nseq, t):
    rows = u.shape[0]
    nh = strip.shape[0]
    d = nh * HEAD_DIM
    w = ck.shape[2] // nh
    ublk = lambda off: pl.BlockSpec((t, d), lambda b, o=off // d: (b, o))
    cblk = pl.BlockSpec((None, None, w * nh, HEAD_DIM), lambda b: (layer, b, 0, 0))
    return pl.pallas_call(
        _band_sample_kernel,
        grid=(nseq,),
        in_specs=[ublk(cols["qb"]), ublk(cols["kb"]), ublk(cols["vb"]), cblk, cblk,
                  pl.BlockSpec((nh, CHUNK, BAND_KEYS), lambda b: (0, 0, 0))],
        out_specs=pl.BlockSpec((t, d), lambda b: (b, 0)),
        out_shape=jax.ShapeDtypeStruct((rows, d), F32),
        scratch_shapes=[pltpu.VMEM((w + t, d), BF16), pltpu.VMEM((w + t, d), BF16)],
        compiler_params=_params(1),
        name="band_sample",
    )(u, u, u, ck, cv, strip)


def _loop(n, body, init):
    if isinstance(n, int):
        val = init
        for k in range(n):
            val = body(k, val)
        return val
    return lax.fori_loop(0, n, body, init)


IDX_BITS = 16


def _f32_to_key(x):
    bits = lax.bitcast_convert_type(x, I32)
    return jnp.where(bits < 0, bits ^ 0x7FFFFFFF, bits)


def _dsa_core(qside, segs, store_out, sk_refs, s_refs, hi_refs, lo_refs, jstar_ref, mpart_ref, lpart_ref, acc_ref, *,
              topk, q_chunk0, n_q, n_dsa):
    qb = qside["t"].shape[0]
    t_t = qside["t"].T
    wrows = [t_t[IDX_DIM + h:IDX_DIM + h + 1, :] for h in range(N_IDX_HEADS)]

    def tiles(x):
        return x.reshape(x.shape[0] // SUBLANES, SUBLANES, qb)

    w8 = [jnp.broadcast_to(w, (SUBLANES, qb)) for w in wrows]
    qcol8 = lax.broadcasted_iota(I32, (SUBLANES, qb), 1)
    q_end8 = (q_chunk0 + qcol8 // CHUNK + 1) * CHUNK
    for seg, sk_ref, hi_ref, lo_ref in zip(segs, sk_refs, hi_refs, lo_refs):
        kblk = seg["kblk"]
        n_t = kblk // SUBLANES
        krow3 = (lax.broadcasted_iota(I32, (n_t, SUBLANES, qb), 0) * SUBLANES
                 + lax.broadcasted_iota(I32, (n_t, SUBLANES, qb), 1))
        padded = seg["valid"] < kblk or n_q < qb
        live3 = (krow3 < seg["valid"]) & (lax.broadcasted_iota(I32, (n_t, SUBLANES, qb), 2) < n_q)

        def score_blk(kb, carry, seg=seg, sk_ref=sk_ref, hi_ref=hi_ref, lo_ref=lo_ref, kblk=kblk, krow3=krow3, padded=padded, live3=live3):
            kib = seg["ki"](kb)
            acc = jnp.zeros(krow3.shape, F32)
            for h in range(N_IDX_HEADS):
                sc = _dot_nt(kib, qside["qi"](h))
                acc = acc + jnp.maximum(tiles(sc), 0.0) * w8[h]
            admissible = krow3 < q_end8 - (seg["pos0"] + kb * kblk)
            key = _f32_to_key(jnp.where(admissible, acc, NEG))
            key = jnp.where(live3, key, INT_MIN) if padded else key
            sk_ref[kb] = key
            key2 = key.reshape(kblk, qb)
            hi_ref[kb] = (key2 >> 16).astype(I16).reshape(hi_ref.shape[1:])
            lo_ref[kb] = ((key2 & 0xFFFF) - 2 ** 15).astype(I16).reshape(lo_ref.shape[1:])
            return carry

        _loop(seg["nblk"], score_blk, 0)

    def count(pred):
        part = jnp.zeros((SUBLANES, qb), I32)
        for seg, sk_ref in zip(segs, sk_refs):
            n_t = seg["kblk"] // SUBLANES
            off = (lax.broadcasted_iota(I32, (n_t, SUBLANES, qb), 0) * SUBLANES
                   + lax.broadcasted_iota(I32, (n_t, SUBLANES, qb), 1))

            def body(kb, part, seg=seg, sk_ref=sk_ref, off=off):
                idx = seg["pos0"] + kb * seg["kblk"] + off
                return part + jnp.sum(jnp.where(pred(sk_ref[kb], idx), 1, 0), axis=0)

            part = _loop(seg["nblk"], body, part)
        return jnp.broadcast_to(jnp.sum(part, axis=0, keepdims=True), (SUBLANES, qb))

    zero = jnp.zeros((SUBLANES, qb), I32)
    n_stored = sum(seg["nblk"] * seg["kblk"] for seg in segs)
    min16 = -(2 ** 15)

    def rep16(x):
        return jnp.broadcast_to(x[0:1, :], (2 * SUBLANES, qb)).astype(I16)

    def count16(refs, pred):
        part = jnp.zeros((2 * SUBLANES, qb), I16)
        for seg, ref in zip(segs, refs):
            def body(kb, part, ref=ref):
                hits = jnp.where(pred(ref[kb]), jnp.int16(1), jnp.int16(0))
                terms = [hits[j] for j in range(hits.shape[0])]
                while len(terms) > 1:
                    terms = [a + b for a, b in zip(terms[0::2], terms[1::2])] + terms[len(terms) & ~1:]
                return part + terms[0]

            part = _loop(seg["nblk"], body, part)
        return jnp.broadcast_to(jnp.sum(part.astype(I32), axis=0, keepdims=True), (SUBLANES, qb))

    def search16(refs, target):
        c_zero = count16(refs, lambda x: x >= jnp.zeros((2 * SUBLANES, qb), I16))
        t0 = jnp.where(c_zero >= target, 0, min16).astype(I32)
        c0 = jnp.where(c_zero >= target, c_zero, zero + n_stored).astype(I32)

        def step(i, state):
            t, c = state
            cand = t + jnp.left_shift(jnp.int32(1), 14 - i)
            cand16 = rep16(cand)
            c_cand = count16(refs, lambda x: x >= cand16)
            ok = c_cand >= target
            return jnp.where(ok, cand, t), jnp.where(ok, c_cand, c)

        return lax.fori_loop(0, 15, step, (t0, c0))

    thr_hi, c_ge_hi = search16(hi_refs, topk)
    thr_hi16 = rep16(thr_hi)
    c_gt_hi = count16(hi_refs, lambda x: x > thr_hi16)
    for seg, hi_ref, lo_ref in zip(segs, hi_refs, lo_refs):
        def keep_cut_bucket(kb, carry, hi_ref=hi_ref, lo_ref=lo_ref):
            lo_ref[kb] = jnp.where(hi_ref[kb] == thr_hi16, lo_ref[kb], jnp.int16(min16))
            return carry

        _loop(seg["nblk"], keep_cut_bucket, 0)
    thr_lo, c_ge_lo = search16(lo_refs, topk - c_gt_hi)
    c_ge_lo = jnp.where(thr_lo == min16, c_ge_hi - c_gt_hi, c_ge_lo)
    thr = thr_hi * (2 ** 16) + (thr_lo - min16)
    c_lo = c_gt_hi + c_ge_lo
    lo = jnp.maximum(thr, KEY_HALF_NEG + 1)

    big = 2 ** IDX_BITS
    jstar_ref[...] = jnp.full((SUBLANES, qb), big, I32)
    surplus = (c_lo > topk) & (thr > KEY_HALF_NEG)

    @pl.when(jnp.max(jnp.where(surplus, 1, 0)) > 0)
    def _():
        need = topk - count(lambda key, idx: key > thr)

        def idx_body(it, j):
            cand = j + jnp.left_shift(jnp.int32(1), IDX_BITS - 1 - it)
            f = count(lambda key, idx: (key == lo) & (idx < cand))
            return jnp.where(f <= need, cand, j)

        j = lax.fori_loop(0, IDX_BITS, idx_body, zero)
        jstar_ref[...] = jnp.where(surplus, j, big)

    jstar = jstar_ref[...]

    mpart_ref[...] = jnp.full(mpart_ref.shape, NEG, F32)
    lpart_ref[...] = jnp.zeros(lpart_ref.shape, F32)
    acc_ref[...] = jnp.zeros(acc_ref.shape, F32)

    for seg, sk_ref, s_ref in zip(segs, sk_refs, s_refs):
        kblk = seg["kblk"]
        n_t = kblk // SUBLANES
        krow3 = (lax.broadcasted_iota(I32, (n_t, SUBLANES, qb), 0) * SUBLANES
                 + lax.broadcasted_iota(I32, (n_t, SUBLANES, qb), 1))

        def pass_a(kb, carry, seg=seg, sk_ref=sk_ref, s_ref=s_ref, kblk=kblk, krow3=krow3):
            key = sk_ref[kb]
            mask = (key >= lo) & ((key > lo) | (krow3 < jstar - (seg["pos0"] + kb * kblk)))
            for h in range(n_dsa):
                s = tiles(_dot_nt(seg["k"](kb, h), qside["q"](h)))
                s = jnp.where(mask, s * (HEAD_DIM ** -0.5), NEG)
                s_ref[h, kb] = s
                mpart_ref[h] = jnp.maximum(mpart_ref[h], jnp.max(s, axis=0))
            return carry

        _loop(seg["nblk"], pass_a, 0)

    m8 = [jnp.broadcast_to(jnp.max(mpart_ref[h], axis=0, keepdims=True), (SUBLANES, qb)) for h in range(n_dsa)]

    for seg, s_ref in zip(segs, s_refs):
        def pass_b(kb, carry, seg=seg, s_ref=s_ref):
            for h in range(n_dsa):
                p = jnp.exp(s_ref[h, kb] - m8[h])
                lpart_ref[h] += jnp.sum(p, axis=0)
                acc_ref[h] += seg["pv"](kb, h, p.reshape(seg["kblk"], qb).astype(BF16))
            return carry

        _loop(seg["nblk"], pass_b, 0)

    for h in range(n_dsa):
        l = jnp.sum(lpart_ref[h], axis=0, keepdims=True)
        store_out(h, (acc_ref[h] / l).T)


def _blk_start(kb, kblk):
    return kb * kblk if isinstance(kb, int) else pl.multiple_of(kb * kblk, kblk)


def _dsa_scratch(qb, seg_shapes, n_dsa):
    return ([pltpu.VMEM((nblk, kblk // SUBLANES, SUBLANES, qb), I32) for nblk, kblk in seg_shapes]
            + [pltpu.VMEM((n_dsa, nblk, kblk // SUBLANES, SUBLANES, qb), F32) for nblk, kblk in seg_shapes]
            + 2 * [pltpu.VMEM((nblk, kblk // (2 * SUBLANES), 2 * SUBLANES, qb), I16) for nblk, kblk in seg_shapes]
            + [pltpu.VMEM((SUBLANES, qb), I32),
               pltpu.VMEM((n_dsa, SUBLANES, qb), F32), pltpu.VMEM((n_dsa, SUBLANES, qb), F32),
               pltpu.VMEM((n_dsa, HEAD_DIM, qb), F32)])


def _dsa_prompt_kernel(qi_ref, t_ref, q_ref, ki_ref, k_ref, vt_ref, o_ref, sk_ref, s_ref, hi_ref, lo_ref, jstar_ref,
                       mpart_ref, lpart_ref, acc_ref, *, kblk, topk):
    i = pl.program_id(1)
    qb = q_ref.shape[0]
    nblk = (i * qb + qb + kblk - 1) // kblk

    def head(h):
        return slice(h * HEAD_DIM, (h + 1) * HEAD_DIM)

    seg = dict(ki=lambda kb: ki_ref[pl.ds(_blk_start(kb, kblk), kblk), :],
               k=lambda kb, h: k_ref[pl.ds(_blk_start(kb, kblk), kblk), head(h)],
               pv=lambda kb, h, p: jnp.dot(vt_ref[head(h), pl.ds(_blk_start(kb, kblk), kblk)], p,
                                           preferred_element_type=F32),
               nblk=nblk, kblk=kblk, pos0=0, valid=kblk)
    qside = dict(t=t_ref[...], qi=lambda h: qi_ref[:, h * LANES:(h + 1) * LANES], q=lambda h: q_ref[:, head(h)])

    def store_out(h, x):
        o_ref[:, head(h)] = x

    _dsa_core(qside, [seg], store_out, [sk_ref], [s_ref], [hi_ref], [lo_ref], jstar_ref, mpart_ref, lpart_ref, acc_ref,
              topk=topk, q_chunk0=(i * qb) // CHUNK, n_q=qb, n_dsa=q_ref.shape[1] // HEAD_DIM)


def dsa_prompt(qi_pad, trot, q_bf, ki_bf, k_bf, v_t, nseq, t, qb, kblk, topk):
    rows, d = q_bf.shape
    n_dsa = d // HEAD_DIM
    nq = t // qb
    row_blk = lambda w: pl.BlockSpec((qb, w), lambda b, i: (b * nq + i, 0))
    seq_blk = lambda w: pl.BlockSpec((t, w), lambda b, i: (b, 0))
    return pl.pallas_call(
        functools.partial(_dsa_prompt_kernel, kblk=kblk, topk=topk),
        grid=(nseq, nq),
        in_specs=[row_blk(qi_pad.shape[1]), row_blk(LANES), row_blk(d),
                  seq_blk(LANES), seq_blk(d), pl.BlockSpec((d, t), lambda b, i: (0, b))],
        out_specs=row_blk(d),
        out_shape=jax.ShapeDtypeStruct((rows, d), F32),
        scratch_shapes=_dsa_scratch(qb, [(t // kblk, kblk)], n_dsa),
        compiler_params=_params(2),
        name="dsa_prompt",
    )(qi_pad, trot, q_bf, ki_bf, k_bf, v_t)


def _dsa_sample_kernel(qi_ref, t_ref, q_ref, kin_ref, kn_ref, vn_ref, cki_ref, ck_ref, cv_ref, o_ref,
                       qi_s, q_s, kin_s, kn_s, vn_s, kit_s, sk0_ref, sk1_ref, s0_ref, s1_ref,
                       hi0_ref, hi1_ref, lo0_ref, lo1_ref, jstar_ref,
                       mpart_ref, lpart_ref, acc_ref, *, kblk, topk):
    past = cki_ref.shape[2]
    tq = q_ref.shape[0] // 2
    qb = 2 * tq
    n_dsa = q_ref.shape[1] // HEAD_DIM
    pair_w = 2 * LANES

    def head(h):
        return slice(h * HEAD_DIM, (h + 1) * HEAD_DIM)

    def rows(a):
        return slice(a * tq, (a + 1) * tq)

    def half(h, a):
        return slice(h * pair_w + a * LANES, h * pair_w + (a + 1) * LANES)

    for ref in (qi_s, q_s, kin_s, kn_s):
        ref[...] = jnp.zeros(ref.shape, ref.dtype)
    for a in range(2):
        for h in range(N_IDX_HEADS):
            qi_s[rows(a), half(h, a)] = qi_ref[rows(a), h * LANES:(h + 1) * LANES]
        for h in range(n_dsa):
            q_s[rows(a), half(h, a)] = q_ref[rows(a), head(h)]
            kn_s[0:tq, half(h, a)] = kn_ref[rows(a), head(h)]
        kin_s[0:tq, half(0, a)] = kin_ref[rows(a), :]
        vn_s[a, 0:tq, :] = vn_ref[rows(a), :]
        vn_s[a, tq:, :] = jnp.zeros((qb - tq, vn_s.shape[2]), F32)
        kit_s[a, IDX_DIM:, :] = jnp.zeros((kit_s.shape[1] - IDX_DIM, kblk), F32)

    first_seq = lax.broadcasted_iota(I32, (HEAD_DIM, qb), 1) < tq

    def pair_pv(v_a, v_b, p):
        out_a = jnp.dot(v_a.T.astype(BF16), p, preferred_element_type=F32)
        out_b = jnp.dot(v_b.T.astype(BF16), p, preferred_element_type=F32)
        return jnp.where(first_seq, out_a, out_b)

    def cache_rows(ref, a, kb, h):
        return ref[a, pl.ds(kb * kblk * n_dsa + h, kblk, stride=n_dsa), :]

    def cache_ki(kb):
        parts = []
        for a in range(2):
            kit_s[a, 0:IDX_DIM, :] = cki_ref[a, :, pl.ds(kb * kblk, kblk)]
            parts.append(kit_s[a].T.astype(BF16))
        return jnp.concatenate(parts, axis=1)

    segs = [dict(ki=cache_ki,
                 k=lambda kb, h: jnp.concatenate([cache_rows(ck_ref, a, kb, h).astype(BF16) for a in range(2)],
                                                 axis=1),
                 pv=lambda kb, h, p: pair_pv(cache_rows(cv_ref, 0, kb, h), cache_rows(cv_ref, 1, kb, h), p),
                 nblk=past // kblk, kblk=kblk, pos0=0, valid=kblk),
            dict(ki=lambda kb: kin_s[...],
                 k=lambda kb, h: kn_s[:, h * pair_w:(h + 1) * pair_w],
                 pv=lambda kb, h, p: pair_pv(vn_s[0, :, head(h)], vn_s[1, :, head(h)], p),
                 nblk=1, kblk=qb, pos0=past, valid=tq)]
    qside = dict(t=t_ref[...],
                 qi=lambda h: qi_s[:, h * pair_w:(h + 1) * pair_w],
                 q=lambda h: q_s[:, h * pair_w:(h + 1) * pair_w])

    def store_out(h, x):
        o_ref[:, head(h)] = x

    _dsa_core(qside, segs, store_out, [sk0_ref, sk1_ref], [s0_ref, s1_ref], [hi0_ref, hi1_ref],
              [lo0_ref, lo1_ref], jstar_ref,
              mpart_ref, lpart_ref, acc_ref, topk=topk, q_chunk0=past // CHUNK, n_q=qb, n_dsa=n_dsa)


def dsa_sample(qi_pad, trot, q_bf, ki_bf, k_bf, u, vc_col, cki, ck, cv, layer, nseq, t, kblk, topk):
    rows, d = q_bf.shape
    n_dsa = d // HEAD_DIM
    past = cki.shape[3]
    qb = 2 * t
    assert qb == LANES and nseq % 2 == 0
    row_blk = lambda w, o=0: pl.BlockSpec((qb, w), lambda b, o=o: (b, o))
    cache_blk = lambda r, w: pl.BlockSpec((None, 2, r, w), lambda b: (layer, b, 0, 0))
    pair_scratch = [pltpu.VMEM((qb, 2 * qi_pad.shape[1]), BF16), pltpu.VMEM((qb, 2 * d), BF16),
                    pltpu.VMEM((qb, 2 * LANES), BF16), pltpu.VMEM((qb, 2 * d), BF16),
                    pltpu.VMEM((2, qb, d), F32), pltpu.VMEM((2, LANES, kblk), F32)]
    return pl.pallas_call(
        functools.partial(_dsa_sample_kernel, kblk=kblk, topk=topk),
        grid=(nseq // 2,),
        in_specs=[row_blk(qi_pad.shape[1]), row_blk(LANES), row_blk(d),
                  row_blk(LANES), row_blk(d), row_blk(d, vc_col),
                  cache_blk(cki.shape[2], past), cache_blk(past * n_dsa, HEAD_DIM),
                  cache_blk(past * n_dsa, HEAD_DIM)],
        out_specs=row_blk(d),
        out_shape=jax.ShapeDtypeStruct((rows, d), F32),
        scratch_shapes=pair_scratch + _dsa_scratch(qb, [(past // kblk, kblk), (1, qb)], n_dsa),
        compiler_params=pltpu.CompilerParams(dimension_semantics=("arbitrary",),
                                             vmem_limit_bytes=DSA_SAMPLE_VMEM_LIMIT),
        name="dsa_sample",
    )(qi_pad, trot, q_bf, ki_bf, k_bf, u, cki, ck, cv)


def _outproj_kernel(oa_ref, ob_ref, oc_ref, za_ref, zb_ref, zc_ref, g_ref, w_ref, x_ref, mod_ref, g2_ref,
                    *rest, d_lru, d_band, last):
    if last:
        yo_ref, y_ref = rest
    else:
        mod2_ref, xo_ref, h_ref, y_ref = rest

    def branch(o_ref, z_ref, lo, hi):
        o = o_ref[...]
        z = z_ref[...]
        ms = jnp.mean(o * o, axis=-1, keepdims=True)
        y = o * lax.rsqrt(ms + EPS) * g_ref[:, lo:hi]
        y_ref[:, lo:hi] = (y * (z * jax.nn.sigmoid(z))).astype(BF16)

    d_mix = y_ref.shape[1]
    branch(oa_ref, za_ref, 0, d_lru)
    branch(ob_ref, zb_ref, d_lru, d_lru + d_band)
    branch(oc_ref, zc_ref, d_lru + d_band, d_mix)
    out = jnp.dot(y_ref[...], w_ref[...], preferred_element_type=F32)
    n_seq = mod_ref.shape[0]
    t = out.shape[0] // n_seq
    for s in range(n_seq):
        rows = slice(s * t, (s + 1) * t)
        x_new = x_ref[rows, :] + mod_ref[s, 2:3, :] * out[rows, :]
        ms = jnp.mean(x_new * x_new, axis=-1, keepdims=True)
        normed = x_new * lax.rsqrt(ms + EPS) * g2_ref[...]
        if last:
            yo_ref[rows, :] = normed
        else:
            xo_ref[rows, :] = x_new
            h_ref[rows, :] = (normed * (1.0 + mod2_ref[s, 1:2, :]) + mod2_ref[s, 0:1, :]).astype(BF16)


def outproj(oa, ob, oc, u, g_branch, w_out, layer, x, mod3, g2, mod3_next, cols, nseq, t, tm):
    rows, d = x.shape
    d_lru, d_band, d_dsa = oa.shape[1], ob.shape[1], oc.shape[1]
    d_mix = d_lru + d_band + d_dsa
    nblk = max(t // tm, 1)
    seqs_per_blk = max(tm // t, 1)
    last = mod3_next is None
    rb = lambda w, o=0: pl.BlockSpec((tm, w), lambda b, i, o=o: (b * nblk + i, o))
    vec = pl.BlockSpec((1, d), lambda b, i: (0, 0))
    mod_spec = pl.BlockSpec((seqs_per_blk, 3, d), lambda b, i: (b, 0, 0))
    in_specs = [rb(d_lru), rb(d_band), rb(d_dsa),
                rb(d_lru, cols["za"] // d_lru), rb(d_band, cols["zb"] // d_band), rb(d_dsa, cols["zc"] // d_dsa),
                pl.BlockSpec((1, d_mix), lambda b, i: (0, 0)),
                pl.BlockSpec((None, d_mix, d), lambda b, i: (layer, 0, 0)),
                rb(d), mod_spec, vec]
    args = [oa, ob, oc, u, u, u, g_branch, w_out, x, mod3, g2]
    if last:
        out_specs = rb(d)
        out_shape = jax.ShapeDtypeStruct((rows, d), F32)
    else:
        in_specs.append(mod_spec)
        args.append(mod3_next)
        out_specs = [rb(d), rb(d)]
        out_shape = [jax.ShapeDtypeStruct((rows, d), F32), jax.ShapeDtypeStruct((rows, d), BF16)]
    return pl.pallas_call(
        functools.partial(_outproj_kernel, d_lru=d_lru, d_band=d_band, last=last),
        grid=(nseq // seqs_per_blk, nblk),
        in_specs=in_specs,
        out_specs=out_specs,
        out_shape=out_shape,
        scratch_shapes=[pltpu.VMEM((tm, d_mix), BF16)],
        compiler_params=_params(2),
        name="outproj",
    )(*args)


def _column_offsets(d_lru, d_band, d_dsa):
    names = ["xa", "za", "qb", "kb", "vb", "zb", "qc", "kc", "vc", "zc", "qi"]
    widths = [d_lru, d_lru, d_band, d_band, d_band, d_band, d_dsa, d_dsa, d_dsa, d_dsa, N_IDX_HEADS * IDX_DIM]
    cols, off = {}, 0
    for n, w in zip(names, widths):
        cols[n] = off
        off += w
    cols["main"] = off
    cols["d_lru"], cols["d_band"], cols["d_dsa"] = d_lru, d_band, d_dsa
    return cols


def _layer_stream(x, h, mod3, mod3_next, g_next, lw, layer, cols, nseq, t, tables, state, strip, prompt):
    d_lru, d_band, d_dsa = cols["d_lru"], cols["d_band"], cols["d_dsa"]
    n_band, n_dsa = d_band // HEAD_DIM, d_dsa // HEAD_DIM
    rows = nseq * t
    tb = 256 if t % 256 == 0 else t
    tm = 512 if rows % 512 == 0 else rows
    u = matmul(h, lw["w_in"], layer, cols["main"], tm, 1024)
    tail = matmul(h, lw["w_tail"], layer, LANES, tm, LANES)

    q_bf, k_rot, k_bf, qi_pad, t_rot, ki_bf, *maybe_vt = rope_call(u, tail, tables, cols,
                                                                 256 if rows % 256 == 0 else tb, prompt)

    conv_s, lru_s = state[0], state[1]
    cs8 = jnp.concatenate([jnp.zeros((nseq, 8 - (CONV_W - 1), d_lru), F32), conv_s], axis=1)
    hs = lru_call(u, cs8, lru_s.reshape(nseq, 1, d_lru), lw["conv_w"], lw["conv_b"], lw["w_rg"], lw["b_rg"],
                  lw["w_ig"], lw["b_ig"], lw["lam"], nseq, t, tb)

    if prompt:
        ob = band_prompt(u, strip, cols, nseq, t)
        oc = dsa_prompt(qi_pad, t_rot, q_bf, ki_bf, k_bf, maybe_vt[0], nseq, t, 128, 512, min(TOPK_MAX, t // 4))
    else:
        bk, bv, dk, dv, dik = state[2:]
        past = dik.shape[3]
        assert (past + t - 1) // CHUNK <= past // CHUNK and t == CHUNK and bk.shape[2] == BAND_PAST * n_band
        ob = band_sample(u, bk, bv, layer, strip, cols, nseq, t)
        oc = dsa_sample(qi_pad, t_rot, q_bf, ki_bf, k_bf, u, cols["vc"] // d_dsa, dik, dk, dv, layer,
                        nseq, t, 512, min(TOPK_MAX, (past + t) // 4))

    res = outproj(hs, ob, oc, u, lw["g_branch"], lw["w_out"], layer, x, mod3, g_next, mod3_next, cols, nseq, t,
                  256 if rows % 256 == 0 else tb)
    x_new, h_next = (res, None) if mod3_next is None else res

    u3 = u.reshape(nseq, t, -1)
    xa = u3[:, :, cols["xa"]:cols["xa"] + d_lru]
    if prompt:
        new_conv = xa[:, t - (CONV_W - 1):]
    else:
        new_conv = jnp.concatenate([conv_s, xa], axis=1)[:, -(CONV_W - 1):]
    nbr = min(BAND_PAST, t)
    new = (new_conv,
           hs.reshape(nseq, t, d_lru)[:, -1],
           u3[:, t - nbr:, cols["kb"]:cols["kb"] + d_band].reshape(nseq, nbr, n_band, HEAD_DIM),
           u3[:, t - nbr:, cols["vb"]:cols["vb"] + d_band].reshape(nseq, nbr, n_band, HEAD_DIM),
           k_rot.reshape(nseq, t, n_dsa, HEAD_DIM),
           u3[:, :, cols["vc"]:cols["vc"] + d_dsa].reshape(nseq, t, n_dsa, HEAD_DIM),
           t_rot.reshape(nseq, t, LANES)[:, :, :IDX_DIM])
    return x_new, h_next, new


def kernel(x_prompt, x_sample, c_prompt, c_sample, state_conv, state_lru, cache_band_k, cache_band_v,
           cache_dsa_k, cache_dsa_v, cache_dsa_idx_k, g_norm, w_ada, b_ada, w_in, conv_w, conv_b,
           w_rg, b_rg, w_ig, b_ig, lru_lambda, rel_bias, g_branch, w_out, g_final):
    depth = w_in.shape[0]
    nb_p, t_p, d = x_prompt.shape
    nb_s, t_s, _ = x_sample.shape
    past = cache_dsa_k.shape[2]
    d_lru = conv_w.shape[2]
    d_band = cache_band_k.shape[3] * HEAD_DIM
    d_dsa = cache_dsa_k.shape[3] * HEAD_DIM
    cols = _column_offsets(d_lru, d_band, d_dsa)
    n_main = cols["main"]

    mod = ada_all(jnp.concatenate([c_prompt, c_sample], axis=0), w_ada, b_ada)
    mod = mod.reshape(depth, nb_p + nb_s, 3, d)

    w_tail = jnp.pad(w_in[:, :, n_main:], ((0, 0), (0, 0), (0, LANES - (w_in.shape[2] - n_main)))).astype(BF16)
    w_out_bf = w_out.astype(BF16)
    w_rg_bf = w_rg.astype(BF16)
    w_ig_bf = w_ig.astype(BF16)

    band_k = cache_band_k.reshape(depth, nb_s, -1, HEAD_DIM)
    band_v = cache_band_v.reshape(depth, nb_s, -1, HEAD_DIM)
    dsa_k = cache_dsa_k.reshape(depth, nb_s, -1, HEAD_DIM)
    dsa_v = cache_dsa_v.reshape(depth, nb_s, -1, HEAD_DIM)
    idx_k_t = jnp.swapaxes(cache_dsa_idx_k, 2, 3)
    w_in_t = jnp.swapaxes(w_in, 1, 2)

    tab_p = rope_tables(jnp.tile(jnp.arange(t_p), nb_p))
    tab_s = rope_tables(jnp.tile(past + jnp.arange(t_s), nb_s))

    xp = x_prompt.reshape(nb_p * t_p, d)
    xs = x_sample.reshape(nb_s * t_s, d)
    zero_state = (jnp.zeros((nb_p, CONV_W - 1, d_lru), F32), jnp.zeros((nb_p, d_lru), F32))
    p_new, s_new = [], []
    hp = normmod(xp, g_norm[0][None], mod[0, :nb_p], nb_p, t_p, 256 if t_p % 256 == 0 else t_p)
    hsm = normmod(xs, g_norm[0][None], mod[0, nb_p:], nb_s, t_s, 256 if t_s % 256 == 0 else t_s)
    for l in range(depth):
        lw = dict(w_in=w_in_t, w_tail=w_tail, conv_w=conv_w[l], conv_b=conv_b[l][None],
                  w_rg=w_rg_bf[l], b_rg=b_rg[l][None], w_ig=w_ig_bf[l], b_ig=b_ig[l][None],
                  lam=lru_lambda[l][None], g_branch=g_branch[l][None], w_out=w_out_bf)
        last = l == depth - 1
        g_next = g_final[None] if last else g_norm[l + 1][None]
        modp_next = None if last else mod[l + 1, :nb_p]
        mods_next = None if last else mod[l + 1, nb_p:]
        strip = bias_strip(rel_bias[l])
        xp, hp, pn = _layer_stream(xp, hp, mod[l, :nb_p], modp_next, g_next, lw, l, cols, nb_p, t_p, tab_p,
                                   zero_state, strip, True)
        st = (state_conv[l], state_lru[l], band_k, band_v, dsa_k, dsa_v, idx_k_t)
        xs, hsm, sn = _layer_stream(xs, hsm, mod[l, nb_p:], mods_next, g_next, lw, l, cols, nb_s, t_s, tab_s,
                                    st, strip, False)
        p_new.append(pn)
        s_new.append(sn)

    y_prompt = xp.reshape(nb_p, t_p, d)
    y_sample = xs.reshape(nb_s, t_s, d)
    p_out = [jnp.stack([t[j] for t in p_new], axis=0) for j in range(7)]
    s_out = [jnp.stack([t[j] for t in s_new], axis=0) for j in range(7)]
    return (y_prompt, y_sample, *p_out, *s_out)
```

```python
import functools
import struct

import jax
import jax.numpy as jnp
from jax import lax
from jax.experimental import pallas as pl
from jax.experimental.pallas import tpu as pltpu

F32 = jnp.float32
BF16 = jnp.bfloat16
I32 = jnp.int32

CHUNK = 64
HEAD_DIM = 128
LRU_BLOCKS = 8
CONV_W = 4
LRU_C = 8.0
BAND_PAST_CHUNKS = 8
BAND_PAST = BAND_PAST_CHUNKS * CHUNK
BAND_KEYS = BAND_PAST + CHUNK
REL_CLIP = 256
N_IDX_HEADS = 16
IDX_DIM = 64
TOPK_MAX = 256
ROPE_THETA = 500000.0
ROPE_FRAC = 4
EPS = 1e-6
NEG = -1e30
LANES = 128
SUBLANES = 8

INT_MIN = -(2 ** 31)


def _sortable_key_of(x):
    b = struct.unpack("<i", struct.pack("<f", x))[0]
    return b ^ 0x7FFFFFFF if b < 0 else b


KEY_HALF_NEG = _sortable_key_of(NEG * 0.5)

VMEM_LIMIT = 48 * 1024 * 1024
DSA_SAMPLE_VMEM_LIMIT = 56 * 1024 * 1024
NT_DIMS = (((1,), (1,)), ((), ()))


def _params(n_grid):
    return pltpu.CompilerParams(dimension_semantics=("arbitrary",) * n_grid,
                                vmem_limit_bytes=VMEM_LIMIT)


def _dot_nt(a, b):
    return lax.dot_general(a, b, NT_DIMS, preferred_element_type=F32)


def _ada_kernel(c_ref, w_ref, b_ref, o_ref):
    c = c_ref[...]
    s = (c * jax.nn.sigmoid(c)).astype(BF16)
    o_ref[...] = jnp.dot(s, w_ref[...].astype(BF16), preferred_element_type=F32) + b_ref[...]


def ada_all(c_all, w_ada, b_ada):
    depth, d, n = w_ada.shape
    nb = c_all.shape[0]
    tn = 512
    return pl.pallas_call(
        _ada_kernel,
        grid=(depth, n // tn),
        in_specs=[pl.BlockSpec((nb, d), lambda l, j: (0, 0)),
                  pl.BlockSpec((None, d, tn), lambda l, j: (l, 0, j)),
                  pl.BlockSpec((None, 1, tn), lambda l, j: (l, 0, j))],
        out_specs=pl.BlockSpec((None, nb, tn), lambda l, j: (l, 0, j)),
        out_shape=jax.ShapeDtypeStruct((depth, nb, n), F32),
        compiler_params=_params(2),
        name="ada",
    )(c_all, w_ada, b_ada.reshape(depth, 1, n))


def _normmod_kernel(x_ref, g_ref, mod_ref, h_ref):
    x = x_ref[...]
    ms = jnp.mean(x * x, axis=-1, keepdims=True)
    y = x * lax.rsqrt(ms + EPS) * g_ref[...]
    shift = mod_ref[0:1, :]
    scale = mod_ref[1:2, :]
    h_ref[...] = (y * (1.0 + scale) + shift).astype(BF16)


def normmod(x, g, mod3, nseq, t, tb):
    rows, d = x.shape
    nblk = t // tb
    return pl.pallas_call(
        _normmod_kernel,
        grid=(nseq, nblk),
        in_specs=[pl.BlockSpec((tb, d), lambda b, i: (b * nblk + i, 0)),
                  pl.BlockSpec((1, d), lambda b, i: (0, 0)),
                  pl.BlockSpec((None, 3, d), lambda b, i: (b, 0, 0))],
        out_specs=pl.BlockSpec((tb, d), lambda b, i: (b * nblk + i, 0)),
        out_shape=jax.ShapeDtypeStruct((rows, d), BF16),
        compiler_params=_params(2),
        name="normmod",
    )(x, g, mod3)


def _mm_kernel(a_ref, b_ref, o_ref, *scratch):
    if scratch:
        wb_ref, = scratch

        @pl.when(pl.program_id(1) == 0)
        def _():
            wb_ref[...] = b_ref[...].T.astype(BF16)

        w = wb_ref[...]
    else:
        w = b_ref[...]
    o_ref[...] = jnp.dot(a_ref[...], w, preferred_element_type=F32)


def matmul(a, w, layer, ncols, tm, tn):
    m, k = a.shape
    if w.dtype == BF16:
        scratch = []
        w_spec = pl.BlockSpec((None, k, tn), lambda j, i: (layer, 0, j))
    else:
        scratch = [pltpu.VMEM((k, tn), BF16)]
        w_spec = pl.BlockSpec((None, tn, k), lambda j, i: (layer, j, 0))
    return pl.pallas_call(
        _mm_kernel,
        grid=(ncols // tn, m // tm),
        in_specs=[pl.BlockSpec((tm, k), lambda j, i: (i, 0)), w_spec],
        out_specs=pl.BlockSpec((tm, tn), lambda j, i: (i, j)),
        out_shape=jax.ShapeDtypeStruct((m, ncols), F32),
        scratch_shapes=scratch,
        compiler_params=_params(2),
        name="inproj",
    )(a, w)


def _rope_kernel(qc_ref, kc_ref, vc_ref, qi_ref, t_ref, c128_ref, s128_ref, c64_ref, s64_ref, *rest,
                 n_dsa, n_idx, n_aliased):
    q_out, kbf_out, qi_out, trot_out, kibf_out, kst_out, vst_out, *maybe_vt_out = rest[n_aliased:]
    tm = qc_ref.shape[0]
    lane = lax.broadcasted_iota(I32, (tm, LANES), 1)
    c128 = c128_ref[...]
    s128 = s128_ref[...]
    c64 = c64_ref[...]
    s64 = s64_ref[...]
    half128 = HEAD_DIM // ROPE_FRAC // 2
    half64 = IDX_DIM // ROPE_FRAC // 2

    def rope128(x):
        partner = jnp.where(lane < half128, pltpu.roll(x, LANES - half128, 1), pltpu.roll(x, half128, 1))
        return x * c128 + partner * s128

    def rope64(x):
        partner = jnp.where((lane & (IDX_DIM - 1)) < half64,
                            pltpu.roll(x, LANES - half64, 1), pltpu.roll(x, half64, 1))
        return x * c64 + partner * s64

    for h in range(n_dsa):
        sl = slice(h * LANES, (h + 1) * LANES)
        q_out[:, sl] = rope128(qc_ref[:, sl]).astype(BF16)
        kr = rope128(kc_ref[:, sl])
        kbf_out[:, sl] = kr.astype(BF16)
        kst_out[pl.ds(h, tm, stride=n_dsa), :] = kr
        vst_out[pl.ds(h, tm, stride=n_dsa), :] = vc_ref[:, sl]
    if maybe_vt_out:
        maybe_vt_out[0][...] = vc_ref[...].T.astype(BF16)
    low = lane < IDX_DIM
    for j in range(n_idx // 2):
        r = rope64(qi_ref[:, j * LANES:(j + 1) * LANES]) * (IDX_DIM ** -0.5)
        qi_out[:, (2 * j) * LANES:(2 * j + 1) * LANES] = jnp.where(low, r, 0.0).astype(BF16)
        qi_out[:, (2 * j + 1) * LANES:(2 * j + 2) * LANES] = jnp.where(
            low, pltpu.roll(r, IDX_DIM, 1), 0.0).astype(BF16)
    t = t_ref[...]
    r = rope64(t)
    trot_out[...] = jnp.where(low, r, jnp.where(lane < IDX_DIM + N_IDX_HEADS, t * (N_IDX_HEADS ** -0.5), 0.0))
    kibf_out[...] = jnp.where(low, r, 0.0).astype(BF16)


def rope_call(u, tail, tables, cols, tm, v_t, layer, depth, stacks):
    rows = u.shape[0]
    d_dsa = cols["d_dsa"]
    n_dsa = d_dsa // HEAD_DIM
    d_qi = N_IDX_HEADS * IDX_DIM
    c128, s128, c64, s64 = tables
    tab_spec = pl.BlockSpec((tm, LANES), lambda i: (i, 0))
    blk = lambda off: pl.BlockSpec((tm, d_dsa), lambda i, o=off // d_dsa: (i, o))
    row_spec = pl.BlockSpec((tm, d_dsa), lambda i: (i, 0))
    stack_spec = pl.BlockSpec((None, tm * n_dsa, HEAD_DIM), lambda i: (layer, i, 0))
    stack_shape = jax.ShapeDtypeStruct((depth, rows * n_dsa, HEAD_DIM), F32)
    out_specs = [row_spec, row_spec, pl.BlockSpec((tm, N_IDX_HEADS * LANES), lambda i: (i, 0)),
                 tab_spec, tab_spec, stack_spec, stack_spec]
    out_shape = [jax.ShapeDtypeStruct((rows, d_dsa), BF16),
                 jax.ShapeDtypeStruct((rows, d_dsa), BF16),
                 jax.ShapeDtypeStruct((rows, N_IDX_HEADS * LANES), BF16),
                 jax.ShapeDtypeStruct((rows, LANES), F32),
                 jax.ShapeDtypeStruct((rows, LANES), BF16),
                 stack_shape, stack_shape]
    if v_t:
        out_specs.append(pl.BlockSpec((d_dsa, tm), lambda i: (0, i)))
        out_shape.append(jax.ShapeDtypeStruct((d_dsa, rows), BF16))
    in_specs = [blk(cols["qc"]), blk(cols["kc"]), blk(cols["vc"]),
                pl.BlockSpec((tm, d_qi), lambda i, o=cols["qi"] // d_qi: (i, o)),
                tab_spec, tab_spec, tab_spec, tab_spec, tab_spec]
    args = [u, u, u, u, tail, c128, s128, c64, s64]
    aliases = {}
    if stacks is not None:
        aliases = {len(args): 5, len(args) + 1: 6}
        in_specs += [pl.BlockSpec(memory_space=pl.ANY)] * 2
        args += list(stacks)
    return pl.pallas_call(
        functools.partial(_rope_kernel, n_dsa=n_dsa, n_idx=N_IDX_HEADS, n_aliased=len(aliases)),
        grid=(rows // tm,),
        in_specs=in_specs,
        out_specs=out_specs,
        out_shape=out_shape,
        input_output_aliases=aliases,
        compiler_params=_params(1),
        name="rope",
    )(*args)


def rope_tables(pos):
    pos = pos.astype(F32)[:, None]
    n = pos.shape[0]

    def tab(dim):
        half = dim // ROPE_FRAC // 2
        inv = ROPE_THETA ** (-jnp.arange(half, dtype=F32) / half)
        ang = pos * inv[None]
        cos, sin = jnp.cos(ang), jnp.sin(ang)
        c = jnp.concatenate([cos, cos, jnp.ones((n, dim - 2 * half), F32)], axis=1)
        s = jnp.concatenate([-sin, sin, jnp.zeros((n, dim - 2 * half), F32)], axis=1)
        return jnp.tile(c, (1, LANES // dim)), jnp.tile(s, (1, LANES // dim))

    c128, s128 = tab(HEAD_DIM)
    c64, s64 = tab(IDX_DIM)
    return c128, s128, c64, s64


def _lru_kernel(xa_ref, cs_ref, h0_ref, cw_ref, cb_ref, wrg_ref, brg_ref, wig_ref, big_ref, lam_ref,
                hs_ref, ext_ref, hc_ref):
    tb, d = xa_ref.shape
    blk = d // LRU_BLOCKS

    @pl.when(pl.program_id(1) == 0)
    def _():
        ext_ref[0:8, :] = cs_ref[...]
        hc_ref[...] = h0_ref[...]

    ext_ref[8:8 + tb, :] = xa_ref[...]
    conv = ext_ref[5:5 + tb, :] * cw_ref[0:1, :]
    for j in range(1, CONV_W):
        conv = conv + ext_ref[5 + j:5 + j + tb, :] * cw_ref[j:j + 1, :]
    conv = conv + cb_ref[...]
    tail = ext_ref[tb:tb + 8, :]
    ext_ref[0:8, :] = tail

    xb = conv.astype(BF16)
    rs, gs = [], []
    for g in range(LRU_BLOCKS):
        xg = xb[:, g * blk:(g + 1) * blk]
        rs.append(jnp.dot(xg, wrg_ref[g], preferred_element_type=F32))
        gs.append(jnp.dot(xg, wig_ref[g], preferred_element_type=F32))
    r = jax.nn.sigmoid(jnp.concatenate(rs, axis=1) + brg_ref[...])
    ig = jax.nn.sigmoid(jnp.concatenate(gs, axis=1) + big_ref[...])
    lam = lam_ref[...]
    softplus_neg_lam = jnp.maximum(-lam, 0.0) + jnp.log1p(jnp.exp(-jnp.abs(lam)))
    log_a = (-LRU_C) * r * softplus_neg_lam
    a = jnp.exp(log_a)
    u = jnp.sqrt(-jnp.tanh(log_a) * (a * a + 1.0)) * (ig * conv)

    row = lax.broadcasted_iota(I32, (tb, d), 0)
    s = 1
    while s < tb:
        keep = row >= s
        a_prev = jnp.where(keep, pltpu.roll(a, s, 0), 1.0)
        u_prev = jnp.where(keep, pltpu.roll(u, s, 0), 0.0)
        u = a * u_prev + u
        a = a * a_prev
        s *= 2
    h = a * hc_ref[...] + u
    hs_ref[...] = h
    hc_ref[...] = h[tb - 1:tb, :]


def lru_call(u, cs8, h0, cw, cb, wrg, brg, wig, big, lam, nseq, t, tb):
    rows = u.shape[0]
    d = cw.shape[1]
    blk = d // LRU_BLOCKS
    nblk = t // tb
    vec = pl.BlockSpec((1, d), lambda b, i: (0, 0))
    wspec = pl.BlockSpec((LRU_BLOCKS, blk, blk), lambda b, i: (0, 0, 0))
    return pl.pallas_call(
        _lru_kernel,
        grid=(nseq, nblk),
        in_specs=[pl.BlockSpec((tb, d), lambda b, i: (b * nblk + i, 0)),
                  pl.BlockSpec((None, 8, d), lambda b, i: (b, 0, 0)),
                  pl.BlockSpec((None, 1, d), lambda b, i: (b, 0, 0)),
                  pl.BlockSpec((CONV_W, d), lambda b, i: (0, 0)),
                  vec, wspec, vec, wspec, vec, vec],
        out_specs=pl.BlockSpec((tb, d), lambda b, i: (b * nblk + i, 0)),
        out_shape=jax.ShapeDtypeStruct((rows, d), F32),
        scratch_shapes=[pltpu.VMEM((tb + 8, d), F32), pltpu.VMEM((1, d), F32)],
        compiler_params=_params(2),
        name="lru",
    )(u, cs8, h0, cw, cb, wrg, brg, wig, big, lam)


def _bias_kernel(relb_ref, o_ref):
    h = pl.program_id(0)
    qi = lax.broadcasted_iota(I32, (CHUNK, BAND_KEYS), 0)
    kj = lax.broadcasted_iota(I32, (CHUNK, BAND_KEYS), 1)
    idx = jnp.clip(qi + BAND_PAST - kj, -REL_CLIP, REL_CLIP) + REL_CLIP

    def body(r, acc):
        return jnp.where(idx == r, relb_ref[h, r], acc)

    o_ref[...] = lax.fori_loop(0, 2 * REL_CLIP + 1, body, jnp.zeros((CHUNK, BAND_KEYS), F32))


def bias_strip(relb):
    nh = relb.shape[0]
    return pl.pallas_call(
        _bias_kernel,
        grid=(nh,),
        in_specs=[pl.BlockSpec(memory_space=pltpu.SMEM)],
        out_specs=pl.BlockSpec((None, CHUNK, BAND_KEYS), lambda h: (h, 0, 0)),
        out_shape=jax.ShapeDtypeStruct((nh, CHUNK, BAND_KEYS), F32),
        compiler_params=_params(1),
        name="bias_strip",
    )(relb)


def _band_prompt_kernel(q_ref, kp_ref, kc_ref, vp_ref, vc_ref, strip_ref, o_ref, kw_ref, vw_ref):
    i = pl.program_id(2)
    tb = q_ref.shape[0]
    kw_ref[0:tb, :] = kp_ref[...].astype(BF16)
    kw_ref[tb:2 * tb, :] = kc_ref[...].astype(BF16)
    vw_ref[0:tb, :] = vp_ref[...].astype(BF16)
    vw_ref[tb:2 * tb, :] = vc_ref[...].astype(BF16)
    strip = strip_ref[...]
    col = lax.broadcasted_iota(I32, (CHUNK, BAND_KEYS), 1)
    chunks = range(tb // CHUNK)
    scores = []
    for a in chunks:
        q = q_ref[a * CHUNK:(a + 1) * CHUNK, :].astype(BF16)
        s = _dot_nt(q, kw_ref[a * CHUNK:a * CHUNK + BAND_KEYS, :]) * (HEAD_DIM ** -0.5) + strip
        min_col = jnp.where(i > 0, 0, (BAND_PAST_CHUNKS - a) * CHUNK)
        scores.append(jnp.where(col >= min_col, s, NEG))
    maxes = [jnp.max(s, axis=-1, keepdims=True) for s in scores]
    probs = [jnp.exp(s - m) for s, m in zip(scores, maxes)]
    sums = [jnp.sum(p, axis=-1, keepdims=True) for p in probs]
    for a in chunks:
        pv = jnp.dot(probs[a].astype(BF16), vw_ref[a * CHUNK:a * CHUNK + BAND_KEYS, :],
                     preferred_element_type=F32)
        o_ref[a * CHUNK:(a + 1) * CHUNK, :] = pv / sums[a]


def band_prompt(u, strip, cols, nseq, t):
    rows = u.shape[0]
    tb = BAND_PAST
    nblk = t // tb
    nh = strip.shape[0]
    qo, ko, vo = cols["qb"] // HEAD_DIM, cols["kb"] // HEAD_DIM, cols["vb"] // HEAD_DIM
    cur = lambda off: pl.BlockSpec((tb, HEAD_DIM), lambda b, h, i: (b * nblk + i, off + h))
    prev = lambda off: pl.BlockSpec((tb, HEAD_DIM), lambda b, h, i: (b * nblk + jnp.maximum(i - 1, 0), off + h))
    return pl.pallas_call(
        _band_prompt_kernel,
        grid=(nseq, nh, nblk),
        in_specs=[cur(qo), prev(ko), cur(ko), prev(vo), cur(vo),
                  pl.BlockSpec((None, CHUNK, BAND_KEYS), lambda b, h, i: (h, 0, 0))],
        out_specs=pl.BlockSpec((tb, HEAD_DIM), lambda b, h, i: (b * nblk + i, h)),
        out_shape=jax.ShapeDtypeStruct((rows, nh * HEAD_DIM), F32),
        scratch_shapes=[pltpu.VMEM((2 * tb, HEAD_DIM), BF16), pltpu.VMEM((2 * tb, HEAD_DIM), BF16)],
        compiler_params=_params(3),
        name="band_prompt",
    )(u, u, u, u, u, strip)


def _band_sample_kernel(q_ref, kn_ref, vn_ref, ck_ref, cv_ref, strip_ref, o_ref, kw_ref, vw_ref):
    nh = strip_ref.shape[0]
    w = ck_ref.shape[0] // nh
    t = q_ref.shape[0]
    kw_ref[w:w + t, :] = kn_ref[...].astype(BF16)
    vw_ref[w:w + t, :] = vn_ref[...].astype(BF16)
    heads = [slice(h * HEAD_DIM, (h + 1) * HEAD_DIM) for h in range(nh)]
    for h, sl in enumerate(heads):
        kw_ref[0:w, sl] = ck_ref[pl.ds(h, w, stride=nh), :].astype(BF16)
        vw_ref[0:w, sl] = cv_ref[pl.ds(h, w, stride=nh), :].astype(BF16)
    scores = [_dot_nt(q_ref[:, sl].astype(BF16), kw_ref[:, sl]) * (HEAD_DIM ** -0.5) + strip_ref[h]
              for h, sl in enumerate(heads)]
    maxes = [jnp.max(s, axis=-1, keepdims=True) for s in scores]
    probs = [jnp.exp(s - m) for s, m in zip(scores, maxes)]
    sums = [jnp.sum(p, axis=-1, keepdims=True) for p in probs]
    for h, sl in enumerate(heads):
        pv = jnp.dot(probs[h].astype(BF16), vw_ref[:, sl], preferred_element_type=F32)
        o_ref[:, sl] = pv / sums[h]


def band_sample(u, ck, cv, layer, strip, cols, nseq, t):
    rows = u.shape[0]
    nh = strip.shape[0]
    d = nh * HEAD_DIM
    w = ck.shape[2] // nh
    ublk = lambda off: pl.BlockSpec((t, d), lambda b, o=off // d: (b, o))
    cblk = pl.BlockSpec((None, None, w * nh, HEAD_DIM), lambda b: (layer, b, 0, 0))
    return pl.pallas_call(
        _band_sample_kernel,
        grid=(nseq,),
        in_specs=[ublk(cols["qb"]), ublk(cols["kb"]), ublk(cols["vb"]), cblk, cblk,
                  pl.BlockSpec((nh, CHUNK, BAND_KEYS), lambda b: (0, 0, 0))],
        out_specs=pl.BlockSpec((t, d), lambda b: (b, 0)),
        out_shape=jax.ShapeDtypeStruct((rows, d), F32),
        scratch_shapes=[pltpu.VMEM((w + t, d), BF16), pltpu.VMEM((w + t, d), BF16)],
        compiler_params=_params(1),
        name="band_sample",
    )(u, u, u, ck, cv, strip)


def _loop(n, body, init):
    if isinstance(n, int):
        val = init
        for k in range(n):
            val = body(k, val)
        return val
    return lax.fori_loop(0, n, body, init)


IDX_BITS = 16


def _f32_to_key(x):
    bits = lax.bitcast_convert_type(x, I32)
    return jnp.where(bits < 0, bits ^ 0x7FFFFFFF, bits)


def _dsa_core(qside, segs, store_out, sk_refs, s_refs, jstar_ref, mpart_ref, lpart_ref, acc_ref, *,
              topk, q_chunk0, n_q, n_dsa):
    qb = qside["t"].shape[0]
    t_t = qside["t"].T
    wrows = [t_t[IDX_DIM + h:IDX_DIM + h + 1, :] for h in range(N_IDX_HEADS)]

    def tiles(x):
        return x.reshape(x.shape[0] // SUBLANES, SUBLANES, qb)

    w8 = [jnp.broadcast_to(w, (SUBLANES, qb)) for w in wrows]
    qcol8 = lax.broadcasted_iota(I32, (SUBLANES, qb), 1)
    q_end8 = (q_chunk0 + qcol8 // CHUNK + 1) * CHUNK
    for seg, sk_ref in zip(segs, sk_refs):
        kblk = seg["kblk"]
        n_t = kblk // SUBLANES
        krow3 = (lax.broadcasted_iota(I32, (n_t, SUBLANES, qb), 0) * SUBLANES
                 + lax.broadcasted_iota(I32, (n_t, SUBLANES, qb), 1))
        padded = seg["valid"] < kblk or n_q < qb
        live3 = (krow3 < seg["valid"]) & (lax.broadcasted_iota(I32, (n_t, SUBLANES, qb), 2) < n_q)

        def score_blk(kb, carry, seg=seg, sk_ref=sk_ref, kblk=kblk, krow3=krow3, padded=padded, live3=live3):
            kib = seg["ki"](kb)
            acc = jnp.zeros(krow3.shape, F32)
            for h in range(N_IDX_HEADS):
                sc = _dot_nt(kib, qside["qi"](h))
                acc = acc + jnp.maximum(tiles(sc), 0.0) * w8[h]
            admissible = krow3 < q_end8 - (seg["pos0"] + kb * kblk)
            key = _f32_to_key(jnp.where(admissible, acc, NEG))
            sk_ref[kb] = jnp.where(live3, key, INT_MIN) if padded else key
            return carry

        _loop(seg["nblk"], score_blk, 0)

    def count(pred):
        part = jnp.zeros((SUBLANES, qb), I32)
        for seg, sk_ref in zip(segs, sk_refs):
            n_t = seg["kblk"] // SUBLANES
            off = (lax.broadcasted_iota(I32, (n_t, SUBLANES, qb), 0) * SUBLANES
                   + lax.broadcasted_iota(I32, (n_t, SUBLANES, qb), 1))

            def body(kb, part, seg=seg, sk_ref=sk_ref, off=off):
                idx = seg["pos0"] + kb * seg["kblk"] + off
                return part + jnp.sum(jnp.where(pred(sk_ref[kb], idx), 1, 0), axis=0)

            part = _loop(seg["nblk"], body, part)
        return jnp.broadcast_to(jnp.sum(part, axis=0, keepdims=True), (SUBLANES, qb))

    zero = jnp.zeros((SUBLANES, qb), I32)
    n_stored = sum(seg["nblk"] * seg["kblk"] for seg in segs)
    c_zero = count(lambda key, idx: key >= zero)
    thr0 = jnp.where(c_zero >= topk, 0, INT_MIN).astype(I32)
    c_lo0 = jnp.where(c_zero >= topk, c_zero, zero + n_stored)

    def bit_step(bit, state):
        t, c = state
        cand = t + jnp.left_shift(jnp.int32(1), bit)
        c_cand = count(lambda key, idx: key >= cand)
        ok = c_cand >= topk
        return jnp.where(ok, cand, t), jnp.where(ok, c_cand, c)

    thr, c_lo = lax.fori_loop(0, 31, lambda i, st: bit_step(30 - i, st), (thr0, c_lo0))
    lo = jnp.maximum(thr, KEY_HALF_NEG + 1)

    big = 2 ** IDX_BITS
    jstar_ref[...] = jnp.full((SUBLANES, qb), big, I32)
    surplus = (c_lo > topk) & (thr > KEY_HALF_NEG)

    @pl.when(jnp.max(jnp.where(surplus, 1, 0)) > 0)
    def _():
        need = topk - count(lambda key, idx: key > thr)

        def idx_body(it, j):
            cand = j + jnp.left_shift(jnp.int32(1), IDX_BITS - 1 - it)
            f = count(lambda key, idx: (key == lo) & (idx < cand))
            return jnp.where(f <= need, cand, j)

        j = lax.fori_loop(0, IDX_BITS, idx_body, zero)
        jstar_ref[...] = jnp.where(surplus, j, big)

    jstar = jstar_ref[...]

    mpart_ref[...] = jnp.full(mpart_ref.shape, NEG, F32)
    lpart_ref[...] = jnp.zeros(lpart_ref.shape, F32)
    acc_ref[...] = jnp.zeros(acc_ref.shape, F32)

    for seg, sk_ref, s_ref in zip(segs, sk_refs, s_refs):
        kblk = seg["kblk"]
        n_t = kblk // SUBLANES
        krow3 = (lax.broadcasted_iota(I32, (n_t, SUBLANES, qb), 0) * SUBLANES
                 + lax.broadcasted_iota(I32, (n_t, SUBLANES, qb), 1))

        def pass_a(kb, carry, seg=seg, sk_ref=sk_ref, s_ref=s_ref, kblk=kblk, krow3=krow3):
            key = sk_ref[kb]
            mask = (key >= lo) & ((key > lo) | (krow3 < jstar - (seg["pos0"] + kb * kblk)))
            for h in range(n_dsa):
                s = tiles(_dot_nt(seg["k"](kb, h), qside["q"](h)))
                s = jnp.where(mask, s * (HEAD_DIM ** -0.5), NEG)
                s_ref[h, kb] = s
                mpart_ref[h] = jnp.maximum(mpart_ref[h], jnp.max(s, axis=0))
            return carry

        _loop(seg["nblk"], pass_a, 0)

    m8 = [jnp.broadcast_to(jnp.max(mpart_ref[h], axis=0, keepdims=True), (SUBLANES, qb)) for h in range(n_dsa)]

    for seg, s_ref in zip(segs, s_refs):
        def pass_b(kb, carry, seg=seg, s_ref=s_ref):
            for h in range(n_dsa):
                p = jnp.exp(s_ref[h, kb] - m8[h])
                lpart_ref[h] += jnp.sum(p, axis=0)
                acc_ref[h] += seg["pv"](kb, h, p.reshape(seg["kblk"], qb).astype(BF16))
            return carry

        _loop(seg["nblk"], pass_b, 0)

    for h in range(n_dsa):
        l = jnp.sum(lpart_ref[h], axis=0, keepdims=True)
        store_out(h, (acc_ref[h] / l).T)


def _blk_start(kb, kblk):
    return kb * kblk if isinstance(kb, int) else pl.multiple_of(kb * kblk, kblk)


def _dsa_scratch(qb, seg_shapes, n_dsa):
    return ([pltpu.VMEM((nblk, kblk // SUBLANES, SUBLANES, qb), I32) for nblk, kblk in seg_shapes]
            + [pltpu.VMEM((n_dsa, nblk, kblk // SUBLANES, SUBLANES, qb), F32) for nblk, kblk in seg_shapes]            + [pltpu.VMEM((SUBLANES, qb), I32),
               pltpu.VMEM((n_dsa, SUBLANES, qb), F32), pltpu.VMEM((n_dsa, SUBLANES, qb), F32),
               pltpu.VMEM((n_dsa, HEAD_DIM, qb), F32)])


def _dsa_prompt_kernel(qi_ref, t_ref, q_ref, ki_ref, k_ref, vt_ref, o_ref, sk_ref, s_ref, jstar_ref,
                       mpart_ref, lpart_ref, acc_ref, *, kblk, topk):
    i = pl.program_id(1)
    qb = q_ref.shape[0]
    nblk = (i * qb + qb + kblk - 1) // kblk

    def head(h):
        return slice(h * HEAD_DIM, (h + 1) * HEAD_DIM)

    seg = dict(ki=lambda kb: ki_ref[pl.ds(_blk_start(kb, kblk), kblk), :],
               k=lambda kb, h: k_ref[pl.ds(_blk_start(kb, kblk), kblk), head(h)],
               pv=lambda kb, h, p: jnp.dot(vt_ref[head(h), pl.ds(_blk_start(kb, kblk), kblk)], p,
                                           preferred_element_type=F32),
               nblk=nblk, kblk=kblk, pos0=0, valid=kblk)
    qside = dict(t=t_ref[...], qi=lambda h: qi_ref[:, h * LANES:(h + 1) * LANES], q=lambda h: q_ref[:, head(h)])

    def store_out(h, x):
        o_ref[:, head(h)] = x

    _dsa_core(qside, [seg], store_out, [sk_ref], [s_ref], jstar_ref, mpart_ref, lpart_ref, acc_ref,
              topk=topk, q_chunk0=(i * qb) // CHUNK, n_q=qb, n_dsa=q_ref.shape[1] // HEAD_DIM)


def dsa_prompt(qi_pad, trot, q_bf, ki_bf, k_bf, v_t, nseq, t, qb, kblk, topk):
    rows, d = q_bf.shape
    n_dsa = d // HEAD_DIM
    nq = t // qb
    row_blk = lambda w: pl.BlockSpec((qb, w), lambda b, i: (b * nq + i, 0))
    seq_blk = lambda w: pl.BlockSpec((t, w), lambda b, i: (b, 0))
    return pl.pallas_call(
        functools.partial(_dsa_prompt_kernel, kblk=kblk, topk=topk),
        grid=(nseq, nq),
        in_specs=[row_blk(qi_pad.shape[1]), row_blk(LANES), row_blk(d),
                  seq_blk(LANES), seq_blk(d), pl.BlockSpec((d, t), lambda b, i: (0, b))],
        out_specs=row_blk(d),
        out_shape=jax.ShapeDtypeStruct((rows, d), F32),
        scratch_shapes=_dsa_scratch(qb, [(t // kblk, kblk)], n_dsa),
        compiler_params=_params(2),
        name="dsa_prompt",
    )(qi_pad, trot, q_bf, ki_bf, k_bf, v_t)


def _dsa_sample_kernel(qi_ref, t_ref, q_ref, kin_ref, kn_ref, vn_ref, cki_ref, ck_ref, cv_ref, o_ref,
                       qi_s, q_s, kin_s, kn_s, vn_s, kit_s, sk0_ref, sk1_ref, s0_ref, s1_ref, jstar_ref,
                       mpart_ref, lpart_ref, acc_ref, *, kblk, topk):
    past = cki_ref.shape[2]
    tq = q_ref.shape[0] // 2
    qb = 2 * tq
    n_dsa = q_ref.shape[1] // HEAD_DIM
    pair_w = 2 * LANES

    def head(h):
        return slice(h * HEAD_DIM, (h + 1) * HEAD_DIM)

    def rows(a):
        return slice(a * tq, (a + 1) * tq)

    def half(h, a):
        return slice(h * pair_w + a * LANES, h * pair_w + (a + 1) * LANES)

    for ref in (qi_s, q_s, kin_s, kn_s):
        ref[...] = jnp.zeros(ref.shape, ref.dtype)
    for a in range(2):
        for h in range(N_IDX_HEADS):
            qi_s[rows(a), half(h, a)] = qi_ref[rows(a), h * LANES:(h + 1) * LANES]
        for h in range(n_dsa):
            q_s[rows(a), half(h, a)] = q_ref[rows(a), head(h)]
            kn_s[0:tq, half(h, a)] = kn_ref[rows(a), head(h)]
        kin_s[0:tq, half(0, a)] = kin_ref[rows(a), :]
        vn_s[a, 0:tq, :] = vn_ref[rows(a), :]
        vn_s[a, tq:, :] = jnp.zeros((qb - tq, vn_s.shape[2]), F32)
        kit_s[a, IDX_DIM:, :] = jnp.zeros((kit_s.shape[1] - IDX_DIM, kblk), F32)

    first_seq = lax.broadcasted_iota(I32, (HEAD_DIM, qb), 1) < tq

    def pair_pv(v_a, v_b, p):
        out_a = jnp.dot(v_a.T.astype(BF16), p, preferred_element_type=F32)
        out_b = jnp.dot(v_b.T.astype(BF16), p, preferred_element_type=F32)
        return jnp.where(first_seq, out_a, out_b)

    def cache_rows(ref, a, kb, h):
        return ref[a, pl.ds(kb * kblk * n_dsa + h, kblk, stride=n_dsa), :]

    def cache_ki(kb):
        parts = []
        for a in range(2):
            kit_s[a, 0:IDX_DIM, :] = cki_ref[a, :, pl.ds(kb * kblk, kblk)]
            parts.append(kit_s[a].T.astype(BF16))
        return jnp.concatenate(parts, axis=1)

    segs = [dict(ki=cache_ki,
                 k=lambda kb, h: jnp.concatenate([cache_rows(ck_ref, a, kb, h).astype(BF16) for a in range(2)],
                                                 axis=1),
                 pv=lambda kb, h, p: pair_pv(cache_rows(cv_ref, 0, kb, h), cache_rows(cv_ref, 1, kb, h), p),
                 nblk=past // kblk, kblk=kblk, pos0=0, valid=kblk),
            dict(ki=lambda kb: kin_s[...],
                 k=lambda kb, h: kn_s[:, h * pair_w:(h + 1) * pair_w],
                 pv=lambda kb, h, p: pair_pv(vn_s[0, :, head(h)], vn_s[1, :, head(h)], p),
                 nblk=1, kblk=qb, pos0=past, valid=tq)]
    qside = dict(t=t_ref[...],
                 qi=lambda h: qi_s[:, h * pair_w:(h + 1) * pair_w],
                 q=lambda h: q_s[:, h * pair_w:(h + 1) * pair_w])

    def store_out(h, x):
        o_ref[:, head(h)] = x

    _dsa_core(qside, segs, store_out, [sk0_ref, sk1_ref], [s0_ref, s1_ref], jstar_ref,
              mpart_ref, lpart_ref, acc_ref, topk=topk, q_chunk0=past // CHUNK, n_q=qb, n_dsa=n_dsa)


def dsa_sample(qi_pad, trot, q_bf, ki_bf, k_bf, u, vc_col, cki, ck, cv, layer, nseq, t, kblk, topk):
    rows, d = q_bf.shape
    n_dsa = d // HEAD_DIM
    past = cki.shape[3]
    qb = 2 * t
    assert qb == LANES and nseq % 2 == 0
    row_blk = lambda w, o=0: pl.BlockSpec((qb, w), lambda b, o=o: (b, o))
    cache_blk = lambda r, w: pl.BlockSpec((None, 2, r, w), lambda b: (layer, b, 0, 0))
    pair_scratch = [pltpu.VMEM((qb, 2 * qi_pad.shape[1]), BF16), pltpu.VMEM((qb, 2 * d), BF16),
                    pltpu.VMEM((qb, 2 * LANES), BF16), pltpu.VMEM((qb, 2 * d), BF16),
                    pltpu.VMEM((2, qb, d), F32), pltpu.VMEM((2, LANES, kblk), F32)]
    return pl.pallas_call(
        functools.partial(_dsa_sample_kernel, kblk=kblk, topk=topk),
        grid=(nseq // 2,),
        in_specs=[row_blk(qi_pad.shape[1]), row_blk(LANES), row_blk(d),
                  row_blk(LANES), row_blk(d), row_blk(d, vc_col),
                  cache_blk(cki.shape[2], past), cache_blk(past * n_dsa, HEAD_DIM),
                  cache_blk(past * n_dsa, HEAD_DIM)],
        out_specs=row_blk(d),
        out_shape=jax.ShapeDtypeStruct((rows, d), F32),
        scratch_shapes=pair_scratch + _dsa_scratch(qb, [(past // kblk, kblk), (1, qb)], n_dsa),
        compiler_params=pltpu.CompilerParams(dimension_semantics=("arbitrary",),
                                             vmem_limit_bytes=DSA_SAMPLE_VMEM_LIMIT),
        name="dsa_sample",
    )(qi_pad, trot, q_bf, ki_bf, k_bf, u, cki, ck, cv)


def _outproj_kernel(oa_ref, ob_ref, oc_ref, za_ref, zb_ref, zc_ref, g_ref, w_ref, x_ref, mod_ref, g2_ref,
                    *rest, d_lru, d_band, last):
    if last:
        yo_ref, y_ref = rest
    else:
        mod2_ref, xo_ref, h_ref, y_ref = rest

    def branch(o_ref, z_ref, lo, hi):
        o = o_ref[...]
        z = z_ref[...]
        ms = jnp.mean(o * o, axis=-1, keepdims=True)
        y = o * lax.rsqrt(ms + EPS) * g_ref[:, lo:hi]
        y_ref[:, lo:hi] = (y * (z * jax.nn.sigmoid(z))).astype(BF16)

    d_mix = y_ref.shape[1]
    branch(oa_ref, za_ref, 0, d_lru)
    branch(ob_ref, zb_ref, d_lru, d_lru + d_band)
    branch(oc_ref, zc_ref, d_lru + d_band, d_mix)
    out = jnp.dot(y_ref[...], w_ref[...], preferred_element_type=F32)
    n_seq = mod_ref.shape[0]
    t = out.shape[0] // n_seq
    for s in range(n_seq):
        rows = slice(s * t, (s + 1) * t)
        x_new = x_ref[rows, :] + mod_ref[s, 2:3, :] * out[rows, :]
        ms = jnp.mean(x_new * x_new, axis=-1, keepdims=True)
        normed = x_new * lax.rsqrt(ms + EPS) * g2_ref[...]
        if last:
            yo_ref[rows, :] = normed
        else:
            xo_ref[rows, :] = x_new
            h_ref[rows, :] = (normed * (1.0 + mod2_ref[s, 1:2, :]) + mod2_ref[s, 0:1, :]).astype(BF16)


def outproj(oa, ob, oc, u, g_branch, w_out, layer, x, mod3, g2, mod3_next, cols, nseq, t, tm):
    rows, d = x.shape
    d_lru, d_band, d_dsa = oa.shape[1], ob.shape[1], oc.shape[1]
    d_mix = d_lru + d_band + d_dsa
    nblk = max(t // tm, 1)
    seqs_per_blk = max(tm // t, 1)
    last = mod3_next is None
    rb = lambda w, o=0: pl.BlockSpec((tm, w), lambda b, i, o=o: (b * nblk + i, o))
    vec = pl.BlockSpec((1, d), lambda b, i: (0, 0))
    mod_spec = pl.BlockSpec((seqs_per_blk, 3, d), lambda b, i: (b, 0, 0))
    in_specs = [rb(d_lru), rb(d_band), rb(d_dsa),
                rb(d_lru, cols["za"] // d_lru), rb(d_band, cols["zb"] // d_band), rb(d_dsa, cols["zc"] // d_dsa),
                pl.BlockSpec((1, d_mix), lambda b, i: (0, 0)),
                pl.BlockSpec((None, d_mix, d), lambda b, i: (layer, 0, 0)),
                rb(d), mod_spec, vec]
    args = [oa, ob, oc, u, u, u, g_branch, w_out, x, mod3, g2]
    if last:
        out_specs = rb(d)
        out_shape = jax.ShapeDtypeStruct((rows, d), F32)
    else:
        in_specs.append(mod_spec)
        args.append(mod3_next)
        out_specs = [rb(d), rb(d)]
        out_shape = [jax.ShapeDtypeStruct((rows, d), F32), jax.ShapeDtypeStruct((rows, d), BF16)]
    return pl.pallas_call(
        functools.partial(_outproj_kernel, d_lru=d_lru, d_band=d_band, last=last),
        grid=(nseq // seqs_per_blk, nblk),
        in_specs=in_specs,
        out_specs=out_specs,
        out_shape=out_shape,
        scratch_shapes=[pltpu.VMEM((tm, d_mix), BF16)],
        compiler_params=_params(2),
        name="outproj",
    )(*args)


def _column_offsets(d_lru, d_band, d_dsa):
    names = ["xa", "za", "qb", "kb", "vb", "zb", "qc", "kc", "vc", "zc", "qi"]
    widths = [d_lru, d_lru, d_band, d_band, d_band, d_band, d_dsa, d_dsa, d_dsa, d_dsa, N_IDX_HEADS * IDX_DIM]
    cols, off = {}, 0
    for n, w in zip(names, widths):
        cols[n] = off
        off += w
    cols["main"] = off
    cols["d_lru"], cols["d_band"], cols["d_dsa"] = d_lru, d_band, d_dsa
    return cols


def _layer_stream(x, h, mod3, mod3_next, g_next, lw, layer, depth, cols, nseq, t, tables, state, strip, stacks,
                  prompt):
    d_lru, d_band, d_dsa = cols["d_lru"], cols["d_band"], cols["d_dsa"]
    n_band, n_dsa = d_band // HEAD_DIM, d_dsa // HEAD_DIM
    rows = nseq * t
    tb = 256 if t % 256 == 0 else t
    tm = 512 if rows % 512 == 0 else rows
    u = matmul(h, lw["w_in"], layer, cols["main"], tm, 1024)
    tail = matmul(h, lw["w_tail"], layer, LANES, tm, LANES)

    q_bf, k_bf, qi_pad, t_rot, ki_bf, k_stack, v_stack, *maybe_vt = rope_call(
        u, tail, tables, cols, 256 if rows % 256 == 0 else tb, prompt, layer, depth, stacks)

    conv_s, lru_s = state[0], state[1]
    cs8 = jnp.concatenate([jnp.zeros((nseq, 8 - (CONV_W - 1), d_lru), F32), conv_s], axis=1)
    hs = lru_call(u, cs8, lru_s.reshape(nseq, 1, d_lru), lw["conv_w"], lw["conv_b"], lw["w_rg"], lw["b_rg"],
                  lw["w_ig"], lw["b_ig"], lw["lam"], nseq, t, tb)

    if prompt:
        ob = band_prompt(u, strip, cols, nseq, t)
        oc = dsa_prompt(qi_pad, t_rot, q_bf, ki_bf, k_bf, maybe_vt[0], nseq, t, 128, 512, min(TOPK_MAX, t // 4))
    else:
        bk, bv, dk, dv, dik = state[2:]
        past = dik.shape[3]
        assert (past + t - 1) // CHUNK <= past // CHUNK and t == CHUNK and bk.shape[2] == BAND_PAST * n_band
        ob = band_sample(u, bk, bv, layer, strip, cols, nseq, t)
        oc = dsa_sample(qi_pad, t_rot, q_bf, ki_bf, k_bf, u, cols["vc"] // d_dsa, dik, dk, dv, layer,
                        nseq, t, 512, min(TOPK_MAX, (past + t) // 4))

    res = outproj(hs, ob, oc, u, lw["g_branch"], lw["w_out"], layer, x, mod3, g_next, mod3_next, cols, nseq, t,
                  256 if rows % 256 == 0 else tb)
    x_new, h_next = (res, None) if mod3_next is None else res

    u3 = u.reshape(nseq, t, -1)
    xa = u3[:, :, cols["xa"]:cols["xa"] + d_lru]
    if prompt:
        new_conv = xa[:, t - (CONV_W - 1):]
    else:
        new_conv = jnp.concatenate([conv_s, xa], axis=1)[:, -(CONV_W - 1):]
    nbr = min(BAND_PAST, t)
    new = (new_conv,
           hs.reshape(nseq, t, d_lru)[:, -1],
           u3[:, t - nbr:, cols["kb"]:cols["kb"] + d_band].reshape(nseq, nbr, n_band, HEAD_DIM),
           u3[:, t - nbr:, cols["vb"]:cols["vb"] + d_band].reshape(nseq, nbr, n_band, HEAD_DIM),
           None,
           None,
           t_rot.reshape(nseq, t, LANES)[:, :, :IDX_DIM])
    return x_new, h_next, new, (k_stack, v_stack)


def kernel(x_prompt, x_sample, c_prompt, c_sample, state_conv, state_lru, cache_band_k, cache_band_v,
           cache_dsa_k, cache_dsa_v, cache_dsa_idx_k, g_norm, w_ada, b_ada, w_in, conv_w, conv_b,
           w_rg, b_rg, w_ig, b_ig, lru_lambda, rel_bias, g_branch, w_out, g_final):
    depth = w_in.shape[0]
    nb_p, t_p, d = x_prompt.shape
    nb_s, t_s, _ = x_sample.shape
    past = cache_dsa_k.shape[2]
    d_lru = conv_w.shape[2]
    d_band = cache_band_k.shape[3] * HEAD_DIM
    d_dsa = cache_dsa_k.shape[3] * HEAD_DIM
    cols = _column_offsets(d_lru, d_band, d_dsa)
    n_main = cols["main"]

    mod = ada_all(jnp.concatenate([c_prompt, c_sample], axis=0), w_ada, b_ada)
    mod = mod.reshape(depth, nb_p + nb_s, 3, d)

    w_tail = jnp.pad(w_in[:, :, n_main:], ((0, 0), (0, 0), (0, LANES - (w_in.shape[2] - n_main)))).astype(BF16)
    w_out_bf = w_out.astype(BF16)
    w_rg_bf = w_rg.astype(BF16)
    w_ig_bf = w_ig.astype(BF16)

    band_k = cache_band_k.reshape(depth, nb_s, -1, HEAD_DIM)
    band_v = cache_band_v.reshape(depth, nb_s, -1, HEAD_DIM)
    dsa_k = cache_dsa_k.reshape(depth, nb_s, -1, HEAD_DIM)
    dsa_v = cache_dsa_v.reshape(depth, nb_s, -1, HEAD_DIM)
    idx_k_t = jnp.swapaxes(cache_dsa_idx_k, 2, 3)
    w_in_t = jnp.swapaxes(w_in, 1, 2)

    tab_p = rope_tables(jnp.tile(jnp.arange(t_p), nb_p))
    tab_s = rope_tables(jnp.tile(past + jnp.arange(t_s), nb_s))

    xp = x_prompt.reshape(nb_p * t_p, d)
    xs = x_sample.reshape(nb_s * t_s, d)
    zero_state = (jnp.zeros((nb_p, CONV_W - 1, d_lru), F32), jnp.zeros((nb_p, d_lru), F32))
    p_new, s_new = [], []
    p_stacks = s_stacks = None
    hp = normmod(xp, g_norm[0][None], mod[0, :nb_p], nb_p, t_p, 256 if t_p % 256 == 0 else t_p)
    hsm = normmod(xs, g_norm[0][None], mod[0, nb_p:], nb_s, t_s, 256 if t_s % 256 == 0 else t_s)
    for l in range(depth):
        lw = dict(w_in=w_in_t, w_tail=w_tail, conv_w=conv_w[l], conv_b=conv_b[l][None],
                  w_rg=w_rg_bf[l], b_rg=b_rg[l][None], w_ig=w_ig_bf[l], b_ig=b_ig[l][None],
                  lam=lru_lambda[l][None], g_branch=g_branch[l][None], w_out=w_out_bf)
        last = l == depth - 1
        g_next = g_final[None] if last else g_norm[l + 1][None]
        modp_next = None if last else mod[l + 1, :nb_p]
        mods_next = None if last else mod[l + 1, nb_p:]
        strip = bias_strip(rel_bias[l])
        xp, hp, pn, p_stacks = _layer_stream(xp, hp, mod[l, :nb_p], modp_next, g_next, lw, l, depth, cols, nb_p, t_p,
                                             tab_p, zero_state, strip, p_stacks, True)
        st = (state_conv[l], state_lru[l], band_k, band_v, dsa_k, dsa_v, idx_k_t)
        xs, hsm, sn, s_stacks = _layer_stream(xs, hsm, mod[l, nb_p:], mods_next, g_next, lw, l, depth, cols, nb_s,
                                              t_s, tab_s, st, strip, s_stacks, False)
        p_new.append(pn)
        s_new.append(sn)

    y_prompt = xp.reshape(nb_p, t_p, d)
    y_sample = xs.reshape(nb_s, t_s, d)
    n_dsa = d_dsa // HEAD_DIM

    def outputs(new, stacks, nseq, t):
        out = [None if new[0][j] is None else jnp.stack([layer_new[j] for layer_new in new], axis=0)
               for j in range(7)]
        out[4], out[5] = (s.reshape(depth, nseq, t, n_dsa, HEAD_DIM) for s in stacks)
        return out

    return (y_prompt, y_sample, *outputs(p_new, p_stacks, nb_p, t_p), *outputs(s_new, s_stacks, nb_s, t_s))
```

```python
import functools
import struct

import jax
import jax.numpy as jnp
from jax import lax
from jax.experimental import pallas as pl
from jax.experimental.pallas import tpu as pltpu

F32 = jnp.float32
BF16 = jnp.bfloat16
I32 = jnp.int32

CHUNK = 64
HEAD_DIM = 128
LRU_BLOCKS = 8
CONV_W = 4
LRU_C = 8.0
BAND_PAST_CHUNKS = 8
BAND_PAST = BAND_PAST_CHUNKS * CHUNK
BAND_KEYS = BAND_PAST + CHUNK
REL_CLIP = 256
N_IDX_HEADS = 16
IDX_DIM = 64
TOPK_MAX = 256
ROPE_THETA = 500000.0
ROPE_FRAC = 4
EPS = 1e-6
NEG = -1e30
LANES = 128
SUBLANES = 8

INT_MIN = -(2 ** 31)


def _sortable_key_of(x):
    b = struct.unpack("<i", struct.pack("<f", x))[0]
    return b ^ 0x7FFFFFFF if b < 0 else b


KEY_HALF_NEG = _sortable_key_of(NEG * 0.5)

VMEM_LIMIT = 48 * 1024 * 1024
DSA_SAMPLE_VMEM_LIMIT = 56 * 1024 * 1024
NT_DIMS = (((1,), (1,)), ((), ()))


def _params(n_grid):
    return pltpu.CompilerParams(dimension_semantics=("arbitrary",) * n_grid,
                                vmem_limit_bytes=VMEM_LIMIT)


def _dot_nt(a, b):
    return lax.dot_general(a, b, NT_DIMS, preferred_element_type=F32)


def _ada_kernel(c_ref, w_ref, b_ref, o_ref):
    c = c_ref[...]
    s = (c * jax.nn.sigmoid(c)).astype(BF16)
    o_ref[...] = jnp.dot(s, w_ref[...].astype(BF16), preferred_element_type=F32) + b_ref[...]


def ada_all(c_all, w_ada, b_ada):
    depth, d, n = w_ada.shape
    nb = c_all.shape[0]
    tn = 512
    return pl.pallas_call(
        _ada_kernel,
        grid=(depth, n // tn),
        in_specs=[pl.BlockSpec((nb, d), lambda l, j: (0, 0)),
                  pl.BlockSpec((None, d, tn), lambda l, j: (l, 0, j)),
                  pl.BlockSpec((None, 1, tn), lambda l, j: (l, 0, j))],
        out_specs=pl.BlockSpec((None, nb, tn), lambda l, j: (l, 0, j)),
        out_shape=jax.ShapeDtypeStruct((depth, nb, n), F32),
        compiler_params=_params(2),
        name="ada",
    )(c_all, w_ada, b_ada.reshape(depth, 1, n))


def _normmod_kernel(x_ref, g_ref, mod_ref, h_ref):
    x = x_ref[...]
    ms = jnp.mean(x * x, axis=-1, keepdims=True)
    y = x * lax.rsqrt(ms + EPS) * g_ref[...]
    shift = mod_ref[0:1, :]
    scale = mod_ref[1:2, :]
    h_ref[...] = (y * (1.0 + scale) + shift).astype(BF16)


def normmod(x, g, mod3, nseq, t, tb):
    rows, d = x.shape
    nblk = t // tb
    return pl.pallas_call(
        _normmod_kernel,
        grid=(nseq, nblk),
        in_specs=[pl.BlockSpec((tb, d), lambda b, i: (b * nblk + i, 0)),
                  pl.BlockSpec((1, d), lambda b, i: (0, 0)),
                  pl.BlockSpec((None, 3, d), lambda b, i: (b, 0, 0))],
        out_specs=pl.BlockSpec((tb, d), lambda b, i: (b * nblk + i, 0)),
        out_shape=jax.ShapeDtypeStruct((rows, d), BF16),
        compiler_params=_params(2),
        name="normmod",
    )(x, g, mod3)


def _mm_kernel(a_ref, b_ref, o_ref, *scratch):
    if scratch:
        wb_ref, = scratch

        @pl.when(pl.program_id(1) == 0)
        def _():
            wb_ref[...] = b_ref[...].T.astype(BF16)

        w = wb_ref[...]
    else:
        w = b_ref[...]
    o_ref[...] = jnp.dot(a_ref[...], w, preferred_element_type=F32)


def matmul(a, w, layer, ncols, tm, tn):
    m, k = a.shape
    if w.dtype == BF16:
        scratch = []
        w_spec = pl.BlockSpec((None, k, tn), lambda j, i: (layer, 0, j))
    else:
        scratch = [pltpu.VMEM((k, tn), BF16)]
        w_spec = pl.BlockSpec((None, tn, k), lambda j, i: (layer, j, 0))
    return pl.pallas_call(
        _mm_kernel,
        grid=(ncols // tn, m // tm),
        in_specs=[pl.BlockSpec((tm, k), lambda j, i: (i, 0)), w_spec],
        out_specs=pl.BlockSpec((tm, tn), lambda j, i: (i, j)),
        out_shape=jax.ShapeDtypeStruct((m, ncols), F32),
        scratch_shapes=scratch,
        compiler_params=_params(2),
        name="inproj",
    )(a, w)


def _rope_kernel(qc_ref, kc_ref, vc_ref, qi_ref, t_ref, c128_ref, s128_ref, c64_ref, s64_ref, *rest,
                 n_dsa, n_idx, n_aliased):
    q_out, kbf_out, qi_out, trot_out, kibf_out, kst_out, vst_out, *maybe_vt_out = rest[n_aliased:]
    tm = qc_ref.shape[0]
    lane = lax.broadcasted_iota(I32, (tm, LANES), 1)
    c128 = c128_ref[...]
    s128 = s128_ref[...]
    c64 = c64_ref[...]
    s64 = s64_ref[...]
    half128 = HEAD_DIM // ROPE_FRAC // 2
    half64 = IDX_DIM // ROPE_FRAC // 2

    def rope128(x):
        partner = jnp.where(lane < half128, pltpu.roll(x, LANES - half128, 1), pltpu.roll(x, half128, 1))
        return x * c128 + partner * s128

    def rope64(x):
        partner = jnp.where((lane & (IDX_DIM - 1)) < half64,
                            pltpu.roll(x, LANES - half64, 1), pltpu.roll(x, half64, 1))
        return x * c64 + partner * s64

    for h in range(n_dsa):
        sl = slice(h * LANES, (h + 1) * LANES)
        q_out[:, sl] = rope128(qc_ref[:, sl]).astype(BF16)
        kr = rope128(kc_ref[:, sl])
        kbf_out[:, sl] = kr.astype(BF16)
        kst_out[pl.ds(h, tm, stride=n_dsa), :] = kr
        vst_out[pl.ds(h, tm, stride=n_dsa), :] = vc_ref[:, sl]
    if maybe_vt_out:
        maybe_vt_out[0][...] = vc_ref[...].T.astype(BF16)
    low = lane < IDX_DIM
    for j in range(n_idx // 2):
        r = rope64(qi_ref[:, j * LANES:(j + 1) * LANES]) * (IDX_DIM ** -0.5)
        qi_out[:, (2 * j) * LANES:(2 * j + 1) * LANES] = jnp.where(low, r, 0.0).astype(BF16)
        qi_out[:, (2 * j + 1) * LANES:(2 * j + 2) * LANES] = jnp.where(low, 0.0, r).astype(BF16)
    t = t_ref[...]
    r = rope64(t)
    trot_out[...] = jnp.where(low, r, jnp.where(lane < IDX_DIM + N_IDX_HEADS, t * (N_IDX_HEADS ** -0.5), 0.0))
    kibf_out[...] = jnp.where(low, r, pltpu.roll(r, IDX_DIM, 1)).astype(BF16)


def rope_call(u, tail, tables, cols, tm, v_t, layer, depth, stacks):
    rows = u.shape[0]
    d_dsa = cols["d_dsa"]
    n_dsa = d_dsa // HEAD_DIM
    d_qi = N_IDX_HEADS * IDX_DIM
    c128, s128, c64, s64 = tables
    tab_spec = pl.BlockSpec((tm, LANES), lambda i: (i, 0))
    blk = lambda off: pl.BlockSpec((tm, d_dsa), lambda i, o=off // d_dsa: (i, o))
    row_spec = pl.BlockSpec((tm, d_dsa), lambda i: (i, 0))
    stack_spec = pl.BlockSpec((None, tm * n_dsa, HEAD_DIM), lambda i: (layer, i, 0))
    stack_shape = jax.ShapeDtypeStruct((depth, rows * n_dsa, HEAD_DIM), F32)
    out_specs = [row_spec, row_spec, pl.BlockSpec((tm, N_IDX_HEADS * LANES), lambda i: (i, 0)),
                 tab_spec, tab_spec, stack_spec, stack_spec]
    out_shape = [jax.ShapeDtypeStruct((rows, d_dsa), BF16),
                 jax.ShapeDtypeStruct((rows, d_dsa), BF16),
                 jax.ShapeDtypeStruct((rows, N_IDX_HEADS * LANES), BF16),
                 jax.ShapeDtypeStruct((rows, LANES), F32),
                 jax.ShapeDtypeStruct((rows, LANES), BF16),
                 stack_shape, stack_shape]
    if v_t:
        out_specs.append(pl.BlockSpec((d_dsa, tm), lambda i: (0, i)))
        out_shape.append(jax.ShapeDtypeStruct((d_dsa, rows), BF16))
    in_specs = [blk(cols["qc"]), blk(cols["kc"]), blk(cols["vc"]),
                pl.BlockSpec((tm, d_qi), lambda i, o=cols["qi"] // d_qi: (i, o)),
                tab_spec, tab_spec, tab_spec, tab_spec, tab_spec]
    args = [u, u, u, u, tail, c128, s128, c64, s64]
    aliases = {}
    if stacks is not None:
        aliases = {len(args): 5, len(args) + 1: 6}
        in_specs += [pl.BlockSpec(memory_space=pl.ANY)] * 2
        args += list(stacks)
    return pl.pallas_call(
        functools.partial(_rope_kernel, n_dsa=n_dsa, n_idx=N_IDX_HEADS, n_aliased=len(aliases)),
        grid=(rows // tm,),
        in_specs=in_specs,
        out_specs=out_specs,
        out_shape=out_shape,
        input_output_aliases=aliases,
        compiler_params=_params(1),
        name="rope",
    )(*args)


def rope_tables(pos):
    pos = pos.astype(F32)[:, None]
    n = pos.shape[0]

    def tab(dim):
        half = dim // ROPE_FRAC // 2
        inv = ROPE_THETA ** (-jnp.arange(half, dtype=F32) / half)
        ang = pos * inv[None]
        cos, sin = jnp.cos(ang), jnp.sin(ang)
        c = jnp.concatenate([cos, cos, jnp.ones((n, dim - 2 * half), F32)], axis=1)
        s = jnp.concatenate([-sin, sin, jnp.zeros((n, dim - 2 * half), F32)], axis=1)
        return jnp.tile(c, (1, LANES // dim)), jnp.tile(s, (1, LANES // dim))

    c128, s128 = tab(HEAD_DIM)
    c64, s64 = tab(IDX_DIM)
    return c128, s128, c64, s64


def _lru_kernel(xa_ref, cs_ref, h0_ref, cw_ref, cb_ref, wrg_ref, brg_ref, wig_ref, big_ref, lam_ref,
                hs_ref, ext_ref, hc_ref):
    tb, d = xa_ref.shape
    blk = d // LRU_BLOCKS

    @pl.when(pl.program_id(1) == 0)
    def _():
        ext_ref[0:8, :] = cs_ref[...]
        hc_ref[...] = h0_ref[...]

    ext_ref[8:8 + tb, :] = xa_ref[...]
    conv = ext_ref[5:5 + tb, :] * cw_ref[0:1, :]
    for j in range(1, CONV_W):
        conv = conv + ext_ref[5 + j:5 + j + tb, :] * cw_ref[j:j + 1, :]
    conv = conv + cb_ref[...]
    tail = ext_ref[tb:tb + 8, :]
    ext_ref[0:8, :] = tail

    xb = conv.astype(BF16)
    rs, gs = [], []
    for g in range(LRU_BLOCKS):
        xg = xb[:, g * blk:(g + 1) * blk]
        rs.append(jnp.dot(xg, wrg_ref[g], preferred_element_type=F32))
        gs.append(jnp.dot(xg, wig_ref[g], preferred_element_type=F32))
    r = jax.nn.sigmoid(jnp.concatenate(rs, axis=1) + brg_ref[...])
    ig = jax.nn.sigmoid(jnp.concatenate(gs, axis=1) + big_ref[...])
    lam = lam_ref[...]
    softplus_neg_lam = jnp.maximum(-lam, 0.0) + jnp.log1p(jnp.exp(-jnp.abs(lam)))
    log_a = (-LRU_C) * r * softplus_neg_lam
    a = jnp.exp(log_a)
    u = jnp.sqrt(-jnp.tanh(log_a) * (a * a + 1.0)) * (ig * conv)

    row = lax.broadcasted_iota(I32, (tb, d), 0)
    s = 1
    while s < tb:
        keep = row >= s
        a_prev = jnp.where(keep, pltpu.roll(a, s, 0), 1.0)
        u_prev = jnp.where(keep, pltpu.roll(u, s, 0), 0.0)
        u = a * u_prev + u
        a = a * a_prev
        s *= 2
    h = a * hc_ref[...] + u
    hs_ref[...] = h
    hc_ref[...] = h[tb - 1:tb, :]


def lru_call(u, cs8, h0, cw, cb, wrg, brg, wig, big, lam, nseq, t, tb):
    rows = u.shape[0]
    d = cw.shape[1]
    blk = d // LRU_BLOCKS
    nblk = t // tb
    vec = pl.BlockSpec((1, d), lambda b, i: (0, 0))
    wspec = pl.BlockSpec((LRU_BLOCKS, blk, blk), lambda b, i: (0, 0, 0))
    return pl.pallas_call(
        _lru_kernel,
        grid=(nseq, nblk),
        in_specs=[pl.BlockSpec((tb, d), lambda b, i: (b * nblk + i, 0)),
                  pl.BlockSpec((None, 8, d), lambda b, i: (b, 0, 0)),
                  pl.BlockSpec((None, 1, d), lambda b, i: (b, 0, 0)),
                  pl.BlockSpec((CONV_W, d), lambda b, i: (0, 0)),
                  vec, wspec, vec, wspec, vec, vec],
        out_specs=pl.BlockSpec((tb, d), lambda b, i: (b * nblk + i, 0)),
        out_shape=jax.ShapeDtypeStruct((rows, d), F32),
        scratch_shapes=[pltpu.VMEM((tb + 8, d), F32), pltpu.VMEM((1, d), F32)],
        compiler_params=_params(2),
        name="lru",
    )(u, cs8, h0, cw, cb, wrg, brg, wig, big, lam)


def _bias_kernel(relb_ref, o_ref):
    h = pl.program_id(0)
    qi = lax.broadcasted_iota(I32, (CHUNK, BAND_KEYS), 0)
    kj = lax.broadcasted_iota(I32, (CHUNK, BAND_KEYS), 1)
    idx = jnp.clip(qi + BAND_PAST - kj, -REL_CLIP, REL_CLIP) + REL_CLIP

    def body(r, acc):
        return jnp.where(idx == r, relb_ref[h, r], acc)

    lowest = max(-(CHUNK - 1), -REL_CLIP) + REL_CLIP
    highest = min(BAND_PAST + CHUNK - 1, REL_CLIP) + REL_CLIP
    o_ref[...] = lax.fori_loop(lowest, highest + 1, body, jnp.zeros((CHUNK, BAND_KEYS), F32))


def bias_strip(relb):
    nh = relb.shape[0]
    return pl.pallas_call(
        _bias_kernel,
        grid=(nh,),
        in_specs=[pl.BlockSpec(memory_space=pltpu.SMEM)],
        out_specs=pl.BlockSpec((None, CHUNK, BAND_KEYS), lambda h: (h, 0, 0)),
        out_shape=jax.ShapeDtypeStruct((nh, CHUNK, BAND_KEYS), F32),
        compiler_params=_params(1),
        name="bias_strip",
    )(relb)


def _band_prompt_kernel(q_ref, kp_ref, kc_ref, vp_ref, vc_ref, strip_ref, o_ref, kw_ref, vw_ref):
    i = pl.program_id(2)
    tb = q_ref.shape[0]
    kw_ref[0:tb, :] = kp_ref[...].astype(BF16)
    kw_ref[tb:2 * tb, :] = kc_ref[...].astype(BF16)
    vw_ref[0:tb, :] = vp_ref[...].astype(BF16)
    vw_ref[tb:2 * tb, :] = vc_ref[...].astype(BF16)
    strip = strip_ref[...]
    col = lax.broadcasted_iota(I32, (CHUNK, BAND_KEYS), 1)
    chunks = range(tb // CHUNK)
    scores = []
    for a in chunks:
        q = q_ref[a * CHUNK:(a + 1) * CHUNK, :].astype(BF16)
        s = _dot_nt(q, kw_ref[a * CHUNK:a * CHUNK + BAND_KEYS, :]) * (HEAD_DIM ** -0.5) + strip
        min_col = jnp.where(i > 0, 0, (BAND_PAST_CHUNKS - a) * CHUNK)
        scores.append(jnp.where(col >= min_col, s, NEG))
    maxes = [jnp.max(s, axis=-1, keepdims=True) for s in scores]
    probs = [jnp.exp(s - m) for s, m in zip(scores, maxes)]
    sums = [jnp.sum(p, axis=-1, keepdims=True) for p in probs]
    for a in chunks:
        pv = jnp.dot(probs[a].astype(BF16), vw_ref[a * CHUNK:a * CHUNK + BAND_KEYS, :],
                     preferred_element_type=F32)
        o_ref[a * CHUNK:(a + 1) * CHUNK, :] = pv / sums[a]


def band_prompt(u, strip, cols, nseq, t):
    rows = u.shape[0]
    tb = BAND_PAST
    nblk = t // tb
    nh = strip.shape[0]
    qo, ko, vo = cols["qb"] // HEAD_DIM, cols["kb"] // HEAD_DIM, cols["vb"] // HEAD_DIM
    cur = lambda off: pl.BlockSpec((tb, HEAD_DIM), lambda b, h, i: (b * nblk + i, off + h))
    prev = lambda off: pl.BlockSpec((tb, HEAD_DIM), lambda b, h, i: (b * nblk + jnp.maximum(i - 1, 0), off + h))
    return pl.pallas_call(
        _band_prompt_kernel,
        grid=(nseq, nh, nblk),
        in_specs=[cur(qo), prev(ko), cur(ko), prev(vo), cur(vo),
                  pl.BlockSpec((None, CHUNK, BAND_KEYS), lambda b, h, i: (h, 0, 0))],
        out_specs=pl.BlockSpec((tb, HEAD_DIM), lambda b, h, i: (b * nblk + i, h)),
        out_shape=jax.ShapeDtypeStruct((rows, nh * HEAD_DIM), F32),
        scratch_shapes=[pltpu.VMEM((2 * tb, HEAD_DIM), BF16), pltpu.VMEM((2 * tb, HEAD_DIM), BF16)],
        compiler_params=_params(3),
        name="band_prompt",
    )(u, u, u, u, u, strip)


def _band_sample_kernel(q_ref, kn_ref, vn_ref, ck_ref, cv_ref, strip_ref, o_ref, kw_ref, vw_ref):
    nh = strip_ref.shape[0]
    w = ck_ref.shape[0] // nh
    t = q_ref.shape[0]
    kw_ref[w:w + t, :] = kn_ref[...].astype(BF16)
    vw_ref[w:w + t, :] = vn_ref[...].astype(BF16)
    heads = [slice(h * HEAD_DIM, (h + 1) * HEAD_DIM) for h in range(nh)]
    for h, sl in enumerate(heads):
        kw_ref[0:w, sl] = ck_ref[pl.ds(h, w, stride=nh), :].astype(BF16)
        vw_ref[0:w, sl] = cv_ref[pl.ds(h, w, stride=nh), :].astype(BF16)
    scores = [_dot_nt(q_ref[:, sl].astype(BF16), kw_ref[:, sl]) * (HEAD_DIM ** -0.5) + strip_ref[h]
              for h, sl in enumerate(heads)]
    maxes = [jnp.max(s, axis=-1, keepdims=True) for s in scores]
    probs = [jnp.exp(s - m) for s, m in zip(scores, maxes)]
    sums = [jnp.sum(p, axis=-1, keepdims=True) for p in probs]
    for h, sl in enumerate(heads):
        pv = jnp.dot(probs[h].astype(BF16), vw_ref[:, sl], preferred_element_type=F32)
        o_ref[:, sl] = pv / sums[h]


def band_sample(u, ck, cv, layer, strip, cols, nseq, t):
    rows = u.shape[0]
    nh = strip.shape[0]
    d = nh * HEAD_DIM
    w = ck.shape[2] // nh
    ublk = lambda off: pl.BlockSpec((t, d), lambda b, o=off // d: (b, o))
    cblk = pl.BlockSpec((None, None, w * nh, HEAD_DIM), lambda b: (layer, b, 0, 0))
    return pl.pallas_call(
        _band_sample_kernel,
        grid=(nseq,),
        in_specs=[ublk(cols["qb"]), ublk(cols["kb"]), ublk(cols["vb"]), cblk, cblk,
                  pl.BlockSpec((nh, CHUNK, BAND_KEYS), lambda b: (0, 0, 0))],
        out_specs=pl.BlockSpec((t, d), lambda b: (b, 0)),
        out_shape=jax.ShapeDtypeStruct((rows, d), F32),
        scratch_shapes=[pltpu.VMEM((w + t, d), BF16), pltpu.VMEM((w + t, d), BF16)],
        compiler_params=_params(1),
        name="band_sample",
    )(u, u, u, ck, cv, strip)


def _loop(n, body, init):
    if isinstance(n, int):
        val = init
        for k in range(n):
            val = body(k, val)
        return val
    return lax.fori_loop(0, n, body, init)


IDX_BITS = 16


def _f32_to_key(x):
    bits = lax.bitcast_convert_type(x, I32)
    return jnp.where(bits < 0, bits ^ 0x7FFFFFFF, bits)


def _dsa_core(qside, segs, store_out, sk_refs, s_refs, jstar_ref, mpart_ref, lpart_ref, acc_ref, *,
              topk, q_chunk0, n_q, n_dsa):
    qb = qside["t"].shape[0]
    t_t = qside["t"].T
    wrows = [t_t[IDX_DIM + h:IDX_DIM + h + 1, :] for h in range(N_IDX_HEADS)]

    def tiles(x):
        return x.reshape(x.shape[0] // SUBLANES, SUBLANES, qb)

    w8 = [jnp.broadcast_to(w, (SUBLANES, qb)) for w in wrows]
    qcol8 = lax.broadcasted_iota(I32, (SUBLANES, qb), 1)
    q_end8 = (q_chunk0 + qcol8 // CHUNK + 1) * CHUNK
    for seg, sk_ref in zip(segs, sk_refs):
        kblk = seg["kblk"]
        n_t = kblk // SUBLANES
        krow3 = (lax.broadcasted_iota(I32, (n_t, SUBLANES, qb), 0) * SUBLANES
                 + lax.broadcasted_iota(I32, (n_t, SUBLANES, qb), 1))
        padded = seg["valid"] < kblk or n_q < qb
        live3 = (krow3 < seg["valid"]) & (lax.broadcasted_iota(I32, (n_t, SUBLANES, qb), 2) < n_q)

        def score_blk(kb, carry, seg=seg, sk_ref=sk_ref, kblk=kblk, krow3=krow3, padded=padded, live3=live3):
            kib = seg["ki"](kb)
            acc = jnp.zeros(krow3.shape, F32)
            for group in qside["qi_groups"]:
                sc = _dot_nt(kib, qside["qi"](group))
                for g, h in enumerate(group):
                    acc = acc + jnp.maximum(tiles(sc[:, g * qb:(g + 1) * qb]), 0.0) * w8[h]
            admissible = krow3 < q_end8 - (seg["pos0"] + kb * kblk)
            key = _f32_to_key(jnp.where(admissible, acc, NEG))
            sk_ref[kb] = jnp.where(live3, key, INT_MIN) if padded else key
            return carry

        _loop(seg["nblk"], score_blk, 0)

    def count(pred):
        part = jnp.zeros((SUBLANES, qb), I32)
        for seg, sk_ref in zip(segs, sk_refs):
            n_t = seg["kblk"] // SUBLANES
            off = (lax.broadcasted_iota(I32, (n_t, SUBLANES, qb), 0) * SUBLANES
                   + lax.broadcasted_iota(I32, (n_t, SUBLANES, qb), 1))

            def body(kb, part, seg=seg, sk_ref=sk_ref, off=off):
                idx = seg["pos0"] + kb * seg["kblk"] + off
                return part + jnp.sum(jnp.where(pred(sk_ref[kb], idx), 1, 0), axis=0)

            part = _loop(seg["nblk"], body, part)
        return jnp.broadcast_to(jnp.sum(part, axis=0, keepdims=True), (SUBLANES, qb))

    zero = jnp.zeros((SUBLANES, qb), I32)
    n_stored = sum(seg["nblk"] * seg["kblk"] for seg in segs)
    c_zero = count(lambda key, idx: key >= zero)
    thr0 = jnp.where(c_zero >= topk, 0, INT_MIN).astype(I32)
    c_lo0 = jnp.where(c_zero >= topk, c_zero, zero + n_stored)

    def bit_step(bit, state):
        t, c = state
        cand = t + jnp.left_shift(jnp.int32(1), bit)
        c_cand = count(lambda key, idx: key >= cand)
        ok = c_cand >= topk
        return jnp.where(ok, cand, t), jnp.where(ok, c_cand, c)

    thr, c_lo = lax.fori_loop(0, 31, lambda i, st: bit_step(30 - i, st), (thr0, c_lo0))
    lo = jnp.maximum(thr, KEY_HALF_NEG + 1)

    big = 2 ** IDX_BITS
    jstar_ref[...] = jnp.full((SUBLANES, qb), big, I32)
    surplus = (c_lo > topk) & (thr > KEY_HALF_NEG)

    @pl.when(jnp.max(jnp.where(surplus, 1, 0)) > 0)
    def _():
        need = topk - count(lambda key, idx: key > thr)

        def idx_body(it, j):
            cand = j + jnp.left_shift(jnp.int32(1), IDX_BITS - 1 - it)
            f = count(lambda key, idx: (key == lo) & (idx < cand))
            return jnp.where(f <= need, cand, j)

        j = lax.fori_loop(0, IDX_BITS, idx_body, zero)
        jstar_ref[...] = jnp.where(surplus, j, big)

    jstar = jstar_ref[...]

    mpart_ref[...] = jnp.full(mpart_ref.shape, NEG, F32)
    lpart_ref[...] = jnp.zeros(lpart_ref.shape, F32)
    acc_ref[...] = jnp.zeros(acc_ref.shape, F32)

    for seg, sk_ref, s_ref in zip(segs, sk_refs, s_refs):
        kblk = seg["kblk"]
        n_t = kblk // SUBLANES
        krow3 = (lax.broadcasted_iota(I32, (n_t, SUBLANES, qb), 0) * SUBLANES
                 + lax.broadcasted_iota(I32, (n_t, SUBLANES, qb), 1))

        def pass_a(kb, carry, seg=seg, sk_ref=sk_ref, s_ref=s_ref, kblk=kblk, krow3=krow3):
            key = sk_ref[kb]
            mask = (key >= lo) & ((key > lo) | (krow3 < jstar - (seg["pos0"] + kb * kblk)))
            raw = [tiles(_dot_nt(seg["k"](kb, h), qside["q"](h))) for h in range(n_dsa)]
            for h in range(n_dsa):
                s = jnp.where(mask, raw[h] * (HEAD_DIM ** -0.5), NEG)
                s_ref[h, kb] = s
                mpart_ref[h] = jnp.maximum(mpart_ref[h], jnp.max(s, axis=0))
            return carry

        _loop(seg["nblk"], pass_a, 0)

    m8 = [jnp.broadcast_to(jnp.max(mpart_ref[h], axis=0, keepdims=True), (SUBLANES, qb)) for h in range(n_dsa)]

    for seg, s_ref in zip(segs, s_refs):
        def pass_b(kb, carry, seg=seg, s_ref=s_ref):
            for h in range(n_dsa):
                p = jnp.exp(s_ref[h, kb] - m8[h])
                lpart_ref[h] += jnp.sum(p, axis=0)
                acc_ref[h] += seg["pv"](kb, h, p.reshape(seg["kblk"], qb).astype(BF16))
            return carry

        _loop(seg["nblk"], pass_b, 0)

    for h in range(n_dsa):
        l = jnp.sum(lpart_ref[h], axis=0, keepdims=True)
        store_out(h, (acc_ref[h] / l).T)


def _blk_start(kb, kblk):
    return kb * kblk if isinstance(kb, int) else pl.multiple_of(kb * kblk, kblk)


def _dsa_scratch(qb, seg_shapes, n_dsa):
    return ([pltpu.VMEM((nblk, kblk // SUBLANES, SUBLANES, qb), I32) for nblk, kblk in seg_shapes]
            + [pltpu.VMEM((n_dsa, nblk, kblk // SUBLANES, SUBLANES, qb), F32) for nblk, kblk in seg_shapes]            + [pltpu.VMEM((SUBLANES, qb), I32),
               pltpu.VMEM((n_dsa, SUBLANES, qb), F32), pltpu.VMEM((n_dsa, SUBLANES, qb), F32),
               pltpu.VMEM((n_dsa, HEAD_DIM, qb), F32)])


def _dsa_prompt_kernel(qi_ref, t_ref, q_ref, ki_ref, k_ref, vt_ref, o_ref, sk_ref, s_ref, jstar_ref,
                       mpart_ref, lpart_ref, acc_ref, *, kblk, topk):
    i = pl.program_id(1)
    qb = q_ref.shape[0]
    nblk = (i * qb + qb + kblk - 1) // kblk

    def head(h):
        return slice(h * HEAD_DIM, (h + 1) * HEAD_DIM)

    seg = dict(ki=lambda kb: ki_ref[pl.ds(_blk_start(kb, kblk), kblk), :],
               k=lambda kb, h: k_ref[pl.ds(_blk_start(kb, kblk), kblk), head(h)],
               pv=lambda kb, h, p: jnp.dot(vt_ref[head(h), pl.ds(_blk_start(kb, kblk), kblk)], p,
                                           preferred_element_type=F32),
               nblk=nblk, kblk=kblk, pos0=0, valid=kblk)
    qside = dict(t=t_ref[...],
                 qi_groups=[(h, h + 1) for h in range(0, N_IDX_HEADS, 2)],
                 qi=lambda group: jnp.concatenate([qi_ref[:, h * LANES:(h + 1) * LANES] for h in group], axis=0),
                 q=lambda h: q_ref[:, head(h)])

    def store_out(h, x):
        o_ref[:, head(h)] = x

    _dsa_core(qside, [seg], store_out, [sk_ref], [s_ref], jstar_ref, mpart_ref, lpart_ref, acc_ref,
              topk=topk, q_chunk0=(i * qb) // CHUNK, n_q=qb, n_dsa=q_ref.shape[1] // HEAD_DIM)


def dsa_prompt(qi_pad, trot, q_bf, ki_bf, k_bf, v_t, nseq, t, qb, kblk, topk):
    rows, d = q_bf.shape
    n_dsa = d // HEAD_DIM
    nq = t // qb
    row_blk = lambda w: pl.BlockSpec((qb, w), lambda b, i: (b * nq + i, 0))
    seq_blk = lambda w: pl.BlockSpec((t, w), lambda b, i: (b, 0))
    return pl.pallas_call(
        functools.partial(_dsa_prompt_kernel, kblk=kblk, topk=topk),
        grid=(nseq, nq),
        in_specs=[row_blk(qi_pad.shape[1]), row_blk(LANES), row_blk(d),
                  seq_blk(LANES), seq_blk(d), pl.BlockSpec((d, t), lambda b, i: (0, b))],
        out_specs=row_blk(d),
        out_shape=jax.ShapeDtypeStruct((rows, d), F32),
        scratch_shapes=_dsa_scratch(qb, [(t // kblk, kblk)], n_dsa),
        compiler_params=_params(2),
        name="dsa_prompt",
    )(qi_pad, trot, q_bf, ki_bf, k_bf, v_t)


def _dsa_sample_kernel(qi_ref, t_ref, q_ref, kin_ref, kn_ref, vn_ref, cki_ref, ck_ref, cv_ref, o_ref,
                       qi_s, q_s, kin_s, kn_s, vn_s, kit_s, sk0_ref, sk1_ref, s0_ref, s1_ref, jstar_ref,
                       mpart_ref, lpart_ref, acc_ref, *, kblk, topk):
    past = cki_ref.shape[2]
    tq = q_ref.shape[0] // 2
    qb = 2 * tq
    n_dsa = q_ref.shape[1] // HEAD_DIM
    pair_w = 2 * LANES

    def head(h):
        return slice(h * HEAD_DIM, (h + 1) * HEAD_DIM)

    def rows(a):
        return slice(a * tq, (a + 1) * tq)

    def half(h, a):
        return slice(h * pair_w + a * LANES, h * pair_w + (a + 1) * LANES)

    for ref in (qi_s, q_s, kin_s, kn_s):
        ref[...] = jnp.zeros(ref.shape, ref.dtype)
    for a in range(2):
        for h in range(N_IDX_HEADS):
            qi_s[rows(a), half(h, a)] = qi_ref[rows(a), h * LANES:(h + 1) * LANES]
        for h in range(n_dsa):
            q_s[rows(a), half(h, a)] = q_ref[rows(a), head(h)]
            kn_s[0:tq, half(h, a)] = kn_ref[rows(a), head(h)]
        kin_s[0:tq, half(0, a)] = kin_ref[rows(a), :]
        vn_s[a, 0:tq, :] = vn_ref[rows(a), :]
        vn_s[a, tq:, :] = jnp.zeros((qb - tq, vn_s.shape[2]), F32)

    first_seq = lax.broadcasted_iota(I32, (HEAD_DIM, qb), 1) < tq

    def pair_pv(v_a, v_b, p):
        out_a = jnp.dot(v_a.T.astype(BF16), p, preferred_element_type=F32)
        out_b = jnp.dot(v_b.T.astype(BF16), p, preferred_element_type=F32)
        return jnp.where(first_seq, out_a, out_b)

    def cache_rows(ref, a, kb, h):
        return ref[a, pl.ds(kb * kblk * n_dsa + h, kblk, stride=n_dsa), :]

    def cache_ki(kb):
        parts = []
        for a in range(2):
            kit_s[a, 0:IDX_DIM, :] = cki_ref[a, :, pl.ds(kb * kblk, kblk)]
            kit_s[a, IDX_DIM:, :] = cki_ref[a, :, pl.ds(kb * kblk, kblk)]
            parts.append(kit_s[a].T.astype(BF16))
        return jnp.concatenate(parts, axis=1)

    segs = [dict(ki=cache_ki,
                 k=lambda kb, h: jnp.concatenate([cache_rows(ck_ref, a, kb, h).astype(BF16) for a in range(2)],
                                                 axis=1),
                 pv=lambda kb, h, p: pair_pv(cache_rows(cv_ref, 0, kb, h), cache_rows(cv_ref, 1, kb, h), p),
                 nblk=past // kblk, kblk=kblk, pos0=0, valid=kblk),
            dict(ki=lambda kb: kin_s[...],
                 k=lambda kb, h: kn_s[:, h * pair_w:(h + 1) * pair_w],
                 pv=lambda kb, h, p: pair_pv(vn_s[0, :, head(h)], vn_s[1, :, head(h)], p),
                 nblk=1, kblk=qb, pos0=past, valid=tq)]
    qside = dict(t=t_ref[...],
                 qi_groups=[(h,) for h in range(N_IDX_HEADS)],
                 qi=lambda group: qi_s[:, group[0] * pair_w:(group[0] + 1) * pair_w],
                 q=lambda h: q_s[:, h * pair_w:(h + 1) * pair_w])

    def store_out(h, x):
        o_ref[:, head(h)] = x

    _dsa_core(qside, segs, store_out, [sk0_ref, sk1_ref], [s0_ref, s1_ref], jstar_ref,
              mpart_ref, lpart_ref, acc_ref, topk=topk, q_chunk0=past // CHUNK, n_q=qb, n_dsa=n_dsa)


def dsa_sample(qi_pad, trot, q_bf, ki_bf, k_bf, u, vc_col, cki, ck, cv, layer, nseq, t, kblk, topk):
    rows, d = q_bf.shape
    n_dsa = d // HEAD_DIM
    past = cki.shape[3]
    qb = 2 * t
    assert qb == LANES and nseq % 2 == 0
    row_blk = lambda w, o=0: pl.BlockSpec((qb, w), lambda b, o=o: (b, o))
    cache_blk = lambda r, w: pl.BlockSpec((None, 2, r, w), lambda b: (layer, b, 0, 0))
    pair_scratch = [pltpu.VMEM((qb, 2 * qi_pad.shape[1]), BF16), pltpu.VMEM((qb, 2 * d), BF16),
                    pltpu.VMEM((qb, 2 * LANES), BF16), pltpu.VMEM((qb, 2 * d), BF16),
                    pltpu.VMEM((2, qb, d), F32), pltpu.VMEM((2, LANES, kblk), F32)]
    return pl.pallas_call(
        functools.partial(_dsa_sample_kernel, kblk=kblk, topk=topk),
        grid=(nseq // 2,),
        in_specs=[row_blk(qi_pad.shape[1]), row_blk(LANES), row_blk(d),
                  row_blk(LANES), row_blk(d), row_blk(d, vc_col),
                  cache_blk(cki.shape[2], past), cache_blk(past * n_dsa, HEAD_DIM),
                  cache_blk(past * n_dsa, HEAD_DIM)],
        out_specs=row_blk(d),
        out_shape=jax.ShapeDtypeStruct((rows, d), F32),
        scratch_shapes=pair_scratch + _dsa_scratch(qb, [(past // kblk, kblk), (1, qb)], n_dsa),
        compiler_params=pltpu.CompilerParams(dimension_semantics=("arbitrary",),
                                             vmem_limit_bytes=DSA_SAMPLE_VMEM_LIMIT),
        name="dsa_sample",
    )(qi_pad, trot, q_bf, ki_bf, k_bf, u, cki, ck, cv)


def _outproj_kernel(oa_ref, ob_ref, oc_ref, za_ref, zb_ref, zc_ref, g_ref, w_ref, x_ref, mod_ref, g2_ref,
                    *rest, d_lru, d_band, last):
    if last:
        yo_ref, y_ref = rest
    else:
        mod2_ref, xo_ref, h_ref, y_ref = rest

    def branch(o_ref, z_ref, lo, hi):
        o = o_ref[...]
        z = z_ref[...]
        ms = jnp.mean(o * o, axis=-1, keepdims=True)
        y = o * lax.rsqrt(ms + EPS) * g_ref[:, lo:hi]
        y_ref[:, lo:hi] = (y * (z * jax.nn.sigmoid(z))).astype(BF16)

    d_mix = y_ref.shape[1]
    branch(oa_ref, za_ref, 0, d_lru)
    branch(ob_ref, zb_ref, d_lru, d_lru + d_band)
    branch(oc_ref, zc_ref, d_lru + d_band, d_mix)
    out = jnp.dot(y_ref[...], w_ref[...], preferred_element_type=F32)
    n_seq = mod_ref.shape[0]
    t = out.shape[0] // n_seq
    for s in range(n_seq):
        rows = slice(s * t, (s + 1) * t)
        x_new = x_ref[rows, :] + mod_ref[s, 2:3, :] * out[rows, :]
        ms = jnp.mean(x_new * x_new, axis=-1, keepdims=True)
        normed = x_new * lax.rsqrt(ms + EPS) * g2_ref[...]
        if last:
            yo_ref[rows, :] = normed
        else:
            xo_ref[rows, :] = x_new
            h_ref[rows, :] = (normed * (1.0 + mod2_ref[s, 1:2, :]) + mod2_ref[s, 0:1, :]).astype(BF16)


def outproj(oa, ob, oc, u, g_branch, w_out, layer, x, mod3, g2, mod3_next, cols, nseq, t, tm):
    rows, d = x.shape
    d_lru, d_band, d_dsa = oa.shape[1], ob.shape[1], oc.shape[1]
    d_mix = d_lru + d_band + d_dsa
    nblk = max(t // tm, 1)
    seqs_per_blk = max(tm // t, 1)
    last = mod3_next is None
    rb = lambda w, o=0: pl.BlockSpec((tm, w), lambda b, i, o=o: (b * nblk + i, o))
    vec = pl.BlockSpec((1, d), lambda b, i: (0, 0))
    mod_spec = pl.BlockSpec((seqs_per_blk, 3, d), lambda b, i: (b, 0, 0))
    in_specs = [rb(d_lru), rb(d_band), rb(d_dsa),
                rb(d_lru, cols["za"] // d_lru), rb(d_band, cols["zb"] // d_band), rb(d_dsa, cols["zc"] // d_dsa),
                pl.BlockSpec((1, d_mix), lambda b, i: (0, 0)),
                pl.BlockSpec((None, d_mix, d), lambda b, i: (layer, 0, 0)),
                rb(d), mod_spec, vec]
    args = [oa, ob, oc, u, u, u, g_branch, w_out, x, mod3, g2]
    if last:
        out_specs = rb(d)
        out_shape = jax.ShapeDtypeStruct((rows, d), F32)
    else:
        in_specs.append(mod_spec)
        args.append(mod3_next)
        out_specs = [rb(d), rb(d)]
        out_shape = [jax.ShapeDtypeStruct((rows, d), F32), jax.ShapeDtypeStruct((rows, d), BF16)]
    return pl.pallas_call(
        functools.partial(_outproj_kernel, d_lru=d_lru, d_band=d_band, last=last),
        grid=(nseq // seqs_per_blk, nblk),
        in_specs=in_specs,
        out_specs=out_specs,
        out_shape=out_shape,
        scratch_shapes=[pltpu.VMEM((tm, d_mix), BF16)],
        compiler_params=_params(2),
        name="outproj",
    )(*args)


def _column_offsets(d_lru, d_band, d_dsa):
    names = ["xa", "za", "qb", "kb", "vb", "zb", "qc", "kc", "vc", "zc", "qi"]
    widths = [d_lru, d_lru, d_band, d_band, d_band, d_band, d_dsa, d_dsa, d_dsa, d_dsa, N_IDX_HEADS * IDX_DIM]
    cols, off = {}, 0
    for n, w in zip(names, widths):
        cols[n] = off
        off += w
    cols["main"] = off
    cols["d_lru"], cols["d_band"], cols["d_dsa"] = d_lru, d_band, d_dsa
    return cols


def _layer_stream(x, h, mod3, mod3_next, g_next, lw, layer, depth, cols, nseq, t, tables, state, strip, stacks,
                  prompt):
    d_lru, d_band, d_dsa = cols["d_lru"], cols["d_band"], cols["d_dsa"]
    n_band, n_dsa = d_band // HEAD_DIM, d_dsa // HEAD_DIM
    rows = nseq * t
    tb = 256 if t % 256 == 0 else t
    tm = 1024 if rows % 1024 == 0 else rows
    u = matmul(h, lw["w_in"], layer, cols["main"], tm, 1024)
    tail = matmul(h, lw["w_tail"], layer, LANES, tm, LANES)

    q_bf, k_bf, qi_pad, t_rot, ki_bf, k_stack, v_stack, *maybe_vt = rope_call(
        u, tail, tables, cols, 256 if rows % 256 == 0 else tb, prompt, layer, depth, stacks)

    conv_s, lru_s = state[0], state[1]
    cs8 = jnp.concatenate([jnp.zeros((nseq, 8 - (CONV_W - 1), d_lru), F32), conv_s], axis=1)
    hs = lru_call(u, cs8, lru_s.reshape(nseq, 1, d_lru), lw["conv_w"], lw["conv_b"], lw["w_rg"], lw["b_rg"],
                  lw["w_ig"], lw["b_ig"], lw["lam"], nseq, t, tb)

    if prompt:
        ob = band_prompt(u, strip, cols, nseq, t)
        oc = dsa_prompt(qi_pad, t_rot, q_bf, ki_bf, k_bf, maybe_vt[0], nseq, t, 128, 512, min(TOPK_MAX, t // 4))
    else:
        bk, bv, dk, dv, dik = state[2:]
        past = dik.shape[3]
        assert (past + t - 1) // CHUNK <= past // CHUNK and t == CHUNK and bk.shape[2] == BAND_PAST * n_band
        ob = band_sample(u, bk, bv, layer, strip, cols, nseq, t)
        oc = dsa_sample(qi_pad, t_rot, q_bf, ki_bf, k_bf, u, cols["vc"] // d_dsa, dik, dk, dv, layer,
                        nseq, t, 512, min(TOPK_MAX, (past + t) // 4))

    res = outproj(hs, ob, oc, u, lw["g_branch"], lw["w_out"], layer, x, mod3, g_next, mod3_next, cols, nseq, t,
                  256 if rows % 256 == 0 else tb)
    x_new, h_next = (res, None) if mod3_next is None else res

    u3 = u.reshape(nseq, t, -1)
    xa = u3[:, :, cols["xa"]:cols["xa"] + d_lru]
    if prompt:
        new_conv = xa[:, t - (CONV_W - 1):]
    else:
        new_conv = jnp.concatenate([conv_s, xa], axis=1)[:, -(CONV_W - 1):]
    nbr = min(BAND_PAST, t)
    new = (new_conv,
           hs.reshape(nseq, t, d_lru)[:, -1],
           u3[:, t - nbr:, cols["kb"]:cols["kb"] + d_band].reshape(nseq, nbr, n_band, HEAD_DIM),
           u3[:, t - nbr:, cols["vb"]:cols["vb"] + d_band].reshape(nseq, nbr, n_band, HEAD_DIM),
           None,
           None,
           t_rot.reshape(nseq, t, LANES)[:, :, :IDX_DIM])
    return x_new, h_next, new, (k_stack, v_stack)


def kernel(x_prompt, x_sample, c_prompt, c_sample, state_conv, state_lru, cache_band_k, cache_band_v,
           cache_dsa_k, cache_dsa_v, cache_dsa_idx_k, g_norm, w_ada, b_ada, w_in, conv_w, conv_b,
           w_rg, b_rg, w_ig, b_ig, lru_lambda, rel_bias, g_branch, w_out, g_final):
    depth = w_in.shape[0]
    nb_p, t_p, d = x_prompt.shape
    nb_s, t_s, _ = x_sample.shape
    past = cache_dsa_k.shape[2]
    d_lru = conv_w.shape[2]
    d_band = cache_band_k.shape[3] * HEAD_DIM
    d_dsa = cache_dsa_k.shape[3] * HEAD_DIM
    cols = _column_offsets(d_lru, d_band, d_dsa)
    n_main = cols["main"]

    mod = ada_all(jnp.concatenate([c_prompt, c_sample], axis=0), w_ada, b_ada)
    mod = mod.reshape(depth, nb_p + nb_s, 3, d)

    w_tail = jnp.pad(w_in[:, :, n_main:], ((0, 0), (0, 0), (0, LANES - (w_in.shape[2] - n_main)))).astype(BF16)
    w_out_bf = w_out.astype(BF16)
    w_rg_bf = w_rg.astype(BF16)
    w_ig_bf = w_ig.astype(BF16)

    band_k = cache_band_k.reshape(depth, nb_s, -1, HEAD_DIM)
    band_v = cache_band_v.reshape(depth, nb_s, -1, HEAD_DIM)
    dsa_k = cache_dsa_k.reshape(depth, nb_s, -1, HEAD_DIM)
    dsa_v = cache_dsa_v.reshape(depth, nb_s, -1, HEAD_DIM)
    idx_k_t = jnp.swapaxes(cache_dsa_idx_k, 2, 3)
    w_in_t = jnp.swapaxes(w_in, 1, 2)

    tab_p = rope_tables(jnp.tile(jnp.arange(t_p), nb_p))
    tab_s = rope_tables(jnp.tile(past + jnp.arange(t_s), nb_s))

    xp = x_prompt.reshape(nb_p * t_p, d)
    xs = x_sample.reshape(nb_s * t_s, d)
    zero_state = (jnp.zeros((nb_p, CONV_W - 1, d_lru), F32), jnp.zeros((nb_p, d_lru), F32))
    p_new, s_new = [], []
    p_stacks = s_stacks = None
    hp = normmod(xp, g_norm[0][None], mod[0, :nb_p], nb_p, t_p, 256 if t_p % 256 == 0 else t_p)
    hsm = normmod(xs, g_norm[0][None], mod[0, nb_p:], nb_s, t_s, 256 if t_s % 256 == 0 else t_s)
    for l in range(depth):
        lw = dict(w_in=w_in_t, w_tail=w_tail, conv_w=conv_w[l], conv_b=conv_b[l][None],
                  w_rg=w_rg_bf[l], b_rg=b_rg[l][None], w_ig=w_ig_bf[l], b_ig=b_ig[l][None],
                  lam=lru_lambda[l][None], g_branch=g_branch[l][None], w_out=w_out_bf)
        last = l == depth - 1
        g_next = g_final[None] if last else g_norm[l + 1][None]
        modp_next = None if last else mod[l + 1, :nb_p]
        mods_next = None if last else mod[l + 1, nb_p:]
        strip = bias_strip(rel_bias[l])
        xp, hp, pn, p_stacks = _layer_stream(xp, hp, mod[l, :nb_p], modp_next, g_next, lw, l, depth, cols, nb_p, t_p,
                                             tab_p, zero_state, strip, p_stacks, True)
        st = (state_conv[l], state_lru[l], band_k, band_v, dsa_k, dsa_v, idx_k_t)
        xs, hsm, sn, s_stacks = _layer_stream(xs, hsm, mod[l, nb_p:], mods_next, g_next, lw, l, depth, cols, nb_s,
                                              t_s, tab_s, st, strip, s_stacks, False)
        p_new.append(pn)
        s_new.append(sn)

    y_prompt = xp.reshape(nb_p, t_p, d)
    y_sample = xs.reshape(nb_s, t_s, d)
    n_dsa = d_dsa // HEAD_DIM

    def outputs(new, stacks, nseq, t):
        out = [None if new[0][j] is None else jnp.stack([layer_new[j] for layer_new in new], axis=0)
               for j in range(7)]
        out[4], out[5] = (s.reshape(depth, nseq, t, n_dsa, HEAD_DIM) for s in stacks)
        return out

    return (y_prompt, y_sample, *outputs(p_new, p_stacks, nb_p, t_p), *outputs(s_new, s_stacks, nb_s, t_s))
```

```python
import functools
import struct

import jax
import jax.numpy as jnp
from jax import lax
from jax.experimental import pallas as pl
from jax.experimental.pallas import tpu as pltpu

F32 = jnp.float32
BF16 = jnp.bfloat16
I32 = jnp.int32

CHUNK = 64
HEAD_DIM = 128
LRU_BLOCKS = 8
CONV_W = 4
LRU_C = 8.0
BAND_PAST_CHUNKS = 8
BAND_PAST = BAND_PAST_CHUNKS * CHUNK
BAND_KEYS = BAND_PAST + CHUNK
REL_CLIP = 256
N_IDX_HEADS = 16
IDX_DIM = 64
TOPK_MAX = 256
ROPE_THETA = 500000.0
ROPE_FRAC = 4
EPS = 1e-6
NEG = -1e30
LANES = 128
SUBLANES = 8

INT_MIN = -(2 ** 31)


def _sortable_key_of(x):
    b = struct.unpack("<i", struct.pack("<f", x))[0]
    return b ^ 0x7FFFFFFF if b < 0 else b


KEY_HALF_NEG = _sortable_key_of(NEG * 0.5)

VMEM_LIMIT = 48 * 1024 * 1024
LARGE_VMEM_LIMIT = 56 * 1024 * 1024
NT_DIMS = (((1,), (1,)), ((), ()))


def _params(n_grid, vmem_limit=VMEM_LIMIT):
    return pltpu.CompilerParams(dimension_semantics=("arbitrary",) * n_grid,
                                vmem_limit_bytes=vmem_limit)


def _dot_nt(a, b):
    return lax.dot_general(a, b, NT_DIMS, preferred_element_type=F32)


def _ada_kernel(c_ref, w_ref, b_ref, o_ref):
    c = c_ref[...]
    s = (c * jax.nn.sigmoid(c)).astype(BF16)
    o_ref[...] = jnp.dot(s, w_ref[...].astype(BF16), preferred_element_type=F32) + b_ref[...]


def ada_all(c_all, w_ada, b_ada):
    depth, d, n = w_ada.shape
    nb = c_all.shape[0]
    tn = 512
    return pl.pallas_call(
        _ada_kernel,
        grid=(depth, n // tn),
        in_specs=[pl.BlockSpec((nb, d), lambda l, j: (0, 0)),
                  pl.BlockSpec((None, d, tn), lambda l, j: (l, 0, j)),
                  pl.BlockSpec((None, 1, tn), lambda l, j: (l, 0, j))],
        out_specs=pl.BlockSpec((None, nb, tn), lambda l, j: (l, 0, j)),
        out_shape=jax.ShapeDtypeStruct((depth, nb, n), F32),
        compiler_params=_params(2),
        name="ada",
    )(c_all, w_ada, b_ada.reshape(depth, 1, n))


def _normmod_kernel(x_ref, g_ref, mod_ref, h_ref):
    x = x_ref[...]
    ms = jnp.mean(x * x, axis=-1, keepdims=True)
    y = x * lax.rsqrt(ms + EPS) * g_ref[...]
    shift = mod_ref[0:1, :]
    scale = mod_ref[1:2, :]
    h_ref[...] = (y * (1.0 + scale) + shift).astype(BF16)


def normmod(x, g, mod3, nseq, t, tb):
    rows, d = x.shape
    nblk = t // tb
    return pl.pallas_call(
        _normmod_kernel,
        grid=(nseq, nblk),
        in_specs=[pl.BlockSpec((tb, d), lambda b, i: (b * nblk + i, 0)),
                  pl.BlockSpec((1, d), lambda b, i: (0, 0)),
                  pl.BlockSpec((None, 3, d), lambda b, i: (b, 0, 0))],
        out_specs=pl.BlockSpec((tb, d), lambda b, i: (b * nblk + i, 0)),
        out_shape=jax.ShapeDtypeStruct((rows, d), BF16),
        compiler_params=_params(2),
        name="normmod",
    )(x, g, mod3)


def _mm_kernel(a_ref, b_ref, o_ref, *scratch):
    if scratch:
        wb_ref, = scratch

        @pl.when(pl.program_id(1) == 0)
        def _():
            wb_ref[...] = b_ref[...].T.astype(BF16)

        w = wb_ref[...]
    else:
        w = b_ref[...]
    o_ref[...] = jnp.dot(a_ref[...], w, preferred_element_type=F32)


def matmul(a, w, layer, ncols, tm, tn):
    m, k = a.shape
    if w.dtype == BF16:
        scratch = []
        w_spec = pl.BlockSpec((None, k, tn), lambda j, i: (layer, 0, j))
    else:
        scratch = [pltpu.VMEM((k, tn), BF16)]
        w_spec = pl.BlockSpec((None, tn, k), lambda j, i: (layer, j, 0))
    return pl.pallas_call(
        _mm_kernel,
        grid=(ncols // tn, m // tm),
        in_specs=[pl.BlockSpec((tm, k), lambda j, i: (i, 0)), w_spec],
        out_specs=pl.BlockSpec((tm, tn), lambda j, i: (i, j)),
        out_shape=jax.ShapeDtypeStruct((m, ncols), F32),
        scratch_shapes=scratch,
        compiler_params=_params(2),
        name="inproj",
    )(a, w)


def _rope_kernel(qc_ref, kc_ref, vc_ref, qi_ref, t_ref, c128_ref, s128_ref, c64_ref, s64_ref, *rest,
                 n_dsa, n_idx, n_aliased):
    q_out, kbf_out, qi_out, trot_out, kibf_out, kst_out, vst_out, *maybe_vt_out = rest[n_aliased:]
    tm = qc_ref.shape[0]
    lane = lax.broadcasted_iota(I32, (tm, LANES), 1)
    c128 = c128_ref[...]
    s128 = s128_ref[...]
    c64 = c64_ref[...]
    s64 = s64_ref[...]
    half128 = HEAD_DIM // ROPE_FRAC // 2
    half64 = IDX_DIM // ROPE_FRAC // 2

    def rope128(x):
        partner = jnp.where(lane < half128, pltpu.roll(x, LANES - half128, 1), pltpu.roll(x, half128, 1))
        return x * c128 + partner * s128

    def rope64(x):
        partner = jnp.where((lane & (IDX_DIM - 1)) < half64,
                            pltpu.roll(x, LANES - half64, 1), pltpu.roll(x, half64, 1))
        return x * c64 + partner * s64

    for h in range(n_dsa):
        sl = slice(h * LANES, (h + 1) * LANES)
        q_out[:, sl] = rope128(qc_ref[:, sl]).astype(BF16)
        kr = rope128(kc_ref[:, sl])
        kbf_out[:, sl] = kr.astype(BF16)
        kst_out[pl.ds(h, tm, stride=n_dsa), :] = kr
        vst_out[pl.ds(h, tm, stride=n_dsa), :] = vc_ref[:, sl]
    if maybe_vt_out:
        maybe_vt_out[0][...] = vc_ref[...].T.astype(BF16)
    low = lane < IDX_DIM
    for j in range(n_idx // 2):
        r = rope64(qi_ref[:, j * LANES:(j + 1) * LANES]) * (IDX_DIM ** -0.5)
        qi_out[:, (2 * j) * LANES:(2 * j + 1) * LANES] = jnp.where(low, r, 0.0).astype(BF16)
        qi_out[:, (2 * j + 1) * LANES:(2 * j + 2) * LANES] = jnp.where(low, 0.0, r).astype(BF16)
    t = t_ref[...]
    r = rope64(t)
    trot_out[...] = jnp.where(low, r, jnp.where(lane < IDX_DIM + N_IDX_HEADS, t * (N_IDX_HEADS ** -0.5), 0.0))
    kibf_out[...] = jnp.where(low, r, pltpu.roll(r, IDX_DIM, 1)).astype(BF16)


def rope_call(u, tail, tables, cols, tm, v_t, layer, depth, stacks):
    rows = u.shape[0]
    d_dsa = cols["d_dsa"]
    n_dsa = d_dsa // HEAD_DIM
    d_qi = N_IDX_HEADS * IDX_DIM
    c128, s128, c64, s64 = tables
    tab_spec = pl.BlockSpec((tm, LANES), lambda i: (i, 0))
    blk = lambda off: pl.BlockSpec((tm, d_dsa), lambda i, o=off // d_dsa: (i, o))
    row_spec = pl.BlockSpec((tm, d_dsa), lambda i: (i, 0))
    stack_spec = pl.BlockSpec((None, tm * n_dsa, HEAD_DIM), lambda i: (layer, i, 0))
    stack_shape = jax.ShapeDtypeStruct((depth, rows * n_dsa, HEAD_DIM), F32)
    out_specs = [row_spec, row_spec, pl.BlockSpec((tm, N_IDX_HEADS * LANES), lambda i: (i, 0)),
                 tab_spec, tab_spec, stack_spec, stack_spec]
    out_shape = [jax.ShapeDtypeStruct((rows, d_dsa), BF16),
                 jax.ShapeDtypeStruct((rows, d_dsa), BF16),
                 jax.ShapeDtypeStruct((rows, N_IDX_HEADS * LANES), BF16),
                 jax.ShapeDtypeStruct((rows, LANES), F32),
                 jax.ShapeDtypeStruct((rows, LANES), BF16),
                 stack_shape, stack_shape]
    if v_t:
        out_specs.append(pl.BlockSpec((d_dsa, tm), lambda i: (0, i)))
        out_shape.append(jax.ShapeDtypeStruct((d_dsa, rows), BF16))
    in_specs = [blk(cols["qc"]), blk(cols["kc"]), blk(cols["vc"]),
                pl.BlockSpec((tm, d_qi), lambda i, o=cols["qi"] // d_qi: (i, o)),
                tab_spec, tab_spec, tab_spec, tab_spec, tab_spec]
    args = [u, u, u, u, tail, c128, s128, c64, s64]
    aliases = {}
    if stacks is not None:
        aliases = {len(args): 5, len(args) + 1: 6}
        in_specs += [pl.BlockSpec(memory_space=pl.ANY)] * 2
        args += list(stacks)
    return pl.pallas_call(
        functools.partial(_rope_kernel, n_dsa=n_dsa, n_idx=N_IDX_HEADS, n_aliased=len(aliases)),
        grid=(rows // tm,),
        in_specs=in_specs,
        out_specs=out_specs,
        out_shape=out_shape,
        input_output_aliases=aliases,
        compiler_params=_params(1),
        name="rope",
    )(*args)


def rope_tables(pos):
    pos = pos.astype(F32)[:, None]
    n = pos.shape[0]

    def tab(dim):
        half = dim // ROPE_FRAC // 2
        inv = ROPE_THETA ** (-jnp.arange(half, dtype=F32) / half)
        ang = pos * inv[None]
        cos, sin = jnp.cos(ang), jnp.sin(ang)
        c = jnp.concatenate([cos, cos, jnp.ones((n, dim - 2 * half), F32)], axis=1)
        s = jnp.concatenate([-sin, sin, jnp.zeros((n, dim - 2 * half), F32)], axis=1)
        return jnp.tile(c, (1, LANES // dim)), jnp.tile(s, (1, LANES // dim))

    c128, s128 = tab(HEAD_DIM)
    c64, s64 = tab(IDX_DIM)
    return c128, s128, c64, s64


def _lru_kernel(xa_ref, cs_ref, h0_ref, cw_ref, cb_ref, wrg_ref, brg_ref, wig_ref, big_ref, lam_ref,
                hs_ref, ext_ref, hc_ref):
    tb, d = xa_ref.shape
    blk = d // LRU_BLOCKS

    @pl.when(pl.program_id(1) == 0)
    def _():
        ext_ref[0:8, :] = cs_ref[...]
        hc_ref[...] = h0_ref[...]

    ext_ref[8:8 + tb, :] = xa_ref[...]
    conv = ext_ref[5:5 + tb, :] * cw_ref[0:1, :]
    for j in range(1, CONV_W):
        conv = conv + ext_ref[5 + j:5 + j + tb, :] * cw_ref[j:j + 1, :]
    conv = conv + cb_ref[...]
    tail = ext_ref[tb:tb + 8, :]
    ext_ref[0:8, :] = tail

    xb = conv.astype(BF16)
    rs, gs = [], []
    for g in range(LRU_BLOCKS):
        xg = xb[:, g * blk:(g + 1) * blk]
        rs.append(jnp.dot(xg, wrg_ref[g], preferred_element_type=F32))
        gs.append(jnp.dot(xg, wig_ref[g], preferred_element_type=F32))
    r = jax.nn.sigmoid(jnp.concatenate(rs, axis=1) + brg_ref[...])
    ig = jax.nn.sigmoid(jnp.concatenate(gs, axis=1) + big_ref[...])
    lam = lam_ref[...]
    softplus_neg_lam = jnp.maximum(-lam, 0.0) + jnp.log1p(jnp.exp(-jnp.abs(lam)))
    log_a = (-LRU_C) * r * softplus_neg_lam
    a = jnp.exp(log_a)
    u = jnp.sqrt(-jnp.tanh(log_a) * (a * a + 1.0)) * (ig * conv)

    n_groups = tb // SUBLANES
    a = a.reshape(n_groups, SUBLANES, d)
    u = u.reshape(n_groups, SUBLANES, d)
    row_in_group = lax.broadcasted_iota(I32, (n_groups, SUBLANES, d), 1)
    s = 1
    while s < SUBLANES:
        keep = row_in_group >= s
        a_prev = jnp.where(keep, pltpu.roll(a, s, 1), 1.0)
        u_prev = jnp.where(keep, pltpu.roll(u, s, 1), 0.0)
        u = a * u_prev + u
        a = a * a_prev
        s *= 2
    h_in = hc_ref[...]
    for g in range(n_groups):
        h_group = a[g] * h_in + u[g]
        hs_ref[g * SUBLANES:(g + 1) * SUBLANES, :] = h_group
        h_in = h_group[SUBLANES - 1:SUBLANES, :]
    hc_ref[...] = h_in


def lru_call(u, cs8, h0, cw, cb, wrg, brg, wig, big, lam, nseq, t, tb):
    rows = u.shape[0]
    d = cw.shape[1]
    blk = d // LRU_BLOCKS
    nblk = t // tb
    vec = pl.BlockSpec((1, d), lambda b, i: (0, 0))
    wspec = pl.BlockSpec((LRU_BLOCKS, blk, blk), lambda b, i: (0, 0, 0))
    return pl.pallas_call(
        _lru_kernel,
        grid=(nseq, nblk),
        in_specs=[pl.BlockSpec((tb, d), lambda b, i: (b * nblk + i, 0)),
                  pl.BlockSpec((None, 8, d), lambda b, i: (b, 0, 0)),
                  pl.BlockSpec((None, 1, d), lambda b, i: (b, 0, 0)),
                  pl.BlockSpec((CONV_W, d), lambda b, i: (0, 0)),
                  vec, wspec, vec, wspec, vec, vec],
        out_specs=pl.BlockSpec((tb, d), lambda b, i: (b * nblk + i, 0)),
        out_shape=jax.ShapeDtypeStruct((rows, d), F32),
        scratch_shapes=[pltpu.VMEM((tb + 8, d), F32), pltpu.VMEM((1, d), F32)],
        compiler_params=_params(2),
        name="lru",
    )(u, cs8, h0, cw, cb, wrg, brg, wig, big, lam)


def _bias_kernel(relb_ref, o_ref):
    h = pl.program_id(0)
    qi = lax.broadcasted_iota(I32, (CHUNK, BAND_KEYS), 0)
    kj = lax.broadcasted_iota(I32, (CHUNK, BAND_KEYS), 1)
    idx = jnp.clip(qi + BAND_PAST - kj, -REL_CLIP, REL_CLIP) + REL_CLIP

    def body(r, acc):
        return jnp.where(idx == r, relb_ref[h, r], acc)

    lowest = max(-(CHUNK - 1), -REL_CLIP) + REL_CLIP
    highest = min(BAND_PAST + CHUNK - 1, REL_CLIP) + REL_CLIP
    o_ref[...] = lax.fori_loop(lowest, highest + 1, body, jnp.zeros((CHUNK, BAND_KEYS), F32))


def bias_strip(relb):
    nh = relb.shape[0]
    return pl.pallas_call(
        _bias_kernel,
        grid=(nh,),
        in_specs=[pl.BlockSpec(memory_space=pltpu.SMEM)],
        out_specs=pl.BlockSpec((None, CHUNK, BAND_KEYS), lambda h: (h, 0, 0)),
        out_shape=jax.ShapeDtypeStruct((nh, CHUNK, BAND_KEYS), F32),
        compiler_params=_params(1),
        name="bias_strip",
    )(relb)


def _band_prompt_kernel(q_ref, kp_ref, kc_ref, vp_ref, vc_ref, strip_ref, o_ref, kw_ref, vw_ref):
    i = pl.program_id(2)
    tb = q_ref.shape[0]
    kw_ref[0:tb, :] = kp_ref[...].astype(BF16)
    kw_ref[tb:2 * tb, :] = kc_ref[...].astype(BF16)
    vw_ref[0:tb, :] = vp_ref[...].astype(BF16)
    vw_ref[tb:2 * tb, :] = vc_ref[...].astype(BF16)
    strip = strip_ref[...]
    col = lax.broadcasted_iota(I32, (CHUNK, BAND_KEYS), 1)
    chunks = range(tb // CHUNK)
    scores = []
    for a in chunks:
        q = q_ref[a * CHUNK:(a + 1) * CHUNK, :].astype(BF16)
        s = _dot_nt(q, kw_ref[a * CHUNK:a * CHUNK + BAND_KEYS, :]) * (HEAD_DIM ** -0.5) + strip
        min_col = jnp.where(i > 0, 0, (BAND_PAST_CHUNKS - a) * CHUNK)
        scores.append(jnp.where(col >= min_col, s, NEG))
    maxes = [jnp.max(s, axis=-1, keepdims=True) for s in scores]
    probs = [jnp.exp(s - m) for s, m in zip(scores, maxes)]
    sums = [jnp.sum(p, axis=-1, keepdims=True) for p in probs]
    for a in chunks:
        pv = jnp.dot(probs[a].astype(BF16), vw_ref[a * CHUNK:a * CHUNK + BAND_KEYS, :],
                     preferred_element_type=F32)
        o_ref[a * CHUNK:(a + 1) * CHUNK, :] = pv / sums[a]


def band_prompt(u, strip, cols, nseq, t):
    rows = u.shape[0]
    tb = BAND_PAST
    nblk = t // tb
    nh = strip.shape[0]
    qo, ko, vo = cols["qb"] // HEAD_DIM, cols["kb"] // HEAD_DIM, cols["vb"] // HEAD_DIM
    cur = lambda off: pl.BlockSpec((tb, HEAD_DIM), lambda b, h, i: (b * nblk + i, off + h))
    prev = lambda off: pl.BlockSpec((tb, HEAD_DIM), lambda b, h, i: (b * nblk + jnp.maximum(i - 1, 0), off + h))
    return pl.pallas_call(
        _band_prompt_kernel,
        grid=(nseq, nh, nblk),
        in_specs=[cur(qo), prev(ko), cur(ko), prev(vo), cur(vo),
                  pl.BlockSpec((None, CHUNK, BAND_KEYS), lambda b, h, i: (h, 0, 0))],
        out_specs=pl.BlockSpec((tb, HEAD_DIM), lambda b, h, i: (b * nblk + i, h)),
        out_shape=jax.ShapeDtypeStruct((rows, nh * HEAD_DIM), F32),
        scratch_shapes=[pltpu.VMEM((2 * tb, HEAD_DIM), BF16), pltpu.VMEM((2 * tb, HEAD_DIM), BF16)],
        compiler_params=_params(3),
        name="band_prompt",
    )(u, u, u, u, u, strip)


def _band_sample_kernel(q_ref, kn_ref, vn_ref, ck_ref, cv_ref, strip_ref, o_ref, kw_ref, vw_ref):
    nh = strip_ref.shape[0]
    w = ck_ref.shape[0] // nh
    t = q_ref.shape[0]
    kw_ref[w:w + t, :] = kn_ref[...].astype(BF16)
    vw_ref[w:w + t, :] = vn_ref[...].astype(BF16)
    heads = [slice(h * HEAD_DIM, (h + 1) * HEAD_DIM) for h in range(nh)]
    for h, sl in enumerate(heads):
        kw_ref[0:w, sl] = ck_ref[pl.ds(h, w, stride=nh), :].astype(BF16)
        vw_ref[0:w, sl] = cv_ref[pl.ds(h, w, stride=nh), :].astype(BF16)
    scores = [_dot_nt(q_ref[:, sl].astype(BF16), kw_ref[:, sl]) * (HEAD_DIM ** -0.5) + strip_ref[h]
              for h, sl in enumerate(heads)]
    maxes = [jnp.max(s, axis=-1, keepdims=True) for s in scores]
    probs = [jnp.exp(s - m) for s, m in zip(scores, maxes)]
    sums = [jnp.sum(p, axis=-1, keepdims=True) for p in probs]
    for h, sl in enumerate(heads):
        pv = jnp.dot(probs[h].astype(BF16), vw_ref[:, sl], preferred_element_type=F32)
        o_ref[:, sl] = pv / sums[h]


def band_sample(u, ck, cv, layer, strip, cols, nseq, t):
    rows = u.shape[0]
    nh = strip.shape[0]
    d = nh * HEAD_DIM
    w = ck.shape[2] // nh
    ublk = lambda off: pl.BlockSpec((t, d), lambda b, o=off // d: (b, o))
    cblk = pl.BlockSpec((None, None, w * nh, HEAD_DIM), lambda b: (layer, b, 0, 0))
    return pl.pallas_call(
        _band_sample_kernel,
        grid=(nseq,),
        in_specs=[ublk(cols["qb"]), ublk(cols["kb"]), ublk(cols["vb"]), cblk, cblk,
                  pl.BlockSpec((nh, CHUNK, BAND_KEYS), lambda b: (0, 0, 0))],
        out_specs=pl.BlockSpec((t, d), lambda b: (b, 0)),
        out_shape=jax.ShapeDtypeStruct((rows, d), F32),
        scratch_shapes=[pltpu.VMEM((w + t, d), BF16), pltpu.VMEM((w + t, d), BF16)],
        compiler_params=_params(1),
        name="band_sample",
    )(u, u, u, ck, cv, strip)


def _loop(n, body, init):
    if isinstance(n, int):
        val = init
        for k in range(n):
            val = body(k, val)
        return val
    return lax.fori_loop(0, n, body, init)


IDX_BITS = 16


def _f32_to_key(x):
    bits = lax.bitcast_convert_type(x, I32)
    return jnp.where(bits < 0, bits ^ 0x7FFFFFFF, bits)


def _dsa_core(qside, segs, store_out, sk_refs, s_refs, jstar_ref, mpart_ref, lpart_ref, acc_ref, *,
              topk, q_chunk0, n_q, n_dsa):
    qb = qside["t"].shape[0]
    t_t = qside["t"].T
    wrows = [t_t[IDX_DIM + h:IDX_DIM + h + 1, :] for h in range(N_IDX_HEADS)]

    def tiles(x):
        return x.reshape(x.shape[0] // SUBLANES, SUBLANES, qb)

    w8 = [jnp.broadcast_to(w, (SUBLANES, qb)) for w in wrows]
    qcol8 = lax.broadcasted_iota(I32, (SUBLANES, qb), 1)
    q_end8 = (q_chunk0 + qcol8 // CHUNK + 1) * CHUNK
    for seg, sk_ref in zip(segs, sk_refs):
        kblk = seg["kblk"]
        n_t = kblk // SUBLANES
        krow3 = (lax.broadcasted_iota(I32, (n_t, SUBLANES, qb), 0) * SUBLANES
                 + lax.broadcasted_iota(I32, (n_t, SUBLANES, qb), 1))
        padded = seg["valid"] < kblk or n_q < qb
        live3 = (krow3 < seg["valid"]) & (lax.broadcasted_iota(I32, (n_t, SUBLANES, qb), 2) < n_q)

        def score_blk(kb, carry, seg=seg, sk_ref=sk_ref, kblk=kblk, krow3=krow3, padded=padded, live3=live3):
            kib = seg["ki"](kb)
            acc = jnp.zeros(krow3.shape, F32)
            for group in qside["qi_groups"]:
                sc = _dot_nt(kib, qside["qi"](group))
                for g, h in enumerate(group):
                    acc = acc + jnp.maximum(tiles(sc[:, g * qb:(g + 1) * qb]), 0.0) * w8[h]
            admissible = krow3 < q_end8 - (seg["pos0"] + kb * kblk)
            key = _f32_to_key(jnp.where(admissible, acc, NEG))
            sk_ref[kb] = jnp.where(live3, key, INT_MIN) if padded else key
            return carry

        _loop(seg["nblk"], score_blk, 0)

    def count(pred):
        part = jnp.zeros((SUBLANES, qb), I32)
        for seg, sk_ref in zip(segs, sk_refs):
            n_t = seg["kblk"] // SUBLANES
            off = (lax.broadcasted_iota(I32, (n_t, SUBLANES, qb), 0) * SUBLANES
                   + lax.broadcasted_iota(I32, (n_t, SUBLANES, qb), 1))

            def body(kb, part, seg=seg, sk_ref=sk_ref, off=off):
                idx = seg["pos0"] + kb * seg["kblk"] + off
                return part + jnp.sum(jnp.where(pred(sk_ref[kb], idx), 1, 0), axis=0)

            part = _loop(seg["nblk"], body, part)
        return jnp.broadcast_to(jnp.sum(part, axis=0, keepdims=True), (SUBLANES, qb))

    zero = jnp.zeros((SUBLANES, qb), I32)
    n_stored = sum(seg["nblk"] * seg["kblk"] for seg in segs)
    c_zero = count(lambda key, idx: key >= zero)
    thr0 = jnp.where(c_zero >= topk, 0, INT_MIN).astype(I32)
    c_lo0 = jnp.where(c_zero >= topk, c_zero, zero + n_stored)

    def bit_step(bit, state):
        t, c = state
        cand = t + jnp.left_shift(jnp.int32(1), bit)
        c_cand = count(lambda key, idx: key >= cand)
        ok = c_cand >= topk
        return jnp.where(ok, cand, t), jnp.where(ok, c_cand, c)

    thr, c_lo = lax.fori_loop(0, 31, lambda i, st: bit_step(30 - i, st), (thr0, c_lo0))
    lo = jnp.maximum(thr, KEY_HALF_NEG + 1)

    big = 2 ** IDX_BITS
    jstar_ref[...] = jnp.full((SUBLANES, qb), big, I32)
    surplus = (c_lo > topk) & (thr > KEY_HALF_NEG)

    @pl.when(jnp.max(jnp.where(surplus, 1, 0)) > 0)
    def _():
        need = topk - count(lambda key, idx: key > thr)

        def idx_body(it, j):
            cand = j + jnp.left_shift(jnp.int32(1), IDX_BITS - 1 - it)
            f = count(lambda key, idx: (key == lo) & (idx < cand))
            return jnp.where(f <= need, cand, j)

        j = lax.fori_loop(0, IDX_BITS, idx_body, zero)
        jstar_ref[...] = jnp.where(surplus, j, big)

    jstar = jstar_ref[...]

    mpart_ref[...] = jnp.full(mpart_ref.shape, NEG, F32)
    lpart_ref[...] = jnp.zeros(lpart_ref.shape, F32)
    acc_ref[...] = jnp.zeros(acc_ref.shape, F32)

    for seg, sk_ref, s_ref in zip(segs, sk_refs, s_refs):
        kblk = seg["kblk"]
        n_t = kblk // SUBLANES
        krow3 = (lax.broadcasted_iota(I32, (n_t, SUBLANES, qb), 0) * SUBLANES
                 + lax.broadcasted_iota(I32, (n_t, SUBLANES, qb), 1))

        def pass_a(kb, carry, seg=seg, sk_ref=sk_ref, s_ref=s_ref, kblk=kblk, krow3=krow3):
            key = sk_ref[kb]
            mask = (key >= lo) & ((key > lo) | (krow3 < jstar - (seg["pos0"] + kb * kblk)))
            raw = [tiles(_dot_nt(seg["k"](kb, h), qside["q"](h))) for h in range(n_dsa)]
            for h in range(n_dsa):
                s = jnp.where(mask, raw[h] * (HEAD_DIM ** -0.5), NEG)
                s_ref[h, kb] = s
                mpart_ref[h] = jnp.maximum(mpart_ref[h], jnp.max(s, axis=0))
            return carry

        _loop(seg["nblk"], pass_a, 0)

    m8 = [jnp.broadcast_to(jnp.max(mpart_ref[h], axis=0, keepdims=True), (SUBLANES, qb)) for h in range(n_dsa)]

    for seg, s_ref in zip(segs, s_refs):
        def pass_b(kb, carry, seg=seg, s_ref=s_ref):
            for h in range(n_dsa):
                p = jnp.exp(s_ref[h, kb] - m8[h])
                lpart_ref[h] += jnp.sum(p, axis=0)
                acc_ref[h] += seg["pv"](kb, h, p.reshape(seg["kblk"], qb).astype(BF16))
            return carry

        _loop(seg["nblk"], pass_b, 0)

    for h in range(n_dsa):
        l = jnp.sum(lpart_ref[h], axis=0, keepdims=True)
        store_out(h, (acc_ref[h] / l).T)


def _blk_start(kb, kblk):
    return kb * kblk if isinstance(kb, int) else pl.multiple_of(kb * kblk, kblk)


def _dsa_scratch(qb, seg_shapes, n_dsa):
    return ([pltpu.VMEM((nblk, kblk // SUBLANES, SUBLANES, qb), I32) for nblk, kblk in seg_shapes]
            + [pltpu.VMEM((n_dsa, nblk, kblk // SUBLANES, SUBLANES, qb), F32) for nblk, kblk in seg_shapes]            + [pltpu.VMEM((SUBLANES, qb), I32),
               pltpu.VMEM((n_dsa, SUBLANES, qb), F32), pltpu.VMEM((n_dsa, SUBLANES, qb), F32),
               pltpu.VMEM((n_dsa, HEAD_DIM, qb), F32)])


def _dsa_prompt_kernel(qi_ref, t_ref, q_ref, ki_ref, k_ref, vt_ref, o_ref, sk_ref, s_ref, jstar_ref,
                       mpart_ref, lpart_ref, acc_ref, *, kblk, topk):
    i = pl.program_id(1)
    qb = q_ref.shape[0]
    nblk = (i * qb + qb + kblk - 1) // kblk

    def head(h):
        return slice(h * HEAD_DIM, (h + 1) * HEAD_DIM)

    seg = dict(ki=lambda kb: ki_ref[pl.ds(_blk_start(kb, kblk), kblk), :],
               k=lambda kb, h: k_ref[pl.ds(_blk_start(kb, kblk), kblk), head(h)],
               pv=lambda kb, h, p: jnp.dot(vt_ref[head(h), pl.ds(_blk_start(kb, kblk), kblk)], p,
                                           preferred_element_type=F32),
               nblk=nblk, kblk=kblk, pos0=0, valid=kblk)
    qside = dict(t=t_ref[...],
                 qi_groups=[(h, h + 1) for h in range(0, N_IDX_HEADS, 2)],
                 qi=lambda group: jnp.concatenate([qi_ref[:, h * LANES:(h + 1) * LANES] for h in group], axis=0),
                 q=lambda h: q_ref[:, head(h)])

    def store_out(h, x):
        o_ref[:, head(h)] = x

    _dsa_core(qside, [seg], store_out, [sk_ref], [s_ref], jstar_ref, mpart_ref, lpart_ref, acc_ref,
              topk=topk, q_chunk0=(i * qb) // CHUNK, n_q=qb, n_dsa=q_ref.shape[1] // HEAD_DIM)


def dsa_prompt(qi_pad, trot, q_bf, ki_bf, k_bf, v_t, nseq, t, qb, kblk, topk):
    rows, d = q_bf.shape
    n_dsa = d // HEAD_DIM
    nq = t // qb
    row_blk = lambda w: pl.BlockSpec((qb, w), lambda b, i: (b * nq + i, 0))
    seq_blk = lambda w: pl.BlockSpec((t, w), lambda b, i: (b, 0))
    return pl.pallas_call(
        functools.partial(_dsa_prompt_kernel, kblk=kblk, topk=topk),
        grid=(nseq, nq),
        in_specs=[row_blk(qi_pad.shape[1]), row_blk(LANES), row_blk(d),
                  seq_blk(LANES), seq_blk(d), pl.BlockSpec((d, t), lambda b, i: (0, b))],
        out_specs=row_blk(d),
        out_shape=jax.ShapeDtypeStruct((rows, d), F32),
        scratch_shapes=_dsa_scratch(qb, [(t // kblk, kblk)], n_dsa),
        compiler_params=_params(2),
        name="dsa_prompt",
    )(qi_pad, trot, q_bf, ki_bf, k_bf, v_t)


def _dsa_sample_kernel(qi_ref, t_ref, q_ref, kin_ref, kn_ref, vn_ref, cki_ref, ck_ref, cv_ref, o_ref,
                       qi_s, q_s, kin_s, kn_s, vn_s, kit_s, sk0_ref, sk1_ref, s0_ref, s1_ref, jstar_ref,
                       mpart_ref, lpart_ref, acc_ref, *, kblk, topk):
    past = cki_ref.shape[2]
    tq = q_ref.shape[0] // 2
    qb = 2 * tq
    n_dsa = q_ref.shape[1] // HEAD_DIM
    pair_w = 2 * LANES

    def head(h):
        return slice(h * HEAD_DIM, (h + 1) * HEAD_DIM)

    def rows(a):
        return slice(a * tq, (a + 1) * tq)

    def half(h, a):
        return slice(h * pair_w + a * LANES, h * pair_w + (a + 1) * LANES)

    for ref in (qi_s, q_s, kin_s, kn_s):
        ref[...] = jnp.zeros(ref.shape, ref.dtype)
    for a in range(2):
        for h in range(N_IDX_HEADS):
            qi_s[rows(a), half(h, a)] = qi_ref[rows(a), h * LANES:(h + 1) * LANES]
        for h in range(n_dsa):
            q_s[rows(a), half(h, a)] = q_ref[rows(a), head(h)]
            kn_s[0:tq, half(h, a)] = kn_ref[rows(a), head(h)]
        kin_s[0:tq, half(0, a)] = kin_ref[rows(a), :]
        vn_s[a, 0:tq, :] = vn_ref[rows(a), :]
        vn_s[a, tq:, :] = jnp.zeros((qb - tq, vn_s.shape[2]), F32)

    first_seq = lax.broadcasted_iota(I32, (HEAD_DIM, qb), 1) < tq

    def pair_pv(v_a, v_b, p):
        out_a = jnp.dot(v_a.T.astype(BF16), p, preferred_element_type=F32)
        out_b = jnp.dot(v_b.T.astype(BF16), p, preferred_element_type=F32)
        return jnp.where(first_seq, out_a, out_b)

    def cache_rows(ref, a, kb, h):
        return ref[a, pl.ds(kb * kblk * n_dsa + h, kblk, stride=n_dsa), :]

    def cache_ki(kb):
        parts = []
        for a in range(2):
            kit_s[a, 0:IDX_DIM, :] = cki_ref[a, :, pl.ds(kb * kblk, kblk)]
            kit_s[a, IDX_DIM:, :] = cki_ref[a, :, pl.ds(kb * kblk, kblk)]
            parts.append(kit_s[a].T.astype(BF16))
        return jnp.concatenate(parts, axis=1)

    segs = [dict(ki=cache_ki,
                 k=lambda kb, h: jnp.concatenate([cache_rows(ck_ref, a, kb, h).astype(BF16) for a in range(2)],
                                                 axis=1),
                 pv=lambda kb, h, p: pair_pv(cache_rows(cv_ref, 0, kb, h), cache_rows(cv_ref, 1, kb, h), p),
                 nblk=past // kblk, kblk=kblk, pos0=0, valid=kblk),
            dict(ki=lambda kb: kin_s[...],
                 k=lambda kb, h: kn_s[:, h * pair_w:(h + 1) * pair_w],
                 pv=lambda kb, h, p: pair_pv(vn_s[0, :, head(h)], vn_s[1, :, head(h)], p),
                 nblk=1, kblk=qb, pos0=past, valid=tq)]
    qside = dict(t=t_ref[...],
                 qi_groups=[(h,) for h in range(N_IDX_HEADS)],
                 qi=lambda group: qi_s[:, group[0] * pair_w:(group[0] + 1) * pair_w],
                 q=lambda h: q_s[:, h * pair_w:(h + 1) * pair_w])

    def store_out(h, x):
        o_ref[:, head(h)] = x

    _dsa_core(qside, segs, store_out, [sk0_ref, sk1_ref], [s0_ref, s1_ref], jstar_ref,
              mpart_ref, lpart_ref, acc_ref, topk=topk, q_chunk0=past // CHUNK, n_q=qb, n_dsa=n_dsa)


def dsa_sample(qi_pad, trot, q_bf, ki_bf, k_bf, u, vc_col, cki, ck, cv, layer, nseq, t, kblk, topk):
    rows, d = q_bf.shape
    n_dsa = d // HEAD_DIM
    past = cki.shape[3]
    qb = 2 * t
    assert qb == LANES and nseq % 2 == 0
    row_blk = lambda w, o=0: pl.BlockSpec((qb, w), lambda b, o=o: (b, o))
    cache_blk = lambda r, w: pl.BlockSpec((None, 2, r, w), lambda b: (layer, b, 0, 0))
    pair_scratch = [pltpu.VMEM((qb, 2 * qi_pad.shape[1]), BF16), pltpu.VMEM((qb, 2 * d), BF16),
                    pltpu.VMEM((qb, 2 * LANES), BF16), pltpu.VMEM((qb, 2 * d), BF16),
                    pltpu.VMEM((2, qb, d), F32), pltpu.VMEM((2, LANES, kblk), F32)]
    return pl.pallas_call(
        functools.partial(_dsa_sample_kernel, kblk=kblk, topk=topk),
        grid=(nseq // 2,),
        in_specs=[row_blk(qi_pad.shape[1]), row_blk(LANES), row_blk(d),
                  row_blk(LANES), row_blk(d), row_blk(d, vc_col),
                  cache_blk(cki.shape[2], past), cache_blk(past * n_dsa, HEAD_DIM),
                  cache_blk(past * n_dsa, HEAD_DIM)],
        out_specs=row_blk(d),
        out_shape=jax.ShapeDtypeStruct((rows, d), F32),
        scratch_shapes=pair_scratch + _dsa_scratch(qb, [(past // kblk, kblk), (1, qb)], n_dsa),
        compiler_params=_params(1, LARGE_VMEM_LIMIT),
        name="dsa_sample",
    )(qi_pad, trot, q_bf, ki_bf, k_bf, u, cki, ck, cv)


def _outproj_kernel(oa_ref, ob_ref, oc_ref, za_ref, zb_ref, zc_ref, g_ref, w_ref, x_ref, mod_ref, g2_ref,
                    *rest, d_lru, d_band, last):
    if last:
        yo_ref, y_ref = rest
    else:
        mod2_ref, xo_ref, h_ref, y_ref = rest

    def branch(o_ref, z_ref, lo, hi):
        o = o_ref[...]
        z = z_ref[...]
        ms = jnp.mean(o * o, axis=-1, keepdims=True)
        y = o * lax.rsqrt(ms + EPS) * g_ref[:, lo:hi]
        y_ref[:, lo:hi] = (y * (z * jax.nn.sigmoid(z))).astype(BF16)

    d_mix = y_ref.shape[1]
    branch(oa_ref, za_ref, 0, d_lru)
    branch(ob_ref, zb_ref, d_lru, d_lru + d_band)
    branch(oc_ref, zc_ref, d_lru + d_band, d_mix)
    out = jnp.dot(y_ref[...], w_ref[...], preferred_element_type=F32)
    n_seq = mod_ref.shape[0]
    t = out.shape[0] // n_seq
    for s in range(n_seq):
        rows = slice(s * t, (s + 1) * t)
        x_new = x_ref[rows, :] + mod_ref[s, 2:3, :] * out[rows, :]
        ms = jnp.mean(x_new * x_new, axis=-1, keepdims=True)
        normed = x_new * lax.rsqrt(ms + EPS) * g2_ref[...]
        if last:
            yo_ref[rows, :] = normed
        else:
            xo_ref[rows, :] = x_new
            h_ref[rows, :] = (normed * (1.0 + mod2_ref[s, 1:2, :]) + mod2_ref[s, 0:1, :]).astype(BF16)


def outproj(oa, ob, oc, u, g_branch, w_out, layer, x, mod3, g2, mod3_next, cols, nseq, t, tm):
    rows, d = x.shape
    d_lru, d_band, d_dsa = oa.shape[1], ob.shape[1], oc.shape[1]
    d_mix = d_lru + d_band + d_dsa
    nblk = max(t // tm, 1)
    seqs_per_blk = max(tm // t, 1)
    last = mod3_next is None
    rb = lambda w, o=0: pl.BlockSpec((tm, w), lambda b, i, o=o: (b * nblk + i, o))
    vec = pl.BlockSpec((1, d), lambda b, i: (0, 0))
    mod_spec = pl.BlockSpec((seqs_per_blk, 3, d), lambda b, i: (b, 0, 0))
    in_specs = [rb(d_lru), rb(d_band), rb(d_dsa),
                rb(d_lru, cols["za"] // d_lru), rb(d_band, cols["zb"] // d_band), rb(d_dsa, cols["zc"] // d_dsa),
                pl.BlockSpec((1, d_mix), lambda b, i: (0, 0)),
                pl.BlockSpec((None, d_mix, d), lambda b, i: (layer, 0, 0), pipeline_mode=pl.Buffered(1)),
                rb(d), mod_spec, vec]
    args = [oa, ob, oc, u, u, u, g_branch, w_out, x, mod3, g2]
    if last:
        out_specs = rb(d)
        out_shape = jax.ShapeDtypeStruct((rows, d), F32)
    else:
        in_specs.append(mod_spec)
        args.append(mod3_next)
        out_specs = [rb(d), rb(d)]
        out_shape = [jax.ShapeDtypeStruct((rows, d), F32), jax.ShapeDtypeStruct((rows, d), BF16)]
    return pl.pallas_call(
        functools.partial(_outproj_kernel, d_lru=d_lru, d_band=d_band, last=last),
        grid=(nseq // seqs_per_blk, nblk),
        in_specs=in_specs,
        out_specs=out_specs,
        out_shape=out_shape,
        scratch_shapes=[pltpu.VMEM((tm, d_mix), BF16)],
        compiler_params=_params(2, LARGE_VMEM_LIMIT),
        name="outproj",
    )(*args)


def _column_offsets(d_lru, d_band, d_dsa):
    names = ["xa", "za", "qb", "kb", "vb", "zb", "qc", "kc", "vc", "zc", "qi"]
    widths = [d_lru, d_lru, d_band, d_band, d_band, d_band, d_dsa, d_dsa, d_dsa, d_dsa, N_IDX_HEADS * IDX_DIM]
    cols, off = {}, 0
    for n, w in zip(names, widths):
        cols[n] = off
        off += w
    cols["main"] = off
    cols["d_lru"], cols["d_band"], cols["d_dsa"] = d_lru, d_band, d_dsa
    return cols


def _layer_stream(x, h, mod3, mod3_next, g_next, lw, layer, depth, cols, nseq, t, tables, state, strip, stacks,
                  prompt):
    d_lru, d_band, d_dsa = cols["d_lru"], cols["d_band"], cols["d_dsa"]
    n_band, n_dsa = d_band // HEAD_DIM, d_dsa // HEAD_DIM
    rows = nseq * t
    tb = 256 if t % 256 == 0 else t
    tm = 1024 if rows % 1024 == 0 else rows
    u = matmul(h, lw["w_in"], layer, cols["main"], tm, 1024)
    tail = matmul(h, lw["w_tail"], layer, LANES, tm, LANES)

    q_bf, k_bf, qi_pad, t_rot, ki_bf, k_stack, v_stack, *maybe_vt = rope_call(
        u, tail, tables, cols, 256 if rows % 256 == 0 else tb, prompt, layer, depth, stacks)

    conv_s, lru_s = state[0], state[1]
    cs8 = jnp.concatenate([jnp.zeros((nseq, 8 - (CONV_W - 1), d_lru), F32), conv_s], axis=1)
    hs = lru_call(u, cs8, lru_s.reshape(nseq, 1, d_lru), lw["conv_w"], lw["conv_b"], lw["w_rg"], lw["b_rg"],
                  lw["w_ig"], lw["b_ig"], lw["lam"], nseq, t, tb)

    if prompt:
        ob = band_prompt(u, strip, cols, nseq, t)
        oc = dsa_prompt(qi_pad, t_rot, q_bf, ki_bf, k_bf, maybe_vt[0], nseq, t, 128, 512, min(TOPK_MAX, t // 4))
    else:
        bk, bv, dk, dv, dik = state[2:]
        past = dik.shape[3]
        assert (past + t - 1) // CHUNK <= past // CHUNK and t == CHUNK and bk.shape[2] == BAND_PAST * n_band
        ob = band_sample(u, bk, bv, layer, strip, cols, nseq, t)
        oc = dsa_sample(qi_pad, t_rot, q_bf, ki_bf, k_bf, u, cols["vc"] // d_dsa, dik, dk, dv, layer,
                        nseq, t, 512, min(TOPK_MAX, (past + t) // 4))

    res = outproj(hs, ob, oc, u, lw["g_branch"], lw["w_out"], layer, x, mod3, g_next, mod3_next, cols, nseq, t,
                  512 if t % 512 == 0 else (256 if rows % 256 == 0 else tb))
    x_new, h_next = (res, None) if mod3_next is None else res

    u3 = u.reshape(nseq, t, -1)
    xa = u3[:, :, cols["xa"]:cols["xa"] + d_lru]
    if prompt:
        new_conv = xa[:, t - (CONV_W - 1):]
    else:
        new_conv = jnp.concatenate([conv_s, xa], axis=1)[:, -(CONV_W - 1):]
    nbr = min(BAND_PAST, t)
    new = (new_conv,
           hs.reshape(nseq, t, d_lru)[:, -1],
           u3[:, t - nbr:, cols["kb"]:cols["kb"] + d_band].reshape(nseq, nbr, n_band, HEAD_DIM),
           u3[:, t - nbr:, cols["vb"]:cols["vb"] + d_band].reshape(nseq, nbr, n_band, HEAD_DIM),
           None,
           None,
           t_rot.reshape(nseq, t, LANES)[:, :, :IDX_DIM])
    return x_new, h_next, new, (k_stack, v_stack)


def kernel(x_prompt, x_sample, c_prompt, c_sample, state_conv, state_lru, cache_band_k, cache_band_v,
           cache_dsa_k, cache_dsa_v, cache_dsa_idx_k, g_norm, w_ada, b_ada, w_in, conv_w, conv_b,
           w_rg, b_rg, w_ig, b_ig, lru_lambda, rel_bias, g_branch, w_out, g_final):
    depth = w_in.shape[0]
    nb_p, t_p, d = x_prompt.shape
    nb_s, t_s, _ = x_sample.shape
    past = cache_dsa_k.shape[2]
    d_lru = conv_w.shape[2]
    d_band = cache_band_k.shape[3] * HEAD_DIM
    d_dsa = cache_dsa_k.shape[3] * HEAD_DIM
    cols = _column_offsets(d_lru, d_band, d_dsa)
    n_main = cols["main"]

    mod = ada_all(jnp.concatenate([c_prompt, c_sample], axis=0), w_ada, b_ada)
    mod = mod.reshape(depth, nb_p + nb_s, 3, d)

    w_tail = jnp.pad(w_in[:, :, n_main:], ((0, 0), (0, 0), (0, LANES - (w_in.shape[2] - n_main)))).astype(BF16)
    w_out_bf = w_out.astype(BF16)
    w_rg_bf = w_rg.astype(BF16)
    w_ig_bf = w_ig.astype(BF16)

    band_k = cache_band_k.reshape(depth, nb_s, -1, HEAD_DIM)
    band_v = cache_band_v.reshape(depth, nb_s, -1, HEAD_DIM)
    dsa_k = cache_dsa_k.reshape(depth, nb_s, -1, HEAD_DIM)
    dsa_v = cache_dsa_v.reshape(depth, nb_s, -1, HEAD_DIM)
    idx_k_t = jnp.swapaxes(cache_dsa_idx_k, 2, 3)
    w_in_t = jnp.swapaxes(w_in, 1, 2)

    tab_p = rope_tables(jnp.tile(jnp.arange(t_p), nb_p))
    tab_s = rope_tables(jnp.tile(past + jnp.arange(t_s), nb_s))

    xp = x_prompt.reshape(nb_p * t_p, d)
    xs = x_sample.reshape(nb_s * t_s, d)
    zero_state = (jnp.zeros((nb_p, CONV_W - 1, d_lru), F32), jnp.zeros((nb_p, d_lru), F32))
    p_new, s_new = [], []
    p_stacks = s_stacks = None
    hp = normmod(xp, g_norm[0][None], mod[0, :nb_p], nb_p, t_p, 256 if t_p % 256 == 0 else t_p)
    hsm = normmod(xs, g_norm[0][None], mod[0, nb_p:], nb_s, t_s, 256 if t_s % 256 == 0 else t_s)
    for l in range(depth):
        lw = dict(w_in=w_in_t, w_tail=w_tail, conv_w=conv_w[l], conv_b=conv_b[l][None],
                  w_rg=w_rg_bf[l], b_rg=b_rg[l][None], w_ig=w_ig_bf[l], b_ig=b_ig[l][None],
                  lam=lru_lambda[l][None], g_branch=g_branch[l][None], w_out=w_out_bf)
        last = l == depth - 1
        g_next = g_final[None] if last else g_norm[l + 1][None]
        modp_next = None if last else mod[l + 1, :nb_p]
        mods_next = None if last else mod[l + 1, nb_p:]
        strip = bias_strip(rel_bias[l])
        xp, hp, pn, p_stacks = _layer_stream(xp, hp, mod[l, :nb_p], modp_next, g_next, lw, l, depth, cols, nb_p, t_p,
                                             tab_p, zero_state, strip, p_stacks, True)
        st = (state_conv[l], state_lru[l], band_k, band_v, dsa_k, dsa_v, idx_k_t)
        xs, hsm, sn, s_stacks = _layer_stream(xs, hsm, mod[l, nb_p:], mods_next, g_next, lw, l, depth, cols, nb_s,
                                              t_s, tab_s, st, strip, s_stacks, False)
        p_new.append(pn)
        s_new.append(sn)

    y_prompt = xp.reshape(nb_p, t_p, d)
    y_sample = xs.reshape(nb_s, t_s, d)
    n_dsa = d_dsa // HEAD_DIM

    def outputs(new, stacks, nseq, t):
        out = [None if new[0][j] is None else jnp.stack([layer_new[j] for layer_new in new], axis=0)
               for j in range(7)]
        out[4], out[5] = (s.reshape(depth, nseq, t, n_dsa, HEAD_DIM) for s in stacks)
        return out

    return (y_prompt, y_sample, *outputs(p_new, p_stacks, nb_p, t_p), *outputs(s_new, s_stacks, nb_s, t_s))
```

```python
import functools
import struct

import jax
import jax.numpy as jnp
from jax import lax
from jax.experimental import pallas as pl
from jax.experimental.pallas import tpu as pltpu

F32 = jnp.float32
BF16 = jnp.bfloat16
I32 = jnp.int32

CHUNK = 64
HEAD_DIM = 128
LRU_BLOCKS = 8
CONV_W = 4
LRU_C = 8.0
BAND_PAST_CHUNKS = 8
BAND_PAST = BAND_PAST_CHUNKS * CHUNK
BAND_KEYS = BAND_PAST + CHUNK
REL_CLIP = 256
N_IDX_HEADS = 16
IDX_DIM = 64
TOPK_MAX = 256
ROPE_THETA = 500000.0
ROPE_FRAC = 4
EPS = 1e-6
NEG = -1e30
LANES = 128
SUBLANES = 8

INT_MIN = -(2 ** 31)


def _sortable_key_of(x):
    b = struct.unpack("<i", struct.pack("<f", x))[0]
    return b ^ 0x7FFFFFFF if b < 0 else b


KEY_HALF_NEG = _sortable_key_of(NEG * 0.5)

VMEM_LIMIT = 48 * 1024 * 1024
LARGE_VMEM_LIMIT = 56 * 1024 * 1024
NT_DIMS = (((1,), (1,)), ((), ()))


def _params(n_grid, vmem_limit=VMEM_LIMIT):
    return pltpu.CompilerParams(dimension_semantics=("arbitrary",) * n_grid,
                                vmem_limit_bytes=vmem_limit)


def _dot_nt(a, b):
    return lax.dot_general(a, b, NT_DIMS, preferred_element_type=F32)


def _ada_kernel(c_ref, w_ref, b_ref, o_ref):
    c = c_ref[...]
    s = (c * jax.nn.sigmoid(c)).astype(BF16)
    o_ref[...] = jnp.dot(s, w_ref[...].astype(BF16), preferred_element_type=F32) + b_ref[...]


def ada_all(c_all, w_ada, b_ada):
    depth, d, n = w_ada.shape
    nb = c_all.shape[0]
    tn = 512
    return pl.pallas_call(
        _ada_kernel,
        grid=(depth, n // tn),
        in_specs=[pl.BlockSpec((nb, d), lambda l, j: (0, 0)),
                  pl.BlockSpec((None, d, tn), lambda l, j: (l, 0, j)),
                  pl.BlockSpec((None, 1, tn), lambda l, j: (l, 0, j))],
        out_specs=pl.BlockSpec((None, nb, tn), lambda l, j: (l, 0, j)),
        out_shape=jax.ShapeDtypeStruct((depth, nb, n), F32),
        compiler_params=_params(2),
        name="ada",
    )(c_all, w_ada, b_ada.reshape(depth, 1, n))


def _normmod_kernel(x_ref, g_ref, mod_ref, h_ref):
    x = x_ref[...]
    ms = jnp.mean(x * x, axis=-1, keepdims=True)
    y = x * lax.rsqrt(ms + EPS) * g_ref[...]
    shift = mod_ref[0:1, :]
    scale = mod_ref[1:2, :]
    h_ref[...] = (y * (1.0 + scale) + shift).astype(BF16)


def normmod(x, g, mod3, nseq, t, tb):
    rows, d = x.shape
    nblk = t // tb
    return pl.pallas_call(
        _normmod_kernel,
        grid=(nseq, nblk),
        in_specs=[pl.BlockSpec((tb, d), lambda b, i: (b * nblk + i, 0)),
                  pl.BlockSpec((1, d), lambda b, i: (0, 0)),
                  pl.BlockSpec((None, 3, d), lambda b, i: (b, 0, 0))],
        out_specs=pl.BlockSpec((tb, d), lambda b, i: (b * nblk + i, 0)),
        out_shape=jax.ShapeDtypeStruct((rows, d), BF16),
        compiler_params=_params(2),
        name="normmod",
    )(x, g, mod3)


def _mm_kernel(a_ref, b_ref, o_ref, *scratch):
    if scratch:
        wb_ref, = scratch

        @pl.when(pl.program_id(1) == 0)
        def _():
            wb_ref[...] = b_ref[...].T.astype(BF16)

        w = wb_ref[...]
    else:
        w = b_ref[...]
    o_ref[...] = jnp.dot(a_ref[...], w, preferred_element_type=F32)


def matmul(a, w, layer, ncols, tm, tn):
    m, k = a.shape
    if w.dtype == BF16:
        scratch = []
        w_spec = pl.BlockSpec((None, k, tn), lambda j, i: (layer, 0, j))
    else:
        scratch = [pltpu.VMEM((k, tn), BF16)]
        w_spec = pl.BlockSpec((None, tn, k), lambda j, i: (layer, j, 0))
    return pl.pallas_call(
        _mm_kernel,
        grid=(ncols // tn, m // tm),
        in_specs=[pl.BlockSpec((tm, k), lambda j, i: (i, 0)), w_spec],
        out_specs=pl.BlockSpec((tm, tn), lambda j, i: (i, j)),
        out_shape=jax.ShapeDtypeStruct((m, ncols), F32),
        scratch_shapes=scratch,
        compiler_params=_params(2),
        name="inproj",
    )(a, w)


def _rope_kernel(qc_ref, kc_ref, vc_ref, qi_ref, t_ref, c128_ref, s128_ref, c64_ref, s64_ref, *rest,
                 n_dsa, n_idx, n_aliased):
    q_out, kbf_out, qi_out, trot_out, kibf_out, kst_out, vst_out, *maybe_vt_out = rest[n_aliased:]
    tm = qc_ref.shape[0]
    lane = lax.broadcasted_iota(I32, (tm, LANES), 1)
    c128 = c128_ref[...]
    s128 = s128_ref[...]
    c64 = c64_ref[...]
    s64 = s64_ref[...]
    half128 = HEAD_DIM // ROPE_FRAC // 2
    half64 = IDX_DIM // ROPE_FRAC // 2

    def rope128(x):
        partner = jnp.where(lane < half128, pltpu.roll(x, LANES - half128, 1), pltpu.roll(x, half128, 1))
        return x * c128 + partner * s128

    def rope64(x):
        partner = jnp.where((lane & (IDX_DIM - 1)) < half64,
                            pltpu.roll(x, LANES - half64, 1), pltpu.roll(x, half64, 1))
        return x * c64 + partner * s64

    for h in range(n_dsa):
        sl = slice(h * LANES, (h + 1) * LANES)
        q_out[:, sl] = rope128(qc_ref[:, sl]).astype(BF16)
        kr = rope128(kc_ref[:, sl])
        kbf_out[:, sl] = kr.astype(BF16)
        kst_out[pl.ds(h, tm, stride=n_dsa), :] = kr
        vst_out[pl.ds(h, tm, stride=n_dsa), :] = vc_ref[:, sl]
    if maybe_vt_out:
        maybe_vt_out[0][...] = vc_ref[...].T.astype(BF16)
    low = lane < IDX_DIM
    for j in range(n_idx // 2):
        r = rope64(qi_ref[:, j * LANES:(j + 1) * LANES]) * (IDX_DIM ** -0.5)
        qi_out[:, (2 * j) * LANES:(2 * j + 1) * LANES] = jnp.where(low, r, 0.0).astype(BF16)
        qi_out[:, (2 * j + 1) * LANES:(2 * j + 2) * LANES] = jnp.where(low, 0.0, r).astype(BF16)
    t = t_ref[...]
    r = rope64(t)
    trot_out[...] = jnp.where(low, r, jnp.where(lane < IDX_DIM + N_IDX_HEADS, t * (N_IDX_HEADS ** -0.5), 0.0))
    kibf_out[...] = jnp.where(low, r, pltpu.roll(r, IDX_DIM, 1)).astype(BF16)


def rope_call(u, tail, tables, cols, tm, v_t, layer, depth, stacks):
    rows = u.shape[0]
    d_dsa = cols["d_dsa"]
    n_dsa = d_dsa // HEAD_DIM
    d_qi = N_IDX_HEADS * IDX_DIM
    c128, s128, c64, s64 = tables
    tab_spec = pl.BlockSpec((tm, LANES), lambda i: (i, 0))
    blk = lambda off: pl.BlockSpec((tm, d_dsa), lambda i, o=off // d_dsa: (i, o))
    row_spec = pl.BlockSpec((tm, d_dsa), lambda i: (i, 0))
    stack_spec = pl.BlockSpec((None, tm * n_dsa, HEAD_DIM), lambda i: (layer, i, 0))
    stack_shape = jax.ShapeDtypeStruct((depth, rows * n_dsa, HEAD_DIM), F32)
    out_specs = [row_spec, row_spec, pl.BlockSpec((tm, N_IDX_HEADS * LANES), lambda i: (i, 0)),
                 tab_spec, tab_spec, stack_spec, stack_spec]
    out_shape = [jax.ShapeDtypeStruct((rows, d_dsa), BF16),
                 jax.ShapeDtypeStruct((rows, d_dsa), BF16),
                 jax.ShapeDtypeStruct((rows, N_IDX_HEADS * LANES), BF16),
                 jax.ShapeDtypeStruct((rows, LANES), F32),
                 jax.ShapeDtypeStruct((rows, LANES), BF16),
                 stack_shape, stack_shape]
    if v_t:
        out_specs.append(pl.BlockSpec((d_dsa, tm), lambda i: (0, i)))
        out_shape.append(jax.ShapeDtypeStruct((d_dsa, rows), BF16))
    in_specs = [blk(cols["qc"]), blk(cols["kc"]), blk(cols["vc"]),
                pl.BlockSpec((tm, d_qi), lambda i, o=cols["qi"] // d_qi: (i, o)),
                tab_spec, tab_spec, tab_spec, tab_spec, tab_spec]
    args = [u, u, u, u, tail, c128, s128, c64, s64]
    aliases = {}
    if stacks is not None:
        aliases = {len(args): 5, len(args) + 1: 6}
        in_specs += [pl.BlockSpec(memory_space=pl.ANY)] * 2
        args += list(stacks)
    return pl.pallas_call(
        functools.partial(_rope_kernel, n_dsa=n_dsa, n_idx=N_IDX_HEADS, n_aliased=len(aliases)),
        grid=(rows // tm,),
        in_specs=in_specs,
        out_specs=out_specs,
        out_shape=out_shape,
        input_output_aliases=aliases,
        compiler_params=_params(1),
        name="rope",
    )(*args)


def rope_tables(pos):
    pos = pos.astype(F32)[:, None]
    n = pos.shape[0]

    def tab(dim):
        half = dim // ROPE_FRAC // 2
        inv = ROPE_THETA ** (-jnp.arange(half, dtype=F32) / half)
        ang = pos * inv[None]
        cos, sin = jnp.cos(ang), jnp.sin(ang)
        c = jnp.concatenate([cos, cos, jnp.ones((n, dim - 2 * half), F32)], axis=1)
        s = jnp.concatenate([-sin, sin, jnp.zeros((n, dim - 2 * half), F32)], axis=1)
        return jnp.tile(c, (1, LANES // dim)), jnp.tile(s, (1, LANES // dim))

    c128, s128 = tab(HEAD_DIM)
    c64, s64 = tab(IDX_DIM)
    return c128, s128, c64, s64


def _lru_kernel(xa_ref, cs_ref, h0_ref, cw_ref, cb_ref, wrg_ref, brg_ref, wig_ref, big_ref, lam_ref,
                hs_ref, ext_ref, hc_ref):
    tb, d = xa_ref.shape
    blk = d // LRU_BLOCKS

    @pl.when(pl.program_id(1) == 0)
    def _():
        ext_ref[0:8, :] = cs_ref[...]
        hc_ref[...] = h0_ref[...]

    ext_ref[8:8 + tb, :] = xa_ref[...]
    conv = ext_ref[5:5 + tb, :] * cw_ref[0:1, :]
    for j in range(1, CONV_W):
        conv = conv + ext_ref[5 + j:5 + j + tb, :] * cw_ref[j:j + 1, :]
    conv = conv + cb_ref[...]
    tail = ext_ref[tb:tb + 8, :]
    ext_ref[0:8, :] = tail

    xb = conv.astype(BF16)
    rs, gs = [], []
    for g in range(LRU_BLOCKS):
        xg = xb[:, g * blk:(g + 1) * blk]
        rs.append(jnp.dot(xg, wrg_ref[g], preferred_element_type=F32))
        gs.append(jnp.dot(xg, wig_ref[g], preferred_element_type=F32))
    r = jax.nn.sigmoid(jnp.concatenate(rs, axis=1) + brg_ref[...])
    ig = jax.nn.sigmoid(jnp.concatenate(gs, axis=1) + big_ref[...])
    lam = lam_ref[...]
    softplus_neg_lam = jnp.maximum(-lam, 0.0) + jnp.log1p(jnp.exp(-jnp.abs(lam)))
    log_a = (-LRU_C) * r * softplus_neg_lam
    a = jnp.exp(log_a)
    u = jnp.sqrt(-jnp.tanh(log_a) * (a * a + 1.0)) * (ig * conv)

    n_groups = tb // SUBLANES
    a = a.reshape(n_groups, SUBLANES, d)
    u = u.reshape(n_groups, SUBLANES, d)
    row_in_group = lax.broadcasted_iota(I32, (n_groups, SUBLANES, d), 1)
    s = 1
    while s < SUBLANES:
        keep = row_in_group >= s
        a_prev = jnp.where(keep, pltpu.roll(a, s, 1), 1.0)
        u_prev = jnp.where(keep, pltpu.roll(u, s, 1), 0.0)
        u = a * u_prev + u
        a = a * a_prev
        s *= 2
    h_in = hc_ref[...]
    for g in range(n_groups):
        h_group = a[g] * h_in + u[g]
        hs_ref[g * SUBLANES:(g + 1) * SUBLANES, :] = h_group
        h_in = h_group[SUBLANES - 1:SUBLANES, :]
    hc_ref[...] = h_in


def lru_call(u, cs8, h0, cw, cb, wrg, brg, wig, big, lam, nseq, t, tb):
    rows = u.shape[0]
    d = cw.shape[1]
    blk = d // LRU_BLOCKS
    nblk = t // tb
    vec = pl.BlockSpec((1, d), lambda b, i: (0, 0))
    wspec = pl.BlockSpec((LRU_BLOCKS, blk, blk), lambda b, i: (0, 0, 0))
    return pl.pallas_call(
        _lru_kernel,
        grid=(nseq, nblk),
        in_specs=[pl.BlockSpec((tb, d), lambda b, i: (b * nblk + i, 0)),
                  pl.BlockSpec((None, 8, d), lambda b, i: (b, 0, 0)),
                  pl.BlockSpec((None, 1, d), lambda b, i: (b, 0, 0)),
                  pl.BlockSpec((CONV_W, d), lambda b, i: (0, 0)),
                  vec, wspec, vec, wspec, vec, vec],
        out_specs=pl.BlockSpec((tb, d), lambda b, i: (b * nblk + i, 0)),
        out_shape=jax.ShapeDtypeStruct((rows, d), F32),
        scratch_shapes=[pltpu.VMEM((tb + 8, d), F32), pltpu.VMEM((1, d), F32)],
        compiler_params=_params(2),
        name="lru",
    )(u, cs8, h0, cw, cb, wrg, brg, wig, big, lam)


def _bias_kernel(relb_ref, o_ref):
    h = pl.program_id(0)
    qi = lax.broadcasted_iota(I32, (CHUNK, BAND_KEYS), 0)
    kj = lax.broadcasted_iota(I32, (CHUNK, BAND_KEYS), 1)
    idx = jnp.clip(qi + BAND_PAST - kj, -REL_CLIP, REL_CLIP) + REL_CLIP

    def body(r, acc):
        return jnp.where(idx == r, relb_ref[h, r], acc)

    lowest = max(-(CHUNK - 1), -REL_CLIP) + REL_CLIP
    highest = min(BAND_PAST + CHUNK - 1, REL_CLIP) + REL_CLIP
    o_ref[...] = lax.fori_loop(lowest, highest + 1, body, jnp.zeros((CHUNK, BAND_KEYS), F32))


def bias_strip(relb):
    nh = relb.shape[0]
    return pl.pallas_call(
        _bias_kernel,
        grid=(nh,),
        in_specs=[pl.BlockSpec(memory_space=pltpu.SMEM)],
        out_specs=pl.BlockSpec((None, CHUNK, BAND_KEYS), lambda h: (h, 0, 0)),
        out_shape=jax.ShapeDtypeStruct((nh, CHUNK, BAND_KEYS), F32),
        compiler_params=_params(1),
        name="bias_strip",
    )(relb)


def _band_prompt_kernel(q_ref, kp_ref, kc_ref, vp_ref, vc_ref, strip_ref, o_ref, kw_ref, vw_ref):
    i = pl.program_id(2)
    tb = q_ref.shape[0]
    kw_ref[0:tb, :] = kp_ref[...].astype(BF16)
    kw_ref[tb:2 * tb, :] = kc_ref[...].astype(BF16)
    vw_ref[0:tb, :] = vp_ref[...].astype(BF16)
    vw_ref[tb:2 * tb, :] = vc_ref[...].astype(BF16)
    strip = strip_ref[...]
    col = lax.broadcasted_iota(I32, (CHUNK, BAND_KEYS), 1)
    chunks = range(tb // CHUNK)
    scores = []
    for a in chunks:
        q = q_ref[a * CHUNK:(a + 1) * CHUNK, :].astype(BF16)
        s = _dot_nt(q, kw_ref[a * CHUNK:a * CHUNK + BAND_KEYS, :]) * (HEAD_DIM ** -0.5) + strip
        min_col = jnp.where(i > 0, 0, (BAND_PAST_CHUNKS - a) * CHUNK)
        scores.append(jnp.where(col >= min_col, s, NEG))
    maxes = [jnp.max(s, axis=-1, keepdims=True) for s in scores]
    probs = [jnp.exp(s - m) for s, m in zip(scores, maxes)]
    sums = [jnp.sum(p, axis=-1, keepdims=True) for p in probs]
    for a in chunks:
        pv = jnp.dot(probs[a].astype(BF16), vw_ref[a * CHUNK:a * CHUNK + BAND_KEYS, :],
                     preferred_element_type=F32)
        o_ref[a * CHUNK:(a + 1) * CHUNK, :] = pv / sums[a]


def band_prompt(u, strip, cols, nseq, t):
    rows = u.shape[0]
    tb = BAND_PAST
    nblk = t // tb
    nh = strip.shape[0]
    qo, ko, vo = cols["qb"] // HEAD_DIM, cols["kb"] // HEAD_DIM, cols["vb"] // HEAD_DIM
    cur = lambda off: pl.BlockSpec((tb, HEAD_DIM), lambda b, h, i: (b * nblk + i, off + h))
    prev = lambda off: pl.BlockSpec((tb, HEAD_DIM), lambda b, h, i: (b * nblk + jnp.maximum(i - 1, 0), off + h))
    return pl.pallas_call(
        _band_prompt_kernel,
        grid=(nseq, nh, nblk),
        in_specs=[cur(qo), prev(ko), cur(ko), prev(vo), cur(vo),
                  pl.BlockSpec((None, CHUNK, BAND_KEYS), lambda b, h, i: (h, 0, 0))],
        out_specs=pl.BlockSpec((tb, HEAD_DIM), lambda b, h, i: (b * nblk + i, h)),
        out_shape=jax.ShapeDtypeStruct((rows, nh * HEAD_DIM), F32),
        scratch_shapes=[pltpu.VMEM((2 * tb, HEAD_DIM), BF16), pltpu.VMEM((2 * tb, HEAD_DIM), BF16)],
        compiler_params=_params(3),
        name="band_prompt",
    )(u, u, u, u, u, strip)


def _band_sample_kernel(q_ref, kn_ref, vn_ref, ck_ref, cv_ref, strip_ref, o_ref, kw_ref, vw_ref):
    nh = strip_ref.shape[0]
    w = ck_ref.shape[0] // nh
    t = q_ref.shape[0]
    kw_ref[w:w + t, :] = kn_ref[...].astype(BF16)
    vw_ref[w:w + t, :] = vn_ref[...].astype(BF16)
    heads = [slice(h * HEAD_DIM, (h + 1) * HEAD_DIM) for h in range(nh)]
    for h, sl in enumerate(heads):
        kw_ref[0:w, sl] = ck_ref[pl.ds(h, w, stride=nh), :].astype(BF16)
        vw_ref[0:w, sl] = cv_ref[pl.ds(h, w, stride=nh), :].astype(BF16)
    scores = [_dot_nt(q_ref[:, sl].astype(BF16), kw_ref[:, sl]) * (HEAD_DIM ** -0.5) + strip_ref[h]
              for h, sl in enumerate(heads)]
    maxes = [jnp.max(s, axis=-1, keepdims=True) for s in scores]
    probs = [jnp.exp(s - m) for s, m in zip(scores, maxes)]
    sums = [jnp.sum(p, axis=-1, keepdims=True) for p in probs]
    for h, sl in enumerate(heads):
        pv = jnp.dot(probs[h].astype(BF16), vw_ref[:, sl], preferred_element_type=F32)
        o_ref[:, sl] = pv / sums[h]


def band_sample(u, ck, cv, layer, strip, cols, nseq, t):
    rows = u.shape[0]
    nh = strip.shape[0]
    d = nh * HEAD_DIM
    w = ck.shape[2] // nh
    ublk = lambda off: pl.BlockSpec((t, d), lambda b, o=off // d: (b, o))
    cblk = pl.BlockSpec((None, None, w * nh, HEAD_DIM), lambda b: (layer, b, 0, 0))
    return pl.pallas_call(
        _band_sample_kernel,
        grid=(nseq,),
        in_specs=[ublk(cols["qb"]), ublk(cols["kb"]), ublk(cols["vb"]), cblk, cblk,
                  pl.BlockSpec((nh, CHUNK, BAND_KEYS), lambda b: (0, 0, 0))],
        out_specs=pl.BlockSpec((t, d), lambda b: (b, 0)),
        out_shape=jax.ShapeDtypeStruct((rows, d), F32),
        scratch_shapes=[pltpu.VMEM((w + t, d), BF16), pltpu.VMEM((w + t, d), BF16)],
        compiler_params=_params(1),
        name="band_sample",
    )(u, u, u, ck, cv, strip)


def _loop(n, body, init):
    if isinstance(n, int):
        val = init
        for k in range(n):
            val = body(k, val)
        return val
    return lax.fori_loop(0, n, body, init)


IDX_BITS = 16


def _f32_to_key(x):
    bits = lax.bitcast_convert_type(x, I32)
    return jnp.where(bits < 0, bits ^ 0x7FFFFFFF, bits)


def _dsa_core(qside, segs, store_out, sk_refs, s_refs, jstar_ref, mpart_ref, lpart_ref, acc_ref, *,
              topk, q_chunk0, n_q, n_dsa):
    qb = qside["t"].shape[0]
    t_t = qside["t"].T
    wrows = [t_t[IDX_DIM + h:IDX_DIM + h + 1, :] for h in range(N_IDX_HEADS)]

    def tiles(x):
        return x.reshape(x.shape[0] // SUBLANES, SUBLANES, qb)

    w8 = [jnp.broadcast_to(w, (SUBLANES, qb)) for w in wrows]
    qcol8 = lax.broadcasted_iota(I32, (SUBLANES, qb), 1)
    q_end8 = (q_chunk0 + qcol8 // CHUNK + 1) * CHUNK
    for seg, sk_ref in zip(segs, sk_refs):
        kblk = seg["kblk"]
        n_t = kblk // SUBLANES
        krow3 = (lax.broadcasted_iota(I32, (n_t, SUBLANES, qb), 0) * SUBLANES
                 + lax.broadcasted_iota(I32, (n_t, SUBLANES, qb), 1))
        padded = seg["valid"] < kblk or n_q < qb
        live3 = (krow3 < seg["valid"]) & (lax.broadcasted_iota(I32, (n_t, SUBLANES, qb), 2) < n_q)

        def score_blk(kb, carry, seg=seg, sk_ref=sk_ref, kblk=kblk, krow3=krow3, padded=padded, live3=live3):
            kib = seg["ki"](kb)
            acc = jnp.zeros(krow3.shape, F32)
            for group in qside["qi_groups"]:
                sc = _dot_nt(kib, qside["qi"](group))
                for g, h in enumerate(group):
                    acc = acc + jnp.maximum(tiles(sc[:, g * qb:(g + 1) * qb]), 0.0) * w8[h]
            admissible = krow3 < q_end8 - (seg["pos0"] + kb * kblk)
            key = _f32_to_key(jnp.where(admissible, acc, NEG))
            sk_ref[kb] = jnp.where(live3, key, INT_MIN) if padded else key
            return carry

        _loop(seg["nblk"], score_blk, 0)

    def count(pred):
        part = jnp.zeros((SUBLANES, qb), I32)
        for seg, sk_ref in zip(segs, sk_refs):
            n_t = seg["kblk"] // SUBLANES
            off = (lax.broadcasted_iota(I32, (n_t, SUBLANES, qb), 0) * SUBLANES
                   + lax.broadcasted_iota(I32, (n_t, SUBLANES, qb), 1))

            def body(kb, part, seg=seg, sk_ref=sk_ref, off=off):
                idx = seg["pos0"] + kb * seg["kblk"] + off
                return part + jnp.sum(jnp.where(pred(sk_ref[kb], idx), 1, 0), axis=0)

            part = _loop(seg["nblk"], body, part)
        return jnp.broadcast_to(jnp.sum(part, axis=0, keepdims=True), (SUBLANES, qb))

    zero = jnp.zeros((SUBLANES, qb), I32)
    n_stored = sum(seg["nblk"] * seg["kblk"] for seg in segs)
    c_zero = count(lambda key, idx: key >= zero)
    thr0 = jnp.where(c_zero >= topk, 0, INT_MIN).astype(I32)
    c_lo0 = jnp.where(c_zero >= topk, c_zero, zero + n_stored)

    def bit_step(bit, state):
        t, c = state
        cand = t + jnp.left_shift(jnp.int32(1), bit)
        c_cand = count(lambda key, idx: key >= cand)
        ok = c_cand >= topk
        return jnp.where(ok, cand, t), jnp.where(ok, c_cand, c)

    thr, c_lo = lax.fori_loop(0, 31, lambda i, st: bit_step(30 - i, st), (thr0, c_lo0))
    lo = jnp.maximum(thr, KEY_HALF_NEG + 1)

    big = 2 ** IDX_BITS
    jstar_ref[...] = jnp.full((SUBLANES, qb), big, I32)
    surplus = (c_lo > topk) & (thr > KEY_HALF_NEG)

    @pl.when(jnp.max(jnp.where(surplus, 1, 0)) > 0)
    def _():
        need = topk - count(lambda key, idx: key > thr)

        def idx_body(it, j):
            cand = j + jnp.left_shift(jnp.int32(1), IDX_BITS - 1 - it)
            f = count(lambda key, idx: (key == lo) & (idx < cand))
            return jnp.where(f <= need, cand, j)

        j = lax.fori_loop(0, IDX_BITS, idx_body, zero)
        jstar_ref[...] = jnp.where(surplus, j, big)

    jstar = jstar_ref[...]

    mpart_ref[...] = jnp.full(mpart_ref.shape, NEG, F32)
    lpart_ref[...] = jnp.zeros(lpart_ref.shape, F32)
    acc_ref[...] = jnp.zeros(acc_ref.shape, F32)

    for seg, sk_ref, s_ref in zip(segs, sk_refs, s_refs):
        kblk = seg["kblk"]
        n_t = kblk // SUBLANES
        krow3 = (lax.broadcasted_iota(I32, (n_t, SUBLANES, qb), 0) * SUBLANES
                 + lax.broadcasted_iota(I32, (n_t, SUBLANES, qb), 1))

        def pass_a(kb, carry, seg=seg, sk_ref=sk_ref, s_ref=s_ref, kblk=kblk, krow3=krow3):
            key = sk_ref[kb]
            mask = (key >= lo) & ((key > lo) | (krow3 < jstar - (seg["pos0"] + kb * kblk)))
            raw = [tiles(_dot_nt(seg["k"](kb, h), qside["q"](h))) for h in range(n_dsa)]
            for h in range(n_dsa):
                s = jnp.where(mask, raw[h] * (HEAD_DIM ** -0.5), NEG)
                s_ref[h, kb] = s
                mpart_ref[h] = jnp.maximum(mpart_ref[h], jnp.max(s, axis=0))
            return carry

        _loop(seg["nblk"], pass_a, 0)

    m8 = [jnp.broadcast_to(jnp.max(mpart_ref[h], axis=0, keepdims=True), (SUBLANES, qb)) for h in range(n_dsa)]

    for seg, s_ref in zip(segs, s_refs):
        def pass_b(kb, carry, seg=seg, s_ref=s_ref):
            for h in range(n_dsa):
                p = jnp.exp(s_ref[h, kb] - m8[h])
                lpart_ref[h] += jnp.sum(p, axis=0)
                acc_ref[h] += seg["pv"](kb, h, p.reshape(seg["kblk"], qb).astype(BF16))
            return carry

        _loop(seg["nblk"], pass_b, 0)

    for h in range(n_dsa):
        l = jnp.sum(lpart_ref[h], axis=0, keepdims=True)
        store_out(h, (acc_ref[h] / l).T)


def _blk_start(kb, kblk):
    return kb * kblk if isinstance(kb, int) else pl.multiple_of(kb * kblk, kblk)


def _dsa_scratch(qb, seg_shapes, n_dsa):
    return ([pltpu.VMEM((nblk, kblk // SUBLANES, SUBLANES, qb), I32) for nblk, kblk in seg_shapes]
            + [pltpu.VMEM((n_dsa, nblk, kblk // SUBLANES, SUBLANES, qb), F32) for nblk, kblk in seg_shapes]
            + [pltpu.VMEM((SUBLANES, qb), I32),
               pltpu.VMEM((n_dsa, SUBLANES, qb), F32), pltpu.VMEM((n_dsa, SUBLANES, qb), F32),
               pltpu.VMEM((n_dsa, HEAD_DIM, qb), F32)])


def _dsa_prompt_kernel(qi_ref, t_ref, q_ref, ki_ref, k_ref, vt_ref, o_ref, sk_ref, s_ref, jstar_ref,
                       mpart_ref, lpart_ref, acc_ref, *, kblk, topk):
    i = pl.program_id(1)
    qb = q_ref.shape[0]
    nblk = (i * qb + qb + kblk - 1) // kblk

    def head(h):
        return slice(h * HEAD_DIM, (h + 1) * HEAD_DIM)

    seg = dict(ki=lambda kb: ki_ref[pl.ds(_blk_start(kb, kblk), kblk), :],
               k=lambda kb, h: k_ref[pl.ds(_blk_start(kb, kblk), kblk), head(h)],
               pv=lambda kb, h, p: jnp.dot(vt_ref[head(h), pl.ds(_blk_start(kb, kblk), kblk)], p,
                                           preferred_element_type=F32),
               nblk=nblk, kblk=kblk, pos0=0, valid=kblk)
    qside = dict(t=t_ref[...],
                 qi_groups=[(h, h + 1) for h in range(0, N_IDX_HEADS, 2)],
                 qi=lambda group: jnp.concatenate([qi_ref[:, h * LANES:(h + 1) * LANES] for h in group], axis=0),
                 q=lambda h: q_ref[:, head(h)])

    def store_out(h, x):
        o_ref[:, head(h)] = x

    _dsa_core(qside, [seg], store_out, [sk_ref], [s_ref], jstar_ref, mpart_ref, lpart_ref, acc_ref,
              topk=topk, q_chunk0=(i * qb) // CHUNK, n_q=qb, n_dsa=q_ref.shape[1] // HEAD_DIM)


def dsa_prompt(qi_pad, trot, q_bf, ki_bf, k_bf, v_t, nseq, t, qb, kblk, topk):
    rows, d = q_bf.shape
    n_dsa = d // HEAD_DIM
    nq = t // qb
    row_blk = lambda w: pl.BlockSpec((qb, w), lambda b, i: (b * nq + i, 0))
    seq_blk = lambda w: pl.BlockSpec((t, w), lambda b, i: (b, 0))
    return pl.pallas_call(
        functools.partial(_dsa_prompt_kernel, kblk=kblk, topk=topk),
        grid=(nseq, nq),
        in_specs=[row_blk(qi_pad.shape[1]), row_blk(LANES), row_blk(d),
                  seq_blk(LANES), seq_blk(d), pl.BlockSpec((d, t), lambda b, i: (0, b))],
        out_specs=row_blk(d),
        out_shape=jax.ShapeDtypeStruct((rows, d), F32),
        scratch_shapes=_dsa_scratch(qb, [(t // kblk, kblk)], n_dsa),
        compiler_params=_params(2),
        name="dsa_prompt",
    )(qi_pad, trot, q_bf, ki_bf, k_bf, v_t)


def _dsa_sample_kernel(qi_ref, t_ref, q_ref, kin_ref, kn_ref, vn_ref, cki_ref, ck_ref, cv_ref, o_ref,
                       qi_s, q_s, kin_s, kn_s, vn_s, kit_s, sk0_ref, sk1_ref, s0_ref, s1_ref, jstar_ref,
                       mpart_ref, lpart_ref, acc_ref, *, kblk, topk):
    past = cki_ref.shape[2]
    tq = q_ref.shape[0] // 2
    qb = 2 * tq
    n_dsa = q_ref.shape[1] // HEAD_DIM
    pair_w = 2 * LANES

    def head(h):
        return slice(h * HEAD_DIM, (h + 1) * HEAD_DIM)

    def rows(a):
        return slice(a * tq, (a + 1) * tq)

    def half(h, a):
        return slice(h * pair_w + a * LANES, h * pair_w + (a + 1) * LANES)

    for ref in (qi_s, q_s, kin_s, kn_s):
        ref[...] = jnp.zeros(ref.shape, ref.dtype)
    for a in range(2):
        for h in range(N_IDX_HEADS):
            qi_s[rows(a), half(h, a)] = qi_ref[rows(a), h * LANES:(h + 1) * LANES]
        for h in range(n_dsa):
            q_s[rows(a), half(h, a)] = q_ref[rows(a), head(h)]
            kn_s[0:tq, half(h, a)] = kn_ref[rows(a), head(h)]
        kin_s[0:tq, half(0, a)] = kin_ref[rows(a), :]
        vn_s[a, 0:tq, :] = vn_ref[rows(a), :]
        vn_s[a, tq:, :] = jnp.zeros((qb - tq, vn_s.shape[2]), F32)

    first_seq = lax.broadcasted_iota(I32, (HEAD_DIM, qb), 1) < tq

    def pair_pv(v_a, v_b, p):
        out_a = jnp.dot(v_a.T.astype(BF16), p, preferred_element_type=F32)
        out_b = jnp.dot(v_b.T.astype(BF16), p, preferred_element_type=F32)
        return jnp.where(first_seq, out_a, out_b)

    def cache_rows(ref, a, kb, h):
        return ref[a, pl.ds(kb * kblk * n_dsa + h, kblk, stride=n_dsa), :]

    def cache_ki(kb):
        parts = []
        for a in range(2):
            kit_s[a, 0:IDX_DIM, :] = cki_ref[a, :, pl.ds(kb * kblk, kblk)]
            kit_s[a, IDX_DIM:, :] = cki_ref[a, :, pl.ds(kb * kblk, kblk)]
            parts.append(kit_s[a].T.astype(BF16))
        return jnp.concatenate(parts, axis=1)

    segs = [dict(ki=cache_ki,
                 k=lambda kb, h: jnp.concatenate([cache_rows(ck_ref, a, kb, h).astype(BF16) for a in range(2)],
                                                 axis=1),
                 pv=lambda kb, h, p: pair_pv(cache_rows(cv_ref, 0, kb, h), cache_rows(cv_ref, 1, kb, h), p),
                 nblk=past // kblk, kblk=kblk, pos0=0, valid=kblk),
            dict(ki=lambda kb: kin_s[...],
                 k=lambda kb, h: kn_s[:, h * pair_w:(h + 1) * pair_w],
                 pv=lambda kb, h, p: pair_pv(vn_s[0, :, head(h)], vn_s[1, :, head(h)], p),
                 nblk=1, kblk=qb, pos0=past, valid=tq)]
    qside = dict(t=t_ref[...],
                 qi_groups=[(h,) for h in range(N_IDX_HEADS)],
                 qi=lambda group: qi_s[:, group[0] * pair_w:(group[0] + 1) * pair_w],
                 q=lambda h: q_s[:, h * pair_w:(h + 1) * pair_w])

    def store_out(h, x):
        o_ref[:, head(h)] = x

    _dsa_core(qside, segs, store_out, [sk0_ref, sk1_ref], [s0_ref, s1_ref], jstar_ref,
              mpart_ref, lpart_ref, acc_ref, topk=topk, q_chunk0=past // CHUNK, n_q=qb, n_dsa=n_dsa)


def dsa_sample(qi_pad, trot, q_bf, ki_bf, k_bf, u, vc_col, cki, ck, cv, layer, nseq, t, kblk, topk):
    rows, d = q_bf.shape
    n_dsa = d // HEAD_DIM
    past = cki.shape[3]
    qb = 2 * t
    assert qb == LANES and nseq % 2 == 0
    row_blk = lambda w, o=0: pl.BlockSpec((qb, w), lambda b, o=o: (b, o))
    cache_blk = lambda r, w: pl.BlockSpec((None, 2, r, w), lambda b: (layer, b, 0, 0))
    pair_scratch = [pltpu.VMEM((qb, 2 * qi_pad.shape[1]), BF16), pltpu.VMEM((qb, 2 * d), BF16),
                    pltpu.VMEM((qb, 2 * LANES), BF16), pltpu.VMEM((qb, 2 * d), BF16),
                    pltpu.VMEM((2, qb, d), F32), pltpu.VMEM((2, LANES, kblk), F32)]
    return pl.pallas_call(
        functools.partial(_dsa_sample_kernel, kblk=kblk, topk=topk),
        grid=(nseq // 2,),
        in_specs=[row_blk(qi_pad.shape[1]), row_blk(LANES), row_blk(d),
                  row_blk(LANES), row_blk(d), row_blk(d, vc_col),
                  cache_blk(cki.shape[2], past), cache_blk(past * n_dsa, HEAD_DIM),
                  cache_blk(past * n_dsa, HEAD_DIM)],
        out_specs=row_blk(d),
        out_shape=jax.ShapeDtypeStruct((rows, d), F32),
        scratch_shapes=pair_scratch + _dsa_scratch(qb, [(past // kblk, kblk), (1, qb)], n_dsa),
        compiler_params=_params(1, LARGE_VMEM_LIMIT),
        name="dsa_sample",
    )(qi_pad, trot, q_bf, ki_bf, k_bf, u, cki, ck, cv)


def _outproj_kernel(oa_ref, ob_ref, oc_ref, za_ref, zb_ref, zc_ref, g_ref, w_ref, x_ref, mod_ref, g2_ref,
                    *rest, d_lru, d_band, last):
    if last:
        yo_ref, y_ref = rest
    else:
        mod2_ref, xo_ref, h_ref, y_ref = rest

    def branch(o_ref, z_ref, lo, hi):
        o = o_ref[...]
        z = z_ref[...]
        ms = jnp.mean(o * o, axis=-1, keepdims=True)
        y = o * lax.rsqrt(ms + EPS) * g_ref[:, lo:hi]
        y_ref[:, lo:hi] = (y * (z * jax.nn.sigmoid(z))).astype(BF16)

    d_mix = y_ref.shape[1]
    branch(oa_ref, za_ref, 0, d_lru)
    branch(ob_ref, zb_ref, d_lru, d_lru + d_band)
    branch(oc_ref, zc_ref, d_lru + d_band, d_mix)
    out = jnp.dot(y_ref[...], w_ref[...], preferred_element_type=F32)
    n_seq = mod_ref.shape[0]
    t = out.shape[0] // n_seq
    for s in range(n_seq):
        rows = slice(s * t, (s + 1) * t)
        x_new = x_ref[rows, :] + mod_ref[s, 2:3, :] * out[rows, :]
        ms = jnp.mean(x_new * x_new, axis=-1, keepdims=True)
        normed = x_new * lax.rsqrt(ms + EPS) * g2_ref[...]
        if last:
            yo_ref[rows, :] = normed
        else:
            xo_ref[rows, :] = x_new
            h_ref[rows, :] = (normed * (1.0 + mod2_ref[s, 1:2, :]) + mod2_ref[s, 0:1, :]).astype(BF16)


def outproj(oa, ob, oc, u, g_branch, w_out, layer, x, mod3, g2, mod3_next, cols, nseq, t, tm):
    rows, d = x.shape
    d_lru, d_band, d_dsa = oa.shape[1], ob.shape[1], oc.shape[1]
    d_mix = d_lru + d_band + d_dsa
    nblk = max(t // tm, 1)
    seqs_per_blk = max(tm // t, 1)
    last = mod3_next is None
    rb = lambda w, o=0: pl.BlockSpec((tm, w), lambda b, i, o=o: (b * nblk + i, o))
    vec = pl.BlockSpec((1, d), lambda b, i: (0, 0))
    mod_spec = pl.BlockSpec((seqs_per_blk, 3, d), lambda b, i: (b, 0, 0))
    in_specs = [rb(d_lru), rb(d_band), rb(d_dsa),
                rb(d_lru, cols["za"] // d_lru), rb(d_band, cols["zb"] // d_band), rb(d_dsa, cols["zc"] // d_dsa),
                pl.BlockSpec((1, d_mix), lambda b, i: (0, 0)),
                pl.BlockSpec((None, d_mix, d), lambda b, i: (layer, 0, 0), pipeline_mode=pl.Buffered(1)),
                rb(d), mod_spec, vec]
    args = [oa, ob, oc, u, u, u, g_branch, w_out, x, mod3, g2]
    if last:
        out_specs = rb(d)
        out_shape = jax.ShapeDtypeStruct((rows, d), F32)
    else:
        in_specs.append(mod_spec)
        args.append(mod3_next)
        out_specs = [rb(d), rb(d)]
        out_shape = [jax.ShapeDtypeStruct((rows, d), F32), jax.ShapeDtypeStruct((rows, d), BF16)]
    return pl.pallas_call(
        functools.partial(_outproj_kernel, d_lru=d_lru, d_band=d_band, last=last),
        grid=(nseq // seqs_per_blk, nblk),
        in_specs=in_specs,
        out_specs=out_specs,
        out_shape=out_shape,
        scratch_shapes=[pltpu.VMEM((tm, d_mix), BF16)],
        compiler_params=_params(2, LARGE_VMEM_LIMIT),
        name="outproj",
    )(*args)


def _column_offsets(d_lru, d_band, d_dsa):
    names = ["xa", "za", "qb", "kb", "vb", "zb", "qc", "kc", "vc", "zc", "qi"]
    widths = [d_lru, d_lru, d_band, d_band, d_band, d_band, d_dsa, d_dsa, d_dsa, d_dsa, N_IDX_HEADS * IDX_DIM]
    cols, off = {}, 0
    for n, w in zip(names, widths):
        cols[n] = off
        off += w
    cols["main"] = off
    cols["d_lru"], cols["d_band"], cols["d_dsa"] = d_lru, d_band, d_dsa
    return cols


def _layer_stream(x, h, mod3, mod3_next, g_next, lw, layer, depth, cols, nseq, t, tables, state, strip, stacks,
                  prompt):
    d_lru, d_band, d_dsa = cols["d_lru"], cols["d_band"], cols["d_dsa"]
    n_band, n_dsa = d_band // HEAD_DIM, d_dsa // HEAD_DIM
    rows = nseq * t
    tb = 256 if t % 256 == 0 else t
    tm = 1024 if rows % 1024 == 0 else rows
    u = matmul(h, lw["w_in"], layer, cols["main"], tm, 1024)
    tail = matmul(h, lw["w_tail"], layer, LANES, tm, LANES)

    q_bf, k_bf, qi_pad, t_rot, ki_bf, k_stack, v_stack, *maybe_vt = rope_call(
        u, tail, tables, cols, 256 if rows % 256 == 0 else tb, prompt, layer, depth, stacks)

    conv_s, lru_s = state[0], state[1]
    cs8 = jnp.concatenate([jnp.zeros((nseq, 8 - (CONV_W - 1), d_lru), F32), conv_s], axis=1)
    hs = lru_call(u, cs8, lru_s.reshape(nseq, 1, d_lru), lw["conv_w"], lw["conv_b"], lw["w_rg"], lw["b_rg"],
                  lw["w_ig"], lw["b_ig"], lw["lam"], nseq, t, tb)

    if prompt:
        ob = band_prompt(u, strip, cols, nseq, t)
        oc = dsa_prompt(qi_pad, t_rot, q_bf, ki_bf, k_bf, maybe_vt[0], nseq, t, 256, 512, min(TOPK_MAX, t // 4))
    else:
        bk, bv, dk, dv, dik = state[2:]
        past = dik.shape[3]
        assert (past + t - 1) // CHUNK <= past // CHUNK and t == CHUNK and bk.shape[2] == BAND_PAST * n_band
        ob = band_sample(u, bk, bv, layer, strip, cols, nseq, t)
        oc = dsa_sample(qi_pad, t_rot, q_bf, ki_bf, k_bf, u, cols["vc"] // d_dsa, dik, dk, dv, layer,
                        nseq, t, 512, min(TOPK_MAX, (past + t) // 4))

    res = outproj(hs, ob, oc, u, lw["g_branch"], lw["w_out"], layer, x, mod3, g_next, mod3_next, cols, nseq, t,
                  512 if t % 512 == 0 else (256 if rows % 256 == 0 else tb))
    x_new, h_next = (res, None) if mod3_next is None else res

    u3 = u.reshape(nseq, t, -1)
    xa = u3[:, :, cols["xa"]:cols["xa"] + d_lru]
    if prompt:
        new_conv = xa[:, t - (CONV_W - 1):]
    else:
        new_conv = jnp.concatenate([conv_s, xa], axis=1)[:, -(CONV_W - 1):]
    nbr = min(BAND_PAST, t)
    new = (new_conv,
           hs.reshape(nseq, t, d_lru)[:, -1],
           u3[:, t - nbr:, cols["kb"]:cols["kb"] + d_band].reshape(nseq, nbr, n_band, HEAD_DIM),
           u3[:, t - nbr:, cols["vb"]:cols["vb"] + d_band].reshape(nseq, nbr, n_band, HEAD_DIM),
           None,
           None,
           t_rot.reshape(nseq, t, LANES)[:, :, :IDX_DIM])
    return x_new, h_next, new, (k_stack, v_stack)


def kernel(x_prompt, x_sample, c_prompt, c_sample, state_conv, state_lru, cache_band_k, cache_band_v,
           cache_dsa_k, cache_dsa_v, cache_dsa_idx_k, g_norm, w_ada, b_ada, w_in, conv_w, conv_b,
           w_rg, b_rg, w_ig, b_ig, lru_lambda, rel_bias, g_branch, w_out, g_final):
    depth = w_in.shape[0]
    nb_p, t_p, d = x_prompt.shape
    nb_s, t_s, _ = x_sample.shape
    past = cache_dsa_k.shape[2]
    d_lru = conv_w.shape[2]
    d_band = cache_band_k.shape[3] * HEAD_DIM
    d_dsa = cache_dsa_k.shape[3] * HEAD_DIM
    cols = _column_offsets(d_lru, d_band, d_dsa)
    n_main = cols["main"]

    mod = ada_all(jnp.concatenate([c_prompt, c_sample], axis=0), w_ada, b_ada)
    mod = mod.reshape(depth, nb_p + nb_s, 3, d)

    w_tail = jnp.pad(w_in[:, :, n_main:], ((0, 0), (0, 0), (0, LANES - (w_in.shape[2] - n_main)))).astype(BF16)
    w_out_bf = w_out.astype(BF16)
    w_rg_bf = w_rg.astype(BF16)
    w_ig_bf = w_ig.astype(BF16)

    band_k = cache_band_k.reshape(depth, nb_s, -1, HEAD_DIM)
    band_v = cache_band_v.reshape(depth, nb_s, -1, HEAD_DIM)
    dsa_k = cache_dsa_k.reshape(depth, nb_s, -1, HEAD_DIM)
    dsa_v = cache_dsa_v.reshape(depth, nb_s, -1, HEAD_DIM)
    idx_k_t = jnp.swapaxes(cache_dsa_idx_k, 2, 3)
    w_in_t = jnp.swapaxes(w_in, 1, 2)

    tab_p = rope_tables(jnp.tile(jnp.arange(t_p), nb_p))
    tab_s = rope_tables(jnp.tile(past + jnp.arange(t_s), nb_s))

    xp = x_prompt.reshape(nb_p * t_p, d)
    xs = x_sample.reshape(nb_s * t_s, d)
    zero_state = (jnp.zeros((nb_p, CONV_W - 1, d_lru), F32), jnp.zeros((nb_p, d_lru), F32))
    p_new, s_new = [], []
    p_stacks = s_stacks = None
    hp = normmod(xp, g_norm[0][None], mod[0, :nb_p], nb_p, t_p, 256 if t_p % 256 == 0 else t_p)
    hsm = normmod(xs, g_norm[0][None], mod[0, nb_p:], nb_s, t_s, 256 if t_s % 256 == 0 else t_s)
    for l in range(depth):
        lw = dict(w_in=w_in_t, w_tail=w_tail, conv_w=conv_w[l], conv_b=conv_b[l][None],
                  w_rg=w_rg_bf[l], b_rg=b_rg[l][None], w_ig=w_ig_bf[l], b_ig=b_ig[l][None],
                  lam=lru_lambda[l][None], g_branch=g_branch[l][None], w_out=w_out_bf)
        last = l == depth - 1
        g_next = g_final[None] if last else g_norm[l + 1][None]
        modp_next = None if last else mod[l + 1, :nb_p]
        mods_next = None if last else mod[l + 1, nb_p:]
        strip = bias_strip(rel_bias[l])
        xp, hp, pn, p_stacks = _layer_stream(xp, hp, mod[l, :nb_p], modp_next, g_next, lw, l, depth, cols, nb_p, t_p,
                                             tab_p, zero_state, strip, p_stacks, True)
        st = (state_conv[l], state_lru[l], band_k, band_v, dsa_k, dsa_v, idx_k_t)
        xs, hsm, sn, s_stacks = _layer_stream(xs, hsm, mod[l, nb_p:], mods_next, g_next, lw, l, depth, cols, nb_s,
                                              t_s, tab_s, st, strip, s_stacks, False)
        p_new.append(pn)
        s_new.append(sn)

    y_prompt = xp.reshape(nb_p, t_p, d)
    y_sample = xs.reshape(nb_s, t_s, d)
    n_dsa = d_dsa // HEAD_DIM

    def outputs(new, stacks, nseq, t):
        out = [None if new[0][j] is None else jnp.stack([layer_new[j] for layer_new in new], axis=0)
               for j in range(7)]
        out[4], out[5] = (s.reshape(depth, nseq, t, n_dsa, HEAD_DIM) for s in stacks)
        return out

    return (y_prompt, y_sample, *outputs(p_new, p_stacks, nb_p, t_p), *outputs(s_new, s_stacks, nb_s, t_s))
```

```python
import functools
import struct

import jax
import jax.numpy as jnp
from jax import lax
from jax.experimental import pallas as pl
from jax.experimental.pallas import tpu as pltpu

F32 = jnp.float32
BF16 = jnp.bfloat16
I32 = jnp.int32

CHUNK = 64
HEAD_DIM = 128
LRU_BLOCKS = 8
CONV_W = 4
LRU_C = 8.0
BAND_PAST_CHUNKS = 8
BAND_PAST = BAND_PAST_CHUNKS * CHUNK
BAND_KEYS = BAND_PAST + CHUNK
REL_CLIP = 256
N_IDX_HEADS = 16
IDX_DIM = 64
TOPK_MAX = 256
ROPE_THETA = 500000.0
ROPE_FRAC = 4
EPS = 1e-6
NEG = -1e30
LANES = 128
SUBLANES = 8

INT_MIN = -(2 ** 31)


def _sortable_key_of(x):
    b = struct.unpack("<i", struct.pack("<f", x))[0]
    return b ^ 0x7FFFFFFF if b < 0 else b


KEY_HALF_NEG = _sortable_key_of(NEG * 0.5)

VMEM_LIMIT = 48 * 1024 * 1024
LARGE_VMEM_LIMIT = 56 * 1024 * 1024
NT_DIMS = (((1,), (1,)), ((), ()))


def _params(n_grid, vmem_limit=VMEM_LIMIT):
    return pltpu.CompilerParams(dimension_semantics=("arbitrary",) * n_grid,
                                vmem_limit_bytes=vmem_limit)


def _dot_nt(a, b):
    return lax.dot_general(a, b, NT_DIMS, preferred_element_type=F32)


def _ada_kernel(c_ref, w_ref, b_ref, o_ref):
    c = c_ref[...]
    s = (c * jax.nn.sigmoid(c)).astype(BF16)
    o_ref[...] = jnp.dot(s, w_ref[...].astype(BF16), preferred_element_type=F32) + b_ref[...]


def ada_all(c_all, w_ada, b_ada):
    depth, d, n = w_ada.shape
    nb = c_all.shape[0]
    tn = 512
    return pl.pallas_call(
        _ada_kernel,
        grid=(depth, n // tn),
        in_specs=[pl.BlockSpec((nb, d), lambda l, j: (0, 0)),
                  pl.BlockSpec((None, d, tn), lambda l, j: (l, 0, j)),
                  pl.BlockSpec((None, 1, tn), lambda l, j: (l, 0, j))],
        out_specs=pl.BlockSpec((None, nb, tn), lambda l, j: (l, 0, j)),
        out_shape=jax.ShapeDtypeStruct((depth, nb, n), F32),
        compiler_params=_params(2),
        name="ada",
    )(c_all, w_ada, b_ada.reshape(depth, 1, n))


def _normmod_kernel(x_ref, g_ref, mod_ref, h_ref):
    x = x_ref[...]
    ms = jnp.mean(x * x, axis=-1, keepdims=True)
    y = x * lax.rsqrt(ms + EPS) * g_ref[...]
    shift = mod_ref[0:1, :]
    scale = mod_ref[1:2, :]
    h_ref[...] = (y * (1.0 + scale) + shift).astype(BF16)


def normmod(x, g, mod3, nseq, t, tb):
    rows, d = x.shape
    nblk = t // tb
    return pl.pallas_call(
        _normmod_kernel,
        grid=(nseq, nblk),
        in_specs=[pl.BlockSpec((tb, d), lambda b, i: (b * nblk + i, 0)),
                  pl.BlockSpec((1, d), lambda b, i: (0, 0)),
                  pl.BlockSpec((None, 3, d), lambda b, i: (b, 0, 0))],
        out_specs=pl.BlockSpec((tb, d), lambda b, i: (b * nblk + i, 0)),
        out_shape=jax.ShapeDtypeStruct((rows, d), BF16),
        compiler_params=_params(2),
        name="normmod",
    )(x, g, mod3)


def _mm_kernel(a_ref, b_ref, o_ref, *scratch):
    if scratch:
        wb_ref, = scratch

        @pl.when(pl.program_id(1) == 0)
        def _():
            wb_ref[...] = b_ref[...].T.astype(BF16)

        w = wb_ref[...]
    else:
        w = b_ref[...]
    o_ref[...] = jnp.dot(a_ref[...], w, preferred_element_type=F32)


def matmul(a, w, layer, ncols, tm, tn):
    m, k = a.shape
    if w.dtype == BF16:
        scratch = []
        w_spec = pl.BlockSpec((None, k, tn), lambda j, i: (layer, 0, j))
    else:
        scratch = [pltpu.VMEM((k, tn), BF16)]
        w_spec = pl.BlockSpec((None, tn, k), lambda j, i: (layer, j, 0))
    return pl.pallas_call(
        _mm_kernel,
        grid=(ncols // tn, m // tm),
        in_specs=[pl.BlockSpec((tm, k), lambda j, i: (i, 0)), w_spec],
        out_specs=pl.BlockSpec((tm, tn), lambda j, i: (i, j)),
        out_shape=jax.ShapeDtypeStruct((m, ncols), F32),
        scratch_shapes=scratch,
        compiler_params=_params(2),
        name="inproj",
    )(a, w)


def _rope_kernel(qc_ref, kc_ref, vc_ref, qi_ref, t_ref, c128_ref, s128_ref, c64_ref, s64_ref, *rest,
                 n_dsa, n_idx, n_aliased):
    q_out, kbf_out, qi_out, trot_out, kibf_out, kst_out, vst_out, *maybe_vt_out = rest[n_aliased:]
    tm = qc_ref.shape[0]
    lane = lax.broadcasted_iota(I32, (tm, LANES), 1)
    c128 = c128_ref[...]
    s128 = s128_ref[...]
    c64 = c64_ref[...]
    s64 = s64_ref[...]
    half128 = HEAD_DIM // ROPE_FRAC // 2
    half64 = IDX_DIM // ROPE_FRAC // 2

    def rope128(x):
        partner = jnp.where(lane < half128, pltpu.roll(x, LANES - half128, 1), pltpu.roll(x, half128, 1))
        return x * c128 + partner * s128

    def rope64(x):
        partner = jnp.where((lane & (IDX_DIM - 1)) < half64,
                            pltpu.roll(x, LANES - half64, 1), pltpu.roll(x, half64, 1))
        return x * c64 + partner * s64

    for h in range(n_dsa):
        sl = slice(h * LANES, (h + 1) * LANES)
        q_out[:, sl] = rope128(qc_ref[:, sl]).astype(BF16)
        kr = rope128(kc_ref[:, sl])
        kbf_out[:, sl] = kr.astype(BF16)
        kst_out[pl.ds(h, tm, stride=n_dsa), :] = kr
        vst_out[pl.ds(h, tm, stride=n_dsa), :] = vc_ref[:, sl]
    if maybe_vt_out:
        maybe_vt_out[0][...] = vc_ref[...].T.astype(BF16)
    low = lane < IDX_DIM
    for j in range(n_idx // 2):
        r = rope64(qi_ref[:, j * LANES:(j + 1) * LANES]) * (IDX_DIM ** -0.5)
        qi_out[:, (2 * j) * LANES:(2 * j + 1) * LANES] = jnp.where(low, r, 0.0).astype(BF16)
        qi_out[:, (2 * j + 1) * LANES:(2 * j + 2) * LANES] = jnp.where(low, 0.0, r).astype(BF16)
    t = t_ref[...]
    r = rope64(t)
    trot_out[...] = jnp.where(low, r, jnp.where(lane < IDX_DIM + N_IDX_HEADS, t * (N_IDX_HEADS ** -0.5), 0.0))
    kibf_out[...] = jnp.where(low, r, pltpu.roll(r, IDX_DIM, 1)).astype(BF16)


def rope_call(u, tail, tables, cols, tm, v_t, layer, depth, stacks):
    rows = u.shape[0]
    d_dsa = cols["d_dsa"]
    n_dsa = d_dsa // HEAD_DIM
    d_qi = N_IDX_HEADS * IDX_DIM
    c128, s128, c64, s64 = tables
    tab_spec = pl.BlockSpec((tm, LANES), lambda i: (i, 0))
    blk = lambda off: pl.BlockSpec((tm, d_dsa), lambda i, o=off // d_dsa: (i, o))
    row_spec = pl.BlockSpec((tm, d_dsa), lambda i: (i, 0))
    stack_spec = pl.BlockSpec((None, tm * n_dsa, HEAD_DIM), lambda i: (layer, i, 0))
    stack_shape = jax.ShapeDtypeStruct((depth, rows * n_dsa, HEAD_DIM), F32)
    out_specs = [row_spec, row_spec, pl.BlockSpec((tm, N_IDX_HEADS * LANES), lambda i: (i, 0)),
                 tab_spec, tab_spec, stack_spec, stack_spec]
    out_shape = [jax.ShapeDtypeStruct((rows, d_dsa), BF16),
                 jax.ShapeDtypeStruct((rows, d_dsa), BF16),
                 jax.ShapeDtypeStruct((rows, N_IDX_HEADS * LANES), BF16),
                 jax.ShapeDtypeStruct((rows, LANES), F32),
                 jax.ShapeDtypeStruct((rows, LANES), BF16),
                 stack_shape, stack_shape]
    if v_t:
        out_specs.append(pl.BlockSpec((d_dsa, tm), lambda i: (0, i)))
        out_shape.append(jax.ShapeDtypeStruct((d_dsa, rows), BF16))
    in_specs = [blk(cols["qc"]), blk(cols["kc"]), blk(cols["vc"]),
                pl.BlockSpec((tm, d_qi), lambda i, o=cols["qi"] // d_qi: (i, o)),
                tab_spec, tab_spec, tab_spec, tab_spec, tab_spec]
    args = [u, u, u, u, tail, c128, s128, c64, s64]
    aliases = {}
    if stacks is not None:
        aliases = {len(args): 5, len(args) + 1: 6}
        in_specs += [pl.BlockSpec(memory_space=pl.ANY)] * 2
        args += list(stacks)
    return pl.pallas_call(
        functools.partial(_rope_kernel, n_dsa=n_dsa, n_idx=N_IDX_HEADS, n_aliased=len(aliases)),
        grid=(rows // tm,),
        in_specs=in_specs,
        out_specs=out_specs,
        out_shape=out_shape,
        input_output_aliases=aliases,
        compiler_params=_params(1),
        name="rope",
    )(*args)


def rope_tables(pos):
    pos = pos.astype(F32)[:, None]
    n = pos.shape[0]

    def tab(dim):
        half = dim // ROPE_FRAC // 2
        inv = ROPE_THETA ** (-jnp.arange(half, dtype=F32) / half)
        ang = pos * inv[None]
        cos, sin = jnp.cos(ang), jnp.sin(ang)
        c = jnp.concatenate([cos, cos, jnp.ones((n, dim - 2 * half), F32)], axis=1)
        s = jnp.concatenate([-sin, sin, jnp.zeros((n, dim - 2 * half), F32)], axis=1)
        return jnp.tile(c, (1, LANES // dim)), jnp.tile(s, (1, LANES // dim))

    c128, s128 = tab(HEAD_DIM)
    c64, s64 = tab(IDX_DIM)
    return c128, s128, c64, s64


def _lru_kernel(xa_ref, cs_ref, h0_ref, cw_ref, cb_ref, wrg_ref, brg_ref, wig_ref, big_ref, lam_ref,
                hs_ref, ext_ref, hc_ref):
    tb, d = xa_ref.shape
    blk = d // LRU_BLOCKS

    @pl.when(pl.program_id(1) == 0)
    def _():
        ext_ref[0:8, :] = cs_ref[...]
        hc_ref[...] = h0_ref[...]

    ext_ref[8:8 + tb, :] = xa_ref[...]
    conv = ext_ref[5:5 + tb, :] * cw_ref[0:1, :]
    for j in range(1, CONV_W):
        conv = conv + ext_ref[5 + j:5 + j + tb, :] * cw_ref[j:j + 1, :]
    conv = conv + cb_ref[...]
    tail = ext_ref[tb:tb + 8, :]
    ext_ref[0:8, :] = tail

    xb = conv.astype(BF16)
    rs, gs = [], []
    for g in range(LRU_BLOCKS):
        xg = xb[:, g * blk:(g + 1) * blk]
        rs.append(jnp.dot(xg, wrg_ref[g], preferred_element_type=F32))
        gs.append(jnp.dot(xg, wig_ref[g], preferred_element_type=F32))
    r = jax.nn.sigmoid(jnp.concatenate(rs, axis=1) + brg_ref[...])
    ig = jax.nn.sigmoid(jnp.concatenate(gs, axis=1) + big_ref[...])
    lam = lam_ref[...]
    softplus_neg_lam = jnp.maximum(-lam, 0.0) + jnp.log1p(jnp.exp(-jnp.abs(lam)))
    log_a = (-LRU_C) * r * softplus_neg_lam
    a = jnp.exp(log_a)
    u = jnp.sqrt(-jnp.tanh(log_a) * (a * a + 1.0)) * (ig * conv)

    n_groups = tb // SUBLANES
    a = a.reshape(n_groups, SUBLANES, d)
    u = u.reshape(n_groups, SUBLANES, d)
    row_in_group = lax.broadcasted_iota(I32, (n_groups, SUBLANES, d), 1)
    s = 1
    while s < SUBLANES:
        keep = row_in_group >= s
        a_prev = jnp.where(keep, pltpu.roll(a, s, 1), 1.0)
        u_prev = jnp.where(keep, pltpu.roll(u, s, 1), 0.0)
        u = a * u_prev + u
        a = a * a_prev
        s *= 2
    h_in = hc_ref[...]
    for g in range(n_groups):
        h_group = a[g] * h_in + u[g]
        hs_ref[g * SUBLANES:(g + 1) * SUBLANES, :] = h_group
        h_in = h_group[SUBLANES - 1:SUBLANES, :]
    hc_ref[...] = h_in


def lru_call(u, cs8, h0, cw, cb, wrg, brg, wig, big, lam, nseq, t, tb):
    rows = u.shape[0]
    d = cw.shape[1]
    blk = d // LRU_BLOCKS
    nblk = t // tb
    vec = pl.BlockSpec((1, d), lambda b, i: (0, 0))
    wspec = pl.BlockSpec((LRU_BLOCKS, blk, blk), lambda b, i: (0, 0, 0))
    return pl.pallas_call(
        _lru_kernel,
        grid=(nseq, nblk),
        in_specs=[pl.BlockSpec((tb, d), lambda b, i: (b * nblk + i, 0)),
                  pl.BlockSpec((None, 8, d), lambda b, i: (b, 0, 0)),
                  pl.BlockSpec((None, 1, d), lambda b, i: (b, 0, 0)),
                  pl.BlockSpec((CONV_W, d), lambda b, i: (0, 0)),
                  vec, wspec, vec, wspec, vec, vec],
        out_specs=pl.BlockSpec((tb, d), lambda b, i: (b * nblk + i, 0)),
        out_shape=jax.ShapeDtypeStruct((rows, d), F32),
        scratch_shapes=[pltpu.VMEM((tb + 8, d), F32), pltpu.VMEM((1, d), F32)],
        compiler_params=_params(2),
        name="lru",
    )(u, cs8, h0, cw, cb, wrg, brg, wig, big, lam)


def _bias_kernel(relb_ref, o_ref):
    h = pl.program_id(0)
    qi = lax.broadcasted_iota(I32, (CHUNK, BAND_KEYS), 0)
    kj = lax.broadcasted_iota(I32, (CHUNK, BAND_KEYS), 1)
    idx = jnp.clip(qi + BAND_PAST - kj, -REL_CLIP, REL_CLIP) + REL_CLIP

    def body(r, acc):
        return jnp.where(idx == r, relb_ref[h, r], acc)

    lowest = max(-(CHUNK - 1), -REL_CLIP) + REL_CLIP
    highest = min(BAND_PAST + CHUNK - 1, REL_CLIP) + REL_CLIP
    o_ref[...] = lax.fori_loop(lowest, highest + 1, body, jnp.zeros((CHUNK, BAND_KEYS), F32))


def bias_strip(relb):
    nh = relb.shape[0]
    return pl.pallas_call(
        _bias_kernel,
        grid=(nh,),
        in_specs=[pl.BlockSpec(memory_space=pltpu.SMEM)],
        out_specs=pl.BlockSpec((None, CHUNK, BAND_KEYS), lambda h: (h, 0, 0)),
        out_shape=jax.ShapeDtypeStruct((nh, CHUNK, BAND_KEYS), F32),
        compiler_params=_params(1),
        name="bias_strip",
    )(relb)


def _band_prompt_kernel(q_ref, kp_ref, kc_ref, vp_ref, vc_ref, strip_ref, o_ref, kw_ref, vw_ref):
    i = pl.program_id(2)
    tb = q_ref.shape[0]
    kw_ref[0:tb, :] = kp_ref[...].astype(BF16)
    kw_ref[tb:2 * tb, :] = kc_ref[...].astype(BF16)
    vw_ref[0:tb, :] = vp_ref[...].astype(BF16)
    vw_ref[tb:2 * tb, :] = vc_ref[...].astype(BF16)
    col = lax.broadcasted_iota(I32, (CHUNK, BAND_KEYS), 1)
    items = [(h, slice(h * HEAD_DIM, (h + 1) * HEAD_DIM), a * CHUNK)
             for h in range(q_ref.shape[1] // HEAD_DIM) for a in range(tb // CHUNK)]
    scores = []
    for h, sl, r0 in items:
        q = q_ref[r0:r0 + CHUNK, sl].astype(BF16)
        s = _dot_nt(q, kw_ref[r0:r0 + BAND_KEYS, sl]) * (HEAD_DIM ** -0.5) + strip_ref[h]
        min_col = jnp.where(i > 0, 0, BAND_PAST - r0)
        scores.append(jnp.where(col >= min_col, s, NEG))
    maxes = [jnp.max(s, axis=-1, keepdims=True) for s in scores]
    probs = [jnp.exp(s - m) for s, m in zip(scores, maxes)]
    sums = [jnp.sum(p, axis=-1, keepdims=True) for p in probs]
    for (h, sl, r0), p, l in zip(items, probs, sums):
        pv = jnp.dot(p.astype(BF16), vw_ref[r0:r0 + BAND_KEYS, sl], preferred_element_type=F32)
        o_ref[r0:r0 + CHUNK, sl] = pv / l


def band_prompt(u, strip, cols, nseq, t):
    rows = u.shape[0]
    tb = BAND_PAST
    nblk = t // tb
    nh = strip.shape[0]
    hps = 4 if nh % 4 == 0 else 1
    w = hps * HEAD_DIM
    qo, ko, vo = cols["qb"] // w, cols["kb"] // w, cols["vb"] // w
    cur = lambda off: pl.BlockSpec((tb, w), lambda b, h, i: (b * nblk + i, off + h))
    prev = lambda off: pl.BlockSpec((tb, w), lambda b, h, i: (b * nblk + jnp.maximum(i - 1, 0), off + h))
    return pl.pallas_call(
        _band_prompt_kernel,
        grid=(nseq, nh // hps, nblk),
        in_specs=[cur(qo), prev(ko), cur(ko), prev(vo), cur(vo),
                  pl.BlockSpec((hps, CHUNK, BAND_KEYS), lambda b, h, i: (h, 0, 0))],
        out_specs=pl.BlockSpec((tb, w), lambda b, h, i: (b * nblk + i, h)),
        out_shape=jax.ShapeDtypeStruct((rows, nh * HEAD_DIM), F32),
        scratch_shapes=[pltpu.VMEM((2 * tb, w), BF16), pltpu.VMEM((2 * tb, w), BF16)],
        compiler_params=_params(3),
        name="band_prompt",
    )(u, u, u, u, u, strip)


def _band_sample_kernel(q_ref, kn_ref, vn_ref, ck_ref, cv_ref, strip_ref, o_ref, kw_ref, vw_ref):
    nh = strip_ref.shape[0]
    w = ck_ref.shape[0] // nh
    t = q_ref.shape[0]
    kw_ref[w:w + t, :] = kn_ref[...].astype(BF16)
    vw_ref[w:w + t, :] = vn_ref[...].astype(BF16)
    heads = [slice(h * HEAD_DIM, (h + 1) * HEAD_DIM) for h in range(nh)]
    for h, sl in enumerate(heads):
        kw_ref[0:w, sl] = ck_ref[pl.ds(h, w, stride=nh), :].astype(BF16)
        vw_ref[0:w, sl] = cv_ref[pl.ds(h, w, stride=nh), :].astype(BF16)
    scores = [_dot_nt(q_ref[:, sl].astype(BF16), kw_ref[:, sl]) * (HEAD_DIM ** -0.5) + strip_ref[h]
              for h, sl in enumerate(heads)]
    maxes = [jnp.max(s, axis=-1, keepdims=True) for s in scores]
    probs = [jnp.exp(s - m) for s, m in zip(scores, maxes)]
    sums = [jnp.sum(p, axis=-1, keepdims=True) for p in probs]
    for h, sl in enumerate(heads):
        pv = jnp.dot(probs[h].astype(BF16), vw_ref[:, sl], preferred_element_type=F32)
        o_ref[:, sl] = pv / sums[h]


def band_sample(u, ck, cv, layer, strip, cols, nseq, t):
    rows = u.shape[0]
    nh = strip.shape[0]
    d = nh * HEAD_DIM
    w = ck.shape[2] // nh
    ublk = lambda off: pl.BlockSpec((t, d), lambda b, o=off // d: (b, o))
    cblk = pl.BlockSpec((None, None, w * nh, HEAD_DIM), lambda b: (layer, b, 0, 0))
    return pl.pallas_call(
        _band_sample_kernel,
        grid=(nseq,),
        in_specs=[ublk(cols["qb"]), ublk(cols["kb"]), ublk(cols["vb"]), cblk, cblk,
                  pl.BlockSpec((nh, CHUNK, BAND_KEYS), lambda b: (0, 0, 0))],
        out_specs=pl.BlockSpec((t, d), lambda b: (b, 0)),
        out_shape=jax.ShapeDtypeStruct((rows, d), F32),
        scratch_shapes=[pltpu.VMEM((w + t, d), BF16), pltpu.VMEM((w + t, d), BF16)],
        compiler_params=_params(1),
        name="band_sample",
    )(u, u, u, ck, cv, strip)


def _loop(n, body, init):
    if isinstance(n, int):
        val = init
        for k in range(n):
            val = body(k, val)
        return val
    return lax.fori_loop(0, n, body, init)


IDX_BITS = 16


def _f32_to_key(x):
    bits = lax.bitcast_convert_type(x, I32)
    return jnp.where(bits < 0, bits ^ 0x7FFFFFFF, bits)


def _dsa_core(qside, segs, store_out, sk_refs, s_refs, jstar_ref, mpart_ref, lpart_ref, acc_ref, *,
              topk, q_chunk0, n_q, n_dsa):
    qb = qside["t"].shape[0]
    t_t = qside["t"].T
    wrows = [t_t[IDX_DIM + h:IDX_DIM + h + 1, :] for h in range(N_IDX_HEADS)]

    def tiles(x):
        return x.reshape(x.shape[0] // SUBLANES, SUBLANES, qb)

    w8 = [jnp.broadcast_to(w, (SUBLANES, qb)) for w in wrows]
    qcol8 = lax.broadcasted_iota(I32, (SUBLANES, qb), 1)
    q_end8 = (q_chunk0 + qcol8 // CHUNK + 1) * CHUNK
    for seg, sk_ref in zip(segs, sk_refs):
        kblk = seg["kblk"]
        n_t = kblk // SUBLANES
        krow3 = (lax.broadcasted_iota(I32, (n_t, SUBLANES, qb), 0) * SUBLANES
                 + lax.broadcasted_iota(I32, (n_t, SUBLANES, qb), 1))
        padded = seg["valid"] < kblk or n_q < qb
        live3 = (krow3 < seg["valid"]) & (lax.broadcasted_iota(I32, (n_t, SUBLANES, qb), 2) < n_q)

        def score_blk(kb, carry, seg=seg, sk_ref=sk_ref, kblk=kblk, krow3=krow3, padded=padded, live3=live3):
            kib = seg["ki"](kb)
            acc = jnp.zeros(krow3.shape, F32)
            for group in qside["qi_groups"]:
                sc = _dot_nt(kib, qside["qi"](group))
                for g, h in enumerate(group):
                    acc = acc + jnp.maximum(tiles(sc[:, g * qb:(g + 1) * qb]), 0.0) * w8[h]
            admissible = krow3 < q_end8 - (seg["pos0"] + kb * kblk)
            key = _f32_to_key(jnp.where(admissible, acc, NEG))
            sk_ref[kb] = jnp.where(live3, key, INT_MIN) if padded else key
            return carry

        _loop(seg["nblk"], score_blk, 0)

    def count(pred):
        part = jnp.zeros((SUBLANES, qb), I32)
        for seg, sk_ref in zip(segs, sk_refs):
            n_t = seg["kblk"] // SUBLANES
            off = (lax.broadcasted_iota(I32, (n_t, SUBLANES, qb), 0) * SUBLANES
                   + lax.broadcasted_iota(I32, (n_t, SUBLANES, qb), 1))

            def body(kb, part, seg=seg, sk_ref=sk_ref, off=off):
                idx = seg["pos0"] + kb * seg["kblk"] + off
                return part + jnp.sum(jnp.where(pred(sk_ref[kb], idx), 1, 0), axis=0)

            part = _loop(seg["nblk"], body, part)
        return jnp.broadcast_to(jnp.sum(part, axis=0, keepdims=True), (SUBLANES, qb))

    zero = jnp.zeros((SUBLANES, qb), I32)
    n_stored = sum(seg["nblk"] * seg["kblk"] for seg in segs)
    c_zero = count(lambda key, idx: key >= zero)
    thr0 = jnp.where(c_zero >= topk, 0, INT_MIN).astype(I32)
    c_lo0 = jnp.where(c_zero >= topk, c_zero, zero + n_stored)

    def bit_step(bit, state):
        t, c = state
        cand = t + jnp.left_shift(jnp.int32(1), bit)
        c_cand = count(lambda key, idx: key >= cand)
        ok = c_cand >= topk
        return jnp.where(ok, cand, t), jnp.where(ok, c_cand, c)

    thr, c_lo = lax.fori_loop(0, 31, lambda i, st: bit_step(30 - i, st), (thr0, c_lo0))
    lo = jnp.maximum(thr, KEY_HALF_NEG + 1)

    big = 2 ** IDX_BITS
    jstar_ref[...] = jnp.full((SUBLANES, qb), big, I32)
    surplus = (c_lo > topk) & (thr > KEY_HALF_NEG)

    @pl.when(jnp.max(jnp.where(surplus, 1, 0)) > 0)
    def _():
        need = topk - count(lambda key, idx: key > thr)

        def idx_body(it, j):
            cand = j + jnp.left_shift(jnp.int32(1), IDX_BITS - 1 - it)
            f = count(lambda key, idx: (key == lo) & (idx < cand))
            return jnp.where(f <= need, cand, j)

        j = lax.fori_loop(0, IDX_BITS, idx_body, zero)
        jstar_ref[...] = jnp.where(surplus, j, big)

    jstar = jstar_ref[...]

    mpart_ref[...] = jnp.full(mpart_ref.shape, NEG, F32)
    lpart_ref[...] = jnp.zeros(lpart_ref.shape, F32)
    acc_ref[...] = jnp.zeros(acc_ref.shape, F32)

    for seg, sk_ref, s_ref in zip(segs, sk_refs, s_refs):
        kblk = seg["kblk"]
        n_t = kblk // SUBLANES
        krow3 = (lax.broadcasted_iota(I32, (n_t, SUBLANES, qb), 0) * SUBLANES
                 + lax.broadcasted_iota(I32, (n_t, SUBLANES, qb), 1))

        def pass_a(kb, carry, seg=seg, sk_ref=sk_ref, s_ref=s_ref, kblk=kblk, krow3=krow3):
            key = sk_ref[kb]
            mask = (key >= lo) & ((key > lo) | (krow3 < jstar - (seg["pos0"] + kb * kblk)))
            raw = [tiles(_dot_nt(seg["k"](kb, h), qside["q"](h))) for h in range(n_dsa)]
            for h in range(n_dsa):
                s = jnp.where(mask, raw[h] * (HEAD_DIM ** -0.5), NEG)
                s_ref[h, kb] = s
                mpart_ref[h] = jnp.maximum(mpart_ref[h], jnp.max(s, axis=0))
            return carry

        _loop(seg["nblk"], pass_a, 0)

    m8 = [jnp.broadcast_to(jnp.max(mpart_ref[h], axis=0, keepdims=True), (SUBLANES, qb)) for h in range(n_dsa)]

    for seg, s_ref in zip(segs, s_refs):
        def pass_b(kb, carry, seg=seg, s_ref=s_ref):
            for h in range(n_dsa):
                p = jnp.exp(s_ref[h, kb] - m8[h])
                lpart_ref[h] += jnp.sum(p, axis=0)
                acc_ref[h] += seg["pv"](kb, h, p.reshape(seg["kblk"], qb).astype(BF16))
            return carry

        _loop(seg["nblk"], pass_b, 0)

    for h in range(n_dsa):
        l = jnp.sum(lpart_ref[h], axis=0, keepdims=True)
        store_out(h, (acc_ref[h] / l).T)


def _blk_start(kb, kblk):
    return kb * kblk if isinstance(kb, int) else pl.multiple_of(kb * kblk, kblk)


def _dsa_scratch(qb, seg_shapes, n_dsa):
    return ([pltpu.VMEM((nblk, kblk // SUBLANES, SUBLANES, qb), I32) for nblk, kblk in seg_shapes]
            + [pltpu.VMEM((n_dsa, nblk, kblk // SUBLANES, SUBLANES, qb), F32) for nblk, kblk in seg_shapes]
            + [pltpu.VMEM((SUBLANES, qb), I32),
               pltpu.VMEM((n_dsa, SUBLANES, qb), F32), pltpu.VMEM((n_dsa, SUBLANES, qb), F32),
               pltpu.VMEM((n_dsa, HEAD_DIM, qb), F32)])


def _dsa_prompt_kernel(qi_ref, t_ref, q_ref, ki_ref, k_ref, vt_ref, o_ref, sk_ref, s_ref, jstar_ref,
                       mpart_ref, lpart_ref, acc_ref, *, kblk, topk):
    i = pl.program_id(1)
    qb = q_ref.shape[0]
    nblk = (i * qb + qb + kblk - 1) // kblk

    def head(h):
        return slice(h * HEAD_DIM, (h + 1) * HEAD_DIM)

    seg = dict(ki=lambda kb: ki_ref[pl.ds(_blk_start(kb, kblk), kblk), :],
               k=lambda kb, h: k_ref[pl.ds(_blk_start(kb, kblk), kblk), head(h)],
               pv=lambda kb, h, p: jnp.dot(vt_ref[head(h), pl.ds(_blk_start(kb, kblk), kblk)], p,
                                           preferred_element_type=F32),
               nblk=nblk, kblk=kblk, pos0=0, valid=kblk)
    qside = dict(t=t_ref[...],
                 qi_groups=[(h, h + 1) for h in range(0, N_IDX_HEADS, 2)],
                 qi=lambda group: jnp.concatenate([qi_ref[:, h * LANES:(h + 1) * LANES] for h in group], axis=0),
                 q=lambda h: q_ref[:, head(h)])

    def store_out(h, x):
        o_ref[:, head(h)] = x

    _dsa_core(qside, [seg], store_out, [sk_ref], [s_ref], jstar_ref, mpart_ref, lpart_ref, acc_ref,
              topk=topk, q_chunk0=(i * qb) // CHUNK, n_q=qb, n_dsa=q_ref.shape[1] // HEAD_DIM)


def dsa_prompt(qi_pad, trot, q_bf, ki_bf, k_bf, v_t, nseq, t, qb, kblk, topk):
    rows, d = q_bf.shape
    n_dsa = d // HEAD_DIM
    nq = t // qb
    row_blk = lambda w: pl.BlockSpec((qb, w), lambda b, i: (b * nq + i, 0))
    seq_blk = lambda w: pl.BlockSpec((t, w), lambda b, i: (b, 0))
    return pl.pallas_call(
        functools.partial(_dsa_prompt_kernel, kblk=kblk, topk=topk),
        grid=(nseq, nq),
        in_specs=[row_blk(qi_pad.shape[1]), row_blk(LANES), row_blk(d),
                  seq_blk(LANES), seq_blk(d), pl.BlockSpec((d, t), lambda b, i: (0, b))],
        out_specs=row_blk(d),
        out_shape=jax.ShapeDtypeStruct((rows, d), F32),
        scratch_shapes=_dsa_scratch(qb, [(t // kblk, kblk)], n_dsa),
        compiler_params=_params(2),
        name="dsa_prompt",
    )(qi_pad, trot, q_bf, ki_bf, k_bf, v_t)


def _dsa_sample_kernel(qi_ref, t_ref, q_ref, kin_ref, kn_ref, vn_ref, cki_ref, ck_ref, cv_ref, o_ref,
                       qi_s, q_s, kin_s, kn_s, vn_s, kit_s, sk0_ref, sk1_ref, s0_ref, s1_ref, jstar_ref,
                       mpart_ref, lpart_ref, acc_ref, *, kblk, topk):
    past = cki_ref.shape[2]
    tq = q_ref.shape[0] // 2
    qb = 2 * tq
    n_dsa = q_ref.shape[1] // HEAD_DIM
    pair_w = 2 * LANES

    def head(h):
        return slice(h * HEAD_DIM, (h + 1) * HEAD_DIM)

    def rows(a):
        return slice(a * tq, (a + 1) * tq)

    def half(h, a):
        return slice(h * pair_w + a * LANES, h * pair_w + (a + 1) * LANES)

    for ref in (qi_s, q_s, kin_s, kn_s):
        ref[...] = jnp.zeros(ref.shape, ref.dtype)
    for a in range(2):
        for h in range(N_IDX_HEADS):
            qi_s[rows(a), half(h, a)] = qi_ref[rows(a), h * LANES:(h + 1) * LANES]
        for h in range(n_dsa):
            q_s[rows(a), half(h, a)] = q_ref[rows(a), head(h)]
            kn_s[0:tq, half(h, a)] = kn_ref[rows(a), head(h)]
        kin_s[0:tq, half(0, a)] = kin_ref[rows(a), :]
        vn_s[a, 0:tq, :] = vn_ref[rows(a), :]
        vn_s[a, tq:, :] = jnp.zeros((qb - tq, vn_s.shape[2]), F32)

    first_seq = lax.broadcasted_iota(I32, (HEAD_DIM, qb), 1) < tq

    def pair_pv(v_a, v_b, p):
        out_a = jnp.dot(v_a.T.astype(BF16), p, preferred_element_type=F32)
        out_b = jnp.dot(v_b.T.astype(BF16), p, preferred_element_type=F32)
        return jnp.where(first_seq, out_a, out_b)

    def cache_rows(ref, a, kb, h):
        return ref[a, pl.ds(kb * kblk * n_dsa + h, kblk, stride=n_dsa), :]

    def cache_ki(kb):
        parts = []
        for a in range(2):
            kit_s[a, 0:IDX_DIM, :] = cki_ref[a, :, pl.ds(kb * kblk, kblk)]
            kit_s[a, IDX_DIM:, :] = cki_ref[a, :, pl.ds(kb * kblk, kblk)]
            parts.append(kit_s[a].T.astype(BF16))
        return jnp.concatenate(parts, axis=1)

    segs = [dict(ki=cache_ki,
                 k=lambda kb, h: jnp.concatenate([cache_rows(ck_ref, a, kb, h).astype(BF16) for a in range(2)],
                                                 axis=1),
                 pv=lambda kb, h, p: pair_pv(cache_rows(cv_ref, 0, kb, h), cache_rows(cv_ref, 1, kb, h), p),
                 nblk=past // kblk, kblk=kblk, pos0=0, valid=kblk),
            dict(ki=lambda kb: kin_s[...],
                 k=lambda kb, h: kn_s[:, h * pair_w:(h + 1) * pair_w],
                 pv=lambda kb, h, p: pair_pv(vn_s[0, :, head(h)], vn_s[1, :, head(h)], p),
                 nblk=1, kblk=qb, pos0=past, valid=tq)]
    qside = dict(t=t_ref[...],
                 qi_groups=[(h,) for h in range(N_IDX_HEADS)],
                 qi=lambda group: qi_s[:, group[0] * pair_w:(group[0] + 1) * pair_w],
                 q=lambda h: q_s[:, h * pair_w:(h + 1) * pair_w])

    def store_out(h, x):
        o_ref[:, head(h)] = x

    _dsa_core(qside, segs, store_out, [sk0_ref, sk1_ref], [s0_ref, s1_ref], jstar_ref,
              mpart_ref, lpart_ref, acc_ref, topk=topk, q_chunk0=past // CHUNK, n_q=qb, n_dsa=n_dsa)


def dsa_sample(qi_pad, trot, q_bf, ki_bf, k_bf, u, vc_col, cki, ck, cv, layer, nseq, t, kblk, topk):
    rows, d = q_bf.shape
    n_dsa = d // HEAD_DIM
    past = cki.shape[3]
    qb = 2 * t
    assert qb == LANES and nseq % 2 == 0
    row_blk = lambda w, o=0: pl.BlockSpec((qb, w), lambda b, o=o: (b, o))
    cache_blk = lambda r, w: pl.BlockSpec((None, 2, r, w), lambda b: (layer, b, 0, 0))
    pair_scratch = [pltpu.VMEM((qb, 2 * qi_pad.shape[1]), BF16), pltpu.VMEM((qb, 2 * d), BF16),
                    pltpu.VMEM((qb, 2 * LANES), BF16), pltpu.VMEM((qb, 2 * d), BF16),
                    pltpu.VMEM((2, qb, d), F32), pltpu.VMEM((2, LANES, kblk), F32)]
    return pl.pallas_call(
        functools.partial(_dsa_sample_kernel, kblk=kblk, topk=topk),
        grid=(nseq // 2,),
        in_specs=[row_blk(qi_pad.shape[1]), row_blk(LANES), row_blk(d),
                  row_blk(LANES), row_blk(d), row_blk(d, vc_col),
                  cache_blk(cki.shape[2], past), cache_blk(past * n_dsa, HEAD_DIM),
                  cache_blk(past * n_dsa, HEAD_DIM)],
        out_specs=row_blk(d),
        out_shape=jax.ShapeDtypeStruct((rows, d), F32),
        scratch_shapes=pair_scratch + _dsa_scratch(qb, [(past // kblk, kblk), (1, qb)], n_dsa),
        compiler_params=_params(1, LARGE_VMEM_LIMIT),
        name="dsa_sample",
    )(qi_pad, trot, q_bf, ki_bf, k_bf, u, cki, ck, cv)


def _outproj_kernel(oa_ref, ob_ref, oc_ref, za_ref, zb_ref, zc_ref, g_ref, w_ref, x_ref, mod_ref, g2_ref,
                    *rest, d_lru, d_band, last):
    if last:
        yo_ref, y_ref = rest
    else:
        mod2_ref, xo_ref, h_ref, y_ref = rest

    def branch(o_ref, z_ref, lo, hi):
        o = o_ref[...]
        z = z_ref[...]
        ms = jnp.mean(o * o, axis=-1, keepdims=True)
        y = o * lax.rsqrt(ms + EPS) * g_ref[:, lo:hi]
        y_ref[:, lo:hi] = (y * (z * jax.nn.sigmoid(z))).astype(BF16)

    d_mix = y_ref.shape[1]
    branch(oa_ref, za_ref, 0, d_lru)
    branch(ob_ref, zb_ref, d_lru, d_lru + d_band)
    branch(oc_ref, zc_ref, d_lru + d_band, d_mix)
    out = jnp.dot(y_ref[...], w_ref[...], preferred_element_type=F32)
    n_seq = mod_ref.shape[0]
    t = out.shape[0] // n_seq
    for s in range(n_seq):
        rows = slice(s * t, (s + 1) * t)
        x_new = x_ref[rows, :] + mod_ref[s, 2:3, :] * out[rows, :]
        ms = jnp.mean(x_new * x_new, axis=-1, keepdims=True)
        normed = x_new * lax.rsqrt(ms + EPS) * g2_ref[...]
        if last:
            yo_ref[rows, :] = normed
        else:
            xo_ref[rows, :] = x_new
            h_ref[rows, :] = (normed * (1.0 + mod2_ref[s, 1:2, :]) + mod2_ref[s, 0:1, :]).astype(BF16)


def outproj(oa, ob, oc, u, g_branch, w_out, layer, x, mod3, g2, mod3_next, cols, nseq, t, tm):
    rows, d = x.shape
    d_lru, d_band, d_dsa = oa.shape[1], ob.shape[1], oc.shape[1]
    d_mix = d_lru + d_band + d_dsa
    nblk = max(t // tm, 1)
    seqs_per_blk = max(tm // t, 1)
    last = mod3_next is None
    rb = lambda w, o=0: pl.BlockSpec((tm, w), lambda b, i, o=o: (b * nblk + i, o))
    vec = pl.BlockSpec((1, d), lambda b, i: (0, 0))
    mod_spec = pl.BlockSpec((seqs_per_blk, 3, d), lambda b, i: (b, 0, 0))
    in_specs = [rb(d_lru), rb(d_band), rb(d_dsa),
                rb(d_lru, cols["za"] // d_lru), rb(d_band, cols["zb"] // d_band), rb(d_dsa, cols["zc"] // d_dsa),
                pl.BlockSpec((1, d_mix), lambda b, i: (0, 0)),
                pl.BlockSpec((None, d_mix, d), lambda b, i: (layer, 0, 0), pipeline_mode=pl.Buffered(1)),
                rb(d), mod_spec, vec]
    args = [oa, ob, oc, u, u, u, g_branch, w_out, x, mod3, g2]
    if last:
        out_specs = rb(d)
        out_shape = jax.ShapeDtypeStruct((rows, d), F32)
    else:
        in_specs.append(mod_spec)
        args.append(mod3_next)
        out_specs = [rb(d), rb(d)]
        out_shape = [jax.ShapeDtypeStruct((rows, d), F32), jax.ShapeDtypeStruct((rows, d), BF16)]
    return pl.pallas_call(
        functools.partial(_outproj_kernel, d_lru=d_lru, d_band=d_band, last=last),
        grid=(nseq // seqs_per_blk, nblk),
        in_specs=in_specs,
        out_specs=out_specs,
        out_shape=out_shape,
        scratch_shapes=[pltpu.VMEM((tm, d_mix), BF16)],
        compiler_params=_params(2, LARGE_VMEM_LIMIT),
        name="outproj",
    )(*args)


def _column_offsets(d_lru, d_band, d_dsa):
    names = ["xa", "za", "qb", "kb", "vb", "zb", "qc", "kc", "vc", "zc", "qi"]
    widths = [d_lru, d_lru, d_band, d_band, d_band, d_band, d_dsa, d_dsa, d_dsa, d_dsa, N_IDX_HEADS * IDX_DIM]
    cols, off = {}, 0
    for n, w in zip(names, widths):
        cols[n] = off
        off += w
    cols["main"] = off
    cols["d_lru"], cols["d_band"], cols["d_dsa"] = d_lru, d_band, d_dsa
    return cols


def _layer_stream(x, h, mod3, mod3_next, g_next, lw, layer, depth, cols, nseq, t, tables, state, strip, stacks,
                  prompt):
    d_lru, d_band, d_dsa = cols["d_lru"], cols["d_band"], cols["d_dsa"]
    n_band, n_dsa = d_band // HEAD_DIM, d_dsa // HEAD_DIM
    rows = nseq * t
    tb = 256 if t % 256 == 0 else t
    tm = 1024 if rows % 1024 == 0 else rows
    u = matmul(h, lw["w_in"], layer, cols["main"], tm, 1024)
    tail = matmul(h, lw["w_tail"], layer, LANES, tm, LANES)

    q_bf, k_bf, qi_pad, t_rot, ki_bf, k_stack, v_stack, *maybe_vt = rope_call(
        u, tail, tables, cols, 512 if rows % 512 == 0 else (256 if rows % 256 == 0 else tb), prompt, layer, depth,
        stacks)

    conv_s, lru_s = state[0], state[1]
    cs8 = jnp.concatenate([jnp.zeros((nseq, 8 - (CONV_W - 1), d_lru), F32), conv_s], axis=1)
    hs = lru_call(u, cs8, lru_s.reshape(nseq, 1, d_lru), lw["conv_w"], lw["conv_b"], lw["w_rg"], lw["b_rg"],
                  lw["w_ig"], lw["b_ig"], lw["lam"], nseq, t, tb)

    if prompt:
        ob = band_prompt(u, strip, cols, nseq, t)
        oc = dsa_prompt(qi_pad, t_rot, q_bf, ki_bf, k_bf, maybe_vt[0], nseq, t, 256, 512, min(TOPK_MAX, t // 4))
    else:
        bk, bv, dk, dv, dik = state[2:]
        past = dik.shape[3]
        assert (past + t - 1) // CHUNK <= past // CHUNK and t == CHUNK and bk.shape[2] == BAND_PAST * n_band
        ob = band_sample(u, bk, bv, layer, strip, cols, nseq, t)
        oc = dsa_sample(qi_pad, t_rot, q_bf, ki_bf, k_bf, u, cols["vc"] // d_dsa, dik, dk, dv, layer,
                        nseq, t, 512, min(TOPK_MAX, (past + t) // 4))

    res = outproj(hs, ob, oc, u, lw["g_branch"], lw["w_out"], layer, x, mod3, g_next, mod3_next, cols, nseq, t,
                  512 if t % 512 == 0 else (256 if rows % 256 == 0 else tb))
    x_new, h_next = (res, None) if mod3_next is None else res

    u3 = u.reshape(nseq, t, -1)
    xa = u3[:, :, cols["xa"]:cols["xa"] + d_lru]
    if prompt:
        new_conv = xa[:, t - (CONV_W - 1):]
    else:
        new_conv = jnp.concatenate([conv_s, xa], axis=1)[:, -(CONV_W - 1):]
    nbr = min(BAND_PAST, t)
    new = (new_conv,
           hs.reshape(nseq, t, d_lru)[:, -1],
           u3[:, t - nbr:, cols["kb"]:cols["kb"] + d_band].reshape(nseq, nbr, n_band, HEAD_DIM),
           u3[:, t - nbr:, cols["vb"]:cols["vb"] + d_band].reshape(nseq, nbr, n_band, HEAD_DIM),
           None,
           None,
           t_rot.reshape(nseq, t, LANES)[:, :, :IDX_DIM])
    return x_new, h_next, new, (k_stack, v_stack)


def kernel(x_prompt, x_sample, c_prompt, c_sample, state_conv, state_lru, cache_band_k, cache_band_v,
           cache_dsa_k, cache_dsa_v, cache_dsa_idx_k, g_norm, w_ada, b_ada, w_in, conv_w, conv_b,
           w_rg, b_rg, w_ig, b_ig, lru_lambda, rel_bias, g_branch, w_out, g_final):
    depth = w_in.shape[0]
    nb_p, t_p, d = x_prompt.shape
    nb_s, t_s, _ = x_sample.shape
    past = cache_dsa_k.shape[2]
    d_lru = conv_w.shape[2]
    d_band = cache_band_k.shape[3] * HEAD_DIM
    d_dsa = cache_dsa_k.shape[3] * HEAD_DIM
    cols = _column_offsets(d_lru, d_band, d_dsa)
    n_main = cols["main"]

    mod = ada_all(jnp.concatenate([c_prompt, c_sample], axis=0), w_ada, b_ada)
    mod = mod.reshape(depth, nb_p + nb_s, 3, d)

    w_tail = jnp.pad(w_in[:, :, n_main:], ((0, 0), (0, 0), (0, LANES - (w_in.shape[2] - n_main)))).astype(BF16)
    w_out_bf = w_out.astype(BF16)
    w_rg_bf = w_rg.astype(BF16)
    w_ig_bf = w_ig.astype(BF16)

    band_k = cache_band_k.reshape(depth, nb_s, -1, HEAD_DIM)
    band_v = cache_band_v.reshape(depth, nb_s, -1, HEAD_DIM)
    dsa_k = cache_dsa_k.reshape(depth, nb_s, -1, HEAD_DIM)
    dsa_v = cache_dsa_v.reshape(depth, nb_s, -1, HEAD_DIM)
    idx_k_t = jnp.swapaxes(cache_dsa_idx_k, 2, 3)
    w_in_t = jnp.swapaxes(w_in, 1, 2)

    tab_p = rope_tables(jnp.tile(jnp.arange(t_p), nb_p))
    tab_s = rope_tables(jnp.tile(past + jnp.arange(t_s), nb_s))

    xp = x_prompt.reshape(nb_p * t_p, d)
    xs = x_sample.reshape(nb_s * t_s, d)
    zero_state = (jnp.zeros((nb_p, CONV_W - 1, d_lru), F32), jnp.zeros((nb_p, d_lru), F32))
    p_new, s_new = [], []
    p_stacks = s_stacks = None
    hp = normmod(xp, g_norm[0][None], mod[0, :nb_p], nb_p, t_p, 256 if t_p % 256 == 0 else t_p)
    hsm = normmod(xs, g_norm[0][None], mod[0, nb_p:], nb_s, t_s, 256 if t_s % 256 == 0 else t_s)
    for l in range(depth):
        lw = dict(w_in=w_in_t, w_tail=w_tail, conv_w=conv_w[l], conv_b=conv_b[l][None],
                  w_rg=w_rg_bf[l], b_rg=b_rg[l][None], w_ig=w_ig_bf[l], b_ig=b_ig[l][None],
                  lam=lru_lambda[l][None], g_branch=g_branch[l][None], w_out=w_out_bf)
        last = l == depth - 1
        g_next = g_final[None] if last else g_norm[l + 1][None]
        modp_next = None if last else mod[l + 1, :nb_p]
        mods_next = None if last else mod[l + 1, nb_p:]
        strip = bias_strip(rel_bias[l])
        xp, hp, pn, p_stacks = _layer_stream(xp, hp, mod[l, :nb_p], modp_next, g_next, lw, l, depth, cols, nb_p, t_p,
                                             tab_p, zero_state, strip, p_stacks, True)
        st = (state_conv[l], state_lru[l], band_k, band_v, dsa_k, dsa_v, idx_k_t)
        xs, hsm, sn, s_stacks = _layer_stream(xs, hsm, mod[l, nb_p:], mods_next, g_next, lw, l, depth, cols, nb_s,
                                              t_s, tab_s, st, strip, s_stacks, False)
        p_new.append(pn)
        s_new.append(sn)

    y_prompt = xp.reshape(nb_p, t_p, d)
    y_sample = xs.reshape(nb_s, t_s, d)
    n_dsa = d_dsa // HEAD_DIM

    def outputs(new, stacks, nseq, t):
        out = [None if new[0][j] is None else jnp.stack([layer_new[j] for layer_new in new], axis=0)
               for j in range(7)]
        out[4], out[5] = (s.reshape(depth, nseq, t, n_dsa, HEAD_DIM) for s in stacks)
        return out

    return (y_prompt, y_sample, *outputs(p_new, p_stacks, nb_p, t_p), *outputs(s_new, s_stacks, nb_s, t_s))
```

```python
import functools
import struct

import jax
import jax.numpy as jnp
from jax import lax
from jax.experimental import pallas as pl
from jax.experimental.pallas import tpu as pltpu

F32 = jnp.float32
BF16 = jnp.bfloat16
I32 = jnp.int32

CHUNK = 64
HEAD_DIM = 128
LRU_BLOCKS = 8
CONV_W = 4
LRU_C = 8.0
BAND_PAST_CHUNKS = 8
BAND_PAST = BAND_PAST_CHUNKS * CHUNK
BAND_KEYS = BAND_PAST + CHUNK
REL_CLIP = 256
N_IDX_HEADS = 16
IDX_DIM = 64
TOPK_MAX = 256
ROPE_THETA = 500000.0
ROPE_FRAC = 4
EPS = 1e-6
NEG = -1e30
LANES = 128
SUBLANES = 8

INT_MIN = -(2 ** 31)


def _sortable_key_of(x):
    b = struct.unpack("<i", struct.pack("<f", x))[0]
    return b ^ 0x7FFFFFFF if b < 0 else b


KEY_HALF_NEG = _sortable_key_of(NEG * 0.5)

VMEM_LIMIT = 48 * 1024 * 1024
LARGE_VMEM_LIMIT = 56 * 1024 * 1024
NT_DIMS = (((1,), (1,)), ((), ()))


def _params(n_grid, vmem_limit=VMEM_LIMIT):
    return pltpu.CompilerParams(dimension_semantics=("arbitrary",) * n_grid,
                                vmem_limit_bytes=vmem_limit)


def _dot_nt(a, b):
    return lax.dot_general(a, b, NT_DIMS, preferred_element_type=F32)


def _ada_kernel(c_ref, w_ref, b_ref, o_ref):
    c = c_ref[...]
    s = (c * jax.nn.sigmoid(c)).astype(BF16)
    o_ref[...] = jnp.dot(s, w_ref[...].astype(BF16), preferred_element_type=F32) + b_ref[...]


def ada_all(c_all, w_ada, b_ada):
    depth, d, n = w_ada.shape
    nb = c_all.shape[0]
    tn = 512
    return pl.pallas_call(
        _ada_kernel,
        grid=(depth, n // tn),
        in_specs=[pl.BlockSpec((nb, d), lambda l, j: (0, 0)),
                  pl.BlockSpec((None, d, tn), lambda l, j: (l, 0, j)),
                  pl.BlockSpec((None, 1, tn), lambda l, j: (l, 0, j))],
        out_specs=pl.BlockSpec((None, nb, tn), lambda l, j: (l, 0, j)),
        out_shape=jax.ShapeDtypeStruct((depth, nb, n), F32),
        compiler_params=_params(2),
        name="ada",
    )(c_all, w_ada, b_ada.reshape(depth, 1, n))


def _normmod_kernel(x_ref, g_ref, mod_ref, h_ref):
    x = x_ref[...]
    ms = jnp.mean(x * x, axis=-1, keepdims=True)
    y = x * lax.rsqrt(ms + EPS) * g_ref[...]
    shift = mod_ref[0:1, :]
    scale = mod_ref[1:2, :]
    h_ref[...] = (y * (1.0 + scale) + shift).astype(BF16)


def normmod(x, g, mod3, nseq, t, tb):
    rows, d = x.shape
    nblk = t // tb
    return pl.pallas_call(
        _normmod_kernel,
        grid=(nseq, nblk),
        in_specs=[pl.BlockSpec((tb, d), lambda b, i: (b * nblk + i, 0)),
                  pl.BlockSpec((1, d), lambda b, i: (0, 0)),
                  pl.BlockSpec((None, 3, d), lambda b, i: (b, 0, 0))],
        out_specs=pl.BlockSpec((tb, d), lambda b, i: (b * nblk + i, 0)),
        out_shape=jax.ShapeDtypeStruct((rows, d), BF16),
        compiler_params=_params(2),
        name="normmod",
    )(x, g, mod3)


def _mm_kernel(a_ref, b_ref, o_ref, *scratch):
    if scratch:
        wb_ref, = scratch

        @pl.when(pl.program_id(1) == 0)
        def _():
            wb_ref[...] = b_ref[...].T.astype(BF16)

        w = wb_ref[...]
    else:
        w = b_ref[...]
    o_ref[...] = jnp.dot(a_ref[...], w, preferred_element_type=F32)


def matmul(a, w, layer, ncols, tm, tn):
    m, k = a.shape
    if w.dtype == BF16:
        scratch = []
        w_spec = pl.BlockSpec((None, k, tn), lambda j, i: (layer, 0, j))
    else:
        scratch = [pltpu.VMEM((k, tn), BF16)]
        w_spec = pl.BlockSpec((None, tn, k), lambda j, i: (layer, j, 0))
    return pl.pallas_call(
        _mm_kernel,
        grid=(ncols // tn, m // tm),
        in_specs=[pl.BlockSpec((tm, k), lambda j, i: (i, 0)), w_spec],
        out_specs=pl.BlockSpec((tm, tn), lambda j, i: (i, j)),
        out_shape=jax.ShapeDtypeStruct((m, ncols), F32),
        scratch_shapes=scratch,
        compiler_params=_params(2),
        name="inproj",
    )(a, w)


def _rope_kernel(qc_ref, kc_ref, vc_ref, qi_ref, t_ref, c128_ref, s128_ref, c64_ref, s64_ref, *rest,
                 n_dsa, n_idx, n_aliased):
    q_out, kbf_out, qi_out, trot_out, kibf_out, kst_out, vst_out, *maybe_vt_out = rest[n_aliased:]
    tm = qc_ref.shape[0]
    lane = lax.broadcasted_iota(I32, (tm, LANES), 1)
    c128 = c128_ref[...]
    s128 = s128_ref[...]
    c64 = c64_ref[...]
    s64 = s64_ref[...]
    half128 = HEAD_DIM // ROPE_FRAC // 2
    half64 = IDX_DIM // ROPE_FRAC // 2

    def rope128(x):
        partner = jnp.where(lane < half128, pltpu.roll(x, LANES - half128, 1), pltpu.roll(x, half128, 1))
        return x * c128 + partner * s128

    def rope64(x):
        partner = jnp.where((lane & (IDX_DIM - 1)) < half64,
                            pltpu.roll(x, LANES - half64, 1), pltpu.roll(x, half64, 1))
        return x * c64 + partner * s64

    for h in range(n_dsa):
        sl = slice(h * LANES, (h + 1) * LANES)
        q_out[:, sl] = rope128(qc_ref[:, sl]).astype(BF16)
        kr = rope128(kc_ref[:, sl])
        kbf_out[:, sl] = kr.astype(BF16)
        kst_out[pl.ds(h, tm, stride=n_dsa), :] = kr
        vst_out[pl.ds(h, tm, stride=n_dsa), :] = vc_ref[:, sl]
    if maybe_vt_out:
        maybe_vt_out[0][...] = vc_ref[...].T.astype(BF16)
    low = lane < IDX_DIM
    for j in range(n_idx // 2):
        r = rope64(qi_ref[:, j * LANES:(j + 1) * LANES]) * (IDX_DIM ** -0.5)
        qi_out[:, (2 * j) * LANES:(2 * j + 1) * LANES] = jnp.where(low, r, 0.0).astype(BF16)
        qi_out[:, (2 * j + 1) * LANES:(2 * j + 2) * LANES] = jnp.where(low, 0.0, r).astype(BF16)
    t = t_ref[...]
    r = rope64(t)
    trot_out[...] = jnp.where(low, r, jnp.where(lane < IDX_DIM + N_IDX_HEADS, t * (N_IDX_HEADS ** -0.5), 0.0))
    kibf_out[...] = jnp.where(low, r, pltpu.roll(r, IDX_DIM, 1)).astype(BF16)


def rope_call(u, tail, tables, cols, tm, v_t, layer, depth, stacks):
    rows = u.shape[0]
    d_dsa = cols["d_dsa"]
    n_dsa = d_dsa // HEAD_DIM
    d_qi = N_IDX_HEADS * IDX_DIM
    c128, s128, c64, s64 = tables
    tab_spec = pl.BlockSpec((tm, LANES), lambda i: (i, 0))
    blk = lambda off: pl.BlockSpec((tm, d_dsa), lambda i, o=off // d_dsa: (i, o))
    row_spec = pl.BlockSpec((tm, d_dsa), lambda i: (i, 0))
    stack_spec = pl.BlockSpec((None, tm * n_dsa, HEAD_DIM), lambda i: (layer, i, 0))
    stack_shape = jax.ShapeDtypeStruct((depth, rows * n_dsa, HEAD_DIM), F32)
    out_specs = [row_spec, row_spec, pl.BlockSpec((tm, N_IDX_HEADS * LANES), lambda i: (i, 0)),
                 tab_spec, tab_spec, stack_spec, stack_spec]
    out_shape = [jax.ShapeDtypeStruct((rows, d_dsa), BF16),
                 jax.ShapeDtypeStruct((rows, d_dsa), BF16),
                 jax.ShapeDtypeStruct((rows, N_IDX_HEADS * LANES), BF16),
                 jax.ShapeDtypeStruct((rows, LANES), F32),
                 jax.ShapeDtypeStruct((rows, LANES), BF16),
                 stack_shape, stack_shape]
    if v_t:
        out_specs.append(pl.BlockSpec((d_dsa, tm), lambda i: (0, i)))
        out_shape.append(jax.ShapeDtypeStruct((d_dsa, rows), BF16))
    in_specs = [blk(cols["qc"]), blk(cols["kc"]), blk(cols["vc"]),
                pl.BlockSpec((tm, d_qi), lambda i, o=cols["qi"] // d_qi: (i, o)),
                tab_spec, tab_spec, tab_spec, tab_spec, tab_spec]
    args = [u, u, u, u, tail, c128, s128, c64, s64]
    aliases = {}
    if stacks is not None:
        aliases = {len(args): 5, len(args) + 1: 6}
        in_specs += [pl.BlockSpec(memory_space=pl.ANY)] * 2
        args += list(stacks)
    return pl.pallas_call(
        functools.partial(_rope_kernel, n_dsa=n_dsa, n_idx=N_IDX_HEADS, n_aliased=len(aliases)),
        grid=(rows // tm,),
        in_specs=in_specs,
        out_specs=out_specs,
        out_shape=out_shape,
        input_output_aliases=aliases,
        compiler_params=_params(1),
        name="rope",
    )(*args)


def rope_tables(pos):
    pos = pos.astype(F32)[:, None]
    n = pos.shape[0]

    def tab(dim):
        half = dim // ROPE_FRAC // 2
        inv = ROPE_THETA ** (-jnp.arange(half, dtype=F32) / half)
        ang = pos * inv[None]
        cos, sin = jnp.cos(ang), jnp.sin(ang)
        c = jnp.concatenate([cos, cos, jnp.ones((n, dim - 2 * half), F32)], axis=1)
        s = jnp.concatenate([-sin, sin, jnp.zeros((n, dim - 2 * half), F32)], axis=1)
        return jnp.tile(c, (1, LANES // dim)), jnp.tile(s, (1, LANES // dim))

    c128, s128 = tab(HEAD_DIM)
    c64, s64 = tab(IDX_DIM)
    return c128, s128, c64, s64


def _lru_kernel(xa_ref, cs_ref, h0_ref, cw_ref, cb_ref, wrg_ref, brg_ref, wig_ref, big_ref, lam_ref,
                hs_ref, ext_ref, hc_ref):
    tb, d = xa_ref.shape
    blk = d // LRU_BLOCKS

    @pl.when(pl.program_id(1) == 0)
    def _():
        ext_ref[0:8, :] = cs_ref[...]
        hc_ref[...] = h0_ref[...]

    ext_ref[8:8 + tb, :] = xa_ref[...]
    conv = ext_ref[5:5 + tb, :] * cw_ref[0:1, :]
    for j in range(1, CONV_W):
        conv = conv + ext_ref[5 + j:5 + j + tb, :] * cw_ref[j:j + 1, :]
    conv = conv + cb_ref[...]
    tail = ext_ref[tb:tb + 8, :]
    ext_ref[0:8, :] = tail

    xb = conv.astype(BF16)
    rs, gs = [], []
    for g in range(LRU_BLOCKS):
        xg = xb[:, g * blk:(g + 1) * blk]
        rs.append(jnp.dot(xg, wrg_ref[g], preferred_element_type=F32))
        gs.append(jnp.dot(xg, wig_ref[g], preferred_element_type=F32))
    r = jax.nn.sigmoid(jnp.concatenate(rs, axis=1) + brg_ref[...])
    ig = jax.nn.sigmoid(jnp.concatenate(gs, axis=1) + big_ref[...])
    lam = lam_ref[...]
    softplus_neg_lam = jnp.maximum(-lam, 0.0) + jnp.log1p(jnp.exp(-jnp.abs(lam)))
    log_a = (-LRU_C) * r * softplus_neg_lam
    a = jnp.exp(log_a)
    u = jnp.sqrt(-jnp.tanh(log_a) * (a * a + 1.0)) * (ig * conv)

    n_groups = tb // SUBLANES
    a = a.reshape(n_groups, SUBLANES, d)
    u = u.reshape(n_groups, SUBLANES, d)
    row_in_group = lax.broadcasted_iota(I32, (n_groups, SUBLANES, d), 1)
    s = 1
    while s < SUBLANES:
        keep = row_in_group >= s
        a_prev = jnp.where(keep, pltpu.roll(a, s, 1), 1.0)
        u_prev = jnp.where(keep, pltpu.roll(u, s, 1), 0.0)
        u = a * u_prev + u
        a = a * a_prev
        s *= 2
    h_in = hc_ref[...]
    for g in range(n_groups):
        h_group = a[g] * h_in + u[g]
        hs_ref[g * SUBLANES:(g + 1) * SUBLANES, :] = h_group
        h_in = h_group[SUBLANES - 1:SUBLANES, :]
    hc_ref[...] = h_in


def lru_call(u, cs8, h0, cw, cb, wrg, brg, wig, big, lam, nseq, t, tb):
    rows = u.shape[0]
    d = cw.shape[1]
    blk = d // LRU_BLOCKS
    nblk = t // tb
    vec = pl.BlockSpec((1, d), lambda b, i: (0, 0))
    wspec = pl.BlockSpec((LRU_BLOCKS, blk, blk), lambda b, i: (0, 0, 0))
    return pl.pallas_call(
        _lru_kernel,
        grid=(nseq, nblk),
        in_specs=[pl.BlockSpec((tb, d), lambda b, i: (b * nblk + i, 0)),
                  pl.BlockSpec((None, 8, d), lambda b, i: (b, 0, 0)),
                  pl.BlockSpec((None, 1, d), lambda b, i: (b, 0, 0)),
                  pl.BlockSpec((CONV_W, d), lambda b, i: (0, 0)),
                  vec, wspec, vec, wspec, vec, vec],
        out_specs=pl.BlockSpec((tb, d), lambda b, i: (b * nblk + i, 0)),
        out_shape=jax.ShapeDtypeStruct((rows, d), F32),
        scratch_shapes=[pltpu.VMEM((tb + 8, d), F32), pltpu.VMEM((1, d), F32)],
        compiler_params=_params(2),
        name="lru",
    )(u, cs8, h0, cw, cb, wrg, brg, wig, big, lam)


def _bias_kernel(relb_ref, o_ref):
    h = pl.program_id(0)
    qi = lax.broadcasted_iota(I32, (CHUNK, BAND_KEYS), 0)
    kj = lax.broadcasted_iota(I32, (CHUNK, BAND_KEYS), 1)
    idx = jnp.clip(qi + BAND_PAST - kj, -REL_CLIP, REL_CLIP) + REL_CLIP

    def body(r, acc):
        return jnp.where(idx == r, relb_ref[h, r], acc)

    lowest = max(-(CHUNK - 1), -REL_CLIP) + REL_CLIP
    highest = min(BAND_PAST + CHUNK - 1, REL_CLIP) + REL_CLIP
    o_ref[...] = lax.fori_loop(lowest, highest + 1, body, jnp.zeros((CHUNK, BAND_KEYS), F32))


def bias_strip(relb):
    nh = relb.shape[0]
    return pl.pallas_call(
        _bias_kernel,
        grid=(nh,),
        in_specs=[pl.BlockSpec(memory_space=pltpu.SMEM)],
        out_specs=pl.BlockSpec((None, CHUNK, BAND_KEYS), lambda h: (h, 0, 0)),
        out_shape=jax.ShapeDtypeStruct((nh, CHUNK, BAND_KEYS), F32),
        compiler_params=_params(1),
        name="bias_strip",
    )(relb)


def _band_prompt_kernel(q_ref, kp_ref, kc_ref, vp_ref, vc_ref, strip_ref, o_ref, kw_ref, vw_ref):
    i = pl.program_id(2)
    tb = q_ref.shape[0]
    kw_ref[0:tb, :] = kp_ref[...].astype(BF16)
    kw_ref[tb:2 * tb, :] = kc_ref[...].astype(BF16)
    vw_ref[0:tb, :] = vp_ref[...].astype(BF16)
    vw_ref[tb:2 * tb, :] = vc_ref[...].astype(BF16)
    col = lax.broadcasted_iota(I32, (CHUNK, BAND_KEYS), 1)
    items = [(h, slice(h * HEAD_DIM, (h + 1) * HEAD_DIM), a * CHUNK)
             for h in range(q_ref.shape[1] // HEAD_DIM) for a in range(tb // CHUNK)]
    scores = []
    for h, sl, r0 in items:
        q = q_ref[r0:r0 + CHUNK, sl].astype(BF16)
        s = _dot_nt(q, kw_ref[r0:r0 + BAND_KEYS, sl]) * (HEAD_DIM ** -0.5) + strip_ref[h]
        min_col = jnp.where(i > 0, 0, BAND_PAST - r0)
        scores.append(jnp.where(col >= min_col, s, NEG))
    maxes = [jnp.max(s, axis=-1, keepdims=True) for s in scores]
    probs = [jnp.exp(s - m) for s, m in zip(scores, maxes)]
    sums = [jnp.sum(p, axis=-1, keepdims=True) for p in probs]
    for (h, sl, r0), p, l in zip(items, probs, sums):
        pv = jnp.dot(p.astype(BF16), vw_ref[r0:r0 + BAND_KEYS, sl], preferred_element_type=F32)
        o_ref[r0:r0 + CHUNK, sl] = pv / l


def band_prompt(u, strip, cols, nseq, t):
    rows = u.shape[0]
    tb = BAND_PAST
    nblk = t // tb
    nh = strip.shape[0]
    hps = 4 if nh % 4 == 0 else 1
    w = hps * HEAD_DIM
    qo, ko, vo = cols["qb"] // w, cols["kb"] // w, cols["vb"] // w
    cur = lambda off: pl.BlockSpec((tb, w), lambda b, h, i: (b * nblk + i, off + h))
    prev = lambda off: pl.BlockSpec((tb, w), lambda b, h, i: (b * nblk + jnp.maximum(i - 1, 0), off + h))
    return pl.pallas_call(
        _band_prompt_kernel,
        grid=(nseq, nh // hps, nblk),
        in_specs=[cur(qo), prev(ko), cur(ko), prev(vo), cur(vo),
                  pl.BlockSpec((hps, CHUNK, BAND_KEYS), lambda b, h, i: (h, 0, 0))],
        out_specs=pl.BlockSpec((tb, w), lambda b, h, i: (b * nblk + i, h)),
        out_shape=jax.ShapeDtypeStruct((rows, nh * HEAD_DIM), F32),
        scratch_shapes=[pltpu.VMEM((2 * tb, w), BF16), pltpu.VMEM((2 * tb, w), BF16)],
        compiler_params=_params(3),
        name="band_prompt",
    )(u, u, u, u, u, strip)


def _band_sample_kernel(q_ref, kn_ref, vn_ref, ck_ref, cv_ref, strip_ref, o_ref, kw_ref, vw_ref):
    n_seq = ck_ref.shape[0]
    nh = strip_ref.shape[0]
    w = ck_ref.shape[1] // nh
    t = q_ref.shape[0] // n_seq
    heads = [slice(h * HEAD_DIM, (h + 1) * HEAD_DIM) for h in range(nh)]
    for s in range(n_seq):
        rows = slice(s * t, (s + 1) * t)
        kw_ref[s, w:w + t, :] = kn_ref[rows, :].astype(BF16)
        vw_ref[s, w:w + t, :] = vn_ref[rows, :].astype(BF16)
        for h, sl in enumerate(heads):
            kw_ref[s, 0:w, sl] = ck_ref[s, pl.ds(h, w, stride=nh), :].astype(BF16)
            vw_ref[s, 0:w, sl] = cv_ref[s, pl.ds(h, w, stride=nh), :].astype(BF16)
    items = [(s, slice(s * t, (s + 1) * t), h, sl) for s in range(n_seq) for h, sl in enumerate(heads)]
    scores = [_dot_nt(q_ref[rows, sl].astype(BF16), kw_ref[s, :, sl]) * (HEAD_DIM ** -0.5) + strip_ref[h]
              for s, rows, h, sl in items]
    maxes = [jnp.max(x, axis=-1, keepdims=True) for x in scores]
    probs = [jnp.exp(x - m) for x, m in zip(scores, maxes)]
    sums = [jnp.sum(p, axis=-1, keepdims=True) for p in probs]
    for (s, rows, h, sl), p, l in zip(items, probs, sums):
        pv = jnp.dot(p.astype(BF16), vw_ref[s, :, sl], preferred_element_type=F32)
        o_ref[rows, sl] = pv / l


def band_sample(u, ck, cv, layer, strip, cols, nseq, t):
    rows = u.shape[0]
    nh = strip.shape[0]
    d = nh * HEAD_DIM
    w = ck.shape[2] // nh
    spb = 2 if nseq % 2 == 0 else 1
    ublk = lambda off: pl.BlockSpec((spb * t, d), lambda b, o=off // d: (b, o))
    cblk = pl.BlockSpec((None, spb, w * nh, HEAD_DIM), lambda b: (layer, b, 0, 0))
    return pl.pallas_call(
        _band_sample_kernel,
        grid=(nseq // spb,),
        in_specs=[ublk(cols["qb"]), ublk(cols["kb"]), ublk(cols["vb"]), cblk, cblk,
                  pl.BlockSpec((nh, CHUNK, BAND_KEYS), lambda b: (0, 0, 0))],
        out_specs=pl.BlockSpec((spb * t, d), lambda b: (b, 0)),
        out_shape=jax.ShapeDtypeStruct((rows, d), F32),
        scratch_shapes=[pltpu.VMEM((spb, w + t, d), BF16), pltpu.VMEM((spb, w + t, d), BF16)],
        compiler_params=_params(1),
        name="band_sample",
    )(u, u, u, ck, cv, strip)


def _loop(n, body, init):
    if isinstance(n, int):
        val = init
        for k in range(n):
            val = body(k, val)
        return val
    return lax.fori_loop(0, n, body, init)


IDX_BITS = 16


def _f32_to_key(x):
    bits = lax.bitcast_convert_type(x, I32)
    return jnp.where(bits < 0, bits ^ 0x7FFFFFFF, bits)


def _dsa_core(qside, segs, store_out, sk_refs, s_refs, jstar_ref, mpart_ref, lpart_ref, acc_ref, *,
              topk, q_chunk0, n_q, n_dsa):
    qb = qside["t"].shape[0]
    t_t = qside["t"].T
    wrows = [t_t[IDX_DIM + h:IDX_DIM + h + 1, :] for h in range(N_IDX_HEADS)]

    def tiles(x):
        return x.reshape(x.shape[0] // SUBLANES, SUBLANES, qb)

    w8 = [jnp.broadcast_to(w, (SUBLANES, qb)) for w in wrows]
    qcol8 = lax.broadcasted_iota(I32, (SUBLANES, qb), 1)
    q_end8 = (q_chunk0 + qcol8 // CHUNK + 1) * CHUNK
    for seg, sk_ref in zip(segs, sk_refs):
        kblk = seg["kblk"]
        n_t = kblk // SUBLANES
        krow3 = (lax.broadcasted_iota(I32, (n_t, SUBLANES, qb), 0) * SUBLANES
                 + lax.broadcasted_iota(I32, (n_t, SUBLANES, qb), 1))
        padded = seg["valid"] < kblk or n_q < qb
        live3 = (krow3 < seg["valid"]) & (lax.broadcasted_iota(I32, (n_t, SUBLANES, qb), 2) < n_q)

        def score_blk(kb, carry, seg=seg, sk_ref=sk_ref, kblk=kblk, krow3=krow3, padded=padded, live3=live3):
            kib = seg["ki"](kb)
            acc = jnp.zeros(krow3.shape, F32)
            for group in qside["qi_groups"]:
                sc = _dot_nt(kib, qside["qi"](group))
                for g, h in enumerate(group):
                    acc = acc + jnp.maximum(tiles(sc[:, g * qb:(g + 1) * qb]), 0.0) * w8[h]
            admissible = krow3 < q_end8 - (seg["pos0"] + kb * kblk)
            key = _f32_to_key(jnp.where(admissible, acc, NEG))
            sk_ref[kb] = jnp.where(live3, key, INT_MIN) if padded else key
            return carry

        _loop(seg["nblk"], score_blk, 0)

    def count(pred):
        part = jnp.zeros((SUBLANES, qb), I32)
        for seg, sk_ref in zip(segs, sk_refs):
            n_t = seg["kblk"] // SUBLANES
            off = (lax.broadcasted_iota(I32, (n_t, SUBLANES, qb), 0) * SUBLANES
                   + lax.broadcasted_iota(I32, (n_t, SUBLANES, qb), 1))

            def body(kb, part, seg=seg, sk_ref=sk_ref, off=off):
                idx = seg["pos0"] + kb * seg["kblk"] + off
                return part + jnp.sum(jnp.where(pred(sk_ref[kb], idx), 1, 0), axis=0)

            part = _loop(seg["nblk"], body, part)
        return jnp.broadcast_to(jnp.sum(part, axis=0, keepdims=True), (SUBLANES, qb))

    zero = jnp.zeros((SUBLANES, qb), I32)
    n_stored = sum(seg["nblk"] * seg["kblk"] for seg in segs)
    c_zero = count(lambda key, idx: key >= zero)
    thr0 = jnp.where(c_zero >= topk, 0, INT_MIN).astype(I32)
    c_lo0 = jnp.where(c_zero >= topk, c_zero, zero + n_stored)

    def bit_step(bit, state):
        t, c = state
        cand = t + jnp.left_shift(jnp.int32(1), bit)
        c_cand = count(lambda key, idx: key >= cand)
        ok = c_cand >= topk
        return jnp.where(ok, cand, t), jnp.where(ok, c_cand, c)

    thr, c_lo = lax.fori_loop(0, 31, lambda i, st: bit_step(30 - i, st), (thr0, c_lo0))
    lo = jnp.maximum(thr, KEY_HALF_NEG + 1)

    big = 2 ** IDX_BITS
    jstar_ref[...] = jnp.full((SUBLANES, qb), big, I32)
    surplus = (c_lo > topk) & (thr > KEY_HALF_NEG)

    @pl.when(jnp.max(jnp.where(surplus, 1, 0)) > 0)
    def _():
        need = topk - count(lambda key, idx: key > thr)

        def idx_body(it, j):
            cand = j + jnp.left_shift(jnp.int32(1), IDX_BITS - 1 - it)
            f = count(lambda key, idx: (key == lo) & (idx < cand))
            return jnp.where(f <= need, cand, j)

        j = lax.fori_loop(0, IDX_BITS, idx_body, zero)
        jstar_ref[...] = jnp.where(surplus, j, big)

    jstar = jstar_ref[...]

    mpart_ref[...] = jnp.full(mpart_ref.shape, NEG, F32)
    lpart_ref[...] = jnp.zeros(lpart_ref.shape, F32)
    acc_ref[...] = jnp.zeros(acc_ref.shape, F32)

    for seg, sk_ref, s_ref in zip(segs, sk_refs, s_refs):
        kblk = seg["kblk"]
        n_t = kblk // SUBLANES
        krow3 = (lax.broadcasted_iota(I32, (n_t, SUBLANES, qb), 0) * SUBLANES
                 + lax.broadcasted_iota(I32, (n_t, SUBLANES, qb), 1))

        def pass_a(kb, carry, seg=seg, sk_ref=sk_ref, s_ref=s_ref, kblk=kblk, krow3=krow3):
            key = sk_ref[kb]
            mask = (key >= lo) & ((key > lo) | (krow3 < jstar - (seg["pos0"] + kb * kblk)))
            raw = [tiles(_dot_nt(seg["k"](kb, h), qside["q"](h))) for h in range(n_dsa)]
            for h in range(n_dsa):
                s = jnp.where(mask, raw[h] * (HEAD_DIM ** -0.5), NEG)
                s_ref[h, kb] = s
                mpart_ref[h] = jnp.maximum(mpart_ref[h], jnp.max(s, axis=0))
            return carry

        _loop(seg["nblk"], pass_a, 0)

    m8 = [jnp.broadcast_to(jnp.max(mpart_ref[h], axis=0, keepdims=True), (SUBLANES, qb)) for h in range(n_dsa)]

    for seg, s_ref in zip(segs, s_refs):
        def pass_b(kb, carry, seg=seg, s_ref=s_ref):
            for h in range(n_dsa):
                p = jnp.exp(s_ref[h, kb] - m8[h])
                lpart_ref[h] += jnp.sum(p, axis=0)
                acc_ref[h] += seg["pv"](kb, h, p.reshape(seg["kblk"], qb).astype(BF16))
            return carry

        _loop(seg["nblk"], pass_b, 0)

    for h in range(n_dsa):
        l = jnp.sum(lpart_ref[h], axis=0, keepdims=True)
        store_out(h, (acc_ref[h] / l).T)


def _blk_start(kb, kblk):
    return kb * kblk if isinstance(kb, int) else pl.multiple_of(kb * kblk, kblk)


def _dsa_scratch(qb, seg_shapes, n_dsa):
    return ([pltpu.VMEM((nblk, kblk // SUBLANES, SUBLANES, qb), I32) for nblk, kblk in seg_shapes]
            + [pltpu.VMEM((n_dsa, nblk, kblk // SUBLANES, SUBLANES, qb), F32) for nblk, kblk in seg_shapes]
            + [pltpu.VMEM((SUBLANES, qb), I32),
               pltpu.VMEM((n_dsa, SUBLANES, qb), F32), pltpu.VMEM((n_dsa, SUBLANES, qb), F32),
               pltpu.VMEM((n_dsa, HEAD_DIM, qb), F32)])


def _dsa_prompt_kernel(qi_ref, t_ref, q_ref, ki_ref, k_ref, vt_ref, o_ref, sk_ref, s_ref, jstar_ref,
                       mpart_ref, lpart_ref, acc_ref, *, kblk, topk):
    i = pl.program_id(1)
    qb = q_ref.shape[0]
    nblk = (i * qb + qb + kblk - 1) // kblk

    def head(h):
        return slice(h * HEAD_DIM, (h + 1) * HEAD_DIM)

    seg = dict(ki=lambda kb: ki_ref[pl.ds(_blk_start(kb, kblk), kblk), :],
               k=lambda kb, h: k_ref[pl.ds(_blk_start(kb, kblk), kblk), head(h)],
               pv=lambda kb, h, p: jnp.dot(vt_ref[head(h), pl.ds(_blk_start(kb, kblk), kblk)], p,
                                           preferred_element_type=F32),
               nblk=nblk, kblk=kblk, pos0=0, valid=kblk)
    qside = dict(t=t_ref[...],
                 qi_groups=[(h, h + 1) for h in range(0, N_IDX_HEADS, 2)],
                 qi=lambda group: jnp.concatenate([qi_ref[:, h * LANES:(h + 1) * LANES] for h in group], axis=0),
                 q=lambda h: q_ref[:, head(h)])

    def store_out(h, x):
        o_ref[:, head(h)] = x

    _dsa_core(qside, [seg], store_out, [sk_ref], [s_ref], jstar_ref, mpart_ref, lpart_ref, acc_ref,
              topk=topk, q_chunk0=(i * qb) // CHUNK, n_q=qb, n_dsa=q_ref.shape[1] // HEAD_DIM)


def dsa_prompt(qi_pad, trot, q_bf, ki_bf, k_bf, v_t, nseq, t, qb, kblk, topk):
    rows, d = q_bf.shape
    n_dsa = d // HEAD_DIM
    nq = t // qb
    row_blk = lambda w: pl.BlockSpec((qb, w), lambda b, i: (b * nq + i, 0))
    seq_blk = lambda w: pl.BlockSpec((t, w), lambda b, i: (b, 0))
    return pl.pallas_call(
        functools.partial(_dsa_prompt_kernel, kblk=kblk, topk=topk),
        grid=(nseq, nq),
        in_specs=[row_blk(qi_pad.shape[1]), row_blk(LANES), row_blk(d),
                  seq_blk(LANES), seq_blk(d), pl.BlockSpec((d, t), lambda b, i: (0, b))],
        out_specs=row_blk(d),
        out_shape=jax.ShapeDtypeStruct((rows, d), F32),
        scratch_shapes=_dsa_scratch(qb, [(t // kblk, kblk)], n_dsa),
        compiler_params=_params(2),
        name="dsa_prompt",
    )(qi_pad, trot, q_bf, ki_bf, k_bf, v_t)


def _dsa_sample_kernel(qi_ref, t_ref, q_ref, kin_ref, kn_ref, vn_ref, cki_ref, ck_ref, cv_ref, o_ref,
                       qi_s, q_s, kin_s, kn_s, vn_s, kit_s, sk0_ref, sk1_ref, s0_ref, s1_ref, jstar_ref,
                       mpart_ref, lpart_ref, acc_ref, *, kblk, topk):
    past = cki_ref.shape[2]
    tq = q_ref.shape[0] // 2
    qb = 2 * tq
    n_dsa = q_ref.shape[1] // HEAD_DIM
    pair_w = 2 * LANES

    def head(h):
        return slice(h * HEAD_DIM, (h + 1) * HEAD_DIM)

    def rows(a):
        return slice(a * tq, (a + 1) * tq)

    def half(h, a):
        return slice(h * pair_w + a * LANES, h * pair_w + (a + 1) * LANES)

    for ref in (qi_s, q_s, kin_s, kn_s):
        ref[...] = jnp.zeros(ref.shape, ref.dtype)
    for a in range(2):
        for h in range(N_IDX_HEADS):
            qi_s[rows(a), half(h, a)] = qi_ref[rows(a), h * LANES:(h + 1) * LANES]
        for h in range(n_dsa):
            q_s[rows(a), half(h, a)] = q_ref[rows(a), head(h)]
            kn_s[0:tq, half(h, a)] = kn_ref[rows(a), head(h)]
        kin_s[0:tq, half(0, a)] = kin_ref[rows(a), :]
        vn_s[a, 0:tq, :] = vn_ref[rows(a), :]
        vn_s[a, tq:, :] = jnp.zeros((qb - tq, vn_s.shape[2]), F32)

    first_seq = lax.broadcasted_iota(I32, (HEAD_DIM, qb), 1) < tq

    def pair_pv(v_a, v_b, p):
        out_a = jnp.dot(v_a.T.astype(BF16), p, preferred_element_type=F32)
        out_b = jnp.dot(v_b.T.astype(BF16), p, preferred_element_type=F32)
        return jnp.where(first_seq, out_a, out_b)

    def cache_rows(ref, a, kb, h):
        return ref[a, pl.ds(kb * kblk * n_dsa + h, kblk, stride=n_dsa), :]

    def cache_ki(kb):
        parts = []
        for a in range(2):
            kit_s[a, 0:IDX_DIM, :] = cki_ref[a, :, pl.ds(kb * kblk, kblk)]
            kit_s[a, IDX_DIM:, :] = cki_ref[a, :, pl.ds(kb * kblk, kblk)]
            parts.append(kit_s[a].T.astype(BF16))
        return jnp.concatenate(parts, axis=1)

    segs = [dict(ki=cache_ki,
                 k=lambda kb, h: jnp.concatenate([cache_rows(ck_ref, a, kb, h).astype(BF16) for a in range(2)],
                                                 axis=1),
                 pv=lambda kb, h, p: pair_pv(cache_rows(cv_ref, 0, kb, h), cache_rows(cv_ref, 1, kb, h), p),
                 nblk=past // kblk, kblk=kblk, pos0=0, valid=kblk),
            dict(ki=lambda kb: kin_s[...],
                 k=lambda kb, h: kn_s[:, h * pair_w:(h + 1) * pair_w],
                 pv=lambda kb, h, p: pair_pv(vn_s[0, :, head(h)], vn_s[1, :, head(h)], p),
                 nblk=1, kblk=qb, pos0=past, valid=tq)]
    qside = dict(t=t_ref[...],
                 qi_groups=[(h,) for h in range(N_IDX_HEADS)],
                 qi=lambda group: qi_s[:, group[0] * pair_w:(group[0] + 1) * pair_w],
                 q=lambda h: q_s[:, h * pair_w:(h + 1) * pair_w])

    def store_out(h, x):
        o_ref[:, head(h)] = x

    _dsa_core(qside, segs, store_out, [sk0_ref, sk1_ref], [s0_ref, s1_ref], jstar_ref,
              mpart_ref, lpart_ref, acc_ref, topk=topk, q_chunk0=past // CHUNK, n_q=qb, n_dsa=n_dsa)


def dsa_sample(qi_pad, trot, q_bf, ki_bf, k_bf, u, vc_col, cki, ck, cv, layer, nseq, t, kblk, topk):
    rows, d = q_bf.shape
    n_dsa = d // HEAD_DIM
    past = cki.shape[3]
    qb = 2 * t
    assert qb == LANES and nseq % 2 == 0
    row_blk = lambda w, o=0: pl.BlockSpec((qb, w), lambda b, o=o: (b, o))
    cache_blk = lambda r, w: pl.BlockSpec((None, 2, r, w), lambda b: (layer, b, 0, 0))
    pair_scratch = [pltpu.VMEM((qb, 2 * qi_pad.shape[1]), BF16), pltpu.VMEM((qb, 2 * d), BF16),
                    pltpu.VMEM((qb, 2 * LANES), BF16), pltpu.VMEM((qb, 2 * d), BF16),
                    pltpu.VMEM((2, qb, d), F32), pltpu.VMEM((2, LANES, kblk), F32)]
    return pl.pallas_call(
        functools.partial(_dsa_sample_kernel, kblk=kblk, topk=topk),
        grid=(nseq // 2,),
        in_specs=[row_blk(qi_pad.shape[1]), row_blk(LANES), row_blk(d),
                  row_blk(LANES), row_blk(d), row_blk(d, vc_col),
                  cache_blk(cki.shape[2], past), cache_blk(past * n_dsa, HEAD_DIM),
                  cache_blk(past * n_dsa, HEAD_DIM)],
        out_specs=row_blk(d),
        out_shape=jax.ShapeDtypeStruct((rows, d), F32),
        scratch_shapes=pair_scratch + _dsa_scratch(qb, [(past // kblk, kblk), (1, qb)], n_dsa),
        compiler_params=_params(1, LARGE_VMEM_LIMIT),
        name="dsa_sample",
    )(qi_pad, trot, q_bf, ki_bf, k_bf, u, cki, ck, cv)


def _outproj_kernel(oa_ref, ob_ref, oc_ref, za_ref, zb_ref, zc_ref, g_ref, w_ref, x_ref, mod_ref, g2_ref,
                    *rest, d_lru, d_band, last):
    if last:
        yo_ref, y_ref = rest
    else:
        mod2_ref, xo_ref, h_ref, y_ref = rest

    def branch(o_ref, z_ref, lo, hi):
        o = o_ref[...]
        z = z_ref[...]
        ms = jnp.mean(o * o, axis=-1, keepdims=True)
        y = o * lax.rsqrt(ms + EPS) * g_ref[:, lo:hi]
        y_ref[:, lo:hi] = (y * (z * jax.nn.sigmoid(z))).astype(BF16)

    d_mix = y_ref.shape[1]
    branch(oa_ref, za_ref, 0, d_lru)
    branch(ob_ref, zb_ref, d_lru, d_lru + d_band)
    branch(oc_ref, zc_ref, d_lru + d_band, d_mix)
    out = jnp.dot(y_ref[...], w_ref[...], preferred_element_type=F32)
    n_seq = mod_ref.shape[0]
    t = out.shape[0] // n_seq
    for s in range(n_seq):
        rows = slice(s * t, (s + 1) * t)
        x_new = x_ref[rows, :] + mod_ref[s, 2:3, :] * out[rows, :]
        ms = jnp.mean(x_new * x_new, axis=-1, keepdims=True)
        normed = x_new * lax.rsqrt(ms + EPS) * g2_ref[...]
        if last:
            yo_ref[rows, :] = normed
        else:
            xo_ref[rows, :] = x_new
            h_ref[rows, :] = (normed * (1.0 + mod2_ref[s, 1:2, :]) + mod2_ref[s, 0:1, :]).astype(BF16)


def outproj(oa, ob, oc, u, g_branch, w_out, layer, x, mod3, g2, mod3_next, cols, nseq, t, tm):
    rows, d = x.shape
    d_lru, d_band, d_dsa = oa.shape[1], ob.shape[1], oc.shape[1]
    d_mix = d_lru + d_band + d_dsa
    nblk = max(t // tm, 1)
    seqs_per_blk = max(tm // t, 1)
    last = mod3_next is None
    rb = lambda w, o=0: pl.BlockSpec((tm, w), lambda b, i, o=o: (b * nblk + i, o))
    vec = pl.BlockSpec((1, d), lambda b, i: (0, 0))
    mod_spec = pl.BlockSpec((seqs_per_blk, 3, d), lambda b, i: (b, 0, 0))
    in_specs = [rb(d_lru), rb(d_band), rb(d_dsa),
                rb(d_lru, cols["za"] // d_lru), rb(d_band, cols["zb"] // d_band), rb(d_dsa, cols["zc"] // d_dsa),
                pl.BlockSpec((1, d_mix), lambda b, i: (0, 0)),
                pl.BlockSpec((None, d_mix, d), lambda b, i: (layer, 0, 0), pipeline_mode=pl.Buffered(1)),
                rb(d), mod_spec, vec]
    args = [oa, ob, oc, u, u, u, g_branch, w_out, x, mod3, g2]
    if last:
        out_specs = rb(d)
        out_shape = jax.ShapeDtypeStruct((rows, d), F32)
    else:
        in_specs.append(mod_spec)
        args.append(mod3_next)
        out_specs = [rb(d), rb(d)]
        out_shape = [jax.ShapeDtypeStruct((rows, d), F32), jax.ShapeDtypeStruct((rows, d), BF16)]
    return pl.pallas_call(
        functools.partial(_outproj_kernel, d_lru=d_lru, d_band=d_band, last=last),
        grid=(nseq // seqs_per_blk, nblk),
        in_specs=in_specs,
        out_specs=out_specs,
        out_shape=out_shape,
        scratch_shapes=[pltpu.VMEM((tm, d_mix), BF16)],
        compiler_params=_params(2, LARGE_VMEM_LIMIT),
        name="outproj",
    )(*args)


def _column_offsets(d_lru, d_band, d_dsa):
    names = ["xa", "za", "qb", "kb", "vb", "zb", "qc", "kc", "vc", "zc", "qi"]
    widths = [d_lru, d_lru, d_band, d_band, d_band, d_band, d_dsa, d_dsa, d_dsa, d_dsa, N_IDX_HEADS * IDX_DIM]
    cols, off = {}, 0
    for n, w in zip(names, widths):
        cols[n] = off
        off += w
    cols["main"] = off
    cols["d_lru"], cols["d_band"], cols["d_dsa"] = d_lru, d_band, d_dsa
    return cols


def _layer_stream(x, h, mod3, mod3_next, g_next, lw, layer, depth, cols, nseq, t, tables, state, strip, stacks,
                  prompt):
    d_lru, d_band, d_dsa = cols["d_lru"], cols["d_band"], cols["d_dsa"]
    n_band, n_dsa = d_band // HEAD_DIM, d_dsa // HEAD_DIM
    rows = nseq * t
    tb = 256 if t % 256 == 0 else t
    tm = 1024 if rows % 1024 == 0 else rows
    u = matmul(h, lw["w_in"], layer, cols["main"], tm, 1024)
    tail = matmul(h, lw["w_tail"], layer, LANES, tm, LANES)

    q_bf, k_bf, qi_pad, t_rot, ki_bf, k_stack, v_stack, *maybe_vt = rope_call(
        u, tail, tables, cols, 512 if rows % 512 == 0 else (256 if rows % 256 == 0 else tb), prompt, layer, depth,
        stacks)

    conv_s, lru_s = state[0], state[1]
    cs8 = jnp.concatenate([jnp.zeros((nseq, 8 - (CONV_W - 1), d_lru), F32), conv_s], axis=1)
    hs = lru_call(u, cs8, lru_s.reshape(nseq, 1, d_lru), lw["conv_w"], lw["conv_b"], lw["w_rg"], lw["b_rg"],
                  lw["w_ig"], lw["b_ig"], lw["lam"], nseq, t, tb)

    if prompt:
        ob = band_prompt(u, strip, cols, nseq, t)
        oc = dsa_prompt(qi_pad, t_rot, q_bf, ki_bf, k_bf, maybe_vt[0], nseq, t, 256, 512, min(TOPK_MAX, t // 4))
    else:
        bk, bv, dk, dv, dik = state[2:]
        past = dik.shape[3]
        assert (past + t - 1) // CHUNK <= past // CHUNK and t == CHUNK and bk.shape[2] == BAND_PAST * n_band
        ob = band_sample(u, bk, bv, layer, strip, cols, nseq, t)
        oc = dsa_sample(qi_pad, t_rot, q_bf, ki_bf, k_bf, u, cols["vc"] // d_dsa, dik, dk, dv, layer,
                        nseq, t, 512, min(TOPK_MAX, (past + t) // 4))

    res = outproj(hs, ob, oc, u, lw["g_branch"], lw["w_out"], layer, x, mod3, g_next, mod3_next, cols, nseq, t,
                  512 if rows % 512 == 0 else (256 if rows % 256 == 0 else tb))
    x_new, h_next = (res, None) if mod3_next is None else res

    u3 = u.reshape(nseq, t, -1)
    xa = u3[:, :, cols["xa"]:cols["xa"] + d_lru]
    if prompt:
        new_conv = xa[:, t - (CONV_W - 1):]
    else:
        new_conv = jnp.concatenate([conv_s, xa], axis=1)[:, -(CONV_W - 1):]
    nbr = min(BAND_PAST, t)
    new = (new_conv,
           hs.reshape(nseq, t, d_lru)[:, -1],
           u3[:, t - nbr:, cols["kb"]:cols["kb"] + d_band].reshape(nseq, nbr, n_band, HEAD_DIM),
           u3[:, t - nbr:, cols["vb"]:cols["vb"] + d_band].reshape(nseq, nbr, n_band, HEAD_DIM),
           None,
           None,
           t_rot.reshape(nseq, t, LANES)[:, :, :IDX_DIM])
    return x_new, h_next, new, (k_stack, v_stack)


def kernel(x_prompt, x_sample, c_prompt, c_sample, state_conv, state_lru, cache_band_k, cache_band_v,
           cache_dsa_k, cache_dsa_v, cache_dsa_idx_k, g_norm, w_ada, b_ada, w_in, conv_w, conv_b,
           w_rg, b_rg, w_ig, b_ig, lru_lambda, rel_bias, g_branch, w_out, g_final):
    depth = w_in.shape[0]
    nb_p, t_p, d = x_prompt.shape
    nb_s, t_s, _ = x_sample.shape
    past = cache_dsa_k.shape[2]
    d_lru = conv_w.shape[2]
    d_band = cache_band_k.shape[3] * HEAD_DIM
    d_dsa = cache_dsa_k.shape[3] * HEAD_DIM
    cols = _column_offsets(d_lru, d_band, d_dsa)
    n_main = cols["main"]

    mod = ada_all(jnp.concatenate([c_prompt, c_sample], axis=0), w_ada, b_ada)
    mod = mod.reshape(depth, nb_p + nb_s, 3, d)

    w_tail = jnp.pad(w_in[:, :, n_main:], ((0, 0), (0, 0), (0, LANES - (w_in.shape[2] - n_main)))).astype(BF16)
    w_out_bf = w_out.astype(BF16)
    w_rg_bf = w_rg.astype(BF16)
    w_ig_bf = w_ig.astype(BF16)

    band_k = cache_band_k.reshape(depth, nb_s, -1, HEAD_DIM)
    band_v = cache_band_v.reshape(depth, nb_s, -1, HEAD_DIM)
    dsa_k = cache_dsa_k.reshape(depth, nb_s, -1, HEAD_DIM)
    dsa_v = cache_dsa_v.reshape(depth, nb_s, -1, HEAD_DIM)
    idx_k_t = jnp.swapaxes(cache_dsa_idx_k, 2, 3)
    w_in_t = jnp.swapaxes(w_in, 1, 2)

    tab_p = rope_tables(jnp.tile(jnp.arange(t_p), nb_p))
    tab_s = rope_tables(jnp.tile(past + jnp.arange(t_s), nb_s))

    xp = x_prompt.reshape(nb_p * t_p, d)
    xs = x_sample.reshape(nb_s * t_s, d)
    zero_state = (jnp.zeros((nb_p, CONV_W - 1, d_lru), F32), jnp.zeros((nb_p, d_lru), F32))
    p_new, s_new = [], []
    p_stacks = s_stacks = None
    hp = normmod(xp, g_norm[0][None], mod[0, :nb_p], nb_p, t_p, 256 if t_p % 256 == 0 else t_p)
    hsm = normmod(xs, g_norm[0][None], mod[0, nb_p:], nb_s, t_s, 256 if t_s % 256 == 0 else t_s)
    for l in range(depth):
        lw = dict(w_in=w_in_t, w_tail=w_tail, conv_w=conv_w[l], conv_b=conv_b[l][None],
                  w_rg=w_rg_bf[l], b_rg=b_rg[l][None], w_ig=w_ig_bf[l], b_ig=b_ig[l][None],
                  lam=lru_lambda[l][None], g_branch=g_branch[l][None], w_out=w_out_bf)
        last = l == depth - 1
        g_next = g_final[None] if last else g_norm[l + 1][None]
        modp_next = None if last else mod[l + 1, :nb_p]
        mods_next = None if last else mod[l + 1, nb_p:]
        strip = bias_strip(rel_bias[l])
        xp, hp, pn, p_stacks = _layer_stream(xp, hp, mod[l, :nb_p], modp_next, g_next, lw, l, depth, cols, nb_p, t_p,
                                             tab_p, zero_state, strip, p_stacks, True)
        st = (state_conv[l], state_lru[l], band_k, band_v, dsa_k, dsa_v, idx_k_t)
        xs, hsm, sn, s_stacks = _layer_stream(xs, hsm, mod[l, nb_p:], mods_next, g_next, lw, l, depth, cols, nb_s,
                                              t_s, tab_s, st, strip, s_stacks, False)
        p_new.append(pn)
        s_new.append(sn)

    y_prompt = xp.reshape(nb_p, t_p, d)
    y_sample = xs.reshape(nb_s, t_s, d)
    n_dsa = d_dsa // HEAD_DIM

    def outputs(new, stacks, nseq, t):
        out = [None if new[0][j] is None else jnp.stack([layer_new[j] for layer_new in new], axis=0)
               for j in range(7)]
        out[4], out[5] = (s.reshape(depth, nseq, t, n_dsa, HEAD_DIM) for s in stacks)
        return out

    return (y_prompt, y_sample, *outputs(p_new, p_stacks, nb_p, t_p), *outputs(s_new, s_stacks, nb_s, t_s))
```
